```python
import jax, jax.numpy as jnp
from jax import lax
import numpy as np


D_MODEL = 2048
BATCH = 2
SEQ = 4096
DEPTH = 1

CTX_LEN = 256
GRID_W = 64
D_CONV = D_MODEL // 2
CONV_WIDTH = 31
CONV_PAD = (CONV_WIDTH - 1) // 2
D_HGRN = D_MODEL // 2
HGRN_HEAD_DIM = 128
HGRN_HEADS = D_HGRN // HGRN_HEAD_DIM
HGRN_CHUNK = 64
D_FF = 4 * D_MODEL
N_MOD = 6
EPS = 1e-6
D_IN = 5 * D_HGRN + 2 * D_CONV + 2 * D_MODEL

kernel_name = 'hybrid_conformer_hgrn2_diffusion_block'


def rms_norm(x, g):
    xf = x.astype(jnp.float32)
    y = xf * lax.rsqrt(jnp.mean(xf * xf, axis=-1, keepdims=True) + EPS)
    return (y * g.astype(jnp.float32)).astype(x.dtype)


def layer_norm(x, g, b):
    xf = x.astype(jnp.float32)
    mu = jnp.mean(xf, axis=-1, keepdims=True)
    var = jnp.mean(jnp.square(xf - mu), axis=-1, keepdims=True)
    y = (xf - mu) * lax.rsqrt(var + EPS)
    return (y * g.astype(jnp.float32) + b.astype(jnp.float32)).astype(x.dtype)


def modulate(h, shift, scale):
    return h * (1 + scale) + shift


def split_proj(p):
    cuts = [D_HGRN, 2 * D_HGRN, 3 * D_HGRN, 4 * D_HGRN, 5 * D_HGRN,
            5 * D_HGRN + D_CONV, 5 * D_HGRN + 2 * D_CONV, 5 * D_HGRN + 2 * D_CONV + D_MODEL]
    return jnp.split(p, cuts, axis=-1)


def dwconv1d(u, w, b):
    ch = u.shape[-1]
    y = lax.conv_general_dilated(u, w[:, None, :].astype(u.dtype), window_strides=(1,),
                                 padding=[(CONV_PAD, CONV_PAD)],
                                 dimension_numbers=('NWC', 'WIO', 'NWC'),
                                 feature_group_count=ch)
    return y + b


def axial_dwconv(u, w, b, rows):
    bsz, seq, ch = u.shape
    half = ch // 2
    uh = u[..., :half].reshape(bsz * rows, GRID_W, half)
    yh = dwconv1d(uh, w[:, :half], b[:half]).reshape(bsz, seq, half)
    uv = u[..., half:].reshape(bsz, rows, GRID_W, ch - half).transpose(0, 2, 1, 3)
    uv = uv.reshape(bsz * GRID_W, rows, ch - half)
    yv = dwconv1d(uv, w[:, half:], b[half:]).reshape(bsz, GRID_W, rows, ch - half)
    yv = yv.transpose(0, 2, 1, 3).reshape(bsz, seq, ch - half)
    return jnp.concatenate([yh, yv], axis=-1)


def conformer_branch(val, gate, w_dw, b_dw, ln_g, ln_b, w_pw, rows):
    u = val * jax.nn.sigmoid(gate)
    u = dwconv1d(u, w_dw, b_dw) if rows is None else axial_dwconv(u, w_dw, b_dw, rows)
    u = jax.nn.silu(layer_norm(u, ln_g, ln_b))
    return u @ w_pw


def to_heads(t):
    return t.reshape(t.shape[:-1] + (HGRN_HEADS, HGRN_HEAD_DIM))


def hgrn_forget(f_raw, lb):
    f = lb + (1.0 - lb) * jax.nn.sigmoid(f_raw.astype(jnp.float32))
    return to_heads(jnp.log(f)), to_heads(1.0 - f)


def q_prep(p_q):
    return to_heads(jax.nn.silu(p_q.astype(jnp.float32))) * (HGRN_HEAD_DIM ** -0.5)


def hgrn_chunk_scan(q, k, v, logf, s0):
    bsz, seq = q.shape[:2]
    n_chunks = seq // HGRN_CHUNK

    def to_chunks(t):
        return t.reshape(bsz, n_chunks, HGRN_CHUNK, HGRN_HEADS, t.shape[-1]).transpose(1, 0, 3, 2, 4)

    tri = jnp.tril(jnp.ones((HGRN_CHUNK, HGRN_CHUNK), dtype=bool))

    def step(s, xs):
        qc, kc, vc, gc = xs
        a = jnp.cumsum(gc, axis=2)
        a_last = a[:, :, -1:, :]
        inter = jnp.einsum('bhtk,bhkv->bhtv', qc * jnp.exp(a), s)
        diff = a[:, :, :, None, :] - a[:, :, None, :, :]
        decay = jnp.exp(jnp.where(tri[:, :, None], diff, -jnp.inf))
        scores = jnp.einsum('bhtk,bhsk,bhtsk->bhts', qc, kc, decay)
        intra = jnp.einsum('bhts,bhsv->bhtv', scores, vc)
        s_new = jnp.exp(a_last[:, :, 0, :])[..., None] * s + \
            jnp.einsum('bhsk,bhsv->bhkv', kc * jnp.exp(a_last - a), vc)
        return s_new, inter + intra

    s_fin, out = lax.scan(step, s0, (to_chunks(q), to_chunks(k), to_chunks(v), to_chunks(logf)))
    out = out.transpose(1, 0, 3, 2, 4).reshape(bsz, seq, HGRN_HEADS, HGRN_HEAD_DIM)
    return out, s_fin


def hgrn_final_state(k, v, logf):
    a = jnp.cumsum(logf, axis=1)
    return jnp.einsum('blhk,blhv->bhkv', k * jnp.exp(a[:, -1:] - a), v)


def hgrn_bidir(q, v, gf, gb, s0f, s0b):
    o_f, s_f = hgrn_chunk_scan(q, gf[1], v, gf[0], s0f)
    fl = lambda t: jnp.flip(t, axis=1)
    o_b, s_b = hgrn_chunk_scan(fl(q), fl(gb[1]), fl(v), fl(gb[0]), s0b)
    return o_f + fl(o_b), s_f, s_b


def hgrn_readout(o, g, norm_g, w_o):
    o = rms_norm(o, norm_g.reshape(HGRN_HEADS, HGRN_HEAD_DIM))
    o = o.reshape(o.shape[:2] + (D_HGRN,)).astype(g.dtype) * jax.nn.silu(g)
    return o @ w_o


def gated_merge(y_conv, y_hgrn, p_gc, p_gh, w_o):
    return (jax.nn.sigmoid(p_gc) * y_conv + jax.nn.sigmoid(p_gh) * y_hgrn) @ w_o


def sq_relu_mlp(h, w1, w2):
    return jnp.square(jax.nn.relu(h @ w1)) @ w2


def setup_inputs(seed: int = 0) -> dict:
    key = jax.random.key(seed)
    ks = jax.random.split(key, 24)
    f32 = jnp.float32

    def nrm(k, shape, scale):
        return jax.random.normal(k, shape, f32) * scale

    def gain(k, shape):
        return 1.0 + 0.05 * jax.random.normal(k, shape, f32)

    return {
        'x': nrm(ks[0], (BATCH, SEQ, D_MODEL), 1.0),
        'c': nrm(ks[1], (BATCH, D_MODEL), 1.0),
        'ctx': nrm(ks[2], (BATCH, CTX_LEN, D_MODEL), 1.0),
        'c_ctx': nrm(ks[3], (D_MODEL,), 1.0),
        'w_mod': nrm(ks[4], (DEPTH, D_MODEL, N_MOD * D_MODEL), 0.5 * D_MODEL ** -0.5),
        'b_mod': nrm(ks[5], (DEPTH, N_MOD * D_MODEL), 0.02),
        'norm_pre_mix': gain(ks[6], (DEPTH, D_MODEL)),
        'norm_post_mix': gain(ks[7], (DEPTH, D_MODEL)),
        'norm_pre_mlp': gain(ks[8], (DEPTH, D_MODEL)),
        'norm_post_mlp': gain(ks[9], (DEPTH, D_MODEL)),
        'w_in': nrm(ks[10], (DEPTH, D_MODEL, D_IN), D_MODEL ** -0.5),
        'conv_dw_w': nrm(ks[11], (DEPTH, CONV_WIDTH, D_CONV), CONV_WIDTH ** -0.5),
        'conv_dw_b': nrm(ks[12], (DEPTH, D_CONV), 0.02),
        'conv_ln_g': gain(ks[13], (DEPTH, D_CONV)),
        'conv_ln_b': nrm(ks[14], (DEPTH, D_CONV), 0.02),
        'conv_pw_w': nrm(ks[15], (DEPTH, D_CONV, D_MODEL), D_CONV ** -0.5),
        'hgrn_lb_logits': nrm(ks[16], (2, DEPTH + 1, D_HGRN), 0.5),
        'hgrn_norm_g': gain(ks[17], (DEPTH, D_HGRN)),
        'hgrn_out_w': nrm(ks[18], (DEPTH, D_HGRN, D_MODEL), D_HGRN ** -0.5),
        'w_out': nrm(ks[19], (DEPTH, D_MODEL, D_MODEL), D_MODEL ** -0.5),
        'mlp_w1': nrm(ks[20], (DEPTH, D_MODEL, D_FF), D_MODEL ** -0.5),
        'mlp_w2': nrm(ks[21], (DEPTH, D_FF, D_MODEL), D_FF ** -0.5),
    }


def reference(x, c, ctx, c_ctx, w_mod, b_mod, norm_pre_mix, norm_post_mix, norm_pre_mlp, norm_post_mlp,
              w_in, conv_dw_w, conv_dw_b, conv_ln_g, conv_ln_b, conv_pw_w, hgrn_lb_logits, hgrn_norm_g,
              hgrn_out_w, w_out, mlp_w1, mlp_w2):
    f32 = jnp.float32
    rows = x.shape[1] // GRID_W
    lb_all = jnp.cumsum(jax.nn.softmax(hgrn_lb_logits.astype(f32), axis=1), axis=1)
    xc = ctx
    for l in range(DEPTH):
        last = l == DEPTH - 1
        mod = jax.nn.silu(c) @ w_mod[l] + b_mod[l]
        sh_m, sc_m, ga_m, sh_f, sc_f, ga_f = [m[:, None, :] for m in jnp.split(mod, N_MOD, axis=-1)]
        mod_c = jax.nn.silu(c_ctx) @ w_mod[l] + b_mod[l]
        shc_m, scc_m, gac_m, shc_f, scc_f, gac_f = jnp.split(mod_c, N_MOD, axis=-1)
        lb_f, lb_b = lb_all[0, l], lb_all[1, l]

        hc = modulate(rms_norm(xc, norm_pre_mix[l]), shc_m, scc_m)
        if last:
            pc_i, pc_ff, pc_fb = jnp.split(hc @ w_in[l][:, :3 * D_HGRN], 3, axis=-1)
            vc = to_heads(pc_i.astype(f32))
            lfc_f, kc_f = hgrn_forget(pc_ff, lb_f)
            lfc_b, kc_b = hgrn_forget(pc_fb, lb_b)
            s_f = hgrn_final_state(kc_f, vc, lfc_f)
            s_b = hgrn_final_state(jnp.flip(kc_b, 1), jnp.flip(vc, 1), jnp.flip(lfc_b, 1))
        else:
            pc = split_proj(hc @ w_in[l])
            vc = to_heads(pc[0].astype(f32))
            zeros = jnp.zeros((xc.shape[0], HGRN_HEADS, HGRN_HEAD_DIM, HGRN_HEAD_DIM), f32)
            oc, s_f, s_b = hgrn_bidir(q_prep(pc[3]), vc, hgrn_forget(pc[1], lb_f),
                                      hgrn_forget(pc[2], lb_b), zeros, zeros)
            yc_h = hgrn_readout(oc, pc[4], hgrn_norm_g[l], hgrn_out_w[l])
            yc_c = conformer_branch(pc[5], pc[6], conv_dw_w[l], conv_dw_b[l], conv_ln_g[l], conv_ln_b[l],
                                    conv_pw_w[l], None)
            yc = gated_merge(yc_c, yc_h, pc[7], pc[8], w_out[l])
            xc_next = xc + gac_m * rms_norm(yc, norm_post_mix[l])
            hc2 = modulate(rms_norm(xc_next, norm_pre_mlp[l]), shc_f, scc_f)
            xc_next = xc_next + gac_f * rms_norm(sq_relu_mlp(hc2, mlp_w1[l], mlp_w2[l]), norm_post_mlp[l])

        h = modulate(rms_norm(x, norm_pre_mix[l]), sh_m, sc_m)
        p_i, p_ff, p_fb, p_q, p_g, p_cv, p_cg, p_gc, p_gh = split_proj(h @ w_in[l])
        o, _, _ = hgrn_bidir(q_prep(p_q), to_heads(p_i.astype(f32)), hgrn_forget(p_ff, lb_f),
                             hgrn_forget(p_fb, lb_b), s_f, s_b)
        y_h = hgrn_readout(o, p_g, hgrn_norm_g[l], hgrn_out_w[l])
        y_c = conformer_branch(p_cv, p_cg, conv_dw_w[l], conv_dw_b[l], conv_ln_g[l], conv_ln_b[l],
                               conv_pw_w[l], rows)
        y = gated_merge(y_c, y_h, p_gc, p_gh, w_out[l])
        x = x + ga_m * rms_norm(y, norm_post_mix[l])
        h2 = modulate(rms_norm(x, norm_pre_mlp[l]), sh_f, sc_f)
        x = x + ga_f * rms_norm(sq_relu_mlp(h2, mlp_w1[l], mlp_w2[l]), norm_post_mlp[l])

        if not last:
            xc = xc_next
    return x
```

```python
import functools

import jax
import jax.numpy as jnp
from jax import lax
from jax.experimental import pallas as pl
from jax.experimental.pallas import tpu as pltpu

F32 = jnp.float32
BF16 = jnp.bfloat16

GRID_W = 64
CONV_WIDTH = 31
CONV_PAD = (CONV_WIDTH - 1) // 2
HEAD_DIM = 128
N_MOD = 6
EPS = 1e-6

SCAN_CHUNK = 64
SCAN_BLOCK = 512
EXP_CLAMP = 80.0

VMEM_LIMIT = 56 * 1024 * 1024

_NT = (((1,), (1,)), ((), ()))
_TN = (((0,), (0,)), ((), ()))


def _params(sem):
    return pltpu.CompilerParams(dimension_semantics=sem, vmem_limit_bytes=VMEM_LIMIT)


def _silu(x):
    return x * jax.nn.sigmoid(x)


def _mod_kernel(c_ref, w_ref, b_ref, o_ref):
    s = _silu(c_ref[...]).astype(BF16)
    o_ref[...] = jnp.dot(s, w_ref[...].astype(BF16), preferred_element_type=F32) + b_ref[...]


def _mod_call(cc, w_mod, b_mod, tn=1024):
    rows, d = cc.shape
    n = w_mod.shape[1]
    return pl.pallas_call(
        _mod_kernel,
        grid=(n // tn,),
        in_specs=[pl.BlockSpec((rows, d), lambda j: (0, 0)),
                  pl.BlockSpec((d, tn), lambda j: (0, j)),
                  pl.BlockSpec((1, tn), lambda j: (0, j))],
        out_specs=pl.BlockSpec((rows, tn), lambda j: (0, j)),
        out_shape=jax.ShapeDtypeStruct((rows, n), F32),
        compiler_params=_params(("parallel",)),
        name="mod",
    )(cc, w_mod, b_mod)


def _prenorm_kernel(x_ref, g_ref, mod_ref, o_ref, *, shift_idx):
    x = x_ref[0]
    ms = jnp.mean(x * x, axis=-1, keepdims=True)
    y = x * lax.rsqrt(ms + EPS) * g_ref[...]
    sh = mod_ref[0, shift_idx:shift_idx + 1, :]
    sc = mod_ref[0, shift_idx + 1:shift_idx + 2, :]
    o_ref[0] = (y * (1.0 + sc) + sh).astype(BF16)


def _prenorm_call(x, g, mod3, row_of_batch, shift_idx, tl):
    b, l, d = x.shape
    return pl.pallas_call(
        functools.partial(_prenorm_kernel, shift_idx=shift_idx),
        grid=(b, l // tl),
        in_specs=[pl.BlockSpec((1, tl, d), lambda i, j: (i, j, 0)),
                  pl.BlockSpec((1, d), lambda i, j: (0, 0)),
                  pl.BlockSpec((1, N_MOD, d), lambda i, j: (row_of_batch(i), 0, 0))],
        out_specs=pl.BlockSpec((1, tl, d), lambda i, j: (i, j, 0)),
        out_shape=jax.ShapeDtypeStruct((b, l, d), BF16),
        compiler_params=_params(("parallel", "parallel")),
        name="prenorm",
    )(x, g, mod3)


def _proj_kernel(*refs, n_w, n_extra, epilogue):
    h_ref = refs[0]
    w_refs = refs[1:1 + n_w]
    extra = refs[1 + n_w:1 + n_w + n_extra]
    n_out = len(refs) - (1 + 2 * n_w + n_extra)
    out_refs = refs[1 + n_w + n_extra:1 + n_w + n_extra + n_out]
    wbf_refs = refs[1 + n_w + n_extra + n_out:]

    @pl.when(pl.program_id(1) == 0)
    def _():
        for w_ref, wbf_ref in zip(w_refs, wbf_refs):
            wbf_ref[...] = w_ref[...].astype(BF16)

    h = h_ref[...]
    accs = [jnp.dot(h, wbf_ref[...], preferred_element_type=F32) for wbf_ref in wbf_refs]
    outs = epilogue(accs, extra)
    for o_ref, o in zip(out_refs, outs):
        o_ref[...] = o.astype(o_ref.dtype)


def _ep_identity(accs, extra):
    return (accs[0],)


def _ep_forget(accs, extra):
    logits = extra[0][...]
    e = jnp.exp(logits - jnp.max(logits, axis=0, keepdims=True))
    lb = e[0:1] / jnp.sum(e, axis=0, keepdims=True)
    f = lb + (1.0 - lb) * jax.nn.sigmoid(accs[0])
    return jnp.log(f), 1.0 - f


def _ep_q(accs, extra):
    return (_silu(accs[0]) * (HEAD_DIM ** -0.5),)


def _ep_silu(accs, extra):
    return (_silu(accs[0]),)


def _ep_glu(accs, extra):
    return (accs[0] * jax.nn.sigmoid(accs[1]),)


def _ep_sigmoid(accs, extra):
    return (jax.nn.sigmoid(accs[0]),)


def _proj_call(h, w, col_offsets, width, epilogue, out_dtypes, extra=(), tm=1024, tn=512, name="proj"):
    m, k = h.shape
    tm = min(tm, m)
    n_w = len(col_offsets)
    in_specs = [pl.BlockSpec((tm, k), lambda j, i: (i, 0))]
    for off in col_offsets:
        in_specs.append(pl.BlockSpec((k, tn), lambda j, i, off=off: (0, off // tn + j)))
    for e in extra:
        in_specs.append(pl.BlockSpec((e.shape[0], tn), lambda j, i: (0, j)))
    return pl.pallas_call(
        functools.partial(_proj_kernel, n_w=n_w, n_extra=len(extra), epilogue=epilogue),
        grid=(width // tn, m // tm),
        in_specs=in_specs,
        out_specs=[pl.BlockSpec((tm, tn), lambda j, i: (i, j)) for _ in out_dtypes],
        out_shape=[jax.ShapeDtypeStruct((m, width), dt) for dt in out_dtypes],
        scratch_shapes=[pltpu.VMEM((k, tn), BF16) for _ in range(n_w)],
        compiler_params=_params(("parallel", "arbitrary")),
        name=name,
    )(h, *([w] * n_w), *extra)


def _scan_cumsum(a, reverse):
    n = a.shape[0]
    row = lax.broadcasted_iota(jnp.int32, a.shape, 0)
    sh = 1
    while sh < n:
        if reverse:
            a = a + jnp.where(row < n - sh, pltpu.roll(a, n - sh, 0), 0.0)
        else:
            a = a + jnp.where(row >= sh, pltpu.roll(a, sh, 0), 0.0)
        sh *= 2
    return a


def _scan_chunk_head(q, k, v, lf, st, mask0, mask1, reverse, need_out):
    c = lf.shape[0]
    half, quarter = c // 2, c // 4
    a = _scan_cumsum(lf, reverse)
    row = lax.broadcasted_iota(jnp.int32, a.shape, 0)
    if reverse:
        m0 = a[half:half + 1]
        m1 = jnp.where(row < half, a[quarter:quarter + 1], a[half + quarter:half + quarter + 1])
        a_tot = a[0:1]
    else:
        m0 = a[half - 1:half]
        m1 = jnp.where(row < half, a[quarter - 1:quarter], a[half + quarter - 1:half + quarter])
        a_tot = a[c - 1:c]

    kf = k.astype(F32)
    o = None
    if need_out:
        qf = q.astype(F32)
        q0 = (qf * jnp.exp(jnp.minimum(a - m0, 0.0))).astype(BF16)
        k0 = (kf * jnp.exp(jnp.minimum(m0 - a, 0.0))).astype(BF16)
        q1 = (qf * jnp.exp(jnp.minimum(a - m1, EXP_CLAMP))).astype(BF16)
        k1 = (kf * jnp.exp(jnp.minimum(m1 - a, EXP_CLAMP))).astype(BF16)
        s0 = lax.dot_general(q0, k0, _NT, preferred_element_type=F32)
        s1 = lax.dot_general(q1, k1, _NT, preferred_element_type=F32)
        scores = jnp.where(mask0, s0, 0.0) + jnp.where(mask1, s1, 0.0)
        intra = jnp.dot(scores.astype(BF16), v, preferred_element_type=F32)
        qi = (qf * jnp.exp(a)).astype(BF16)
        inter = lax.dot_general(qi, st.astype(BF16), _NT, preferred_element_type=F32)
        o = inter + intra
    kh = (kf * jnp.exp(a_tot - a)).astype(BF16)
    upd = lax.dot_general(v, kh, _TN, preferred_element_type=F32)
    st_new = st * jnp.exp(a_tot) + upd
    return o, st_new


def _scan_kernel(*refs, reverse, mode, n_blocks):
    k_ref, v_ref, lf_ref, s0_ref = refs[:4]
    if mode == "state":
        (sfin_ref, st_ref) = refs[4:]
    elif mode == "out":
        (q_ref, o_ref, st_ref) = refs[4:]
    else:
        (q_ref, ob_ref, g_ref, ng_ref, o_ref, st_ref) = refs[4:]
    need_out = mode != "state"
    c = SCAN_CHUNK
    rows = k_ref.shape[1]
    n_chunks = rows // c
    n_heads = k_ref.shape[2] // HEAD_DIM
    step = pl.program_id(1)

    @pl.when(step == 0)
    def _():
        st_ref[...] = s0_ref[0]

    t = lax.broadcasted_iota(jnp.int32, (c, c), 0)
    s = lax.broadcasted_iota(jnp.int32, (c, c), 1)
    half = c // 2
    if reverse:
        mask0 = (t < half) & (s >= half)
        mask1 = ((t < half) == (s < half)) & (s >= t)
    else:
        mask0 = (t >= half) & (s < half)
        mask1 = ((t < half) == (s < half)) & (s <= t)

    def chunk_body(ci, carry):
        cidx = (n_chunks - 1 - ci) if reverse else ci
        r0 = pl.multiple_of(cidx * c, c)
        for h in range(n_heads):
            lanes = slice(h * HEAD_DIM, (h + 1) * HEAD_DIM)
            q = q_ref[0, pl.ds(r0, c), lanes] if need_out else None
            o, st_new = _scan_chunk_head(
                q, k_ref[0, pl.ds(r0, c), lanes], v_ref[0, pl.ds(r0, c), lanes],
                lf_ref[0, pl.ds(r0, c), lanes], st_ref[h], mask0, mask1, reverse, need_out)
            st_ref[h] = st_new
            if mode == "out":
                o_ref[0, pl.ds(r0, c), lanes] = o
            elif mode == "readout":
                o = o + ob_ref[0, pl.ds(r0, c), lanes]
                ms = jnp.mean(o * o, axis=-1, keepdims=True)
                y = o * lax.rsqrt(ms + EPS) * ng_ref[:, lanes]
                o_ref[0, pl.ds(r0, c), lanes] = (y * g_ref[0, pl.ds(r0, c), lanes].astype(F32)).astype(BF16)
        return carry

    lax.fori_loop(0, n_chunks, chunk_body, 0)

    if mode == "state":
        @pl.when(step == n_blocks - 1)
        def _():
            sfin_ref[0] = st_ref[...]


def _scan_call(k, v, lf, s0, *, reverse, mode, q=None, ob=None, g=None, ng=None):
    b, l, dh = k.shape
    n_heads = dh // HEAD_DIM
    tb = min(SCAN_BLOCK, l)
    n_blocks = l // tb
    if reverse:
        seq = lambda i, j: (i, n_blocks - 1 - j, 0)
    else:
        seq = lambda i, j: (i, j, 0)
    seq_spec = pl.BlockSpec((1, tb, dh), seq)
    st_spec = pl.BlockSpec((1, n_heads, HEAD_DIM, HEAD_DIM), lambda i, j: (i, 0, 0, 0))
    in_specs = [seq_spec, seq_spec, seq_spec, st_spec]
    args = [k, v, lf, s0]
    if mode == "state":
        out_specs, out_shape = st_spec, jax.ShapeDtypeStruct(s0.shape, F32)
    else:
        in_specs.append(seq_spec)
        args.append(q)
        if mode == "out":
            out_shape = jax.ShapeDtypeStruct((b, l, dh), F32)
        else:
            in_specs += [seq_spec, seq_spec, pl.BlockSpec((1, dh), lambda i, j: (0, 0))]
            args += [ob, g, ng]
            out_shape = jax.ShapeDtypeStruct((b, l, dh), BF16)
        out_specs = seq_spec
    return pl.pallas_call(
        functools.partial(_scan_kernel, reverse=reverse, mode=mode, n_blocks=n_blocks),
        grid=(b, n_blocks),
        in_specs=in_specs,
        out_specs=out_specs,
        out_shape=out_shape,
        scratch_shapes=[pltpu.VMEM((n_heads, HEAD_DIM, HEAD_DIM), F32)],
        compiler_params=_params(("parallel", "arbitrary")),
        name="scan_" + mode + ("_bwd" if reverse else "_fwd"),
    )(*args)


CONV_TILE_ROWS = 16
_HPAD = 16
_HROW = GRID_W + 2 * _HPAD


def _conv_kernel(cur_ref, prev_ref, next_ref, w_ref, b_ref, lg_ref, lb_ref, o_ref,
                 hpad_ref, vpad_ref, y_ref, *, n_tiles):
    tr = CONV_TILE_ROWS
    half = w_ref.shape[1] // 2
    n_lane_tiles = half // 128
    i = pl.program_id(1)

    hpad_ref[...] = jnp.zeros(hpad_ref.shape, F32)
    for r in range(tr):
        hpad_ref[r * _HROW + _HPAD:r * _HROW + _HPAD + GRID_W, :] = cur_ref[0, r * GRID_W:(r + 1) * GRID_W, :half]
    halo = CONV_PAD * GRID_W
    top = prev_ref[0, tr * GRID_W - halo:, :]
    bot = next_ref[0, :halo, :]
    vpad_ref[:halo, :] = jnp.where(i > 0, top, 0.0)
    vpad_ref[halo:halo + tr * GRID_W, :] = cur_ref[0, :, half:]
    vpad_ref[halo + tr * GRID_W:, :] = jnp.where(i < n_tiles - 1, bot, 0.0)

    def row_body(r, carry):
        h0 = pl.multiple_of(r * _HROW, 8)
        v0 = pl.multiple_of(r * GRID_W, GRID_W)
        o0 = pl.multiple_of(r * GRID_W, GRID_W)
        for lt in range(n_lane_tiles):
            lanes = slice(lt * 128, (lt + 1) * 128)
            win = hpad_ref[pl.ds(h0, _HROW), lanes]
            acc = jnp.zeros((GRID_W, 128), F32)
            for sub in range(8):
                taps = [jj for jj in range(CONV_WIDTH) if (_HPAD - CONV_PAD + jj) % 8 == sub]
                if not taps:
                    continue
                shifted = win if sub == 0 else pltpu.roll(win, _HROW - sub, 0)
                for jj in taps:
                    base = (_HPAD - CONV_PAD + jj) - sub
                    acc = acc + shifted[base:base + GRID_W, :] * w_ref[jj:jj + 1, lanes]
            y_ref[pl.ds(o0, GRID_W), lanes] = acc + b_ref[:, lanes]
            vl = slice(half + lt * 128, half + (lt + 1) * 128)
            acc = jnp.zeros((GRID_W, 128), F32)
            for jj in range(CONV_WIDTH):
                src = pl.multiple_of(v0 + jj * GRID_W, GRID_W)
                acc = acc + vpad_ref[pl.ds(src, GRID_W), lanes] * w_ref[jj:jj + 1, vl]
            y_ref[pl.ds(o0, GRID_W), vl] = acc + b_ref[:, vl]
        y = y_ref[pl.ds(o0, GRID_W), :]
        mu = jnp.mean(y, axis=-1, keepdims=True)
        yc = y - mu
        var = jnp.mean(yc * yc, axis=-1, keepdims=True)
        z = yc * lax.rsqrt(var + EPS) * lg_ref[...] + lb_ref[...]
        o_ref[0, pl.ds(o0, GRID_W), :] = _silu(z).astype(BF16)
        return carry

    lax.fori_loop(0, tr, row_body, 0)


def _conv_call(u, w, bias, ln_g, ln_b):
    b, l, ch = u.shape
    half = ch // 2
    tile = CONV_TILE_ROWS * GRID_W
    n_tiles = l // tile
    return pl.pallas_call(
        functools.partial(_conv_kernel, n_tiles=n_tiles),
        grid=(b, n_tiles),
        in_specs=[pl.BlockSpec((1, tile, ch), lambda i, j: (i, j, 0)),
                  pl.BlockSpec((1, tile, half), lambda i, j: (i, jnp.maximum(j - 1, 0), 1)),
                  pl.BlockSpec((1, tile, half), lambda i, j: (i, jnp.minimum(j + 1, n_tiles - 1), 1)),
                  pl.BlockSpec((CONV_WIDTH, ch), lambda i, j: (0, 0)),
                  pl.BlockSpec((1, ch), lambda i, j: (0, 0)),
                  pl.BlockSpec((1, ch), lambda i, j: (0, 0)),
                  pl.BlockSpec((1, ch), lambda i, j: (0, 0))],
        out_specs=pl.BlockSpec((1, tile, ch), lambda i, j: (i, j, 0)),
        out_shape=jax.ShapeDtypeStruct((b, l, ch), BF16),
        scratch_shapes=[pltpu.VMEM((CONV_TILE_ROWS * _HROW, half), F32),
                        pltpu.VMEM((tile + 2 * CONV_PAD * GRID_W, half), F32),
                        pltpu.VMEM((tile, ch), F32)],
        compiler_params=_params(("parallel", "parallel")),
        name="conv",
    )(u, u, u, w, bias, ln_g, ln_b)


def _mix_kernel(uc_ref, og_ref, gc_ref, gh_ref, x_ref, wpw_ref, wh_ref, wo_ref,
                npost_ref, npre_ref, mod_ref, x1_ref, h2_ref):
    y_c = jnp.dot(uc_ref[0], wpw_ref[...], preferred_element_type=F32)
    y_h = jnp.dot(og_ref[0], wh_ref[...], preferred_element_type=F32)
    z = gc_ref[0].astype(F32) * y_c + gh_ref[0].astype(F32) * y_h
    y = jnp.dot(z.astype(BF16), wo_ref[...], preferred_element_type=F32)
    ms = jnp.mean(y * y, axis=-1, keepdims=True)
    yn = y * lax.rsqrt(ms + EPS) * npost_ref[...]
    x1 = x_ref[0] + mod_ref[0, 2:3, :] * yn
    x1_ref[0] = x1
    ms1 = jnp.mean(x1 * x1, axis=-1, keepdims=True)
    hn = x1 * lax.rsqrt(ms1 + EPS) * npre_ref[...]
    h2_ref[0] = (hn * (1.0 + mod_ref[0, 4:5, :]) + mod_ref[0, 3:4, :]).astype(BF16)


def _mix_call(uc, og, gc, gh, x, wpw, wh, wo, npost, npre, mod3, tm=256):
    b, l, d = x.shape
    dc = uc.shape[2]
    row = lambda width: pl.BlockSpec((1, tm, width), lambda i, j: (i, j, 0))
    const = lambda shape: pl.BlockSpec(shape, lambda i, j: (0,) * len(shape), pipeline_mode=pl.Buffered(1))
    return pl.pallas_call(
        _mix_kernel,
        grid=(b, l // tm),
        in_specs=[row(dc), row(dc), row(d), row(d), row(d),
                  const(wpw.shape), const(wh.shape), const(wo.shape),
                  const((1, d)), const((1, d)),
                  pl.BlockSpec((1, N_MOD, d), lambda i, j: (i, 0, 0))],
        out_specs=[row(d), row(d)],
        out_shape=[jax.ShapeDtypeStruct((b, l, d), F32), jax.ShapeDtypeStruct((b, l, d), BF16)],
        compiler_params=_params(("parallel", "parallel")),
        name="mix",
    )(uc, og, gc, gh, x, wpw, wh, wo, npost, npre, mod3)


def _mlp_kernel(h_ref, w1_ref, w2_ref, x1_ref, npost_ref, mod_ref, o_ref, acc_ref, *, n_ff):
    j = pl.program_id(2)
    a = jnp.dot(h_ref[0], w1_ref[...], preferred_element_type=F32)
    a = jnp.square(jnp.maximum(a, 0.0)).astype(BF16)
    part = jnp.dot(a, w2_ref[...], preferred_element_type=F32)

    @pl.when(j == 0)
    def _():
        acc_ref[...] = part

    @pl.when(j > 0)
    def _():
        acc_ref[...] += part

    @pl.when(j == n_ff - 1)
    def _():
        y = acc_ref[...]
        ms = jnp.mean(y * y, axis=-1, keepdims=True)
        yn = y * lax.rsqrt(ms + EPS) * npost_ref[...]
        o_ref[0] = x1_ref[0] + mod_ref[0, 5:6, :] * yn


def _mlp_call(h2, w1, w2, x1, npost, mod3, tm=512, tf=512):
    b, l, d = x1.shape
    dff = w1.shape[1]
    n_ff = dff // tf
    return pl.pallas_call(
        functools.partial(_mlp_kernel, n_ff=n_ff),
        grid=(b, l // tm, n_ff),
        in_specs=[pl.BlockSpec((1, tm, d), lambda i, m, j: (i, m, 0)),
                  pl.BlockSpec((d, tf), lambda i, m, j: (0, j)),
                  pl.BlockSpec((tf, d), lambda i, m, j: (j, 0)),
                  pl.BlockSpec((1, tm, d), lambda i, m, j: (i, m, 0)),
                  pl.BlockSpec((1, d), lambda i, m, j: (0, 0)),
                  pl.BlockSpec((1, N_MOD, d), lambda i, m, j: (i, 0, 0))],
        out_specs=pl.BlockSpec((1, tm, d), lambda i, m, j: (i, m, 0)),
        out_shape=jax.ShapeDtypeStruct((b, l, d), F32),
        scratch_shapes=[pltpu.VMEM((tm, d), F32)],
        compiler_params=_params(("parallel", "parallel", "arbitrary")),
        name="mlp",
    )(h2, w1, w2, x1, npost, mod3)


def kernel(x, c, ctx, c_ctx, w_mod, b_mod, norm_pre_mix, norm_post_mix, norm_pre_mlp, norm_post_mlp, w_in,
           conv_dw_w, conv_dw_b, conv_ln_g, conv_ln_b, conv_pw_w, hgrn_lb_logits, hgrn_norm_g, hgrn_out_w,
           w_out, mlp_w1, mlp_w2):
    assert w_mod.shape[0] == 1, "single-layer block"
    b, l, d = x.shape
    lc = ctx.shape[1]
    dh = hgrn_norm_g.shape[1]
    dc = conv_dw_w.shape[2]
    n_heads = dh // HEAD_DIM
    assert l % (CONV_TILE_ROWS * GRID_W) == 0 and CONV_PAD <= CONV_TILE_ROWS and CONV_PAD <= _HPAD
    assert l % SCAN_BLOCK == 0 and lc % SCAN_CHUNK == 0

    mod_rows = 16
    cc = jnp.concatenate([c, c_ctx[None, :], jnp.zeros((mod_rows - b - 1, d), F32)], axis=0)
    mod3 = _mod_call(cc, w_mod[0], b_mod).reshape(mod_rows, N_MOD, d)

    w_in0 = w_in[0]
    lb_f, lb_b = hgrn_lb_logits[0], hgrn_lb_logits[1]
    off = lambda n: n * dh

    hc = _prenorm_call(ctx, norm_pre_mix, mod3, lambda i: b, 0, tl=lc).reshape(b * lc, d)
    seq_c = lambda t: t.reshape(b, lc, dh)
    (vc,) = _proj_call(hc, w_in0, [off(0)], dh, _ep_identity, [BF16], name="proj_ctx_v")
    lfc_f, kc_f = _proj_call(hc, w_in0, [off(1)], dh, _ep_forget, [F32, BF16], extra=(lb_f,), name="proj_ctx_ff")
    lfc_b, kc_b = _proj_call(hc, w_in0, [off(2)], dh, _ep_forget, [F32, BF16], extra=(lb_b,), name="proj_ctx_fb")
    zeros = jnp.zeros((b, n_heads, HEAD_DIM, HEAD_DIM), F32)
    s_f = _scan_call(seq_c(kc_f), seq_c(vc), seq_c(lfc_f), zeros, reverse=False, mode="state")
    s_b = _scan_call(seq_c(kc_b), seq_c(vc), seq_c(lfc_b), zeros, reverse=True, mode="state")

    h = _prenorm_call(x, norm_pre_mix, mod3, lambda i: i, 0, tl=512).reshape(b * l, d)
    seq = lambda t: t.reshape(b, l, t.shape[-1])
    (v,) = _proj_call(h, w_in0, [off(0)], dh, _ep_identity, [BF16], name="proj_v")
    lf_f, k_f = _proj_call(h, w_in0, [off(1)], dh, _ep_forget, [F32, BF16], extra=(lb_f,), name="proj_ff")
    lf_b, k_b = _proj_call(h, w_in0, [off(2)], dh, _ep_forget, [F32, BF16], extra=(lb_b,), name="proj_fb")
    (q,) = _proj_call(h, w_in0, [off(3)], dh, _ep_q, [BF16], name="proj_q")
    (g,) = _proj_call(h, w_in0, [off(4)], dh, _ep_silu, [BF16], name="proj_g")
    (u,) = _proj_call(h, w_in0, [off(5), off(5) + dc], dc, _ep_glu, [F32], name="proj_glu")
    (gc,) = _proj_call(h, w_in0, [off(5) + 2 * dc], d, _ep_sigmoid, [BF16], name="proj_gc")
    (gh,) = _proj_call(h, w_in0, [off(5) + 2 * dc + d], d, _ep_sigmoid, [BF16], name="proj_gh")

    o_b = _scan_call(seq(k_b), seq(v), seq(lf_b), s_b, reverse=True, mode="out", q=seq(q))
    og = _scan_call(seq(k_f), seq(v), seq(lf_f), s_f, reverse=False, mode="readout", q=seq(q),
                    ob=o_b, g=seq(g), ng=hgrn_norm_g)

    uc = _conv_call(seq(u), conv_dw_w[0], conv_dw_b, conv_ln_g, conv_ln_b)

    x1, h2 = _mix_call(uc, og, seq(gc), seq(gh), x, conv_pw_w[0].astype(BF16), hgrn_out_w[0].astype(BF16),
                       w_out[0].astype(BF16), norm_post_mix, norm_pre_mlp, mod3)
    return _mlp_call(h2, mlp_w1[0].astype(BF16), mlp_w2[0].astype(BF16), x1, norm_post_mlp, mod3)
```

```python
import functools

import jax
import jax.numpy as jnp
from jax import lax
from jax.experimental import pallas as pl
from jax.experimental.pallas import tpu as pltpu

F32 = jnp.float32
BF16 = jnp.bfloat16

GRID_W = 64
CONV_WIDTH = 31
CONV_PAD = (CONV_WIDTH - 1) // 2
HEAD_DIM = 128
N_MOD = 6
EPS = 1e-6

SCAN_CHUNK = 64
SCAN_BLOCK = 512
EXP_CLAMP = 80.0

VMEM_LIMIT = 56 * 1024 * 1024

_NT = (((1,), (1,)), ((), ()))
_TN = (((0,), (0,)), ((), ()))


def _params(sem):
    return pltpu.CompilerParams(dimension_semantics=sem, vmem_limit_bytes=VMEM_LIMIT)


def _sigmoid(x):
    return 0.5 * jnp.tanh(0.5 * x) + 0.5


def _silu(x):
    return x * _sigmoid(x)


def _mod_kernel(c_ref, w_ref, b_ref, o_ref):
    s = _silu(c_ref[...]).astype(BF16)
    o_ref[...] = jnp.dot(s, w_ref[...].astype(BF16), preferred_element_type=F32) + b_ref[...]


def _mod_call(cc, w_mod, b_mod, tn=1024):
    rows, d = cc.shape
    n = w_mod.shape[1]
    return pl.pallas_call(
        _mod_kernel,
        grid=(n // tn,),
        in_specs=[pl.BlockSpec((rows, d), lambda j: (0, 0)),
                  pl.BlockSpec((d, tn), lambda j: (0, j)),
                  pl.BlockSpec((1, tn), lambda j: (0, j))],
        out_specs=pl.BlockSpec((rows, tn), lambda j: (0, j)),
        out_shape=jax.ShapeDtypeStruct((rows, n), F32),
        compiler_params=_params(("parallel",)),
        name="mod",
    )(cc, w_mod, b_mod)


def _prenorm_kernel(x_ref, g_ref, mod_ref, o_ref, *, shift_idx):
    x = x_ref[0]
    ms = jnp.mean(x * x, axis=-1, keepdims=True)
    y = x * lax.rsqrt(ms + EPS) * g_ref[...]
    sh = mod_ref[0, shift_idx:shift_idx + 1, :]
    sc = mod_ref[0, shift_idx + 1:shift_idx + 2, :]
    o_ref[0] = (y * (1.0 + sc) + sh).astype(BF16)


def _prenorm_call(x, g, mod3, row_of_batch, shift_idx, tl):
    b, l, d = x.shape
    return pl.pallas_call(
        functools.partial(_prenorm_kernel, shift_idx=shift_idx),
        grid=(b, l // tl),
        in_specs=[pl.BlockSpec((1, tl, d), lambda i, j: (i, j, 0)),
                  pl.BlockSpec((1, d), lambda i, j: (0, 0)),
                  pl.BlockSpec((1, N_MOD, d), lambda i, j: (row_of_batch(i), 0, 0))],
        out_specs=pl.BlockSpec((1, tl, d), lambda i, j: (i, j, 0)),
        out_shape=jax.ShapeDtypeStruct((b, l, d), BF16),
        compiler_params=_params(("parallel", "parallel")),
        name="prenorm",
    )(x, g, mod3)


def _proj_kernel(*refs, n_w, n_extra, epilogue):
    h_ref = refs[0]
    w_refs = refs[1:1 + n_w]
    extra = refs[1 + n_w:1 + n_w + n_extra]
    n_out = len(refs) - (1 + 2 * n_w + n_extra)
    out_refs = refs[1 + n_w + n_extra:1 + n_w + n_extra + n_out]
    wbf_refs = refs[1 + n_w + n_extra + n_out:]

    @pl.when(pl.program_id(1) == 0)
    def _():
        for w_ref, wbf_ref in zip(w_refs, wbf_refs):
            wbf_ref[...] = w_ref[...].astype(BF16)

    h = h_ref[...]
    accs = [jnp.dot(h, wbf_ref[...], preferred_element_type=F32) for wbf_ref in wbf_refs]
    outs = epilogue(accs, extra)
    for o_ref, o in zip(out_refs, outs):
        o_ref[...] = o.astype(o_ref.dtype)


def _ep_identity(accs, extra):
    return (accs[0],)


def _ep_forget(accs, extra):
    logits = extra[0][...]
    e = jnp.exp(logits - jnp.max(logits, axis=0, keepdims=True))
    lb = e[0:1] / jnp.sum(e, axis=0, keepdims=True)
    f = lb + (1.0 - lb) * _sigmoid(accs[0])
    return jnp.log(f), 1.0 - f


def _ep_q(accs, extra):
    return (_silu(accs[0]) * (HEAD_DIM ** -0.5),)


def _ep_silu(accs, extra):
    return (_silu(accs[0]),)


def _ep_glu(accs, extra):
    return (accs[0] * _sigmoid(accs[1]),)


def _ep_sigmoid(accs, extra):
    return (_sigmoid(accs[0]),)


def _proj_call(h, w, col_offsets, width, epilogue, out_dtypes, extra=(), tm=1024, tn=1024, name="proj"):
    m, k = h.shape
    tm = min(tm, m)
    n_w = len(col_offsets)
    in_specs = [pl.BlockSpec((tm, k), lambda j, i: (i, 0))]
    for off in col_offsets:
        in_specs.append(pl.BlockSpec((k, tn), lambda j, i, off=off: (0, off // tn + j)))
    for e in extra:
        in_specs.append(pl.BlockSpec((e.shape[0], tn), lambda j, i: (0, j)))
    return pl.pallas_call(
        functools.partial(_proj_kernel, n_w=n_w, n_extra=len(extra), epilogue=epilogue),
        grid=(width // tn, m // tm),
        in_specs=in_specs,
        out_specs=[pl.BlockSpec((tm, tn), lambda j, i: (i, j)) for _ in out_dtypes],
        out_shape=[jax.ShapeDtypeStruct((m, width), dt) for dt in out_dtypes],
        scratch_shapes=[pltpu.VMEM((k, tn), BF16) for _ in range(n_w)],
        compiler_params=_params(("parallel", "arbitrary")),
        name=name,
    )(h, *([w] * n_w), *extra)


def _scan_cumsum(a, reverse):
    n = a.shape[0]
    row = lax.broadcasted_iota(jnp.int32, a.shape, 0)
    sh = 1
    while sh < n:
        if reverse:
            a = a + jnp.where(row < n - sh, pltpu.roll(a, n - sh, 0), 0.0)
        else:
            a = a + jnp.where(row >= sh, pltpu.roll(a, sh, 0), 0.0)
        sh *= 2
    return a


def _scan_chunk_head(q, k, v, lf, st, mask0, mask1, reverse, need_out):
    c = lf.shape[0]
    half, quarter = c // 2, c // 4
    a = _scan_cumsum(lf, reverse)
    row = lax.broadcasted_iota(jnp.int32, a.shape, 0)
    if reverse:
        m0 = a[half:half + 1]
        m1 = jnp.where(row < half, a[quarter:quarter + 1], a[half + quarter:half + quarter + 1])
        a_tot = a[0:1]
    else:
        m0 = a[half - 1:half]
        m1 = jnp.where(row < half, a[quarter - 1:quarter], a[half + quarter - 1:half + quarter])
        a_tot = a[c - 1:c]

    kf = k.astype(F32)
    o = None
    if need_out:
        qf = q.astype(F32)
        q0 = (qf * jnp.exp(jnp.minimum(a - m0, 0.0))).astype(BF16)
        k0 = (kf * jnp.exp(jnp.minimum(m0 - a, 0.0))).astype(BF16)
        q1 = (qf * jnp.exp(jnp.minimum(a - m1, EXP_CLAMP))).astype(BF16)
        k1 = (kf * jnp.exp(jnp.minimum(m1 - a, EXP_CLAMP))).astype(BF16)
        s0 = lax.dot_general(q0, k0, _NT, preferred_element_type=F32)
        s1 = lax.dot_general(q1, k1, _NT, preferred_element_type=F32)
        scores = jnp.where(mask0, s0, 0.0) + jnp.where(mask1, s1, 0.0)
        intra = jnp.dot(scores.astype(BF16), v, preferred_element_type=F32)
        qi = (qf * jnp.exp(a)).astype(BF16)
        inter = lax.dot_general(qi, st.astype(BF16), _NT, preferred_element_type=F32)
        o = inter + intra
    kh = (kf * jnp.exp(a_tot - a)).astype(BF16)
    upd = lax.dot_general(v, kh, _TN, preferred_element_type=F32)
    st_new = st * jnp.exp(a_tot) + upd
    return o, st_new


def _scan_kernel(*refs, reverse, mode, n_blocks):
    k_ref, v_ref, lf_ref, s0_ref = refs[:4]
    if mode == "state":
        (sfin_ref, st_ref) = refs[4:]
    elif mode == "out":
        (q_ref, o_ref, st_ref) = refs[4:]
    else:
        (q_ref, ob_ref, g_ref, ng_ref, o_ref, st_ref) = refs[4:]
    need_out = mode != "state"
    c = SCAN_CHUNK
    rows = k_ref.shape[1]
    n_chunks = rows // c
    n_heads = k_ref.shape[2] // HEAD_DIM
    step = pl.program_id(1)

    @pl.when(step == 0)
    def _():
        st_ref[...] = s0_ref[0]

    t = lax.broadcasted_iota(jnp.int32, (c, c), 0)
    s = lax.broadcasted_iota(jnp.int32, (c, c), 1)
    half = c // 2
    if reverse:
        mask0 = (t < half) & (s >= half)
        mask1 = ((t < half) == (s < half)) & (s >= t)
    else:
        mask0 = (t >= half) & (s < half)
        mask1 = ((t < half) == (s < half)) & (s <= t)

    def chunk_body(ci, carry):
        cidx = (n_chunks - 1 - ci) if reverse else ci
        r0 = pl.multiple_of(cidx * c, c)
        for h in range(n_heads):
            lanes = slice(h * HEAD_DIM, (h + 1) * HEAD_DIM)
            q = q_ref[0, pl.ds(r0, c), lanes] if need_out else None
            o, st_new = _scan_chunk_head(
                q, k_ref[0, pl.ds(r0, c), lanes], v_ref[0, pl.ds(r0, c), lanes],
                lf_ref[0, pl.ds(r0, c), lanes], st_ref[h], mask0, mask1, reverse, need_out)
            st_ref[h] = st_new
            if mode == "out":
                o_ref[0, pl.ds(r0, c), lanes] = o
            elif mode == "readout":
                o = o + ob_ref[0, pl.ds(r0, c), lanes]
                ms = jnp.mean(o * o, axis=-1, keepdims=True)
                y = o * lax.rsqrt(ms + EPS) * ng_ref[:, lanes]
                o_ref[0, pl.ds(r0, c), lanes] = (y * g_ref[0, pl.ds(r0, c), lanes].astype(F32)).astype(BF16)
        return carry

    lax.fori_loop(0, n_chunks, chunk_body, 0)

    if mode == "state":
        @pl.when(step == n_blocks - 1)
        def _():
            sfin_ref[0] = st_ref[...]


def _scan_call(k, v, lf, s0, *, reverse, mode, q=None, ob=None, g=None, ng=None):
    b, l, dh = k.shape
    n_heads = dh // HEAD_DIM
    tb = min(SCAN_BLOCK, l)
    n_blocks = l // tb
    if reverse:
        seq = lambda i, j: (i, n_blocks - 1 - j, 0)
    else:
        seq = lambda i, j: (i, j, 0)
    seq_spec = pl.BlockSpec((1, tb, dh), seq)
    st_spec = pl.BlockSpec((1, n_heads, HEAD_DIM, HEAD_DIM), lambda i, j: (i, 0, 0, 0))
    in_specs = [seq_spec, seq_spec, seq_spec, st_spec]
    args = [k, v, lf, s0]
    if mode == "state":
        out_specs, out_shape = st_spec, jax.ShapeDtypeStruct(s0.shape, F32)
    else:
        in_specs.append(seq_spec)
        args.append(q)
        if mode == "out":
            out_shape = jax.ShapeDtypeStruct((b, l, dh), F32)
        else:
            in_specs += [seq_spec, seq_spec, pl.BlockSpec((1, dh), lambda i, j: (0, 0))]
            args += [ob, g, ng]
            out_shape = jax.ShapeDtypeStruct((b, l, dh), BF16)
        out_specs = seq_spec
    return pl.pallas_call(
        functools.partial(_scan_kernel, reverse=reverse, mode=mode, n_blocks=n_blocks),
        grid=(b, n_blocks),
        in_specs=in_specs,
        out_specs=out_specs,
        out_shape=out_shape,
        scratch_shapes=[pltpu.VMEM((n_heads, HEAD_DIM, HEAD_DIM), F32)],
        compiler_params=_params(("parallel", "arbitrary")),
        name="scan_" + mode + ("_bwd" if reverse else "_fwd"),
    )(*args)


CONV_TILE_ROWS = 16
_HPAD = 16
_HROW = GRID_W + 2 * _HPAD


def _conv_kernel(cur_ref, prev_ref, next_ref, w_ref, b_ref, lg_ref, lb_ref, o_ref,
                 hpad_ref, vpad_ref, y_ref, *, n_tiles):
    tr = CONV_TILE_ROWS
    half = w_ref.shape[1] // 2
    n_lane_tiles = half // 128
    i = pl.program_id(1)

    hpad_ref[...] = jnp.zeros(hpad_ref.shape, F32)
    for r in range(tr):
        hpad_ref[r * _HROW + _HPAD:r * _HROW + _HPAD + GRID_W, :] = cur_ref[0, r * GRID_W:(r + 1) * GRID_W, :half]
    halo = CONV_PAD * GRID_W
    top = prev_ref[0, tr * GRID_W - halo:, :]
    bot = next_ref[0, :halo, :]
    vpad_ref[:halo, :] = jnp.where(i > 0, top, 0.0)
    vpad_ref[halo:halo + tr * GRID_W, :] = cur_ref[0, :, half:]
    vpad_ref[halo + tr * GRID_W:, :] = jnp.where(i < n_tiles - 1, bot, 0.0)

    def row_body(r, carry):
        h0 = pl.multiple_of(r * _HROW, 8)
        v0 = pl.multiple_of(r * GRID_W, GRID_W)
        o0 = pl.multiple_of(r * GRID_W, GRID_W)
        for lt in range(n_lane_tiles):
            lanes = slice(lt * 128, (lt + 1) * 128)
            win = hpad_ref[pl.ds(h0, _HROW), lanes]
            acc = jnp.zeros((GRID_W, 128), F32)
            for sub in range(8):
                taps = [jj for jj in range(CONV_WIDTH) if (_HPAD - CONV_PAD + jj) % 8 == sub]
                if not taps:
                    continue
                shifted = win if sub == 0 else pltpu.roll(win, _HROW - sub, 0)
                for jj in taps:
                    base = (_HPAD - CONV_PAD + jj) - sub
                    acc = acc + shifted[base:base + GRID_W, :] * w_ref[jj:jj + 1, lanes]
            y_ref[pl.ds(o0, GRID_W), lanes] = acc + b_ref[:, lanes]
            vl = slice(half + lt * 128, half + (lt + 1) * 128)
            acc = jnp.zeros((GRID_W, 128), F32)
            for jj in range(CONV_WIDTH):
                src = pl.multiple_of(v0 + jj * GRID_W, GRID_W)
                acc = acc + vpad_ref[pl.ds(src, GRID_W), lanes] * w_ref[jj:jj + 1, vl]
            y_ref[pl.ds(o0, GRID_W), vl] = acc + b_ref[:, vl]
        y = y_ref[pl.ds(o0, GRID_W), :]
        mu = jnp.mean(y, axis=-1, keepdims=True)
        yc = y - mu
        var = jnp.mean(yc * yc, axis=-1, keepdims=True)
        z = yc * lax.rsqrt(var + EPS) * lg_ref[...] + lb_ref[...]
        o_ref[0, pl.ds(o0, GRID_W), :] = _silu(z).astype(BF16)
        return carry

    lax.fori_loop(0, tr, row_body, 0)


def _conv_call(u, w, bias, ln_g, ln_b):
    b, l, ch = u.shape
    half = ch // 2
    tile = CONV_TILE_ROWS * GRID_W
    n_tiles = l // tile
    return pl.pallas_call(
        functools.partial(_conv_kernel, n_tiles=n_tiles),
        grid=(b, n_tiles),
        in_specs=[pl.BlockSpec((1, tile, ch), lambda i, j: (i, j, 0)),
                  pl.BlockSpec((1, tile, half), lambda i, j: (i, jnp.maximum(j - 1, 0), 1)),
                  pl.BlockSpec((1, tile, half), lambda i, j: (i, jnp.minimum(j + 1, n_tiles - 1), 1)),
                  pl.BlockSpec((CONV_WIDTH, ch), lambda i, j: (0, 0)),
                  pl.BlockSpec((1, ch), lambda i, j: (0, 0)),
                  pl.BlockSpec((1, ch), lambda i, j: (0, 0)),
                  pl.BlockSpec((1, ch), lambda i, j: (0, 0))],
        out_specs=pl.BlockSpec((1, tile, ch), lambda i, j: (i, j, 0)),
        out_shape=jax.ShapeDtypeStruct((b, l, ch), BF16),
        scratch_shapes=[pltpu.VMEM((CONV_TILE_ROWS * _HROW, half), F32),
                        pltpu.VMEM((tile + 2 * CONV_PAD * GRID_W, half), F32),
                        pltpu.VMEM((tile, ch), F32)],
        compiler_params=_params(("parallel", "parallel")),
        name="conv",
    )(u, u, u, w, bias, ln_g, ln_b)


def _mix_kernel(uc_ref, og_ref, gc_ref, gh_ref, x_ref, wpw_ref, wh_ref, wo_ref,
                npost_ref, npre_ref, mod_ref, x1_ref, h2_ref):
    y_c = jnp.dot(uc_ref[0], wpw_ref[...], preferred_element_type=F32)
    y_h = jnp.dot(og_ref[0], wh_ref[...], preferred_element_type=F32)
    z = gc_ref[0].astype(F32) * y_c + gh_ref[0].astype(F32) * y_h
    y = jnp.dot(z.astype(BF16), wo_ref[...], preferred_element_type=F32)
    ms = jnp.mean(y * y, axis=-1, keepdims=True)
    yn = y * lax.rsqrt(ms + EPS) * npost_ref[...]
    x1 = x_ref[0] + mod_ref[0, 2:3, :] * yn
    x1_ref[0] = x1
    ms1 = jnp.mean(x1 * x1, axis=-1, keepdims=True)
    hn = x1 * lax.rsqrt(ms1 + EPS) * npre_ref[...]
    h2_ref[0] = (hn * (1.0 + mod_ref[0, 4:5, :]) + mod_ref[0, 3:4, :]).astype(BF16)


def _mix_call(uc, og, gc, gh, x, wpw, wh, wo, npost, npre, mod3, tm=256):
    b, l, d = x.shape
    dc = uc.shape[2]
    row = lambda width: pl.BlockSpec((1, tm, width), lambda i, j: (i, j, 0))
    const = lambda shape: pl.BlockSpec(shape, lambda i, j: (0,) * len(shape), pipeline_mode=pl.Buffered(1))
    return pl.pallas_call(
        _mix_kernel,
        grid=(b, l // tm),
        in_specs=[row(dc), row(dc), row(d), row(d), row(d),
                  const(wpw.shape), const(wh.shape), const(wo.shape),
                  const((1, d)), const((1, d)),
                  pl.BlockSpec((1, N_MOD, d), lambda i, j: (i, 0, 0))],
        out_specs=[row(d), row(d)],
        out_shape=[jax.ShapeDtypeStruct((b, l, d), F32), jax.ShapeDtypeStruct((b, l, d), BF16)],
        compiler_params=_params(("parallel", "parallel")),
        name="mix",
    )(uc, og, gc, gh, x, wpw, wh, wo, npost, npre, mod3)


def _mlp_kernel(h_ref, w1_ref, w2_ref, x1_ref, npost_ref, mod_ref, o_ref, *, n_ff):
    j = pl.program_id(2)

    @pl.when(j == 0)
    def _():
        o_ref[...] = jnp.zeros(o_ref.shape, F32)

    a = jnp.dot(h_ref[0], w1_ref[...], preferred_element_type=F32)
    a = jnp.square(jnp.maximum(a, 0.0)).astype(BF16)
    o_ref[0] += jnp.dot(a, w2_ref[...], preferred_element_type=F32)

    @pl.when(j == n_ff - 1)
    def _():
        y = o_ref[0]
        ms = jnp.mean(y * y, axis=-1, keepdims=True)
        yn = y * lax.rsqrt(ms + EPS) * npost_ref[...]
        o_ref[0] = x1_ref[0] + mod_ref[0, 5:6, :] * yn


def _mlp_call(h2, w1, w2, x1, npost, mod3, tm=1024, tf=512):
    b, l, d = x1.shape
    dff = w1.shape[1]
    n_ff = dff // tf
    return pl.pallas_call(
        functools.partial(_mlp_kernel, n_ff=n_ff),
        grid=(b, l // tm, n_ff),
        in_specs=[pl.BlockSpec((1, tm, d), lambda i, m, j: (i, m, 0)),
                  pl.BlockSpec((d, tf), lambda i, m, j: (0, j)),
                  pl.BlockSpec((tf, d), lambda i, m, j: (j, 0)),
                  pl.BlockSpec((1, tm, d), lambda i, m, j: (i, m, 0), pipeline_mode=pl.Buffered(1)),
                  pl.BlockSpec((1, d), lambda i, m, j: (0, 0)),
                  pl.BlockSpec((1, N_MOD, d), lambda i, m, j: (i, 0, 0))],
        out_specs=pl.BlockSpec((1, tm, d), lambda i, m, j: (i, m, 0)),
        out_shape=jax.ShapeDtypeStruct((b, l, d), F32),
        compiler_params=_params(("parallel", "parallel", "arbitrary")),
        name="mlp",
    )(h2, w1, w2, x1, npost, mod3)


def kernel(x, c, ctx, c_ctx, w_mod, b_mod, norm_pre_mix, norm_post_mix, norm_pre_mlp, norm_post_mlp, w_in,
           conv_dw_w, conv_dw_b, conv_ln_g, conv_ln_b, conv_pw_w, hgrn_lb_logits, hgrn_norm_g, hgrn_out_w,
           w_out, mlp_w1, mlp_w2):
    assert w_mod.shape[0] == 1, "single-layer block"
    b, l, d = x.shape
    lc = ctx.shape[1]
    dh = hgrn_norm_g.shape[1]
    dc = conv_dw_w.shape[2]
    n_heads = dh // HEAD_DIM
    assert l % (CONV_TILE_ROWS * GRID_W) == 0 and CONV_PAD <= CONV_TILE_ROWS and CONV_PAD <= _HPAD
    assert l % SCAN_BLOCK == 0 and lc % SCAN_CHUNK == 0

    mod_rows = 16
    cc = jnp.concatenate([c, c_ctx[None, :], jnp.zeros((mod_rows - b - 1, d), F32)], axis=0)
    mod3 = _mod_call(cc, w_mod[0], b_mod).reshape(mod_rows, N_MOD, d)

    w_in0 = w_in[0]
    lb_f, lb_b = hgrn_lb_logits[0], hgrn_lb_logits[1]
    off = lambda n: n * dh

    hc = _prenorm_call(ctx, norm_pre_mix, mod3, lambda i: b, 0, tl=lc).reshape(b * lc, d)
    seq_c = lambda t: t.reshape(b, lc, dh)
    (vc,) = _proj_call(hc, w_in0, [off(0)], dh, _ep_identity, [BF16], name="proj_ctx_v")
    lfc_f, kc_f = _proj_call(hc, w_in0, [off(1)], dh, _ep_forget, [F32, BF16], extra=(lb_f,), name="proj_ctx_ff")
    lfc_b, kc_b = _proj_call(hc, w_in0, [off(2)], dh, _ep_forget, [F32, BF16], extra=(lb_b,), name="proj_ctx_fb")
    zeros = jnp.zeros((b, n_heads, HEAD_DIM, HEAD_DIM), F32)
    s_f = _scan_call(seq_c(kc_f), seq_c(vc), seq_c(lfc_f), zeros, reverse=False, mode="state")
    s_b = _scan_call(seq_c(kc_b), seq_c(vc), seq_c(lfc_b), zeros, reverse=True, mode="state")

    h = _prenorm_call(x, norm_pre_mix, mod3, lambda i: i, 0, tl=512).reshape(b * l, d)
    seq = lambda t: t.reshape(b, l, t.shape[-1])
    (v,) = _proj_call(h, w_in0, [off(0)], dh, _ep_identity, [BF16], name="proj_v")
    lf_f, k_f = _proj_call(h, w_in0, [off(1)], dh, _ep_forget, [F32, BF16], extra=(lb_f,), name="proj_ff")
    lf_b, k_b = _proj_call(h, w_in0, [off(2)], dh, _ep_forget, [F32, BF16], extra=(lb_b,), name="proj_fb")
    (q,) = _proj_call(h, w_in0, [off(3)], dh, _ep_q, [BF16], name="proj_q")
    (g,) = _proj_call(h, w_in0, [off(4)], dh, _ep_silu, [BF16], name="proj_g")
    (u,) = _proj_call(h, w_in0, [off(5), off(5) + dc], dc, _ep_glu, [F32], tn=512, name="proj_glu")
    (gc,) = _proj_call(h, w_in0, [off(5) + 2 * dc], d, _ep_sigmoid, [BF16], name="proj_gc")
    (gh,) = _proj_call(h, w_in0, [off(5) + 2 * dc + d], d, _ep_sigmoid, [BF16], name="proj_gh")

    o_b = _scan_call(seq(k_b), seq(v), seq(lf_b), s_b, reverse=True, mode="out", q=seq(q))
    og = _scan_call(seq(k_f), seq(v), seq(lf_f), s_f, reverse=False, mode="readout", q=seq(q),
                    ob=o_b, g=seq(g), ng=hgrn_norm_g)

    uc = _conv_call(seq(u), conv_dw_w[0], conv_dw_b, conv_ln_g, conv_ln_b)

    x1, h2 = _mix_call(uc, og, seq(gc), seq(gh), x, conv_pw_w[0].astype(BF16), hgrn_out_w[0].astype(BF16),
                       w_out[0].astype(BF16), norm_post_mix, norm_pre_mlp, mod3)
    return _mlp_call(h2, mlp_w1[0].astype(BF16), mlp_w2[0].astype(BF16), x1, norm_post_mlp, mod3)
```

```python
import functools

import jax
import jax.numpy as jnp
from jax import lax
from jax.experimental import pallas as pl
from jax.experimental.pallas import tpu as pltpu

F32 = jnp.float32
BF16 = jnp.bfloat16

GRID_W = 64
CONV_WIDTH = 31
CONV_PAD = (CONV_WIDTH - 1) // 2
HEAD_DIM = 128
N_MOD = 6
EPS = 1e-6

SCAN_CHUNK = 64
SCAN_BLOCK = 512

VMEM_LIMIT = 56 * 1024 * 1024

_NT = (((1,), (1,)), ((), ()))
_TN = (((0,), (0,)), ((), ()))


def _params(sem):
    return pltpu.CompilerParams(dimension_semantics=sem, vmem_limit_bytes=VMEM_LIMIT)


def _sigmoid(x):
    return 0.5 * jnp.tanh(0.5 * x) + 0.5


def _silu(x):
    return x * _sigmoid(x)


def _mod_kernel(c_ref, w_ref, b_ref, o_ref):
    s = _silu(c_ref[...]).astype(BF16)
    o_ref[...] = jnp.dot(s, w_ref[...].astype(BF16), preferred_element_type=F32) + b_ref[...]


def _mod_call(cc, w_mod, b_mod, tn=1024):
    rows, d = cc.shape
    n = w_mod.shape[1]
    return pl.pallas_call(
        _mod_kernel,
        grid=(n // tn,),
        in_specs=[pl.BlockSpec((rows, d), lambda j: (0, 0)),
                  pl.BlockSpec((d, tn), lambda j: (0, j)),
                  pl.BlockSpec((1, tn), lambda j: (0, j))],
        out_specs=pl.BlockSpec((rows, tn), lambda j: (0, j)),
        out_shape=jax.ShapeDtypeStruct((rows, n), F32),
        compiler_params=_params(("parallel",)),
        name="mod",
    )(cc, w_mod, b_mod)


def _prenorm_kernel(x_ref, g_ref, mod_ref, o_ref, *, shift_idx):
    x = x_ref[0]
    ms = jnp.mean(x * x, axis=-1, keepdims=True)
    y = x * lax.rsqrt(ms + EPS) * g_ref[...]
    sh = mod_ref[0, shift_idx:shift_idx + 1, :]
    sc = mod_ref[0, shift_idx + 1:shift_idx + 2, :]
    o_ref[0] = (y * (1.0 + sc) + sh).astype(BF16)


def _prenorm_call(x, g, mod3, row_of_batch, shift_idx, tl):
    b, l, d = x.shape
    return pl.pallas_call(
        functools.partial(_prenorm_kernel, shift_idx=shift_idx),
        grid=(b, l // tl),
        in_specs=[pl.BlockSpec((1, tl, d), lambda i, j: (i, j, 0)),
                  pl.BlockSpec((1, d), lambda i, j: (0, 0)),
                  pl.BlockSpec((1, N_MOD, d), lambda i, j: (row_of_batch(i), 0, 0))],
        out_specs=pl.BlockSpec((1, tl, d), lambda i, j: (i, j, 0)),
        out_shape=jax.ShapeDtypeStruct((b, l, d), BF16),
        compiler_params=_params(("parallel", "parallel")),
        name="prenorm",
    )(x, g, mod3)


def _proj_kernel(*refs, n_w, n_extra, n_side, n_out, epilogue):
    h_ref = refs[0]
    pos = 1
    w_refs = refs[pos:pos + n_w]
    pos += n_w
    extra = refs[pos:pos + n_extra]
    pos += n_extra
    side_in = refs[pos:pos + n_side]
    pos += n_side
    out_refs = refs[pos:pos + n_out]
    pos += n_out
    side_out = refs[pos:pos + n_side]
    pos += n_side
    wbf_refs = refs[pos:]

    @pl.when(pl.program_id(1) == 0)
    def _():
        for w_ref, wbf_ref in zip(w_refs, wbf_refs):
            wbf_ref[...] = w_ref[...].astype(BF16)

    h = h_ref[...]
    accs = [jnp.dot(h, wbf_ref[...], preferred_element_type=F32) for wbf_ref in wbf_refs]
    outs = epilogue(accs, extra)
    for o_ref, o in zip(out_refs, outs):
        o_ref[...] = o.astype(o_ref.dtype)
    for si_ref, so_ref in zip(side_in, side_out):
        so_ref[...] = si_ref[...].astype(BF16)


def _ep_identity(accs, extra):
    return (accs[0],)


def _ep_forget(accs, extra):
    logits = extra[0][...]
    e = jnp.exp(logits - jnp.max(logits, axis=0, keepdims=True))
    lb = e[0:1] / jnp.sum(e, axis=0, keepdims=True)
    f = lb + (1.0 - lb) * _sigmoid(accs[0])
    return jnp.log2(f), 1.0 - f


def _ep_q(accs, extra):
    return (_silu(accs[0]) * (HEAD_DIM ** -0.5),)


def _ep_silu(accs, extra):
    return (_silu(accs[0]),)


def _ep_glu(accs, extra):
    return (accs[0] * _sigmoid(accs[1]),)


def _ep_sigmoid(accs, extra):
    return (_sigmoid(accs[0]),)


def _proj_call(h, w, col_offsets, width, epilogue, out_dtypes, extra=(), side=(), tm=1024, tn=1024,
               name="proj"):
    m, k = h.shape
    tm = min(tm, m)
    n_w = len(col_offsets)
    n_j, n_i = width // tn, m // tm
    in_specs = [pl.BlockSpec((tm, k), lambda j, i: (i, 0))]
    for off in col_offsets:
        in_specs.append(pl.BlockSpec((k, tn), lambda j, i, off=off: (0, off // tn + j)))
    for e in extra:
        in_specs.append(pl.BlockSpec((e.shape[0], tn), lambda j, i: (0, j)))
    side_specs = [pl.BlockSpec((a.shape[0] // (n_j * n_i), a.shape[1]), lambda j, i: (j * n_i + i, 0))
                  for a in side]
    return pl.pallas_call(
        functools.partial(_proj_kernel, n_w=n_w, n_extra=len(extra), n_side=len(side),
                          n_out=len(out_dtypes), epilogue=epilogue),
        grid=(n_j, n_i),
        in_specs=in_specs + side_specs,
        out_specs=[pl.BlockSpec((tm, tn), lambda j, i: (i, j)) for _ in out_dtypes] + side_specs,
        out_shape=[jax.ShapeDtypeStruct((m, width), dt) for dt in out_dtypes]
                  + [jax.ShapeDtypeStruct(a.shape, BF16) for a in side],
        scratch_shapes=[pltpu.VMEM((k, tn), BF16) for _ in range(n_w)],
        compiler_params=_params(("parallel", "arbitrary")),
        name=name,
    )(h, *([w] * n_w), *extra, *side)


def _prenorm_v_kernel(x_ref, g_ref, mod_ref, w_ref, h_ref, v_ref, wbf_ref):
    @pl.when((pl.program_id(0) == 0) & (pl.program_id(1) == 0))
    def _():
        wbf_ref[...] = w_ref[...].astype(BF16)

    x = x_ref[0]
    ms = jnp.mean(x * x, axis=-1, keepdims=True)
    y = x * lax.rsqrt(ms + EPS) * g_ref[...]
    h = (y * (1.0 + mod_ref[0, 1:2, :]) + mod_ref[0, 0:1, :]).astype(BF16)
    h_ref[0] = h
    v_ref[0] = jnp.dot(h, wbf_ref[...], preferred_element_type=F32).astype(BF16)


def _prenorm_v_call(x, g, mod3, w, width, tl=512):
    b, l, d = x.shape
    return pl.pallas_call(
        _prenorm_v_kernel,
        grid=(b, l // tl),
        in_specs=[pl.BlockSpec((1, tl, d), lambda i, j: (i, j, 0)),
                  pl.BlockSpec((1, d), lambda i, j: (0, 0)),
                  pl.BlockSpec((1, N_MOD, d), lambda i, j: (i, 0, 0)),
                  pl.BlockSpec((d, width), lambda i, j: (0, 0), pipeline_mode=pl.Buffered(1))],
        out_specs=[pl.BlockSpec((1, tl, d), lambda i, j: (i, j, 0)),
                   pl.BlockSpec((1, tl, width), lambda i, j: (i, j, 0))],
        out_shape=[jax.ShapeDtypeStruct((b, l, d), BF16), jax.ShapeDtypeStruct((b, l, width), BF16)],
        scratch_shapes=[pltpu.VMEM((d, width), BF16)],
        compiler_params=_params(("arbitrary", "arbitrary")),
        name="prenorm_v",
    )(x, g, mod3, w)


def _scan_cumsum(a, reverse):
    n = a.shape[0]
    row = lax.broadcasted_iota(jnp.int32, a.shape, 0)
    sh = 1
    while sh < n:
        if reverse:
            a = a + jnp.where(row < n - sh, pltpu.roll(a, n - sh, 0), 0.0)
        else:
            a = a + jnp.where(row >= sh, pltpu.roll(a, sh, 0), 0.0)
        sh *= 2
    return a


def _scan_chunk_head(q, k, v, lf, st, mask0, mask1, reverse, need_out):
    c = lf.shape[0]
    half, quarter = c // 2, c // 4
    a = _scan_cumsum(lf, reverse)
    row = lax.broadcasted_iota(jnp.int32, a.shape, 0)
    if reverse:
        m0 = a[half:half + 1]
        m1 = jnp.where(row < half, a[quarter:quarter + 1], a[half + quarter:half + quarter + 1])
        a_tot = a[0:1]
    else:
        m0 = a[half - 1:half]
        m1 = jnp.where(row < half, a[quarter - 1:quarter], a[half + quarter - 1:half + quarter])
        a_tot = a[c - 1:c]

    kf = k.astype(F32)
    o = None
    if need_out:
        qf = q.astype(F32)
        q0 = (qf * jnp.exp2(a - m0)).astype(BF16)
        k0 = (kf * jnp.exp2(m0 - a)).astype(BF16)
        q1 = (qf * jnp.exp2(a - m1)).astype(BF16)
        k1 = (kf * jnp.exp2(m1 - a)).astype(BF16)
        s0 = lax.dot_general(q0, k0, _NT, preferred_element_type=F32)
        s1 = lax.dot_general(q1, k1, _NT, preferred_element_type=F32)
        scores = jnp.where(mask0, s0, 0.0) + jnp.where(mask1, s1, 0.0)
        intra = jnp.dot(scores.astype(BF16), v, preferred_element_type=F32)
        qi = (qf * jnp.exp2(a)).astype(BF16)
        inter = lax.dot_general(qi, st.astype(BF16), _NT, preferred_element_type=F32)
        o = inter + intra
    kh = (kf * jnp.exp2(a_tot - a)).astype(BF16)
    upd = lax.dot_general(v, kh, _TN, preferred_element_type=F32)
    st_new = st * jnp.exp2(a_tot) + upd
    return o, st_new


def _scan_kernel(*refs, reverse, mode, n_blocks):
    k_ref, v_ref, lf_ref, s0_ref = refs[:4]
    if mode == "state":
        (sfin_ref, st_ref) = refs[4:]
    elif mode == "out":
        (q_ref, o_ref, st_ref) = refs[4:]
    else:
        (q_ref, ob_ref, g_ref, ng_ref, o_ref, st_ref) = refs[4:]
    need_out = mode != "state"
    c = SCAN_CHUNK
    n_batch, rows = k_ref.shape[0], k_ref.shape[1]
    n_chunks = rows // c
    n_heads = k_ref.shape[2] // HEAD_DIM
    step = pl.program_id(0)

    @pl.when(step == 0)
    def _():
        st_ref[...] = s0_ref[...]

    t = lax.broadcasted_iota(jnp.int32, (c, c), 0)
    s = lax.broadcasted_iota(jnp.int32, (c, c), 1)
    half = c // 2
    if reverse:
        mask0 = (t < half) & (s >= half)
        mask1 = ((t < half) == (s < half)) & (s >= t)
    else:
        mask0 = (t >= half) & (s < half)
        mask1 = ((t < half) == (s < half)) & (s <= t)

    def chunk_body(ci, carry):
        cidx = (n_chunks - 1 - ci) if reverse else ci
        r0 = pl.multiple_of(cidx * c, c)
        for h in range(n_heads):
            lanes = slice(h * HEAD_DIM, (h + 1) * HEAD_DIM)
            for bi in range(n_batch):
                q = q_ref[bi, pl.ds(r0, c), lanes] if need_out else None
                o, st_new = _scan_chunk_head(
                    q, k_ref[bi, pl.ds(r0, c), lanes], v_ref[bi, pl.ds(r0, c), lanes],
                    lf_ref[bi, pl.ds(r0, c), lanes], st_ref[bi, h], mask0, mask1, reverse, need_out)
                st_ref[bi, h] = st_new
                if mode == "out":
                    o_ref[bi, pl.ds(r0, c), lanes] = o
                elif mode == "readout":
                    o = o + ob_ref[bi, pl.ds(r0, c), lanes]
                    ms = jnp.mean(o * o, axis=-1, keepdims=True)
                    y = o * lax.rsqrt(ms + EPS) * ng_ref[:, lanes]
                    o_ref[bi, pl.ds(r0, c), lanes] = (
                        y * g_ref[bi, pl.ds(r0, c), lanes].astype(F32)).astype(BF16)
        return carry

    lax.fori_loop(0, n_chunks, chunk_body, 0)

    if mode == "state":
        @pl.when(step == n_blocks - 1)
        def _():
            sfin_ref[...] = st_ref[...]


def _scan_call(k, v, lf, s0, *, reverse, mode, q=None, ob=None, g=None, ng=None):
    b, l, dh = k.shape
    tb = min(SCAN_BLOCK, l)
    n_blocks = l // tb
    if reverse:
        seq = lambda j: (0, n_blocks - 1 - j, 0)
    else:
        seq = lambda j: (0, j, 0)
    seq_spec = pl.BlockSpec((b, tb, dh), seq)
    st_spec = pl.BlockSpec(s0.shape, lambda j: (0, 0, 0, 0))
    in_specs = [seq_spec, seq_spec, seq_spec, st_spec]
    args = [k, v, lf, s0]
    if mode == "state":
        out_specs, out_shape = st_spec, jax.ShapeDtypeStruct(s0.shape, F32)
    else:
        in_specs.append(seq_spec)
        args.append(q)
        if mode == "out":
            out_shape = jax.ShapeDtypeStruct((b, l, dh), F32)
        else:
            in_specs += [seq_spec, seq_spec, pl.BlockSpec((1, dh), lambda j: (0, 0))]
            args += [ob, g, ng]
            out_shape = jax.ShapeDtypeStruct((b, l, dh), BF16)
        out_specs = seq_spec
    return pl.pallas_call(
        functools.partial(_scan_kernel, reverse=reverse, mode=mode, n_blocks=n_blocks),
        grid=(n_blocks,),
        in_specs=in_specs,
        out_specs=out_specs,
        out_shape=out_shape,
        scratch_shapes=[pltpu.VMEM(s0.shape, F32)],
        compiler_params=_params(("arbitrary",)),
        name="scan_" + mode + ("_bwd" if reverse else "_fwd"),
    )(*args)


CONV_TILE_ROWS = 16
_HPAD = 16
_HROW = GRID_W + 2 * _HPAD


def _conv_kernel(cur_ref, prev_ref, next_ref, w_ref, b_ref, lg_ref, lb_ref, o_ref,
                 hpad_ref, vpad_ref, y_ref, *, n_tiles):
    tr = CONV_TILE_ROWS
    half = w_ref.shape[1] // 2
    n_lane_tiles = half // 128
    i = pl.program_id(1)

    hpad_ref[...] = jnp.zeros(hpad_ref.shape, F32)
    for r in range(tr):
        hpad_ref[r * _HROW + _HPAD:r * _HROW + _HPAD + GRID_W, :] = cur_ref[0, r * GRID_W:(r + 1) * GRID_W, :half]
    halo = CONV_PAD * GRID_W
    top = prev_ref[0, tr * GRID_W - halo:, :]
    bot = next_ref[0, :halo, :]
    vpad_ref[:halo, :] = jnp.where(i > 0, top, 0.0)
    vpad_ref[halo:halo + tr * GRID_W, :] = cur_ref[0, :, half:]
    vpad_ref[halo + tr * GRID_W:, :] = jnp.where(i < n_tiles - 1, bot, 0.0)

    def row_body(r, carry):
        h0 = pl.multiple_of(r * _HROW, 8)
        v0 = pl.multiple_of(r * GRID_W, GRID_W)
        o0 = pl.multiple_of(r * GRID_W, GRID_W)
        for lt in range(n_lane_tiles):
            lanes = slice(lt * 128, (lt + 1) * 128)
            win = hpad_ref[pl.ds(h0, _HROW), lanes]
            acc = jnp.zeros((GRID_W, 128), F32)
            for sub in range(8):
                taps = [jj for jj in range(CONV_WIDTH) if (_HPAD - CONV_PAD + jj) % 8 == sub]
                if not taps:
                    continue
                shifted = win if sub == 0 else pltpu.roll(win, _HROW - sub, 0)
                for jj in taps:
                    base = (_HPAD - CONV_PAD + jj) - sub
                    acc = acc + shifted[base:base + GRID_W, :] * w_ref[jj:jj + 1, lanes]
            y_ref[pl.ds(o0, GRID_W), lanes] = acc + b_ref[:, lanes]
            vl = slice(half + lt * 128, half + (lt + 1) * 128)
            acc = jnp.zeros((GRID_W, 128), F32)
            for jj in range(CONV_WIDTH):
                src = pl.multiple_of(v0 + jj * GRID_W, GRID_W)
                acc = acc + vpad_ref[pl.ds(src, GRID_W), lanes] * w_ref[jj:jj + 1, vl]
            y_ref[pl.ds(o0, GRID_W), vl] = acc + b_ref[:, vl]
        y = y_ref[pl.ds(o0, GRID_W), :]
        mu = jnp.mean(y, axis=-1, keepdims=True)
        yc = y - mu
        var = jnp.mean(yc * yc, axis=-1, keepdims=True)
        z = yc * lax.rsqrt(var + EPS) * lg_ref[...] + lb_ref[...]
        o_ref[0, pl.ds(o0, GRID_W), :] = _silu(z).astype(BF16)
        return carry

    lax.fori_loop(0, tr, row_body, 0)


def _conv_call(u, w, bias, ln_g, ln_b):
    b, l, ch = u.shape
    half = ch // 2
    tile = CONV_TILE_ROWS * GRID_W
    n_tiles = l // tile
    return pl.pallas_call(
        functools.partial(_conv_kernel, n_tiles=n_tiles),
        grid=(b, n_tiles),
        in_specs=[pl.BlockSpec((1, tile, ch), lambda i, j: (i, j, 0)),
                  pl.BlockSpec((1, tile, half), lambda i, j: (i, jnp.maximum(j - 1, 0), 1)),
                  pl.BlockSpec((1, tile, half), lambda i, j: (i, jnp.minimum(j + 1, n_tiles - 1), 1)),
                  pl.BlockSpec((CONV_WIDTH, ch), lambda i, j: (0, 0)),
                  pl.BlockSpec((1, ch), lambda i, j: (0, 0)),
                  pl.BlockSpec((1, ch), lambda i, j: (0, 0)),
                  pl.BlockSpec((1, ch), lambda i, j: (0, 0))],
        out_specs=pl.BlockSpec((1, tile, ch), lambda i, j: (i, j, 0)),
        out_shape=jax.ShapeDtypeStruct((b, l, ch), BF16),
        scratch_shapes=[pltpu.VMEM((CONV_TILE_ROWS * _HROW, half), F32),
                        pltpu.VMEM((tile + 2 * CONV_PAD * GRID_W, half), F32),
                        pltpu.VMEM((tile, ch), F32)],
        compiler_params=_params(("parallel", "parallel")),
        name="conv",
    )(u, u, u, w, bias, ln_g, ln_b)


def _mix_kernel(uc_ref, og_ref, gc_ref, gh_ref, x_ref, wpw_ref, wh_ref, wo_ref,
                npost_ref, npre_ref, mod_ref, x1_ref, h2_ref):
    y_c = jnp.dot(uc_ref[0], wpw_ref[...], preferred_element_type=F32)
    y_h = jnp.dot(og_ref[0], wh_ref[...], preferred_element_type=F32)
    z = gc_ref[0].astype(F32) * y_c + gh_ref[0].astype(F32) * y_h
    y = jnp.dot(z.astype(BF16), wo_ref[...], preferred_element_type=F32)
    ms = jnp.mean(y * y, axis=-1, keepdims=True)
    yn = y * lax.rsqrt(ms + EPS) * npost_ref[...]
    x1 = x_ref[0] + mod_ref[0, 2:3, :] * yn
    x1_ref[0] = x1
    ms1 = jnp.mean(x1 * x1, axis=-1, keepdims=True)
    hn = x1 * lax.rsqrt(ms1 + EPS) * npre_ref[...]
    h2_ref[0] = (hn * (1.0 + mod_ref[0, 4:5, :]) + mod_ref[0, 3:4, :]).astype(BF16)


def _mix_call(uc, og, gc, gh, x, wpw, wh, wo, npost, npre, mod3, tm=256):
    b, l, d = x.shape
    dc = uc.shape[2]
    row = lambda width: pl.BlockSpec((1, tm, width), lambda i, j: (i, j, 0))
    const = lambda shape: pl.BlockSpec(shape, lambda i, j: (0,) * len(shape), pipeline_mode=pl.Buffered(1))
    return pl.pallas_call(
        _mix_kernel,
        grid=(b, l // tm),
        in_specs=[row(dc), row(dc), row(d), row(d), row(d),
                  const(wpw.shape), const(wh.shape), const(wo.shape),
                  const((1, d)), const((1, d)),
                  pl.BlockSpec((1, N_MOD, d), lambda i, j: (i, 0, 0))],
        out_specs=[row(d), row(d)],
        out_shape=[jax.ShapeDtypeStruct((b, l, d), F32), jax.ShapeDtypeStruct((b, l, d), BF16)],
        compiler_params=_params(("parallel", "parallel")),
        name="mix",
    )(uc, og, gc, gh, x, wpw, wh, wo, npost, npre, mod3)


def _mlp_kernel(h_ref, w1_ref, w2_ref, x1_ref, npost_ref, mod_ref, o_ref, *, n_ff):
    j = pl.program_id(2)

    @pl.when(j == 0)
    def _():
        o_ref[...] = jnp.zeros(o_ref.shape, F32)

    a = jnp.dot(h_ref[0], w1_ref[...], preferred_element_type=F32)
    a = jnp.square(jnp.maximum(a, 0.0)).astype(BF16)
    o_ref[0] += jnp.dot(a, w2_ref[...], preferred_element_type=F32)

    @pl.when(j == n_ff - 1)
    def _():
        y = o_ref[0]
        ms = jnp.mean(y * y, axis=-1, keepdims=True)
        yn = y * lax.rsqrt(ms + EPS) * npost_ref[...]
        o_ref[0] = x1_ref[0] + mod_ref[0, 5:6, :] * yn


def _mlp_call(h2, w1, w2, x1, npost, mod3, tm=1024, tf=512):
    b, l, d = x1.shape
    dff = w1.shape[1]
    n_ff = dff // tf
    return pl.pallas_call(
        functools.partial(_mlp_kernel, n_ff=n_ff),
        grid=(b, l // tm, n_ff),
        in_specs=[pl.BlockSpec((1, tm, d), lambda i, m, j: (i, m, 0)),
                  pl.BlockSpec((d, tf), lambda i, m, j: (0, j)),
                  pl.BlockSpec((tf, d), lambda i, m, j: (j, 0)),
                  pl.BlockSpec((1, tm, d), lambda i, m, j: (i, m, 0), pipeline_mode=pl.Buffered(1)),
                  pl.BlockSpec((1, d), lambda i, m, j: (0, 0)),
                  pl.BlockSpec((1, N_MOD, d), lambda i, m, j: (i, 0, 0))],
        out_specs=pl.BlockSpec((1, tm, d), lambda i, m, j: (i, m, 0)),
        out_shape=jax.ShapeDtypeStruct((b, l, d), F32),
        compiler_params=_params(("parallel", "parallel", "arbitrary")),
        name="mlp",
    )(h2, w1, w2, x1, npost, mod3)


def kernel(x, c, ctx, c_ctx, w_mod, b_mod, norm_pre_mix, norm_post_mix, norm_pre_mlp, norm_post_mlp, w_in,
           conv_dw_w, conv_dw_b, conv_ln_g, conv_ln_b, conv_pw_w, hgrn_lb_logits, hgrn_norm_g, hgrn_out_w,
           w_out, mlp_w1, mlp_w2):
    assert w_mod.shape[0] == 1, "single-layer block"
    b, l, d = x.shape
    lc = ctx.shape[1]
    dh = hgrn_norm_g.shape[1]
    dc = conv_dw_w.shape[2]
    n_heads = dh // HEAD_DIM
    assert l % (CONV_TILE_ROWS * GRID_W) == 0 and CONV_PAD <= CONV_TILE_ROWS and CONV_PAD <= _HPAD
    assert l % SCAN_BLOCK == 0 and lc % SCAN_CHUNK == 0

    mod_rows = 16
    cc = jnp.concatenate([c, c_ctx[None, :], jnp.zeros((mod_rows - b - 1, d), F32)], axis=0)
    mod3 = _mod_call(cc, w_mod[0], b_mod).reshape(mod_rows, N_MOD, d)

    w_in0 = w_in[0]
    lb_f, lb_b = hgrn_lb_logits[0], hgrn_lb_logits[1]
    off = lambda n: n * dh

    hc = _prenorm_call(ctx, norm_pre_mix, mod3, lambda i: b, 0, tl=lc).reshape(b * lc, d)
    seq_c = lambda t: t.reshape(b, lc, dh)
    (vc,) = _proj_call(hc, w_in0, [off(0)], dh, _ep_identity, [BF16], name="proj_ctx_v")
    lfc_f, kc_f = _proj_call(hc, w_in0, [off(1)], dh, _ep_forget, [F32, BF16], extra=(lb_f,), name="proj_ctx_ff")
    lfc_b, kc_b = _proj_call(hc, w_in0, [off(2)], dh, _ep_forget, [F32, BF16], extra=(lb_b,), name="proj_ctx_fb")
    zeros = jnp.zeros((b, n_heads, HEAD_DIM, HEAD_DIM), F32)
    s_f = _scan_call(seq_c(kc_f), seq_c(vc), seq_c(lfc_f), zeros, reverse=False, mode="state")
    s_b = _scan_call(seq_c(kc_b), seq_c(vc), seq_c(lfc_b), zeros, reverse=True, mode="state")

    h, v = _prenorm_v_call(x, norm_pre_mix, mod3, w_in0, dh)
    h = h.reshape(b * l, d)
    seq = lambda t: t.reshape(b, l, t.shape[-1])
    lf_f, k_f = _proj_call(h, w_in0, [off(1)], dh, _ep_forget, [F32, BF16], extra=(lb_f,), name="proj_ff")
    lf_b, k_b = _proj_call(h, w_in0, [off(2)], dh, _ep_forget, [F32, BF16], extra=(lb_b,), name="proj_fb")
    q, wpw = _proj_call(h, w_in0, [off(3)], dh, _ep_q, [BF16], side=(conv_pw_w[0],), name="proj_q")
    g, wh = _proj_call(h, w_in0, [off(4)], dh, _ep_silu, [BF16], side=(hgrn_out_w[0],), name="proj_g")
    u, wo = _proj_call(h, w_in0, [off(5), off(5) + dc], dc, _ep_glu, [F32], side=(w_out[0],), tn=512,
                       name="proj_glu")
    gc, w1 = _proj_call(h, w_in0, [off(5) + 2 * dc], d, _ep_sigmoid, [BF16], side=(mlp_w1[0],), name="proj_gc")
    gh, w2 = _proj_call(h, w_in0, [off(5) + 2 * dc + d], d, _ep_sigmoid, [BF16], side=(mlp_w2[0],),
                        name="proj_gh")

    o_b = _scan_call(seq(k_b), v, seq(lf_b), s_b, reverse=True, mode="out", q=seq(q))
    og = _scan_call(seq(k_f), v, seq(lf_f), s_f, reverse=False, mode="readout", q=seq(q),
                    ob=o_b, g=seq(g), ng=hgrn_norm_g)

    uc = _conv_call(seq(u), conv_dw_w[0], conv_dw_b, conv_ln_g, conv_ln_b)

    x1, h2 = _mix_call(uc, og, seq(gc), seq(gh), x, wpw, wh, wo, norm_post_mix, norm_pre_mlp, mod3)
    return _mlp_call(h2, w1, w2, x1, norm_post_mlp, mod3)
```

```python
import functools

import jax
import jax.numpy as jnp
from jax import lax
from jax.experimental import pallas as pl
from jax.experimental.pallas import tpu as pltpu

F32 = jnp.float32
BF16 = jnp.bfloat16

GRID_W = 64
CONV_WIDTH = 31
CONV_PAD = (CONV_WIDTH - 1) // 2
HEAD_DIM = 128
N_MOD = 6
EPS = 1e-6

SCAN_CHUNK = 64
SCAN_BLOCK = 512

VMEM_LIMIT = 56 * 1024 * 1024

_NT = (((1,), (1,)), ((), ()))
_TN = (((0,), (0,)), ((), ()))


def _params(sem):
    return pltpu.CompilerParams(dimension_semantics=sem, vmem_limit_bytes=VMEM_LIMIT)


def _sigmoid(x):
    return 0.5 * jnp.tanh(0.5 * x) + 0.5


def _silu(x):
    return x * _sigmoid(x)


def _mod_kernel(c_ref, w_ref, b_ref, o_ref):
    s = _silu(c_ref[...]).astype(BF16)
    o_ref[...] = jnp.dot(s, w_ref[...].astype(BF16), preferred_element_type=F32) + b_ref[...]


def _mod_call(cc, w_mod, b_mod, tn=1024):
    rows, d = cc.shape
    n = w_mod.shape[1]
    return pl.pallas_call(
        _mod_kernel,
        grid=(n // tn,),
        in_specs=[pl.BlockSpec((rows, d), lambda j: (0, 0)),
                  pl.BlockSpec((d, tn), lambda j: (0, j)),
                  pl.BlockSpec((1, tn), lambda j: (0, j))],
        out_specs=pl.BlockSpec((rows, tn), lambda j: (0, j)),
        out_shape=jax.ShapeDtypeStruct((rows, n), F32),
        compiler_params=_params(("parallel",)),
        name="mod",
    )(cc, w_mod, b_mod)


def _prenorm_kernel(x_ref, g_ref, mod_ref, o_ref, *, shift_idx):
    x = x_ref[0]
    ms = jnp.mean(x * x, axis=-1, keepdims=True)
    y = x * lax.rsqrt(ms + EPS) * g_ref[...]
    sh = mod_ref[0, shift_idx:shift_idx + 1, :]
    sc = mod_ref[0, shift_idx + 1:shift_idx + 2, :]
    o_ref[0] = (y * (1.0 + sc) + sh).astype(BF16)


def _prenorm_call(x, g, mod3, row_of_batch, shift_idx, tl):
    b, l, d = x.shape
    return pl.pallas_call(
        functools.partial(_prenorm_kernel, shift_idx=shift_idx),
        grid=(b, l // tl),
        in_specs=[pl.BlockSpec((1, tl, d), lambda i, j: (i, j, 0)),
                  pl.BlockSpec((1, d), lambda i, j: (0, 0)),
                  pl.BlockSpec((1, N_MOD, d), lambda i, j: (row_of_batch(i), 0, 0))],
        out_specs=pl.BlockSpec((1, tl, d), lambda i, j: (i, j, 0)),
        out_shape=jax.ShapeDtypeStruct((b, l, d), BF16),
        compiler_params=_params(("parallel", "parallel")),
        name="prenorm",
    )(x, g, mod3)


def _proj_kernel(*refs, n_w, n_extra, n_side, n_out, epilogue):
    h_ref = refs[0]
    pos = 1
    w_refs = refs[pos:pos + n_w]
    pos += n_w
    extra = refs[pos:pos + n_extra]
    pos += n_extra
    side_in = refs[pos:pos + n_side]
    pos += n_side
    out_refs = refs[pos:pos + n_out]
    pos += n_out
    side_out = refs[pos:pos + n_side]
    pos += n_side
    wbf_refs = refs[pos:]

    @pl.when(pl.program_id(1) == 0)
    def _():
        for w_ref, wbf_ref in zip(w_refs, wbf_refs):
            wbf_ref[...] = w_ref[...].astype(BF16)

    for si_ref, so_ref in zip(side_in, side_out):
        so_ref[...] = si_ref[...].astype(BF16)

    h = h_ref[...]
    accs = [jnp.dot(h, wbf_ref[...], preferred_element_type=F32) for wbf_ref in wbf_refs]
    outs = epilogue(accs, extra)
    for o_ref, o in zip(out_refs, outs):
        o_ref[...] = o.astype(o_ref.dtype)


def _ep_identity(accs, extra):
    return (accs[0],)


def _ep_forget(accs, extra):
    logits = extra[0][...]
    e = jnp.exp(logits - jnp.max(logits, axis=0, keepdims=True))
    lb = e[0:1] / jnp.sum(e, axis=0, keepdims=True)
    f = lb + (1.0 - lb) * _sigmoid(accs[0])
    return jnp.log2(f), 1.0 - f


def _ep_q(accs, extra):
    return (_silu(accs[0]) * (HEAD_DIM ** -0.5),)


def _ep_silu(accs, extra):
    return (_silu(accs[0]),)


def _ep_glu(accs, extra):
    return (accs[0] * _sigmoid(accs[1]),)


def _ep_sigmoid(accs, extra):
    return (_sigmoid(accs[0]),)


def _proj_call(h, w, col_offsets, width, epilogue, out_dtypes, extra=(), side=(), tm=1024, tn=1024,
               name="proj"):
    m, k = h.shape
    tm = min(tm, m)
    n_w = len(col_offsets)
    n_j, n_i = width // tn, m // tm
    in_specs = [pl.BlockSpec((tm, k), lambda j, i: (i, 0))]
    for off in col_offsets:
        in_specs.append(pl.BlockSpec((k, tn), lambda j, i, off=off: (0, off // tn + j)))
    for e in extra:
        in_specs.append(pl.BlockSpec((e.shape[0], tn), lambda j, i: (0, j)))
    side_specs = [pl.BlockSpec((a.shape[0] // (n_j * n_i), a.shape[1]), lambda j, i: (j * n_i + i, 0))
                  for a in side]
    return pl.pallas_call(
        functools.partial(_proj_kernel, n_w=n_w, n_extra=len(extra), n_side=len(side),
                          n_out=len(out_dtypes), epilogue=epilogue),
        grid=(n_j, n_i),
        in_specs=in_specs + side_specs,
        out_specs=[pl.BlockSpec((tm, tn), lambda j, i: (i, j)) for _ in out_dtypes] + side_specs,
        out_shape=[jax.ShapeDtypeStruct((m, width), dt) for dt in out_dtypes]
                  + [jax.ShapeDtypeStruct(a.shape, BF16) for a in side],
        scratch_shapes=[pltpu.VMEM((k, tn), BF16) for _ in range(n_w)],
        compiler_params=_params(("parallel", "arbitrary")),
        name=name,
    )(h, *([w] * n_w), *extra, *side)


def _prenorm_v_kernel(x_ref, g_ref, mod_ref, w_ref, h_ref, v_ref, wbf_ref):
    @pl.when((pl.program_id(0) == 0) & (pl.program_id(1) == 0))
    def _():
        wbf_ref[...] = w_ref[...].astype(BF16)

    x = x_ref[0]
    ms = jnp.mean(x * x, axis=-1, keepdims=True)
    y = x * lax.rsqrt(ms + EPS) * g_ref[...]
    h = (y * (1.0 + mod_ref[0, 1:2, :]) + mod_ref[0, 0:1, :]).astype(BF16)
    h_ref[0] = h
    v_ref[0] = jnp.dot(h, wbf_ref[...], preferred_element_type=F32).astype(BF16)


def _prenorm_v_call(x, g, mod3, w, width, tl=512):
    b, l, d = x.shape
    return pl.pallas_call(
        _prenorm_v_kernel,
        grid=(b, l // tl),
        in_specs=[pl.BlockSpec((1, tl, d), lambda i, j: (i, j, 0)),
                  pl.BlockSpec((1, d), lambda i, j: (0, 0)),
                  pl.BlockSpec((1, N_MOD, d), lambda i, j: (i, 0, 0)),
                  pl.BlockSpec((d, width), lambda i, j: (0, 0), pipeline_mode=pl.Buffered(1))],
        out_specs=[pl.BlockSpec((1, tl, d), lambda i, j: (i, j, 0)),
                   pl.BlockSpec((1, tl, width), lambda i, j: (i, j, 0))],
        out_shape=[jax.ShapeDtypeStruct((b, l, d), BF16), jax.ShapeDtypeStruct((b, l, width), BF16)],
        scratch_shapes=[pltpu.VMEM((d, width), BF16)],
        compiler_params=_params(("arbitrary", "arbitrary")),
        name="prenorm_v",
    )(x, g, mod3, w)


def _scan_cumsum(lf, tri2):
    hi = lf.astype(BF16)
    lo = (lf - hi.astype(F32)).astype(BF16)
    return jnp.dot(tri2, jnp.concatenate([hi, lo], axis=0), preferred_element_type=F32)


def _scan_refs(a, reverse):
    c = a.shape[0]
    half, quarter = c // 2, c // 4
    if reverse:
        return a[quarter:quarter + 1], a[half + quarter:half + quarter + 1], a[half:half + 1], a[0:1]
    return a[quarter - 1:quarter], a[half + quarter - 1:half + quarter], a[half - 1:half], a[c - 1:c]


def _scan_products(a, q, k, v, st, reverse):
    c = a.shape[0]
    half = c // 2
    m1_lo, m1_hi, m0, a_tot = _scan_refs(a, reverse)
    lo_rows, hi_rows = slice(0, half), slice(half, c)
    row = lax.broadcasted_iota(jnp.int32, a.shape, 0)
    e1 = a - jnp.where(row < half, m1_lo, m1_hi)
    pq = q.astype(F32) * jnp.exp2(e1)
    pk = k.astype(F32) * jnp.exp2(-e1)
    s1 = lax.dot_general(pq.astype(BF16), pk.astype(BF16), _NT, preferred_element_type=F32)
    if reverse:
        q_rows, m1_q, k_rows, m1_k = lo_rows, m1_lo, hi_rows, m1_hi
    else:
        q_rows, m1_q, k_rows, m1_k = hi_rows, m1_hi, lo_rows, m1_lo
    q0 = (pq[q_rows] * jnp.exp2(m1_q - m0)).astype(BF16)
    k0 = (pk[k_rows] * jnp.exp2(m0 - m1_k)).astype(BF16)
    s0 = lax.dot_general(q0, k0, _NT, preferred_element_type=F32)
    qi = jnp.concatenate([pq[lo_rows] * jnp.exp2(m1_lo), pq[hi_rows] * jnp.exp2(m1_hi)], axis=0)
    inter = lax.dot_general(qi.astype(BF16), st.astype(BF16), _NT, preferred_element_type=F32)
    kh = jnp.concatenate([pk[lo_rows] * jnp.exp2(a_tot - m1_lo), pk[hi_rows] * jnp.exp2(a_tot - m1_hi)], axis=0)
    upd = lax.dot_general(v, kh.astype(BF16), _TN, preferred_element_type=F32)
    return s1, s0, inter, st * jnp.exp2(a_tot) + upd


def _scan_output(s1, s0, inter, v, mask1, reverse):
    c = s1.shape[0]
    half = c // 2
    lo_rows, hi_rows = slice(0, half), slice(half, c)
    q_rows, k_rows = (lo_rows, hi_rows) if reverse else (hi_rows, lo_rows)
    intra = jnp.dot(jnp.where(mask1, s1, 0.0).astype(BF16), v, preferred_element_type=F32)
    cross = jnp.dot(s0.astype(BF16), v[k_rows], preferred_element_type=F32)
    o = inter + intra
    o_q = o[q_rows] + cross
    return jnp.concatenate([o_q, o[hi_rows]] if reverse else [o[lo_rows], o_q], axis=0)


SCAN_GROUP = 16


def _scan_kernel(*refs, reverse, mode, n_blocks):
    k_ref, v_ref, lf_ref, s0_ref = refs[:4]
    if mode == "state":
        (sfin_ref, st_ref) = refs[4:]
    elif mode == "out":
        (q_ref, o_ref, st_ref) = refs[4:]
    else:
        (q_ref, ob_ref, g_ref, ng_ref, o_ref, st_ref) = refs[4:]
    c = SCAN_CHUNK
    n_batch, rows = k_ref.shape[0], k_ref.shape[1]
    n_chunks = rows // c
    n_heads = k_ref.shape[2] // HEAD_DIM
    step = pl.program_id(0)

    @pl.when(step == 0)
    def _():
        st_ref[...] = s0_ref[...]

    t = lax.broadcasted_iota(jnp.int32, (c, c), 0)
    s = lax.broadcasted_iota(jnp.int32, (c, c), 1)
    half = c // 2
    before = (s >= t) if reverse else (s <= t)
    mask1 = ((t < half) == (s < half)) & before
    tri = jnp.where(before, 1.0, 0.0).astype(BF16)
    tri2 = jnp.concatenate([tri, tri], axis=1)
    recurrences = [(h, bi) for h in range(n_heads) for bi in range(n_batch)]

    def chunk_body(ci, carry):
        cidx = (n_chunks - 1 - ci) if reverse else ci
        r0 = pl.multiple_of(cidx * c, c)
        tile = lambda ref, h, bi: ref[bi, pl.ds(r0, c), h * HEAD_DIM:(h + 1) * HEAD_DIM]
        for g0 in range(0, len(recurrences), SCAN_GROUP):
            group = recurrences[g0:g0 + SCAN_GROUP]
            a_all = [_scan_cumsum(tile(lf_ref, h, bi), tri2) for h, bi in group]
            if mode == "state":
                for (h, bi), a in zip(group, a_all):
                    a_tot = _scan_refs(a, reverse)[3]
                    kh = (tile(k_ref, h, bi).astype(F32) * jnp.exp2(a_tot - a)).astype(BF16)
                    upd = lax.dot_general(tile(v_ref, h, bi), kh, _TN, preferred_element_type=F32)
                    st_ref[bi, h] = st_ref[bi, h] * jnp.exp2(a_tot) + upd
                continue
            wave = []
            for (h, bi), a in zip(group, a_all):
                s1, s0, inter, st_new = _scan_products(
                    a, tile(q_ref, h, bi), tile(k_ref, h, bi), tile(v_ref, h, bi), st_ref[bi, h], reverse)
                st_ref[bi, h] = st_new
                wave.append((s1, s0, inter))
            for (h, bi), (s1, s0, inter) in zip(group, wave):
                lanes = slice(h * HEAD_DIM, (h + 1) * HEAD_DIM)
                o = _scan_output(s1, s0, inter, tile(v_ref, h, bi), mask1, reverse)
                if mode == "out":
                    o_ref[bi, pl.ds(r0, c), lanes] = o
                else:
                    o = o + tile(ob_ref, h, bi)
                    ms = jnp.mean(o * o, axis=-1, keepdims=True)
                    y = o * lax.rsqrt(ms + EPS) * ng_ref[:, lanes]
                    o_ref[bi, pl.ds(r0, c), lanes] = (y * tile(g_ref, h, bi).astype(F32)).astype(BF16)
        return carry

    lax.fori_loop(0, n_chunks, chunk_body, 0)

    if mode == "state":
        @pl.when(step == n_blocks - 1)
        def _():
            sfin_ref[...] = st_ref[...]


def _scan_call(k, v, lf, s0, *, reverse, mode, q=None, ob=None, g=None, ng=None):
    b, l, dh = k.shape
    tb = min(SCAN_BLOCK, l)
    n_blocks = l // tb
    if reverse:
        seq = lambda j: (0, n_blocks - 1 - j, 0)
    else:
        seq = lambda j: (0, j, 0)
    seq_spec = pl.BlockSpec((b, tb, dh), seq)
    st_spec = pl.BlockSpec(s0.shape, lambda j: (0, 0, 0, 0))
    in_specs = [seq_spec, seq_spec, seq_spec, st_spec]
    args = [k, v, lf, s0]
    if mode == "state":
        out_specs, out_shape = st_spec, jax.ShapeDtypeStruct(s0.shape, F32)
    else:
        in_specs.append(seq_spec)
        args.append(q)
        if mode == "out":
            out_shape = jax.ShapeDtypeStruct((b, l, dh), F32)
        else:
            in_specs += [seq_spec, seq_spec, pl.BlockSpec((1, dh), lambda j: (0, 0))]
            args += [ob, g, ng]
            out_shape = jax.ShapeDtypeStruct((b, l, dh), BF16)
        out_specs = seq_spec
    return pl.pallas_call(
        functools.partial(_scan_kernel, reverse=reverse, mode=mode, n_blocks=n_blocks),
        grid=(n_blocks,),
        in_specs=in_specs,
        out_specs=out_specs,
        out_shape=out_shape,
        scratch_shapes=[pltpu.VMEM(s0.shape, F32)],
        compiler_params=_params(("arbitrary",)),
        name="scan_" + mode + ("_bwd" if reverse else "_fwd"),
    )(*args)


CONV_TILE_ROWS = 16
_HPAD = 16
_HROW = GRID_W + 2 * _HPAD


def _conv_kernel(cur_ref, prev_ref, next_ref, w_ref, b_ref, lg_ref, lb_ref, o_ref,
                 hpad_ref, vpad_ref, y_ref, *, n_tiles):
    tr = CONV_TILE_ROWS
    half = w_ref.shape[1] // 2
    n_lane_tiles = half // 128
    i = pl.program_id(1)

    hpad_ref[:, :, :_HPAD, :] = jnp.zeros((tr, n_lane_tiles, _HPAD, 128), F32)
    hpad_ref[:, :, _HPAD + GRID_W:, :] = jnp.zeros((tr, n_lane_tiles, _HPAD, 128), F32)
    for r in range(tr):
        for lt in range(n_lane_tiles):
            hpad_ref[r, lt, _HPAD:_HPAD + GRID_W, :] = cur_ref[0, r * GRID_W:(r + 1) * GRID_W,
                                                               lt * 128:(lt + 1) * 128]
    halo = CONV_PAD * GRID_W
    for lt in range(n_lane_tiles):
        lanes = slice(lt * 128, (lt + 1) * 128)
        top = prev_ref[0, tr * GRID_W - halo:, lanes]
        bot = next_ref[0, :halo, lanes]
        vpad_ref[lt, :halo, :] = jnp.where(i > 0, top, 0.0)
        vpad_ref[lt, halo:halo + tr * GRID_W, :] = cur_ref[0, :, half + lt * 128:half + (lt + 1) * 128]
        vpad_ref[lt, halo + tr * GRID_W:, :] = jnp.where(i < n_tiles - 1, bot, 0.0)

    def h_row(r, carry):
        o0 = pl.multiple_of(r * GRID_W, GRID_W)
        for lt in range(n_lane_tiles):
            lanes = slice(lt * 128, (lt + 1) * 128)
            acc = jnp.zeros((GRID_W, 128), F32)
            for jj in range(CONV_WIDTH):
                off = _HPAD - CONV_PAD + jj
                acc = acc + hpad_ref[r, lt, off:off + GRID_W, :] * w_ref[jj:jj + 1, lanes]
            y_ref[pl.ds(o0, GRID_W), lanes] = acc + b_ref[:, lanes]
        return carry

    lax.fori_loop(0, tr, h_row, 0)

    n_sub = GRID_W // 8
    for lt in range(n_lane_tiles):
        vl = slice(half + lt * 128, half + (lt + 1) * 128)
        wv = [jnp.broadcast_to(w_ref[jj:jj + 1, vl], (8, 128)) for jj in range(CONV_WIDTH)]
        bias = b_ref[:, vl]

        def v_row(r, carry, lt=lt, vl=vl, wv=wv, bias=bias):
            v0 = pl.multiple_of(r * GRID_W, GRID_W)
            acc = [jnp.zeros((8, 128), F32)] * n_sub
            for jj in range(CONV_WIDTH):
                src = pl.multiple_of(v0 + jj * GRID_W, GRID_W)
                tap = vpad_ref[lt, pl.ds(src, GRID_W), :]
                acc = [acc[s] + tap[8 * s:8 * s + 8, :] * wv[jj] for s in range(n_sub)]
            y_ref[pl.ds(v0, GRID_W), vl] = jnp.concatenate(acc, axis=0) + bias
            return carry

        lax.fori_loop(0, tr, v_row, 0)

    def ln_row(r, carry):
        o0 = pl.multiple_of(r * GRID_W, GRID_W)
        y = y_ref[pl.ds(o0, GRID_W), :]
        mu = jnp.mean(y, axis=-1, keepdims=True)
        yc = y - mu
        var = jnp.mean(yc * yc, axis=-1, keepdims=True)
        z = yc * lax.rsqrt(var + EPS) * lg_ref[...] + lb_ref[...]
        o_ref[0, pl.ds(o0, GRID_W), :] = _silu(z).astype(BF16)
        return carry

    lax.fori_loop(0, tr, ln_row, 0, unroll=2)


def _conv_call(u, w, bias, ln_g, ln_b):
    b, l, ch = u.shape
    half = ch // 2
    tile = CONV_TILE_ROWS * GRID_W
    n_tiles = l // tile
    return pl.pallas_call(
        functools.partial(_conv_kernel, n_tiles=n_tiles),
        grid=(b, n_tiles),
        in_specs=[pl.BlockSpec((1, tile, ch), lambda i, j: (i, j, 0)),
                  pl.BlockSpec((1, tile, half), lambda i, j: (i, jnp.maximum(j - 1, 0), 1)),
                  pl.BlockSpec((1, tile, half), lambda i, j: (i, jnp.minimum(j + 1, n_tiles - 1), 1)),
                  pl.BlockSpec((CONV_WIDTH, ch), lambda i, j: (0, 0)),
                  pl.BlockSpec((1, ch), lambda i, j: (0, 0)),
                  pl.BlockSpec((1, ch), lambda i, j: (0, 0)),
                  pl.BlockSpec((1, ch), lambda i, j: (0, 0))],
        out_specs=pl.BlockSpec((1, tile, ch), lambda i, j: (i, j, 0)),
        out_shape=jax.ShapeDtypeStruct((b, l, ch), BF16),
        scratch_shapes=[pltpu.VMEM((CONV_TILE_ROWS, half // 128, _HROW, 128), F32),
                        pltpu.VMEM((half // 128, tile + 2 * CONV_PAD * GRID_W, 128), F32),
                        pltpu.VMEM((tile, ch), F32)],
        compiler_params=_params(("parallel", "parallel")),
        name="conv",
    )(u, u, u, w, bias, ln_g, ln_b)


def _mix_kernel(uc_ref, og_ref, gc_ref, gh_ref, x_ref, wpw_ref, wh_ref, wo_ref,
                npost_ref, npre_ref, mod_ref, x1_ref, h2_ref):
    tm = x_ref.shape[1]
    n_split = 2
    for r in range(n_split):
        rows = slice(r * tm // n_split, (r + 1) * tm // n_split)
        y_c = jnp.dot(uc_ref[0, rows, :], wpw_ref[...], preferred_element_type=F32)
        y_h = jnp.dot(og_ref[0, rows, :], wh_ref[...], preferred_element_type=F32)
        z = gc_ref[0, rows, :].astype(F32) * y_c + gh_ref[0, rows, :].astype(F32) * y_h
        y = jnp.dot(z.astype(BF16), wo_ref[...], preferred_element_type=F32)
        ms = jnp.mean(y * y, axis=-1, keepdims=True)
        yn = y * lax.rsqrt(ms + EPS) * npost_ref[...]
        x1 = x_ref[0, rows, :] + mod_ref[0, 2:3, :] * yn
        x1_ref[0, rows, :] = x1
        ms1 = jnp.mean(x1 * x1, axis=-1, keepdims=True)
        hn = x1 * lax.rsqrt(ms1 + EPS) * npre_ref[...]
        h2_ref[0, rows, :] = (hn * (1.0 + mod_ref[0, 4:5, :]) + mod_ref[0, 3:4, :]).astype(BF16)


def _mix_call(uc, og, gc, gh, x, wpw, wh, wo, npost, npre, mod3, tm=256):
    b, l, d = x.shape
    dc = uc.shape[2]
    row = lambda width: pl.BlockSpec((1, tm, width), lambda i, j: (i, j, 0))
    const = lambda shape: pl.BlockSpec(shape, lambda i, j: (0,) * len(shape), pipeline_mode=pl.Buffered(1))
    return pl.pallas_call(
        _mix_kernel,
        grid=(b, l // tm),
        in_specs=[row(dc), row(dc), row(d), row(d), row(d),
                  const(wpw.shape), const(wh.shape), const(wo.shape),
                  const((1, d)), const((1, d)),
                  pl.BlockSpec((1, N_MOD, d), lambda i, j: (i, 0, 0))],
        out_specs=[row(d), row(d)],
        out_shape=[jax.ShapeDtypeStruct((b, l, d), F32), jax.ShapeDtypeStruct((b, l, d), BF16)],
        compiler_params=_params(("parallel", "parallel")),
        name="mix",
    )(uc, og, gc, gh, x, wpw, wh, wo, npost, npre, mod3)


def _mlp_kernel(h_ref, w1_ref, w2_ref, x1_ref, npost_ref, mod_ref, o_ref, *, n_ff):
    j = pl.program_id(2)

    @pl.when(j == 0)
    def _():
        o_ref[...] = jnp.zeros(o_ref.shape, F32)

    a = jnp.dot(h_ref[0], w1_ref[...], preferred_element_type=F32)
    a = jnp.square(jnp.maximum(a, 0.0)).astype(BF16)
    o_ref[0] += jnp.dot(a, w2_ref[...], preferred_element_type=F32)

    @pl.when(j == n_ff - 1)
    def _():
        y = o_ref[0]
        ms = jnp.mean(y * y, axis=-1, keepdims=True)
        yn = y * lax.rsqrt(ms + EPS) * npost_ref[...]
        o_ref[0] = x1_ref[0] + mod_ref[0, 5:6, :] * yn


def _mlp_call(h2, w1, w2, x1, npost, mod3, tm=1024, tf=512):
    b, l, d = x1.shape
    dff = w1.shape[1]
    n_ff = dff // tf
    return pl.pallas_call(
        functools.partial(_mlp_kernel, n_ff=n_ff),
        grid=(b, l // tm, n_ff),
        in_specs=[pl.BlockSpec((1, tm, d), lambda i, m, j: (i, m, 0)),
                  pl.BlockSpec((d, tf), lambda i, m, j: (0, j)),
                  pl.BlockSpec((tf, d), lambda i, m, j: (j, 0)),
                  pl.BlockSpec((1, tm, d), lambda i, m, j: (i, m, 0), pipeline_mode=pl.Buffered(1)),
                  pl.BlockSpec((1, d), lambda i, m, j: (0, 0)),
                  pl.BlockSpec((1, N_MOD, d), lambda i, m, j: (i, 0, 0))],
        out_specs=pl.BlockSpec((1, tm, d), lambda i, m, j: (i, m, 0)),
        out_shape=jax.ShapeDtypeStruct((b, l, d), F32),
        compiler_params=_params(("parallel", "parallel", "arbitrary")),
        name="mlp",
    )(h2, w1, w2, x1, npost, mod3)


def kernel(x, c, ctx, c_ctx, w_mod, b_mod, norm_pre_mix, norm_post_mix, norm_pre_mlp, norm_post_mlp, w_in,
           conv_dw_w, conv_dw_b, conv_ln_g, conv_ln_b, conv_pw_w, hgrn_lb_logits, hgrn_norm_g, hgrn_out_w,
           w_out, mlp_w1, mlp_w2):
    assert w_mod.shape[0] == 1, "single-layer block"
    b, l, d = x.shape
    lc = ctx.shape[1]
    dh = hgrn_norm_g.shape[1]
    dc = conv_dw_w.shape[2]
    n_heads = dh // HEAD_DIM
    assert l % (CONV_TILE_ROWS * GRID_W) == 0 and CONV_PAD <= CONV_TILE_ROWS and CONV_PAD <= _HPAD
    assert l % SCAN_BLOCK == 0 and lc % SCAN_CHUNK == 0

    mod_rows = 16
    cc = jnp.concatenate([c, c_ctx[None, :], jnp.zeros((mod_rows - b - 1, d), F32)], axis=0)
    mod3 = _mod_call(cc, w_mod[0], b_mod).reshape(mod_rows, N_MOD, d)

    w_in0 = w_in[0]
    lb_f, lb_b = hgrn_lb_logits[0], hgrn_lb_logits[1]
    off = lambda n: n * dh

    hc = _prenorm_call(ctx, norm_pre_mix, mod3, lambda i: b, 0, tl=lc).reshape(b * lc, d)
    seq_c = lambda t: t.reshape(b, lc, dh)
    (vc,) = _proj_call(hc, w_in0, [off(0)], dh, _ep_identity, [BF16], name="proj_ctx_v")
    lfc_f, kc_f = _proj_call(hc, w_in0, [off(1)], dh, _ep_forget, [F32, BF16], extra=(lb_f,), name="proj_ctx_ff")
    lfc_b, kc_b = _proj_call(hc, w_in0, [off(2)], dh, _ep_forget, [F32, BF16], extra=(lb_b,), name="proj_ctx_fb")
    zeros = jnp.zeros((b, n_heads, HEAD_DIM, HEAD_DIM), F32)
    s_f = _scan_call(seq_c(kc_f), seq_c(vc), seq_c(lfc_f), zeros, reverse=False, mode="state")
    s_b = _scan_call(seq_c(kc_b), seq_c(vc), seq_c(lfc_b), zeros, reverse=True, mode="state")

    h, v = _prenorm_v_call(x, norm_pre_mix, mod3, w_in0, dh)
    h = h.reshape(b * l, d)
    seq = lambda t: t.reshape(b, l, t.shape[-1])
    lf_f, k_f = _proj_call(h, w_in0, [off(1)], dh, _ep_forget, [F32, BF16], extra=(lb_f,), name="proj_ff")
    lf_b, k_b = _proj_call(h, w_in0, [off(2)], dh, _ep_forget, [F32, BF16], extra=(lb_b,), name="proj_fb")
    q, wpw = _proj_call(h, w_in0, [off(3)], dh, _ep_q, [BF16], side=(conv_pw_w[0],), name="proj_q")
    g, wh = _proj_call(h, w_in0, [off(4)], dh, _ep_silu, [BF16], side=(hgrn_out_w[0],), name="proj_g")
    u, wo = _proj_call(h, w_in0, [off(5), off(5) + dc], dc, _ep_glu, [F32], side=(w_out[0],), tn=512,
                       name="proj_glu")
    gc, w1 = _proj_call(h, w_in0, [off(5) + 2 * dc], d, _ep_sigmoid, [BF16], side=(mlp_w1[0],), name="proj_gc")
    gh, w2 = _proj_call(h, w_in0, [off(5) + 2 * dc + d], d, _ep_sigmoid, [BF16], side=(mlp_w2[0],),
                        name="proj_gh")

    o_b = _scan_call(seq(k_b), v, seq(lf_b), s_b, reverse=True, mode="out", q=seq(q))
    og = _scan_call(seq(k_f), v, seq(lf_f), s_f, reverse=False, mode="readout", q=seq(q),
                    ob=o_b, g=seq(g), ng=hgrn_norm_g)

    uc = _conv_call(seq(u), conv_dw_w[0], conv_dw_b, conv_ln_g, conv_ln_b)

    x1, h2 = _mix_call(uc, og, seq(gc), seq(gh), x, wpw, wh, wo, norm_post_mix, norm_pre_mlp, mod3)
    return _mlp_call(h2, w1, w2, x1, norm_post_mlp, mod3)
```

```python
import functools

import jax
import jax.numpy as jnp
from jax import lax
from jax.experimental import pallas as pl
from jax.experimental.pallas import tpu as pltpu

F32 = jnp.float32
BF16 = jnp.bfloat16

GRID_W = 64
CONV_WIDTH = 31
CONV_PAD = (CONV_WIDTH - 1) // 2
HEAD_DIM = 128
N_MOD = 6
EPS = 1e-6

SCAN_CHUNK = 64
SCAN_BLOCK = 512

VMEM_LIMIT = 56 * 1024 * 1024

_NT = (((1,), (1,)), ((), ()))
_TN = (((0,), (0,)), ((), ()))


def _params(sem):
    return pltpu.CompilerParams(dimension_semantics=sem, vmem_limit_bytes=VMEM_LIMIT)


def _sigmoid(x):
    return 0.5 * jnp.tanh(0.5 * x) + 0.5


def _silu(x):
    return x * _sigmoid(x)


def _norm_modulate(x, gain, shift, scale):
    ms = jnp.mean(x * x, axis=-1, keepdims=True)
    return (x * lax.rsqrt(ms + EPS) * gain) * (1.0 + scale) + shift


def _forget_gate(raw, logits):
    e = jnp.exp(logits - jnp.max(logits, axis=0, keepdims=True))
    lb = e[0:1] / jnp.sum(e, axis=0, keepdims=True)
    return lb + (1.0 - lb) * _sigmoid(raw)


def _mod_kernel(c_ref, w_ref, b_ref, o_ref):
    s = _silu(c_ref[...]).astype(BF16)
    o_ref[...] = jnp.dot(s, w_ref[...].astype(BF16), preferred_element_type=F32) + b_ref[...]


def _mod_call(cc, w_mod, b_mod, tn=1024):
    rows, d = cc.shape
    n = w_mod.shape[1]
    return pl.pallas_call(
        _mod_kernel,
        grid=(n // tn,),
        in_specs=[pl.BlockSpec((rows, d), lambda j: (0, 0)),
                  pl.BlockSpec((d, tn), lambda j: (0, j)),
                  pl.BlockSpec((1, tn), lambda j: (0, j))],
        out_specs=pl.BlockSpec((rows, tn), lambda j: (0, j)),
        out_shape=jax.ShapeDtypeStruct((rows, n), F32),
        compiler_params=_params(("parallel",)),
        name="mod",
    )(cc, w_mod, b_mod)


def _proj_kernel(*refs, n_w, n_extra, n_side, n_out, epilogue):
    h_ref = refs[0]
    pos = 1
    w_refs = refs[pos:pos + n_w]
    pos += n_w
    extra = refs[pos:pos + n_extra]
    pos += n_extra
    side_in = refs[pos:pos + n_side]
    pos += n_side
    out_refs = refs[pos:pos + n_out]
    pos += n_out
    side_out = refs[pos:pos + n_side]
    pos += n_side
    wbf_refs = refs[pos:]

    @pl.when(pl.program_id(1) == 0)
    def _():
        for w_ref, wbf_ref in zip(w_refs, wbf_refs):
            wbf_ref[...] = w_ref[...].astype(BF16)

    for si_ref, so_ref in zip(side_in, side_out):
        so_ref[...] = si_ref[...].astype(BF16)

    h = h_ref[...]
    accs = [jnp.dot(h, wbf_ref[...], preferred_element_type=F32) for wbf_ref in wbf_refs]
    outs = epilogue(accs, extra)
    for o_ref, o in zip(out_refs, outs):
        o_ref[...] = o.astype(o_ref.dtype)


def _ep_forget(accs, extra):
    f = _forget_gate(accs[0], extra[0][...])
    return jnp.log2(f), 1.0 - f


def _ep_q(accs, extra):
    return (_silu(accs[0]) * (HEAD_DIM ** -0.5),)


def _ep_silu(accs, extra):
    return (_silu(accs[0]),)


def _ep_glu(accs, extra):
    return (accs[0] * _sigmoid(accs[1]),)


def _ep_sigmoid(accs, extra):
    return (_sigmoid(accs[0]),)


def _proj_call(h, w, col_offsets, width, epilogue, out_dtypes, extra=(), side=(), tm=1024, tn=1024,
               name="proj"):
    m, k = h.shape
    tm = min(tm, m)
    n_w = len(col_offsets)
    n_j, n_i = width // tn, m // tm
    in_specs = [pl.BlockSpec((tm, k), lambda j, i: (i, 0))]
    for off in col_offsets:
        in_specs.append(pl.BlockSpec((k, tn), lambda j, i, off=off: (0, off // tn + j)))
    for e in extra:
        in_specs.append(pl.BlockSpec((e.shape[0], tn), lambda j, i: (0, j)))
    side_specs = [pl.BlockSpec((a.shape[0] // (n_j * n_i), a.shape[1]), lambda j, i: (j * n_i + i, 0))
                  for a in side]
    return pl.pallas_call(
        functools.partial(_proj_kernel, n_w=n_w, n_extra=len(extra), n_side=len(side),
                          n_out=len(out_dtypes), epilogue=epilogue),
        grid=(n_j, n_i),
        in_specs=in_specs + side_specs,
        out_specs=[pl.BlockSpec((tm, tn), lambda j, i: (i, j)) for _ in out_dtypes] + side_specs,
        out_shape=[jax.ShapeDtypeStruct((m, width), dt) for dt in out_dtypes]
                  + [jax.ShapeDtypeStruct(a.shape, BF16) for a in side],
        scratch_shapes=[pltpu.VMEM((k, tn), BF16) for _ in range(n_w)],
        compiler_params=_params(("parallel", "arbitrary")),
        name=name,
    )(h, *([w] * n_w), *extra, *side)


def _prenorm_v_kernel(x_ref, g_ref, mod_ref, w_ref, h_ref, v_ref, wbf_ref):
    @pl.when((pl.program_id(0) == 0) & (pl.program_id(1) == 0))
    def _():
        wbf_ref[...] = w_ref[...].astype(BF16)

    h = _norm_modulate(x_ref[0], g_ref[...], mod_ref[0, 0:1, :], mod_ref[0, 1:2, :]).astype(BF16)
    h_ref[0] = h
    v_ref[0] = jnp.dot(h, wbf_ref[...], preferred_element_type=F32).astype(BF16)


def _prenorm_v_call(x, g, mod3, w, width, tl=512):
    b, l, d = x.shape
    return pl.pallas_call(
        _prenorm_v_kernel,
        grid=(b, l // tl),
        in_specs=[pl.BlockSpec((1, tl, d), lambda i, j: (i, j, 0)),
                  pl.BlockSpec((1, d), lambda i, j: (0, 0)),
                  pl.BlockSpec((1, N_MOD, d), lambda i, j: (i, 0, 0)),
                  pl.BlockSpec((d, width), lambda i, j: (0, 0), pipeline_mode=pl.Buffered(1))],
        out_specs=[pl.BlockSpec((1, tl, d), lambda i, j: (i, j, 0)),
                   pl.BlockSpec((1, tl, width), lambda i, j: (i, j, 0))],
        out_shape=[jax.ShapeDtypeStruct((b, l, d), BF16), jax.ShapeDtypeStruct((b, l, width), BF16)],
        scratch_shapes=[pltpu.VMEM((d, width), BF16)],
        compiler_params=_params(("arbitrary", "arbitrary")),
        name="prenorm_v",
    )(x, g, mod3, w)


def _scan_masks(c, reverse):
    t = lax.broadcasted_iota(jnp.int32, (c, c), 0)
    s = lax.broadcasted_iota(jnp.int32, (c, c), 1)
    half = c // 2
    before = (s >= t) if reverse else (s <= t)
    mask1 = ((t < half) == (s < half)) & before
    tri = jnp.where(before, 1.0, 0.0).astype(BF16)
    return mask1, jnp.concatenate([tri, tri], axis=1)


def _scan_cumsum(lf, tri2):
    hi = lf.astype(BF16)
    lo = (lf - hi.astype(F32)).astype(BF16)
    return jnp.dot(tri2, jnp.concatenate([hi, lo], axis=0), preferred_element_type=F32)


def _scan_refs(a, reverse):
    c = a.shape[0]
    half, quarter = c // 2, c // 4
    if reverse:
        return a[quarter:quarter + 1], a[half + quarter:half + quarter + 1], a[half:half + 1], a[0:1]
    return a[quarter - 1:quarter], a[half + quarter - 1:half + quarter], a[half - 1:half], a[c - 1:c]


def _scan_products(a, q, k, v, st, st_kv, reverse):
    c = a.shape[0]
    half = c // 2
    m1_lo, m1_hi, m0, a_tot = _scan_refs(a, reverse)
    lo_rows, hi_rows = slice(0, half), slice(half, c)
    row = lax.broadcasted_iota(jnp.int32, a.shape, 0)
    e1 = a - jnp.where(row < half, m1_lo, m1_hi)
    pq = q.astype(F32) * jnp.exp2(e1)
    pk = k.astype(F32) * jnp.exp2(-e1)
    s1 = lax.dot_general(pq.astype(BF16), pk.astype(BF16), _NT, preferred_element_type=F32)
    if reverse:
        q_rows, m1_q, k_rows, m1_k = lo_rows, m1_lo, hi_rows, m1_hi
    else:
        q_rows, m1_q, k_rows, m1_k = hi_rows, m1_hi, lo_rows, m1_lo
    q0 = (pq[q_rows] * jnp.exp2(m1_q - m0)).astype(BF16)
    k0 = (pk[k_rows] * jnp.exp2(m0 - m1_k)).astype(BF16)
    s0 = lax.dot_general(q0, k0, _NT, preferred_element_type=F32)
    qi = jnp.concatenate([pq[lo_rows] * jnp.exp2(m1_lo), pq[hi_rows] * jnp.exp2(m1_hi)], axis=0)
    inter = jnp.dot(qi.astype(BF16), st_kv, preferred_element_type=F32)
    kh = jnp.concatenate([pk[lo_rows] * jnp.exp2(a_tot - m1_lo), pk[hi_rows] * jnp.exp2(a_tot - m1_hi)], axis=0)
    upd = lax.dot_general(v, kh.astype(BF16), _TN, preferred_element_type=F32)
    return s1, s0, inter, st * jnp.exp2(a_tot) + upd


def _scan_output(s1, s0, inter, v, mask1, reverse):
    c = s1.shape[0]
    half = c // 2
    lo_rows, hi_rows = slice(0, half), slice(half, c)
    q_rows, k_rows = (lo_rows, hi_rows) if reverse else (hi_rows, lo_rows)
    intra = jnp.dot(jnp.where(mask1, s1, 0.0).astype(BF16), v, preferred_element_type=F32)
    cross = jnp.dot(s0.astype(BF16), v[k_rows], preferred_element_type=F32)
    o = inter + intra
    o_q = o[q_rows] + cross
    return jnp.concatenate([o_q, o[hi_rows]] if reverse else [o[lo_rows], o_q], axis=0)


def _scan_kernel(*refs, reverse, readout):
    k_ref, v_ref, lf_ref, s0_ref, q_ref = refs[:5]
    if readout:
        (ob_ref, g_ref, ng_ref, o_ref, st_ref, stkv_ref, a_ref) = refs[5:]
    else:
        (o_ref, st_ref, stkv_ref, a_ref) = refs[5:]
    c = SCAN_CHUNK
    n_batch, rows = k_ref.shape[0], k_ref.shape[1]
    n_chunks = rows // c
    n_heads = k_ref.shape[2] // HEAD_DIM
    recurrences = [(h, bi) for h in range(n_heads) for bi in range(n_batch)]

    @pl.when(pl.program_id(0) == 0)
    def _():
        st_ref[...] = s0_ref[...]
        for h, bi in recurrences:
            stkv_ref[bi, h] = s0_ref[bi, h].T.astype(BF16)

    mask1, tri2 = _scan_masks(c, reverse)

    def cumsum_body(ci, carry):
        r0 = pl.multiple_of(ci * c, c)
        for bi in range(n_batch):
            a_ref[bi, pl.ds(r0, c), :] = _scan_cumsum(lf_ref[bi, pl.ds(r0, c), :], tri2)
        return carry

    lax.fori_loop(0, n_chunks, cumsum_body, 0, unroll=4)

    def chunk_body(ci, carry):
        cidx = (n_chunks - 1 - ci) if reverse else ci
        r0 = pl.multiple_of(cidx * c, c)
        tile = lambda ref, h, bi: ref[bi, pl.ds(r0, c), h * HEAD_DIM:(h + 1) * HEAD_DIM]
        wave = []
        for h, bi in recurrences:
            s1, s0, inter, st_new = _scan_products(
                tile(a_ref, h, bi), tile(q_ref, h, bi), tile(k_ref, h, bi), tile(v_ref, h, bi),
                st_ref[bi, h], stkv_ref[bi, h], reverse)
            st_ref[bi, h] = st_new
            stkv_ref[bi, h] = st_new.T.astype(BF16)
            wave.append((s1, s0, inter))
        for (h, bi), (s1, s0, inter) in zip(recurrences, wave):
            lanes = slice(h * HEAD_DIM, (h + 1) * HEAD_DIM)
            o = _scan_output(s1, s0, inter, tile(v_ref, h, bi), mask1, reverse)
            if readout:
                o = o + tile(ob_ref, h, bi)
                ms = jnp.mean(o * o, axis=-1, keepdims=True)
                y = o * lax.rsqrt(ms + EPS) * ng_ref[:, lanes]
                o = y * tile(g_ref, h, bi).astype(F32)
            o_ref[bi, pl.ds(r0, c), lanes] = o.astype(o_ref.dtype)
        return carry

    lax.fori_loop(0, n_chunks, chunk_body, 0, unroll=2)


def _scan_call(k, v, lf, s0, q, *, reverse, ob=None, g=None, ng=None):
    b, l, dh = k.shape
    tb = min(SCAN_BLOCK, l)
    n_blocks = l // tb
    readout = ob is not None
    if reverse:
        seq = lambda j: (0, n_blocks - 1 - j, 0)
    else:
        seq = lambda j: (0, j, 0)
    seq_spec = pl.BlockSpec((b, tb, dh), seq)
    in_specs = [seq_spec, seq_spec, seq_spec, pl.BlockSpec(s0.shape, lambda j: (0, 0, 0, 0)), seq_spec]
    args = [k, v, lf, s0, q]
    if readout:
        in_specs += [seq_spec, seq_spec, pl.BlockSpec((1, dh), lambda j: (0, 0))]
        args += [ob, g, ng]
    return pl.pallas_call(
        functools.partial(_scan_kernel, reverse=reverse, readout=readout),
        grid=(n_blocks,),
        in_specs=in_specs,
        out_specs=seq_spec,
        out_shape=jax.ShapeDtypeStruct((b, l, dh), BF16 if readout else F32),
        scratch_shapes=[pltpu.VMEM(s0.shape, F32), pltpu.VMEM(s0.shape, BF16), pltpu.VMEM((b, tb, dh), F32)],
        compiler_params=_params(("arbitrary",)),
        name="scan_readout_fwd" if readout else "scan_out_bwd",
    )(*args)


def _ctx_proj_kernel(ctx_ref, g_ref, mod_ref, w_ref, o_ref, hc_ref):
    @pl.when(pl.program_id(0) == 0)
    def _():
        for bi in range(ctx_ref.shape[0]):
            hc_ref[bi] = _norm_modulate(ctx_ref[bi], g_ref[...], mod_ref[0, 0:1, :],
                                        mod_ref[0, 1:2, :]).astype(BF16)

    w = w_ref[...].astype(BF16)
    for bi in range(ctx_ref.shape[0]):
        o_ref[bi] = jnp.dot(hc_ref[bi], w, preferred_element_type=F32)


def _ctx_proj_call(ctx, g, mod3, mod_row, w, width, tn=512):
    b, lc, d = ctx.shape
    return pl.pallas_call(
        _ctx_proj_kernel,
        grid=(width // tn,),
        in_specs=[pl.BlockSpec((b, lc, d), lambda j: (0, 0, 0)),
                  pl.BlockSpec((1, d), lambda j: (0, 0)),
                  pl.BlockSpec((1, N_MOD, d), lambda j: (mod_row, 0, 0)),
                  pl.BlockSpec((d, tn), lambda j: (0, j))],
        out_specs=pl.BlockSpec((b, lc, tn), lambda j: (0, 0, j)),
        out_shape=jax.ShapeDtypeStruct((b, lc, width), F32),
        scratch_shapes=[pltpu.VMEM((b, lc, d), BF16)],
        compiler_params=_params(("arbitrary",)),
        name="ctx_proj",
    )(ctx, g, mod3, w)


def _ctx_state_kernel(pv_ref, pf_ref, pb_ref, lgf_ref, lgb_ref, sf_ref, sb_ref):
    c = SCAN_CHUNK
    n_batch, rows = pv_ref.shape[0], pv_ref.shape[1]
    n_chunks = rows // c
    n_heads = pv_ref.shape[2] // HEAD_DIM
    for reverse, p_ref, lg_ref, s_ref in ((False, pf_ref, lgf_ref, sf_ref), (True, pb_ref, lgb_ref, sb_ref)):
        _, tri2 = _scan_masks(c, reverse)
        s_ref[...] = jnp.zeros(s_ref.shape, F32)

        def chunk_body(ci, carry, reverse=reverse, p_ref=p_ref, lg_ref=lg_ref, s_ref=s_ref, tri2=tri2):
            cidx = (n_chunks - 1 - ci) if reverse else ci
            r0 = pl.multiple_of(cidx * c, c)
            recurrences = [(h, bi) for h in range(n_heads) for bi in range(n_batch)]
            tile = lambda ref, h, bi: ref[bi, pl.ds(r0, c), h * HEAD_DIM:(h + 1) * HEAD_DIM]
            gates = [_forget_gate(tile(p_ref, h, bi), lg_ref[:, h * HEAD_DIM:(h + 1) * HEAD_DIM])
                     for h, bi in recurrences]
            sums = [_scan_cumsum(jnp.log2(f), tri2) for f in gates]
            for (h, bi), f, a in zip(recurrences, gates, sums):
                a_tot = _scan_refs(a, reverse)[3]
                kh = ((1.0 - f) * jnp.exp2(a_tot - a)).astype(BF16)
                upd = lax.dot_general(tile(pv_ref, h, bi).astype(BF16), kh, _TN, preferred_element_type=F32)
                s_ref[bi, h] = s_ref[bi, h] * jnp.exp2(a_tot) + upd
            return carry

        lax.fori_loop(0, n_chunks, chunk_body, 0)


def _ctx_state_call(pc, logits_f, logits_b, dh):
    b, lc, _ = pc.shape
    n_heads = dh // HEAD_DIM
    group = lambda n: pl.BlockSpec((b, lc, dh), lambda j: (0, 0, n))
    lg_spec = pl.BlockSpec(logits_f.shape, lambda j: (0, 0))
    st_spec = pl.BlockSpec((b, n_heads, HEAD_DIM, HEAD_DIM), lambda j: (0, 0, 0, 0))
    st_shape = jax.ShapeDtypeStruct((b, n_heads, HEAD_DIM, HEAD_DIM), F32)
    return pl.pallas_call(
        _ctx_state_kernel,
        grid=(1,),
        in_specs=[group(0), group(1), group(2), lg_spec, lg_spec],
        out_specs=[st_spec, st_spec],
        out_shape=[st_shape, st_shape],
        compiler_params=_params(("arbitrary",)),
        name="ctx_state",
    )(pc, pc, pc, logits_f, logits_b)


CONV_TILE_ROWS = 16
_HPAD = 16
_HROW = GRID_W + 2 * _HPAD


def _conv_kernel(cur_ref, prev_ref, next_ref, w_ref, b_ref, lg_ref, lb_ref, o_ref,
                 hpad_ref, vpad_ref, y_ref, *, n_tiles):
    tr = CONV_TILE_ROWS
    half = w_ref.shape[1] // 2
    n_lane_tiles = half // 128
    i = pl.program_id(1)

    hpad_ref[:, :, :_HPAD, :] = jnp.zeros((tr, n_lane_tiles, _HPAD, 128), F32)
    hpad_ref[:, :, _HPAD + GRID_W:, :] = jnp.zeros((tr, n_lane_tiles, _HPAD, 128), F32)
    for r in range(tr):
        for lt in range(n_lane_tiles):
            hpad_ref[r, lt, _HPAD:_HPAD + GRID_W, :] = cur_ref[0, r * GRID_W:(r + 1) * GRID_W,
                                                               lt * 128:(lt + 1) * 128]
    halo = CONV_PAD * GRID_W
    for lt in range(n_lane_tiles):
        lanes = slice(lt * 128, (lt + 1) * 128)
        top = prev_ref[0, tr * GRID_W - halo:, lanes]
        bot = next_ref[0, :halo, lanes]
        vpad_ref[lt, :halo, :] = jnp.where(i > 0, top, 0.0)
        vpad_ref[lt, halo:halo + tr * GRID_W, :] = cur_ref[0, :, half + lt * 128:half + (lt + 1) * 128]
        vpad_ref[lt, halo + tr * GRID_W:, :] = jnp.where(i < n_tiles - 1, bot, 0.0)

    def h_row(r, carry):
        o0 = pl.multiple_of(r * GRID_W, GRID_W)
        for lt in range(n_lane_tiles):
            lanes = slice(lt * 128, (lt + 1) * 128)
            acc = jnp.zeros((GRID_W, 128), F32)
            for jj in range(CONV_WIDTH):
                off = _HPAD - CONV_PAD + jj
                acc = acc + hpad_ref[r, lt, off:off + GRID_W, :] * w_ref[jj:jj + 1, lanes]
            y_ref[pl.ds(o0, GRID_W), lanes] = acc + b_ref[:, lanes]
        return carry

    lax.fori_loop(0, tr, h_row, 0)

    n_sub = GRID_W // 8
    for lt in range(n_lane_tiles):
        vl = slice(half + lt * 128, half + (lt + 1) * 128)
        wv = [jnp.broadcast_to(w_ref[jj:jj + 1, vl], (8, 128)) for jj in range(CONV_WIDTH)]
        bias = b_ref[:, vl]

        def v_row(r, carry, lt=lt, vl=vl, wv=wv, bias=bias):
            v0 = pl.multiple_of(r * GRID_W, GRID_W)
            acc = [jnp.zeros((8, 128), F32)] * n_sub
            for jj in range(CONV_WIDTH):
                src = pl.multiple_of(v0 + jj * GRID_W, GRID_W)
                tap = vpad_ref[lt, pl.ds(src, GRID_W), :]
                acc = [acc[s] + tap[8 * s:8 * s + 8, :] * wv[jj] for s in range(n_sub)]
            y_ref[pl.ds(v0, GRID_W), vl] = jnp.concatenate(acc, axis=0) + bias
            return carry

        lax.fori_loop(0, tr, v_row, 0)

    def ln_row(r, carry):
        o0 = pl.multiple_of(r * GRID_W, GRID_W)
        y = y_ref[pl.ds(o0, GRID_W), :]
        mu = jnp.mean(y, axis=-1, keepdims=True)
        yc = y - mu
        var = jnp.mean(yc * yc, axis=-1, keepdims=True)
        z = yc * lax.rsqrt(var + EPS) * lg_ref[...] + lb_ref[...]
        o_ref[0, pl.ds(o0, GRID_W), :] = _silu(z).astype(BF16)
        return carry

    lax.fori_loop(0, tr, ln_row, 0, unroll=2)


def _conv_call(u, w, bias, ln_g, ln_b):
    b, l, ch = u.shape
    half = ch // 2
    tile = CONV_TILE_ROWS * GRID_W
    n_tiles = l // tile
    return pl.pallas_call(
        functools.partial(_conv_kernel, n_tiles=n_tiles),
        grid=(b, n_tiles),
        in_specs=[pl.BlockSpec((1, tile, ch), lambda i, j: (i, j, 0)),
                  pl.BlockSpec((1, tile, half), lambda i, j: (i, jnp.maximum(j - 1, 0), 1)),
                  pl.BlockSpec((1, tile, half), lambda i, j: (i, jnp.minimum(j + 1, n_tiles - 1), 1)),
                  pl.BlockSpec((CONV_WIDTH, ch), lambda i, j: (0, 0)),
                  pl.BlockSpec((1, ch), lambda i, j: (0, 0)),
                  pl.BlockSpec((1, ch), lambda i, j: (0, 0)),
                  pl.BlockSpec((1, ch), lambda i, j: (0, 0))],
        out_specs=pl.BlockSpec((1, tile, ch), lambda i, j: (i, j, 0)),
        out_shape=jax.ShapeDtypeStruct((b, l, ch), BF16),
        scratch_shapes=[pltpu.VMEM((CONV_TILE_ROWS, half // 128, _HROW, 128), F32),
                        pltpu.VMEM((half // 128, tile + 2 * CONV_PAD * GRID_W, 128), F32),
                        pltpu.VMEM((tile, ch), F32)],
        compiler_params=_params(("parallel", "parallel")),
        name="conv",
    )(u, u, u, w, bias, ln_g, ln_b)


def _mix_kernel(uc_ref, og_ref, gc_ref, gh_ref, x_ref, wpw_ref, wh_ref, wo_ref,
                npost_ref, npre_ref, mod_ref, x1_ref, h2_ref):
    tm = x_ref.shape[1]
    n_split = 2
    for r in range(n_split):
        rows = slice(r * tm // n_split, (r + 1) * tm // n_split)
        y_c = jnp.dot(uc_ref[0, rows, :], wpw_ref[...], preferred_element_type=F32)
        y_h = jnp.dot(og_ref[0, rows, :], wh_ref[...], preferred_element_type=F32)
        z = gc_ref[0, rows, :].astype(F32) * y_c + gh_ref[0, rows, :].astype(F32) * y_h
        y = jnp.dot(z.astype(BF16), wo_ref[...], preferred_element_type=F32)
        ms = jnp.mean(y * y, axis=-1, keepdims=True)
        yn = y * lax.rsqrt(ms + EPS) * npost_ref[...]
        x1 = x_ref[0, rows, :] + mod_ref[0, 2:3, :] * yn
        x1_ref[0, rows, :] = x1
        h2_ref[0, rows, :] = _norm_modulate(x1, npre_ref[...], mod_ref[0, 3:4, :],
                                            mod_ref[0, 4:5, :]).astype(BF16)


def _mix_call(uc, og, gc, gh, x, wpw, wh, wo, npost, npre, mod3, tm=256):
    b, l, d = x.shape
    dc = uc.shape[2]
    row = lambda width: pl.BlockSpec((1, tm, width), lambda i, j: (i, j, 0))
    const = lambda shape: pl.BlockSpec(shape, lambda i, j: (0,) * len(shape), pipeline_mode=pl.Buffered(1))
    return pl.pallas_call(
        _mix_kernel,
        grid=(b, l // tm),
        in_specs=[row(dc), row(dc), row(d), row(d), row(d),
                  const(wpw.shape), const(wh.shape), const(wo.shape),
                  const((1, d)), const((1, d)),
                  pl.BlockSpec((1, N_MOD, d), lambda i, j: (i, 0, 0))],
        out_specs=[row(d), row(d)],
        out_shape=[jax.ShapeDtypeStruct((b, l, d), F32), jax.ShapeDtypeStruct((b, l, d), BF16)],
        compiler_params=_params(("parallel", "parallel")),
        name="mix",
    )(uc, og, gc, gh, x, wpw, wh, wo, npost, npre, mod3)


def _mlp_kernel(h_ref, w1_ref, w2_ref, x1_ref, npost_ref, mod_ref, o_ref, *, n_ff):
    j = pl.program_id(2)

    @pl.when(j == 0)
    def _():
        o_ref[...] = jnp.zeros(o_ref.shape, F32)

    a = jnp.dot(h_ref[0], w1_ref[...], preferred_element_type=F32)
    a = jnp.square(jnp.maximum(a, 0.0)).astype(BF16)
    o_ref[0] += jnp.dot(a, w2_ref[...], preferred_element_type=F32)

    @pl.when(j == n_ff - 1)
    def _():
        y = o_ref[0]
        ms = jnp.mean(y * y, axis=-1, keepdims=True)
        yn = y * lax.rsqrt(ms + EPS) * npost_ref[...]
        o_ref[0] = x1_ref[0] + mod_ref[0, 5:6, :] * yn


def _mlp_call(h2, w1, w2, x1, npost, mod3, tm=1024, tf=512):
    b, l, d = x1.shape
    dff = w1.shape[1]
    n_ff = dff // tf
    return pl.pallas_call(
        functools.partial(_mlp_kernel, n_ff=n_ff),
        grid=(b, l // tm, n_ff),
        in_specs=[pl.BlockSpec((1, tm, d), lambda i, m, j: (i, m, 0)),
                  pl.BlockSpec((d, tf), lambda i, m, j: (0, j)),
                  pl.BlockSpec((tf, d), lambda i, m, j: (j, 0)),
                  pl.BlockSpec((1, tm, d), lambda i, m, j: (i, m, 0), pipeline_mode=pl.Buffered(1)),
                  pl.BlockSpec((1, d), lambda i, m, j: (0, 0)),
                  pl.BlockSpec((1, N_MOD, d), lambda i, m, j: (i, 0, 0))],
        out_specs=pl.BlockSpec((1, tm, d), lambda i, m, j: (i, m, 0)),
        out_shape=jax.ShapeDtypeStruct((b, l, d), F32),
        compiler_params=_params(("parallel", "parallel", "arbitrary")),
        name="mlp",
    )(h2, w1, w2, x1, npost, mod3)


def kernel(x, c, ctx, c_ctx, w_mod, b_mod, norm_pre_mix, norm_post_mix, norm_pre_mlp, norm_post_mlp, w_in,
           conv_dw_w, conv_dw_b, conv_ln_g, conv_ln_b, conv_pw_w, hgrn_lb_logits, hgrn_norm_g, hgrn_out_w,
           w_out, mlp_w1, mlp_w2):
    assert w_mod.shape[0] == 1, "single-layer block"
    b, l, d = x.shape
    lc = ctx.shape[1]
    dh = hgrn_norm_g.shape[1]
    dc = conv_dw_w.shape[2]
    assert l % (CONV_TILE_ROWS * GRID_W) == 0 and CONV_PAD <= CONV_TILE_ROWS and CONV_PAD <= _HPAD
    assert l % SCAN_BLOCK == 0 and lc % SCAN_CHUNK == 0

    mod_rows = 16
    cc = jnp.concatenate([c, c_ctx[None, :], jnp.zeros((mod_rows - b - 1, d), F32)], axis=0)
    mod3 = _mod_call(cc, w_mod[0], b_mod).reshape(mod_rows, N_MOD, d)

    w_in0 = w_in[0]
    lb_f, lb_b = hgrn_lb_logits[0], hgrn_lb_logits[1]
    off = lambda n: n * dh

    pc = _ctx_proj_call(ctx, norm_pre_mix, mod3, b, w_in0, 3 * dh)
    s_f, s_b = _ctx_state_call(pc, lb_f, lb_b, dh)

    h, v = _prenorm_v_call(x, norm_pre_mix, mod3, w_in0, dh)
    h = h.reshape(b * l, d)
    seq = lambda t: t.reshape(b, l, t.shape[-1])
    lf_f, k_f = _proj_call(h, w_in0, [off(1)], dh, _ep_forget, [F32, BF16], extra=(lb_f,), name="proj_ff")
    lf_b, k_b = _proj_call(h, w_in0, [off(2)], dh, _ep_forget, [F32, BF16], extra=(lb_b,), name="proj_fb")
    q, wpw = _proj_call(h, w_in0, [off(3)], dh, _ep_q, [BF16], side=(conv_pw_w[0],), name="proj_q")
    g, wh = _proj_call(h, w_in0, [off(4)], dh, _ep_silu, [BF16], side=(hgrn_out_w[0],), name="proj_g")
    u, wo = _proj_call(h, w_in0, [off(5), off(5) + dc], dc, _ep_glu, [F32], side=(w_out[0],), tn=512,
                       name="proj_glu")
    gc, w1 = _proj_call(h, w_in0, [off(5) + 2 * dc], d, _ep_sigmoid, [BF16], side=(mlp_w1[0],), name="proj_gc")
    gh, w2 = _proj_call(h, w_in0, [off(5) + 2 * dc + d], d, _ep_sigmoid, [BF16], side=(mlp_w2[0],),
                        name="proj_gh")

    o_b = _scan_call(seq(k_b), v, seq(lf_b), s_b, seq(q), reverse=True)
    og = _scan_call(seq(k_f), v, seq(lf_f), s_f, seq(q), reverse=False, ob=o_b, g=seq(g), ng=hgrn_norm_g)

    uc = _conv_call(seq(u), conv_dw_w[0], conv_dw_b, conv_ln_g, conv_ln_b)

    x1, h2 = _mix_call(uc, og, seq(gc), seq(gh), x, wpw, wh, wo, norm_post_mix, norm_pre_mlp, mod3)
    return _mlp_call(h2, w1, w2, x1, norm_post_mlp, mod3)
```

```python
import functools

import jax
import jax.numpy as jnp
from jax import lax
from jax.experimental import pallas as pl
from jax.experimental.pallas import tpu as pltpu

F32 = jnp.float32
BF16 = jnp.bfloat16

GRID_W = 64
CONV_WIDTH = 31
CONV_PAD = (CONV_WIDTH - 1) // 2
HEAD_DIM = 128
N_MOD = 6
EPS = 1e-6

SCAN_CHUNK = 64
SCAN_BLOCK = 512

VMEM_LIMIT = 56 * 1024 * 1024

_NT = (((1,), (1,)), ((), ()))
_TN = (((0,), (0,)), ((), ()))


def _params(sem):
    return pltpu.CompilerParams(dimension_semantics=sem, vmem_limit_bytes=VMEM_LIMIT)


def _sigmoid(x):
    return 0.5 * jnp.tanh(0.5 * x) + 0.5


def _silu(x):
    return x * _sigmoid(x)


def _norm_modulate(x, gain, shift, scale):
    ms = jnp.mean(x * x, axis=-1, keepdims=True)
    return (x * lax.rsqrt(ms + EPS) * gain) * (1.0 + scale) + shift


def _forget_gate(raw, logits):
    e = jnp.exp(logits - jnp.max(logits, axis=0, keepdims=True))
    lb = e[0:1] / jnp.sum(e, axis=0, keepdims=True)
    return lb + (1.0 - lb) * _sigmoid(raw)


def _mod_kernel(c_ref, w_ref, b_ref, o_ref):
    s = _silu(c_ref[...]).astype(BF16)
    o_ref[...] = jnp.dot(s, w_ref[...].astype(BF16), preferred_element_type=F32) + b_ref[...]


def _mod_call(cc, w_mod, b_mod, tn=1024):
    rows, d = cc.shape
    n = w_mod.shape[1]
    return pl.pallas_call(
        _mod_kernel,
        grid=(n // tn,),
        in_specs=[pl.BlockSpec((rows, d), lambda j: (0, 0)),
                  pl.BlockSpec((d, tn), lambda j: (0, j)),
                  pl.BlockSpec((1, tn), lambda j: (0, j))],
        out_specs=pl.BlockSpec((rows, tn), lambda j: (0, j)),
        out_shape=jax.ShapeDtypeStruct((rows, n), F32),
        compiler_params=_params(("parallel",)),
        name="mod",
    )(cc, w_mod, b_mod)


def _proj_kernel(*refs, n_w, n_extra, n_side, n_out, epilogue):
    h_ref = refs[0]
    pos = 1
    w_refs = refs[pos:pos + n_w]
    pos += n_w
    extra = refs[pos:pos + n_extra]
    pos += n_extra
    side_in = refs[pos:pos + n_side]
    pos += n_side
    out_refs = refs[pos:pos + n_out]
    pos += n_out
    side_out = refs[pos:pos + n_side]
    pos += n_side
    wbf_refs = refs[pos:]

    @pl.when(pl.program_id(1) == 0)
    def _():
        for w_ref, wbf_ref in zip(w_refs, wbf_refs):
            wbf_ref[...] = w_ref[...].astype(BF16)

    for si_ref, so_ref in zip(side_in, side_out):
        so_ref[...] = si_ref[...].astype(BF16)

    h = h_ref[...]
    accs = [jnp.dot(h, wbf_ref[...], preferred_element_type=F32) for wbf_ref in wbf_refs]
    outs = epilogue(accs, extra)
    for o_ref, o in zip(out_refs, outs):
        o_ref[...] = o.astype(o_ref.dtype)


def _ep_forget(accs, extra):
    f = _forget_gate(accs[0], extra[0][...])
    return jnp.log2(f), 1.0 - f


def _ep_q(accs, extra):
    return (_silu(accs[0]) * (HEAD_DIM ** -0.5),)


def _ep_silu(accs, extra):
    return (_silu(accs[0]),)


def _ep_glu(accs, extra):
    return (accs[0] * _sigmoid(accs[1]),)


def _ep_sigmoid(accs, extra):
    return (_sigmoid(accs[0]),)


def _proj_call(h, w, col_offsets, width, epilogue, out_dtypes, extra=(), side=(), tm=1024, tn=1024,
               name="proj"):
    m, k = h.shape
    tm = min(tm, m)
    n_w = len(col_offsets)
    n_j, n_i = width // tn, m // tm
    in_specs = [pl.BlockSpec((tm, k), lambda j, i: (i, 0))]
    for off in col_offsets:
        in_specs.append(pl.BlockSpec((k, tn), lambda j, i, off=off: (0, off // tn + j)))
    for e in extra:
        in_specs.append(pl.BlockSpec((e.shape[0], tn), lambda j, i: (0, j)))
    side_specs = [pl.BlockSpec((a.shape[0] // (n_j * n_i), a.shape[1]), lambda j, i: (j * n_i + i, 0))
                  for a in side]
    return pl.pallas_call(
        functools.partial(_proj_kernel, n_w=n_w, n_extra=len(extra), n_side=len(side),
                          n_out=len(out_dtypes), epilogue=epilogue),
        grid=(n_j, n_i),
        in_specs=in_specs + side_specs,
        out_specs=[pl.BlockSpec((tm, tn), lambda j, i: (i, j)) for _ in out_dtypes] + side_specs,
        out_shape=[jax.ShapeDtypeStruct((m, width), dt) for dt in out_dtypes]
                  + [jax.ShapeDtypeStruct(a.shape, BF16) for a in side],
        scratch_shapes=[pltpu.VMEM((k, tn), BF16) for _ in range(n_w)],
        compiler_params=_params(("parallel", "arbitrary")),
        name=name,
    )(h, *([w] * n_w), *extra, *side)


def _prenorm_v_kernel(x_ref, g_ref, mod_ref, w_ref, h_ref, v_ref, wbf_ref):
    @pl.when((pl.program_id(0) == 0) & (pl.program_id(1) == 0))
    def _():
        wbf_ref[...] = w_ref[...].astype(BF16)

    h = _norm_modulate(x_ref[0], g_ref[...], mod_ref[0, 0:1, :], mod_ref[0, 1:2, :]).astype(BF16)
    h_ref[0] = h
    v_ref[0] = jnp.dot(h, wbf_ref[...], preferred_element_type=F32).astype(BF16)


def _prenorm_v_call(x, g, mod3, w, width, tl=512):
    b, l, d = x.shape
    return pl.pallas_call(
        _prenorm_v_kernel,
        grid=(b, l // tl),
        in_specs=[pl.BlockSpec((1, tl, d), lambda i, j: (i, j, 0)),
                  pl.BlockSpec((1, d), lambda i, j: (0, 0)),
                  pl.BlockSpec((1, N_MOD, d), lambda i, j: (i, 0, 0)),
                  pl.BlockSpec((d, width), lambda i, j: (0, 0), pipeline_mode=pl.Buffered(1))],
        out_specs=[pl.BlockSpec((1, tl, d), lambda i, j: (i, j, 0)),
                   pl.BlockSpec((1, tl, width), lambda i, j: (i, j, 0))],
        out_shape=[jax.ShapeDtypeStruct((b, l, d), BF16), jax.ShapeDtypeStruct((b, l, width), BF16)],
        scratch_shapes=[pltpu.VMEM((d, width), BF16)],
        compiler_params=_params(("arbitrary", "arbitrary")),
        name="prenorm_v",
    )(x, g, mod3, w)


def _scan_masks(c, reverse):
    t = lax.broadcasted_iota(jnp.int32, (c, c), 0)
    s = lax.broadcasted_iota(jnp.int32, (c, c), 1)
    half = c // 2
    before = (s >= t) if reverse else (s <= t)
    mask1 = ((t < half) == (s < half)) & before
    tri = jnp.where(before, 1.0, 0.0).astype(BF16)
    return mask1, jnp.concatenate([tri, tri], axis=1)


def _scan_cumsum(lf, tri2):
    hi = lf.astype(BF16)
    lo = (lf - hi.astype(F32)).astype(BF16)
    return jnp.dot(tri2, jnp.concatenate([hi, lo], axis=0), preferred_element_type=F32)


def _scan_refs(a, reverse):
    c = a.shape[0]
    half, quarter = c // 2, c // 4
    if reverse:
        return a[quarter:quarter + 1], a[half + quarter:half + quarter + 1], a[half:half + 1], a[0:1]
    return a[quarter - 1:quarter], a[half + quarter - 1:half + quarter], a[half - 1:half], a[c - 1:c]


def _scan_products(a, q, k, v, st, st_kv, reverse):
    c = a.shape[0]
    half = c // 2
    m1_lo, m1_hi, m0, a_tot = _scan_refs(a, reverse)
    lo_rows, hi_rows = slice(0, half), slice(half, c)
    row = lax.broadcasted_iota(jnp.int32, a.shape, 0)
    e1 = a - jnp.where(row < half, m1_lo, m1_hi)
    pq = q.astype(F32) * jnp.exp2(e1)
    pk = k.astype(F32) * jnp.exp2(-e1)
    s1 = lax.dot_general(pq.astype(BF16), pk.astype(BF16), _NT, preferred_element_type=F32)
    if reverse:
        q_rows, m1_q, k_rows, m1_k = lo_rows, m1_lo, hi_rows, m1_hi
    else:
        q_rows, m1_q, k_rows, m1_k = hi_rows, m1_hi, lo_rows, m1_lo
    q0 = (pq[q_rows] * jnp.exp2(m1_q - m0)).astype(BF16)
    k0 = (pk[k_rows] * jnp.exp2(m0 - m1_k)).astype(BF16)
    s0 = lax.dot_general(q0, k0, _NT, preferred_element_type=F32)
    qi = jnp.concatenate([pq[lo_rows] * jnp.exp2(m1_lo), pq[hi_rows] * jnp.exp2(m1_hi)], axis=0)
    inter = jnp.dot(qi.astype(BF16), st_kv, preferred_element_type=F32)
    kh = jnp.concatenate([pk[lo_rows] * jnp.exp2(a_tot - m1_lo), pk[hi_rows] * jnp.exp2(a_tot - m1_hi)], axis=0)
    upd = lax.dot_general(v, kh.astype(BF16), _TN, preferred_element_type=F32)
    return s1, s0, inter, st * jnp.exp2(a_tot) + upd


def _scan_output(s1, s0, inter, v, mask1, reverse):
    c = s1.shape[0]
    half = c // 2
    lo_rows, hi_rows = slice(0, half), slice(half, c)
    q_rows, k_rows = (lo_rows, hi_rows) if reverse else (hi_rows, lo_rows)
    intra = jnp.dot(jnp.where(mask1, s1, 0.0).astype(BF16), v, preferred_element_type=F32)
    cross = jnp.dot(s0.astype(BF16), v[k_rows], preferred_element_type=F32)
    o = inter + intra
    o_q = o[q_rows] + cross
    return jnp.concatenate([o_q, o[hi_rows]] if reverse else [o[lo_rows], o_q], axis=0)


def _scan_kernel(*refs, reverse, readout):
    k_ref, v_ref, lf_ref, s0_ref, q_ref = refs[:5]
    if readout:
        (ob_ref, g_ref, ng_ref, o_ref, st_ref, stkv_ref, a_ref) = refs[5:]
    else:
        (o_ref, st_ref, stkv_ref, a_ref) = refs[5:]
    c = SCAN_CHUNK
    n_batch, rows = k_ref.shape[0], k_ref.shape[1]
    n_chunks = rows // c
    n_heads = k_ref.shape[2] // HEAD_DIM
    recurrences = [(h, bi) for h in range(n_heads) for bi in range(n_batch)]

    @pl.when(pl.program_id(0) == 0)
    def _():
        st_ref[...] = s0_ref[...]
        for h, bi in recurrences:
            stkv_ref[bi, h] = s0_ref[bi, h].T.astype(BF16)

    mask1, tri2 = _scan_masks(c, reverse)

    def cumsum_body(ci, carry):
        r0 = pl.multiple_of(ci * c, c)
        for bi in range(n_batch):
            a_ref[bi, pl.ds(r0, c), :] = _scan_cumsum(lf_ref[bi, pl.ds(r0, c), :], tri2)
        return carry

    lax.fori_loop(0, n_chunks, cumsum_body, 0, unroll=4)

    def chunk_body(ci, carry):
        cidx = (n_chunks - 1 - ci) if reverse else ci
        r0 = pl.multiple_of(cidx * c, c)
        tile = lambda ref, h, bi: ref[bi, pl.ds(r0, c), h * HEAD_DIM:(h + 1) * HEAD_DIM]
        wave = []
        for h, bi in recurrences:
            s1, s0, inter, st_new = _scan_products(
                tile(a_ref, h, bi), tile(q_ref, h, bi), tile(k_ref, h, bi), tile(v_ref, h, bi),
                st_ref[bi, h], stkv_ref[bi, h], reverse)
            st_ref[bi, h] = st_new
            stkv_ref[bi, h] = st_new.T.astype(BF16)
            wave.append((s1, s0, inter))
        for (h, bi), (s1, s0, inter) in zip(recurrences, wave):
            lanes = slice(h * HEAD_DIM, (h + 1) * HEAD_DIM)
            o = _scan_output(s1, s0, inter, tile(v_ref, h, bi), mask1, reverse)
            if readout:
                o = o + tile(ob_ref, h, bi)
                ms = jnp.mean(o * o, axis=-1, keepdims=True)
                y = o * lax.rsqrt(ms + EPS) * ng_ref[:, lanes]
                o = y * tile(g_ref, h, bi).astype(F32)
            o_ref[bi, pl.ds(r0, c), lanes] = o.astype(o_ref.dtype)
        return carry

    lax.fori_loop(0, n_chunks, chunk_body, 0, unroll=4)


def _scan_call(k, v, lf, s0, q, *, reverse, ob=None, g=None, ng=None):
    b, l, dh = k.shape
    tb = min(SCAN_BLOCK, l)
    n_blocks = l // tb
    readout = ob is not None
    if reverse:
        seq = lambda j: (0, n_blocks - 1 - j, 0)
    else:
        seq = lambda j: (0, j, 0)
    seq_spec = pl.BlockSpec((b, tb, dh), seq)
    in_specs = [seq_spec, seq_spec, seq_spec, pl.BlockSpec(s0.shape, lambda j: (0, 0, 0, 0)), seq_spec]
    args = [k, v, lf, s0, q]
    if readout:
        in_specs += [seq_spec, seq_spec, pl.BlockSpec((1, dh), lambda j: (0, 0))]
        args += [ob, g, ng]
    return pl.pallas_call(
        functools.partial(_scan_kernel, reverse=reverse, readout=readout),
        grid=(n_blocks,),
        in_specs=in_specs,
        out_specs=seq_spec,
        out_shape=jax.ShapeDtypeStruct((b, l, dh), BF16 if readout else F32),
        scratch_shapes=[pltpu.VMEM(s0.shape, F32), pltpu.VMEM(s0.shape, BF16), pltpu.VMEM((b, tb, dh), F32)],
        compiler_params=_params(("arbitrary",)),
        name="scan_readout_fwd" if readout else "scan_out_bwd",
    )(*args)


def _ctx_proj_kernel(ctx_ref, g_ref, mod_ref, w_ref, o_ref, hc_ref):
    @pl.when(pl.program_id(0) == 0)
    def _():
        for bi in range(ctx_ref.shape[0]):
            hc_ref[bi] = _norm_modulate(ctx_ref[bi], g_ref[...], mod_ref[0, 0:1, :],
                                        mod_ref[0, 1:2, :]).astype(BF16)

    w = w_ref[...].astype(BF16)
    for bi in range(ctx_ref.shape[0]):
        o_ref[bi] = jnp.dot(hc_ref[bi], w, preferred_element_type=F32)


def _ctx_proj_call(ctx, g, mod3, mod_row, w, width, tn=1024):
    b, lc, d = ctx.shape
    return pl.pallas_call(
        _ctx_proj_kernel,
        grid=(width // tn,),
        in_specs=[pl.BlockSpec((b, lc, d), lambda j: (0, 0, 0)),
                  pl.BlockSpec((1, d), lambda j: (0, 0)),
                  pl.BlockSpec((1, N_MOD, d), lambda j: (mod_row, 0, 0)),
                  pl.BlockSpec((d, tn), lambda j: (0, j))],
        out_specs=pl.BlockSpec((b, lc, tn), lambda j: (0, 0, j)),
        out_shape=jax.ShapeDtypeStruct((b, lc, width), F32),
        scratch_shapes=[pltpu.VMEM((b, lc, d), BF16)],
        compiler_params=_params(("arbitrary",)),
        name="ctx_proj",
    )(ctx, g, mod3, w)


def _ctx_state_kernel(pv_ref, pf_ref, pb_ref, lgf_ref, lgb_ref, sf_ref, sb_ref):
    c = SCAN_CHUNK
    n_batch, rows = pv_ref.shape[0], pv_ref.shape[1]
    n_chunks = rows // c
    n_heads = pv_ref.shape[2] // HEAD_DIM
    for reverse, p_ref, lg_ref, s_ref in ((False, pf_ref, lgf_ref, sf_ref), (True, pb_ref, lgb_ref, sb_ref)):
        _, tri2 = _scan_masks(c, reverse)
        s_ref[...] = jnp.zeros(s_ref.shape, F32)

        def chunk_body(ci, carry, reverse=reverse, p_ref=p_ref, lg_ref=lg_ref, s_ref=s_ref, tri2=tri2):
            cidx = (n_chunks - 1 - ci) if reverse else ci
            r0 = pl.multiple_of(cidx * c, c)
            recurrences = [(h, bi) for h in range(n_heads) for bi in range(n_batch)]
            tile = lambda ref, h, bi: ref[bi, pl.ds(r0, c), h * HEAD_DIM:(h + 1) * HEAD_DIM]
            gates = [_forget_gate(tile(p_ref, h, bi), lg_ref[:, h * HEAD_DIM:(h + 1) * HEAD_DIM])
                     for h, bi in recurrences]
            sums = [_scan_cumsum(jnp.log2(f), tri2) for f in gates]
            for (h, bi), f, a in zip(recurrences, gates, sums):
                a_tot = _scan_refs(a, reverse)[3]
                kh = ((1.0 - f) * jnp.exp2(a_tot - a)).astype(BF16)
                upd = lax.dot_general(tile(pv_ref, h, bi).astype(BF16), kh, _TN, preferred_element_type=F32)
                s_ref[bi, h] = s_ref[bi, h] * jnp.exp2(a_tot) + upd
            return carry

        lax.fori_loop(0, n_chunks, chunk_body, 0)


def _ctx_state_call(pc, logits_f, logits_b, dh):
    b, lc, _ = pc.shape
    n_heads = dh // HEAD_DIM
    group = lambda n: pl.BlockSpec((b, lc, dh), lambda j: (0, 0, n))
    lg_spec = pl.BlockSpec(logits_f.shape, lambda j: (0, 0))
    st_spec = pl.BlockSpec((b, n_heads, HEAD_DIM, HEAD_DIM), lambda j: (0, 0, 0, 0))
    st_shape = jax.ShapeDtypeStruct((b, n_heads, HEAD_DIM, HEAD_DIM), F32)
    return pl.pallas_call(
        _ctx_state_kernel,
        grid=(1,),
        in_specs=[group(0), group(1), group(2), lg_spec, lg_spec],
        out_specs=[st_spec, st_spec],
        out_shape=[st_shape, st_shape],
        compiler_params=_params(("arbitrary",)),
        name="ctx_state",
    )(pc, pc, pc, logits_f, logits_b)


CONV_TILE_ROWS = 16
_HPAD = 16
_HROW = GRID_W + 2 * _HPAD


def _conv_kernel(cur_ref, prev_ref, next_ref, w_ref, b_ref, lg_ref, lb_ref, o_ref,
                 hpad_ref, vpad_ref, y_ref, *, n_tiles):
    tr = CONV_TILE_ROWS
    half = w_ref.shape[1] // 2
    n_lane_tiles = half // 128
    i = pl.program_id(1)

    hpad_ref[:, :, :_HPAD, :] = jnp.zeros((tr, n_lane_tiles, _HPAD, 128), F32)
    hpad_ref[:, :, _HPAD + GRID_W:, :] = jnp.zeros((tr, n_lane_tiles, _HPAD, 128), F32)
    for r in range(tr):
        for lt in range(n_lane_tiles):
            hpad_ref[r, lt, _HPAD:_HPAD + GRID_W, :] = cur_ref[0, r * GRID_W:(r + 1) * GRID_W,
                                                               lt * 128:(lt + 1) * 128]
    halo = CONV_PAD * GRID_W
    for lt in range(n_lane_tiles):
        lanes = slice(lt * 128, (lt + 1) * 128)
        top = prev_ref[0, tr * GRID_W - halo:, lanes]
        bot = next_ref[0, :halo, lanes]
        vpad_ref[lt, :halo, :] = jnp.where(i > 0, top, 0.0)
        vpad_ref[lt, halo:halo + tr * GRID_W, :] = cur_ref[0, :, half + lt * 128:half + (lt + 1) * 128]
        vpad_ref[lt, halo + tr * GRID_W:, :] = jnp.where(i < n_tiles - 1, bot, 0.0)

    def h_row(r, carry):
        o0 = pl.multiple_of(r * GRID_W, GRID_W)
        for lt in range(n_lane_tiles):
            lanes = slice(lt * 128, (lt + 1) * 128)
            acc = jnp.zeros((GRID_W, 128), F32)
            for jj in range(CONV_WIDTH):
                off = _HPAD - CONV_PAD + jj
                acc = acc + hpad_ref[r, lt, off:off + GRID_W, :] * w_ref[jj:jj + 1, lanes]
            y_ref[pl.ds(o0, GRID_W), lanes] = acc + b_ref[:, lanes]
        return carry

    lax.fori_loop(0, tr, h_row, 0)

    n_sub = GRID_W // 8
    for lt in range(n_lane_tiles):
        vl = slice(half + lt * 128, half + (lt + 1) * 128)
        wv = [jnp.broadcast_to(w_ref[jj:jj + 1, vl], (8, 128)) for jj in range(CONV_WIDTH)]
        bias = b_ref[:, vl]

        def v_row(r, carry, lt=lt, vl=vl, wv=wv, bias=bias):
            v0 = pl.multiple_of(r * GRID_W, GRID_W)
            acc = [jnp.zeros((8, 128), F32)] * n_sub
            for jj in range(CONV_WIDTH):
                src = pl.multiple_of(v0 + jj * GRID_W, GRID_W)
                tap = vpad_ref[lt, pl.ds(src, GRID_W), :]
                acc = [acc[s] + tap[8 * s:8 * s + 8, :] * wv[jj] for s in range(n_sub)]
            y_ref[pl.ds(v0, GRID_W), vl] = jnp.concatenate(acc, axis=0) + bias
            return carry

        lax.fori_loop(0, tr, v_row, 0, unroll=2)

    def ln_row(r, carry):
        o0 = pl.multiple_of(r * GRID_W, GRID_W)
        y = y_ref[pl.ds(o0, GRID_W), :]
        mu = jnp.mean(y, axis=-1, keepdims=True)
        yc = y - mu
        var = jnp.mean(yc * yc, axis=-1, keepdims=True)
        z = yc * lax.rsqrt(var + EPS) * lg_ref[...] + lb_ref[...]
        o_ref[0, pl.ds(o0, GRID_W), :] = _silu(z).astype(BF16)
        return carry

    lax.fori_loop(0, tr, ln_row, 0, unroll=2)


def _conv_call(u, w, bias, ln_g, ln_b):
    b, l, ch = u.shape
    half = ch // 2
    tile = CONV_TILE_ROWS * GRID_W
    n_tiles = l // tile
    return pl.pallas_call(
        functools.partial(_conv_kernel, n_tiles=n_tiles),
        grid=(b, n_tiles),
        in_specs=[pl.BlockSpec((1, tile, ch), lambda i, j: (i, j, 0)),
                  pl.BlockSpec((1, tile, half), lambda i, j: (i, jnp.maximum(j - 1, 0), 1)),
                  pl.BlockSpec((1, tile, half), lambda i, j: (i, jnp.minimum(j + 1, n_tiles - 1), 1)),
                  pl.BlockSpec((CONV_WIDTH, ch), lambda i, j: (0, 0)),
                  pl.BlockSpec((1, ch), lambda i, j: (0, 0)),
                  pl.BlockSpec((1, ch), lambda i, j: (0, 0)),
                  pl.BlockSpec((1, ch), lambda i, j: (0, 0))],
        out_specs=pl.BlockSpec((1, tile, ch), lambda i, j: (i, j, 0)),
        out_shape=jax.ShapeDtypeStruct((b, l, ch), BF16),
        scratch_shapes=[pltpu.VMEM((CONV_TILE_ROWS, half // 128, _HROW, 128), F32),
                        pltpu.VMEM((half // 128, tile + 2 * CONV_PAD * GRID_W, 128), F32),
                        pltpu.VMEM((tile, ch), F32)],
        compiler_params=_params(("parallel", "parallel")),
        name="conv",
    )(u, u, u, w, bias, ln_g, ln_b)


def _mix_kernel(uc_ref, og_ref, gc_ref, gh_ref, x_ref, wpw_ref, wh_ref, wo_ref,
                npost_ref, npre_ref, mod_ref, x1_ref, h2_ref):
    tm = x_ref.shape[1]
    n_split = 2
    for r in range(n_split):
        rows = slice(r * tm // n_split, (r + 1) * tm // n_split)
        y_c = jnp.dot(uc_ref[0, rows, :], wpw_ref[...], preferred_element_type=F32)
        y_h = jnp.dot(og_ref[0, rows, :], wh_ref[...], preferred_element_type=F32)
        z = gc_ref[0, rows, :].astype(F32) * y_c + gh_ref[0, rows, :].astype(F32) * y_h
        y = jnp.dot(z.astype(BF16), wo_ref[...], preferred_element_type=F32)
        ms = jnp.mean(y * y, axis=-1, keepdims=True)
        yn = y * lax.rsqrt(ms + EPS) * npost_ref[...]
        x1 = x_ref[0, rows, :] + mod_ref[0, 2:3, :] * yn
        x1_ref[0, rows, :] = x1
        h2_ref[0, rows, :] = _norm_modulate(x1, npre_ref[...], mod_ref[0, 3:4, :],
                                            mod_ref[0, 4:5, :]).astype(BF16)


def _mix_call(uc, og, gc, gh, x, wpw, wh, wo, npost, npre, mod3, tm=256):
    b, l, d = x.shape
    dc = uc.shape[2]
    row = lambda width: pl.BlockSpec((1, tm, width), lambda i, j: (i, j, 0))
    const = lambda shape: pl.BlockSpec(shape, lambda i, j: (0,) * len(shape), pipeline_mode=pl.Buffered(1))
    return pl.pallas_call(
        _mix_kernel,
        grid=(b, l // tm),
        in_specs=[row(dc), row(dc), row(d), row(d), row(d),
                  const(wpw.shape), const(wh.shape), const(wo.shape),
                  const((1, d)), const((1, d)),
                  pl.BlockSpec((1, N_MOD, d), lambda i, j: (i, 0, 0))],
        out_specs=[row(d), row(d)],
        out_shape=[jax.ShapeDtypeStruct((b, l, d), F32), jax.ShapeDtypeStruct((b, l, d), BF16)],
        compiler_params=_params(("parallel", "parallel")),
        name="mix",
    )(uc, og, gc, gh, x, wpw, wh, wo, npost, npre, mod3)


def _mlp_kernel(h_ref, w1_ref, w2_ref, x1_ref, npost_ref, mod_ref, o_ref, *, n_ff):
    j = pl.program_id(2)

    @pl.when(j == 0)
    def _():
        o_ref[...] = jnp.zeros(o_ref.shape, F32)

    a = jnp.dot(h_ref[0], w1_ref[...], preferred_element_type=F32)
    a = jnp.square(jnp.maximum(a, 0.0)).astype(BF16)
    o_ref[0] += jnp.dot(a, w2_ref[...], preferred_element_type=F32)

    @pl.when(j == n_ff - 1)
    def _():
        rb = 32
        for r in range(o_ref.shape[1] // rb):
            rows = slice(r * rb, (r + 1) * rb)
            y = o_ref[0, rows, :]
            ms = jnp.mean(y * y, axis=-1, keepdims=True)
            yn = y * lax.rsqrt(ms + EPS) * npost_ref[...]
            o_ref[0, rows, :] = x1_ref[0, rows, :] + mod_ref[0, 5:6, :] * yn


def _mlp_call(h2, w1, w2, x1, npost, mod3, tm=1024, tf=512):
    b, l, d = x1.shape
    dff = w1.shape[1]
    n_ff = dff // tf
    return pl.pallas_call(
        functools.partial(_mlp_kernel, n_ff=n_ff),
        grid=(b, l // tm, n_ff),
        in_specs=[pl.BlockSpec((1, tm, d), lambda i, m, j: (i, m, 0)),
                  pl.BlockSpec((d, tf), lambda i, m, j: (0, j)),
                  pl.BlockSpec((tf, d), lambda i, m, j: (j, 0)),
                  pl.BlockSpec((1, tm, d), lambda i, m, j: (i, m, 0), pipeline_mode=pl.Buffered(1)),
                  pl.BlockSpec((1, d), lambda i, m, j: (0, 0)),
                  pl.BlockSpec((1, N_MOD, d), lambda i, m, j: (i, 0, 0))],
        out_specs=pl.BlockSpec((1, tm, d), lambda i, m, j: (i, m, 0)),
        out_shape=jax.ShapeDtypeStruct((b, l, d), F32),
        compiler_params=_params(("parallel", "parallel", "arbitrary")),
        name="mlp",
    )(h2, w1, w2, x1, npost, mod3)


def kernel(x, c, ctx, c_ctx, w_mod, b_mod, norm_pre_mix, norm_post_mix, norm_pre_mlp, norm_post_mlp, w_in,
           conv_dw_w, conv_dw_b, conv_ln_g, conv_ln_b, conv_pw_w, hgrn_lb_logits, hgrn_norm_g, hgrn_out_w,
           w_out, mlp_w1, mlp_w2):
    assert w_mod.shape[0] == 1, "single-layer block"
    b, l, d = x.shape
    lc = ctx.shape[1]
    dh = hgrn_norm_g.shape[1]
    dc = conv_dw_w.shape[2]
    assert l % (CONV_TILE_ROWS * GRID_W) == 0 and CONV_PAD <= CONV_TILE_ROWS and CONV_PAD <= _HPAD
    assert l % SCAN_BLOCK == 0 and lc % SCAN_CHUNK == 0

    mod_rows = 16
    cc = jnp.concatenate([c, c_ctx[None, :], jnp.zeros((mod_rows - b - 1, d), F32)], axis=0)
    mod3 = _mod_call(cc, w_mod[0], b_mod).reshape(mod_rows, N_MOD, d)

    w_in0 = w_in[0]
    lb_f, lb_b = hgrn_lb_logits[0], hgrn_lb_logits[1]
    off = lambda n: n * dh

    pc = _ctx_proj_call(ctx, norm_pre_mix, mod3, b, w_in0, 3 * dh)
    s_f, s_b = _ctx_state_call(pc, lb_f, lb_b, dh)

    h, v = _prenorm_v_call(x, norm_pre_mix, mod3, w_in0, dh)
    h = h.reshape(b * l, d)
    seq = lambda t: t.reshape(b, l, t.shape[-1])
    lf_f, k_f = _proj_call(h, w_in0, [off(1)], dh, _ep_forget, [F32, BF16], extra=(lb_f,), name="proj_ff")
    lf_b, k_b = _proj_call(h, w_in0, [off(2)], dh, _ep_forget, [F32, BF16], extra=(lb_b,), name="proj_fb")
    q, wpw = _proj_call(h, w_in0, [off(3)], dh, _ep_q, [BF16], side=(conv_pw_w[0],), name="proj_q")
    g, wh = _proj_call(h, w_in0, [off(4)], dh, _ep_silu, [BF16], side=(hgrn_out_w[0],), name="proj_g")
    u, wo = _proj_call(h, w_in0, [off(5), off(5) + dc], dc, _ep_glu, [F32], side=(w_out[0],), tn=512,
                       name="proj_glu")
    gc, w1 = _proj_call(h, w_in0, [off(5) + 2 * dc], d, _ep_sigmoid, [BF16], side=(mlp_w1[0],), name="proj_gc")
    gh, w2 = _proj_call(h, w_in0, [off(5) + 2 * dc + d], d, _ep_sigmoid, [BF16], side=(mlp_w2[0],),
                        name="proj_gh")

    o_b = _scan_call(seq(k_b), v, seq(lf_b), s_b, seq(q), reverse=True)
    og = _scan_call(seq(k_f), v, seq(lf_f), s_f, seq(q), reverse=False, ob=o_b, g=seq(g), ng=hgrn_norm_g)

    uc = _conv_call(seq(u), conv_dw_w[0], conv_dw_b, conv_ln_g, conv_ln_b)

    x1, h2 = _mix_call(uc, og, seq(gc), seq(gh), x, wpw, wh, wo, norm_post_mix, norm_pre_mlp, mod3)
    return _mlp_call(h2, w1, w2, x1, norm_post_mlp, mod3)
```

```python
import functools

import jax
import jax.numpy as jnp
from jax import lax
from jax.experimental import pallas as pl
from jax.experimental.pallas import tpu as pltpu

F32 = jnp.float32
BF16 = jnp.bfloat16

GRID_W = 64
CONV_WIDTH = 31
CONV_PAD = (CONV_WIDTH - 1) // 2
HEAD_DIM = 128
N_MOD = 6
EPS = 1e-6

SCAN_CHUNK = 64
SCAN_BLOCK = 512

VMEM_LIMIT = 56 * 1024 * 1024

_NT = (((1,), (1,)), ((), ()))
_TN = (((0,), (0,)), ((), ()))


def _params(sem):
    return pltpu.CompilerParams(dimension_semantics=sem, vmem_limit_bytes=VMEM_LIMIT)


def _sigmoid(x):
    return 0.5 * jnp.tanh(0.5 * x) + 0.5


def _silu(x):
    return x * _sigmoid(x)


def _norm_modulate(x, gain, shift, scale):
    ms = jnp.mean(x * x, axis=-1, keepdims=True)
    return (x * lax.rsqrt(ms + EPS) * gain) * (1.0 + scale) + shift


def _forget_gate(raw, logits):
    e = jnp.exp(logits - jnp.max(logits, axis=0, keepdims=True))
    lb = e[0:1] / jnp.sum(e, axis=0, keepdims=True)
    return lb + (1.0 - lb) * _sigmoid(raw)


def _mod_kernel(c_ref, w_ref, b_ref, o_ref):
    s = _silu(c_ref[...]).astype(BF16)
    o_ref[...] = jnp.dot(s, w_ref[...].astype(BF16), preferred_element_type=F32) + b_ref[...]


def _mod_call(cc, w_mod, b_mod, tn=1024):
    rows, d = cc.shape
    n = w_mod.shape[1]
    return pl.pallas_call(
        _mod_kernel,
        grid=(n // tn,),
        in_specs=[pl.BlockSpec((rows, d), lambda j: (0, 0)),
                  pl.BlockSpec((d, tn), lambda j: (0, j)),
                  pl.BlockSpec((1, tn), lambda j: (0, j))],
        out_specs=pl.BlockSpec((rows, tn), lambda j: (0, j)),
        out_shape=jax.ShapeDtypeStruct((rows, n), F32),
        compiler_params=_params(("parallel",)),
        name="mod",
    )(cc, w_mod, b_mod)


def _proj_kernel(*refs, n_w, n_extra, n_side, n_out, epilogue):
    h_ref = refs[0]
    pos = 1
    w_refs = refs[pos:pos + n_w]
    pos += n_w
    extra = refs[pos:pos + n_extra]
    pos += n_extra
    side_in = refs[pos:pos + n_side]
    pos += n_side
    out_refs = refs[pos:pos + n_out]
    pos += n_out
    side_out = refs[pos:pos + n_side]
    pos += n_side
    wbf_refs = refs[pos:]

    @pl.when(pl.program_id(1) == 0)
    def _():
        for w_ref, wbf_ref in zip(w_refs, wbf_refs):
            wbf_ref[...] = w_ref[...].astype(BF16)

    for si_ref, so_ref in zip(side_in, side_out):
        so_ref[...] = si_ref[...].astype(BF16)

    h = h_ref[...]
    accs = [jnp.dot(h, wbf_ref[...], preferred_element_type=F32) for wbf_ref in wbf_refs]
    outs = epilogue(accs, extra)
    for o_ref, o in zip(out_refs, outs):
        o_ref[...] = o.astype(o_ref.dtype)


def _ep_forget(accs, extra):
    f = _forget_gate(accs[0], extra[0][...])
    return jnp.log2(f), 1.0 - f


def _ep_scaled_silu(accs, extra):
    return (_silu(accs[0]) * extra[0][...],)


def _ep_glu(accs, extra):
    return (accs[0] * _sigmoid(accs[1]),)


def _ep_sigmoid(accs, extra):
    return (_sigmoid(accs[0]),)


def _proj_call(h, w, col_offsets, width, epilogue, out_dtypes, extra=(), side=(), tm=1024, tn=1024,
               name="proj"):
    m, k = h.shape
    tm = min(tm, m)
    n_w = len(col_offsets)
    n_j, n_i = width // tn, m // tm
    in_specs = [pl.BlockSpec((tm, k), lambda j, i: (i, 0))]
    for off in col_offsets:
        in_specs.append(pl.BlockSpec((k, tn), lambda j, i, off=off: (0, off // tn + j)))
    for e in extra:
        in_specs.append(pl.BlockSpec((e.shape[0], tn), lambda j, i: (0, j)))
    side_specs = [pl.BlockSpec((a.shape[0] // (n_j * n_i), a.shape[1]), lambda j, i: (j * n_i + i, 0))
                  for a in side]
    return pl.pallas_call(
        functools.partial(_proj_kernel, n_w=n_w, n_extra=len(extra), n_side=len(side),
                          n_out=len(out_dtypes), epilogue=epilogue),
        grid=(n_j, n_i),
        in_specs=in_specs + side_specs,
        out_specs=[pl.BlockSpec((tm, tn), lambda j, i: (i, j)) for _ in out_dtypes] + side_specs,
        out_shape=[jax.ShapeDtypeStruct((m, width), dt) for dt in out_dtypes]
                  + [jax.ShapeDtypeStruct(a.shape, BF16) for a in side],
        scratch_shapes=[pltpu.VMEM((k, tn), BF16) for _ in range(n_w)],
        compiler_params=_params(("parallel", "arbitrary")),
        name=name,
    )(h, *([w] * n_w), *extra, *side)


def _prenorm_v_kernel(x_ref, g_ref, mod_ref, w_ref, h_ref, v_ref, wbf_ref):
    @pl.when((pl.program_id(0) == 0) & (pl.program_id(1) == 0))
    def _():
        wbf_ref[...] = w_ref[...].astype(BF16)

    h = _norm_modulate(x_ref[0], g_ref[...], mod_ref[0, 0:1, :], mod_ref[0, 1:2, :]).astype(BF16)
    h_ref[0] = h
    v_ref[0] = jnp.dot(h, wbf_ref[...], preferred_element_type=F32).astype(BF16)


def _prenorm_v_call(x, g, mod3, w, width, tl=512):
    b, l, d = x.shape
    return pl.pallas_call(
        _prenorm_v_kernel,
        grid=(b, l // tl),
        in_specs=[pl.BlockSpec((1, tl, d), lambda i, j: (i, j, 0)),
                  pl.BlockSpec((1, d), lambda i, j: (0, 0)),
                  pl.BlockSpec((1, N_MOD, d), lambda i, j: (i, 0, 0)),
                  pl.BlockSpec((d, width), lambda i, j: (0, 0), pipeline_mode=pl.Buffered(1))],
        out_specs=[pl.BlockSpec((1, tl, d), lambda i, j: (i, j, 0)),
                   pl.BlockSpec((1, tl, width), lambda i, j: (i, j, 0))],
        out_shape=[jax.ShapeDtypeStruct((b, l, d), BF16), jax.ShapeDtypeStruct((b, l, width), BF16)],
        scratch_shapes=[pltpu.VMEM((d, width), BF16)],
        compiler_params=_params(("arbitrary", "arbitrary")),
        name="prenorm_v",
    )(x, g, mod3, w)


def _scan_masks(c, reverse):
    t = lax.broadcasted_iota(jnp.int32, (c, c), 0)
    s = lax.broadcasted_iota(jnp.int32, (c, c), 1)
    half = c // 2
    before = (s >= t) if reverse else (s <= t)
    mask1 = ((t < half) == (s < half)) & before
    tri = jnp.where(before, 1.0, 0.0).astype(BF16)
    return mask1, jnp.concatenate([tri, tri], axis=1)


def _scan_cumsum(lf, tri2):
    hi = lf.astype(BF16)
    lo = (lf - hi.astype(F32)).astype(BF16)
    return jnp.dot(tri2, jnp.concatenate([hi, lo], axis=0), preferred_element_type=F32)


def _scan_refs(a, reverse):
    c = a.shape[0]
    half, quarter = c // 2, c // 4
    if reverse:
        return a[quarter:quarter + 1], a[half + quarter:half + quarter + 1], a[half:half + 1], a[0:1]
    return a[quarter - 1:quarter], a[half + quarter - 1:half + quarter], a[half - 1:half], a[c - 1:c]


def _scan_products(a, q, k, v, st, st_kv, reverse):
    c = a.shape[0]
    half = c // 2
    m1_lo, m1_hi, m0, a_tot = _scan_refs(a, reverse)
    lo_rows, hi_rows = slice(0, half), slice(half, c)
    row = lax.broadcasted_iota(jnp.int32, a.shape, 0)
    e1 = a - jnp.where(row < half, m1_lo, m1_hi)
    pq = q.astype(F32) * jnp.exp2(e1)
    pk = k.astype(F32) * jnp.exp2(-e1)
    s1 = lax.dot_general(pq.astype(BF16), pk.astype(BF16), _NT, preferred_element_type=F32)
    if reverse:
        q_rows, m1_q, k_rows, m1_k = lo_rows, m1_lo, hi_rows, m1_hi
    else:
        q_rows, m1_q, k_rows, m1_k = hi_rows, m1_hi, lo_rows, m1_lo
    q0 = (pq[q_rows] * jnp.exp2(m1_q - m0)).astype(BF16)
    k0 = (pk[k_rows] * jnp.exp2(m0 - m1_k)).astype(BF16)
    s0 = lax.dot_general(q0, k0, _NT, preferred_element_type=F32)
    qi = jnp.concatenate([pq[lo_rows] * jnp.exp2(m1_lo), pq[hi_rows] * jnp.exp2(m1_hi)], axis=0)
    inter = jnp.dot(qi.astype(BF16), st_kv, preferred_element_type=F32)
    kh = jnp.concatenate([pk[lo_rows] * jnp.exp2(a_tot - m1_lo), pk[hi_rows] * jnp.exp2(a_tot - m1_hi)], axis=0)
    upd = lax.dot_general(v, kh.astype(BF16), _TN, preferred_element_type=F32)
    return s1, s0, inter, st * jnp.exp2(a_tot) + upd


def _scan_output(s1, s0, inter, v, mask1, reverse):
    c = s1.shape[0]
    half = c // 2
    lo_rows, hi_rows = slice(0, half), slice(half, c)
    q_rows, k_rows = (lo_rows, hi_rows) if reverse else (hi_rows, lo_rows)
    intra = jnp.dot(jnp.where(mask1, s1, 0.0).astype(BF16), v, preferred_element_type=F32)
    cross = jnp.dot(s0.astype(BF16), v[k_rows], preferred_element_type=F32)
    o = inter + intra
    o_q = o[q_rows] + cross
    return jnp.concatenate([o_q, o[hi_rows]] if reverse else [o[lo_rows], o_q], axis=0)


def _scan_kernel(*refs, reverse, readout):
    k_ref, v_ref, lf_ref, s0_ref, q_ref = refs[:5]
    if readout:
        (ob_ref, g_ref, ng_ref, o_ref, st_ref, stkv_ref, a_ref) = refs[5:]
    else:
        (o_ref, st_ref, stkv_ref, a_ref) = refs[5:]
    c = SCAN_CHUNK
    n_batch, rows = k_ref.shape[0], k_ref.shape[1]
    n_chunks = rows // c
    n_heads = k_ref.shape[2] // HEAD_DIM
    recurrences = [(h, bi) for h in range(n_heads) for bi in range(n_batch)]

    @pl.when(pl.program_id(0) == 0)
    def _():
        st_ref[...] = s0_ref[...]
        for h, bi in recurrences:
            stkv_ref[bi, h] = s0_ref[bi, h].T.astype(BF16)

    mask1, tri2 = _scan_masks(c, reverse)

    def cumsum_body(ci, carry):
        r0 = pl.multiple_of(ci * c, c)
        for bi in range(n_batch):
            a_ref[bi, pl.ds(r0, c), :] = _scan_cumsum(lf_ref[bi, pl.ds(r0, c), :], tri2)
        return carry

    lax.fori_loop(0, n_chunks, cumsum_body, 0, unroll=4)

    def chunk_body(ci, carry):
        cidx = (n_chunks - 1 - ci) if reverse else ci
        r0 = pl.multiple_of(cidx * c, c)
        tile = lambda ref, h, bi: ref[bi, pl.ds(r0, c), h * HEAD_DIM:(h + 1) * HEAD_DIM]
        wave = []
        for h, bi in recurrences:
            s1, s0, inter, st_new = _scan_products(
                tile(a_ref, h, bi), tile(q_ref, h, bi), tile(k_ref, h, bi), tile(v_ref, h, bi),
                st_ref[bi, h], stkv_ref[bi, h], reverse)
            st_ref[bi, h] = st_new
            stkv_ref[bi, h] = st_new.T.astype(BF16)
            wave.append((s1, s0, inter))
        for (h, bi), (s1, s0, inter) in zip(recurrences, wave):
            lanes = slice(h * HEAD_DIM, (h + 1) * HEAD_DIM)
            o = _scan_output(s1, s0, inter, tile(v_ref, h, bi), mask1, reverse)
            if readout:
                o = o + tile(ob_ref, h, bi)
                ms = jnp.mean(o * o, axis=-1, keepdims=True)
                y = o * lax.rsqrt(ms + EPS) * ng_ref[:, lanes]
                o = y * tile(g_ref, h, bi).astype(F32)
            o_ref[bi, pl.ds(r0, c), lanes] = o.astype(o_ref.dtype)
        return carry

    lax.fori_loop(0, n_chunks, chunk_body, 0, unroll=4)


def _scan_call(k, v, lf, s0, q, *, reverse, ob=None, g=None, ng=None):
    b, l, _ = v[0].shape
    dh = ng.shape[1] if ng is not None else s0.shape[1] * HEAD_DIM
    tb = min(SCAN_BLOCK, l)
    n_blocks = l // tb
    readout = ob is not None
    blk = (lambda j: n_blocks - 1 - j) if reverse else (lambda j: j)
    seq_spec = lambda group: pl.BlockSpec((b, tb, dh), lambda j: (0, blk(j), group))
    st_spec = pl.BlockSpec(s0.shape, lambda j: (0, 0, 0, 0))
    in_specs = [seq_spec(k[1]), seq_spec(v[1]), seq_spec(lf[1]), st_spec, seq_spec(q[1])]
    args = [k[0], v[0], lf[0], s0, q[0]]
    if readout:
        in_specs += [seq_spec(ob[1]), seq_spec(g[1]), pl.BlockSpec((1, dh), lambda j: (0, 0))]
        args += [ob[0], g[0], ng]
    return pl.pallas_call(
        functools.partial(_scan_kernel, reverse=reverse, readout=readout),
        grid=(n_blocks,),
        in_specs=in_specs,
        out_specs=seq_spec(0),
        out_shape=jax.ShapeDtypeStruct((b, l, dh), BF16 if readout else F32),
        scratch_shapes=[pltpu.VMEM(s0.shape, F32), pltpu.VMEM(s0.shape, BF16), pltpu.VMEM((b, tb, dh), F32)],
        compiler_params=_params(("arbitrary",)),
        name="scan_readout_fwd" if readout else "scan_out_bwd",
    )(*args)


def _ctx_proj_kernel(ctx_ref, g_ref, mod_ref, w_ref, o_ref, hc_ref):
    @pl.when(pl.program_id(0) == 0)
    def _():
        for bi in range(ctx_ref.shape[0]):
            hc_ref[bi] = _norm_modulate(ctx_ref[bi], g_ref[...], mod_ref[0, 0:1, :],
                                        mod_ref[0, 1:2, :]).astype(BF16)

    w = w_ref[...].astype(BF16)
    for bi in range(ctx_ref.shape[0]):
        o_ref[bi] = jnp.dot(hc_ref[bi], w, preferred_element_type=F32)


def _ctx_proj_call(ctx, g, mod3, mod_row, w, width, tn=1024):
    b, lc, d = ctx.shape
    return pl.pallas_call(
        _ctx_proj_kernel,
        grid=(width // tn,),
        in_specs=[pl.BlockSpec((b, lc, d), lambda j: (0, 0, 0)),
                  pl.BlockSpec((1, d), lambda j: (0, 0)),
                  pl.BlockSpec((1, N_MOD, d), lambda j: (mod_row, 0, 0)),
                  pl.BlockSpec((d, tn), lambda j: (0, j))],
        out_specs=pl.BlockSpec((b, lc, tn), lambda j: (0, 0, j)),
        out_shape=jax.ShapeDtypeStruct((b, lc, width), F32),
        scratch_shapes=[pltpu.VMEM((b, lc, d), BF16)],
        compiler_params=_params(("arbitrary",)),
        name="ctx_proj",
    )(ctx, g, mod3, w)


def _ctx_state_kernel(pv_ref, pf_ref, pb_ref, lgf_ref, lgb_ref, sf_ref, sb_ref):
    c = SCAN_CHUNK
    n_batch, rows = pv_ref.shape[0], pv_ref.shape[1]
    n_chunks = rows // c
    n_heads = pv_ref.shape[2] // HEAD_DIM
    for reverse, p_ref, lg_ref, s_ref in ((False, pf_ref, lgf_ref, sf_ref), (True, pb_ref, lgb_ref, sb_ref)):
        _, tri2 = _scan_masks(c, reverse)
        s_ref[...] = jnp.zeros(s_ref.shape, F32)

        def chunk_body(ci, carry, reverse=reverse, p_ref=p_ref, lg_ref=lg_ref, s_ref=s_ref, tri2=tri2):
            cidx = (n_chunks - 1 - ci) if reverse else ci
            r0 = pl.multiple_of(cidx * c, c)
            recurrences = [(h, bi) for h in range(n_heads) for bi in range(n_batch)]
            tile = lambda ref, h, bi: ref[bi, pl.ds(r0, c), h * HEAD_DIM:(h + 1) * HEAD_DIM]
            gates = [_forget_gate(tile(p_ref, h, bi), lg_ref[:, h * HEAD_DIM:(h + 1) * HEAD_DIM])
                     for h, bi in recurrences]
            sums = [_scan_cumsum(jnp.log2(f), tri2) for f in gates]
            for (h, bi), f, a in zip(recurrences, gates, sums):
                a_tot = _scan_refs(a, reverse)[3]
                kh = ((1.0 - f) * jnp.exp2(a_tot - a)).astype(BF16)
                upd = lax.dot_general(tile(pv_ref, h, bi).astype(BF16), kh, _TN, preferred_element_type=F32)
                s_ref[bi, h] = s_ref[bi, h] * jnp.exp2(a_tot) + upd
            return carry

        lax.fori_loop(0, n_chunks, chunk_body, 0)


def _ctx_state_call(pc, logits_f, logits_b, dh):
    b, lc, _ = pc.shape
    n_heads = dh // HEAD_DIM
    group = lambda n: pl.BlockSpec((b, lc, dh), lambda j: (0, 0, n))
    lg_spec = pl.BlockSpec(logits_f.shape, lambda j: (0, 0))
    st_spec = pl.BlockSpec((b, n_heads, HEAD_DIM, HEAD_DIM), lambda j: (0, 0, 0, 0))
    st_shape = jax.ShapeDtypeStruct((b, n_heads, HEAD_DIM, HEAD_DIM), F32)
    return pl.pallas_call(
        _ctx_state_kernel,
        grid=(1,),
        in_specs=[group(0), group(1), group(2), lg_spec, lg_spec],
        out_specs=[st_spec, st_spec],
        out_shape=[st_shape, st_shape],
        compiler_params=_params(("arbitrary",)),
        name="ctx_state",
    )(pc, pc, pc, logits_f, logits_b)


CONV_TILE_ROWS = 16
_HPAD = 16
_HROW = GRID_W + 2 * _HPAD


def _conv_kernel(cur_ref, prev_ref, next_ref, w_ref, b_ref, lg_ref, lb_ref, o_ref,
                 hpad_ref, vpad_ref, y_ref, *, n_tiles):
    tr = CONV_TILE_ROWS
    half = w_ref.shape[1] // 2
    n_lane_tiles = half // 128
    i = pl.program_id(1)

    hpad_ref[:, :, :_HPAD, :] = jnp.zeros((tr, n_lane_tiles, _HPAD, 128), F32)
    hpad_ref[:, :, _HPAD + GRID_W:, :] = jnp.zeros((tr, n_lane_tiles, _HPAD, 128), F32)
    for r in range(tr):
        for lt in range(n_lane_tiles):
            hpad_ref[r, lt, _HPAD:_HPAD + GRID_W, :] = cur_ref[0, r * GRID_W:(r + 1) * GRID_W,
                                                               lt * 128:(lt + 1) * 128]
    halo = CONV_PAD * GRID_W
    for lt in range(n_lane_tiles):
        lanes = slice(lt * 128, (lt + 1) * 128)
        top = prev_ref[0, tr * GRID_W - halo:, lanes]
        bot = next_ref[0, :halo, lanes]
        vpad_ref[lt, :halo, :] = jnp.where(i > 0, top, 0.0)
        vpad_ref[lt, halo:halo + tr * GRID_W, :] = cur_ref[0, :, half + lt * 128:half + (lt + 1) * 128]
        vpad_ref[lt, halo + tr * GRID_W:, :] = jnp.where(i < n_tiles - 1, bot, 0.0)

    def h_row(r, carry):
        o0 = pl.multiple_of(r * GRID_W, GRID_W)
        for lt in range(n_lane_tiles):
            lanes = slice(lt * 128, (lt + 1) * 128)
            acc = jnp.zeros((GRID_W, 128), F32)
            for jj in range(CONV_WIDTH):
                off = _HPAD - CONV_PAD + jj
                acc = acc + hpad_ref[r, lt, off:off + GRID_W, :] * w_ref[jj:jj + 1, lanes]
            y_ref[pl.ds(o0, GRID_W), lanes] = acc + b_ref[:, lanes]
        return carry

    lax.fori_loop(0, tr, h_row, 0)

    n_sub = GRID_W // 8
    for lt in range(n_lane_tiles):
        vl = slice(half + lt * 128, half + (lt + 1) * 128)
        wv = [jnp.broadcast_to(w_ref[jj:jj + 1, vl], (8, 128)) for jj in range(CONV_WIDTH)]
        bias = b_ref[:, vl]

        def v_row(r, carry, lt=lt, vl=vl, wv=wv, bias=bias):
            v0 = pl.multiple_of(r * GRID_W, GRID_W)
            acc = [jnp.zeros((8, 128), F32)] * n_sub
            for jj in range(CONV_WIDTH):
                src = pl.multiple_of(v0 + jj * GRID_W, GRID_W)
                tap = vpad_ref[lt, pl.ds(src, GRID_W), :]
                acc = [acc[s] + tap[8 * s:8 * s + 8, :] * wv[jj] for s in range(n_sub)]
            y_ref[pl.ds(v0, GRID_W), vl] = jnp.concatenate(acc, axis=0) + bias
            return carry

        lax.fori_loop(0, tr, v_row, 0, unroll=2)

    def ln_row(r, carry):
        o0 = pl.multiple_of(r * GRID_W, GRID_W)
        y = y_ref[pl.ds(o0, GRID_W), :]
        mu = jnp.mean(y, axis=-1, keepdims=True)
        yc = y - mu
        var = jnp.mean(yc * yc, axis=-1, keepdims=True)
        z = yc * lax.rsqrt(var + EPS) * lg_ref[...] + lb_ref[...]
        o_ref[0, pl.ds(o0, GRID_W), :] = _silu(z).astype(BF16)
        return carry

    lax.fori_loop(0, tr, ln_row, 0, unroll=2)


def _conv_call(u, w, bias, ln_g, ln_b):
    b, l, ch = u.shape
    half = ch // 2
    tile = CONV_TILE_ROWS * GRID_W
    n_tiles = l // tile
    return pl.pallas_call(
        functools.partial(_conv_kernel, n_tiles=n_tiles),
        grid=(b, n_tiles),
        in_specs=[pl.BlockSpec((1, tile, ch), lambda i, j: (i, j, 0)),
                  pl.BlockSpec((1, tile, half), lambda i, j: (i, jnp.maximum(j - 1, 0), 1)),
                  pl.BlockSpec((1, tile, half), lambda i, j: (i, jnp.minimum(j + 1, n_tiles - 1), 1)),
                  pl.BlockSpec((CONV_WIDTH, ch), lambda i, j: (0, 0)),
                  pl.BlockSpec((1, ch), lambda i, j: (0, 0)),
                  pl.BlockSpec((1, ch), lambda i, j: (0, 0)),
                  pl.BlockSpec((1, ch), lambda i, j: (0, 0))],
        out_specs=pl.BlockSpec((1, tile, ch), lambda i, j: (i, j, 0)),
        out_shape=jax.ShapeDtypeStruct((b, l, ch), BF16),
        scratch_shapes=[pltpu.VMEM((CONV_TILE_ROWS, half // 128, _HROW, 128), F32),
                        pltpu.VMEM((half // 128, tile + 2 * CONV_PAD * GRID_W, 128), F32),
                        pltpu.VMEM((tile, ch), F32)],
        compiler_params=_params(("parallel", "parallel")),
        name="conv",
    )(u, u, u, w, bias, ln_g, ln_b)


MIX_SLICE = 128

def _mix_kernel(uc_ref, og_ref, gc_ref, gh_ref, x_ref, wpw_ref, wh_ref, wo_ref,
                npost_ref, npre_ref, mod_ref, x1_ref, h2_ref):
    tm = x_ref.shape[1]
    n_split = tm // MIX_SLICE
    for r in range(n_split):
        rows = slice(r * MIX_SLICE, (r + 1) * MIX_SLICE)
        y_c = jnp.dot(uc_ref[0, rows, :], wpw_ref[...], preferred_element_type=F32)
        y_h = jnp.dot(og_ref[0, rows, :], wh_ref[...], preferred_element_type=F32)
        z = gc_ref[0, rows, :].astype(F32) * y_c + gh_ref[0, rows, :].astype(F32) * y_h
        y = jnp.dot(z.astype(BF16), wo_ref[...], preferred_element_type=F32)
        ms = jnp.mean(y * y, axis=-1, keepdims=True)
        yn = y * lax.rsqrt(ms + EPS) * npost_ref[...]
        x1 = x_ref[0, rows, :] + mod_ref[0, 2:3, :] * yn
        x1_ref[0, rows, :] = x1
        h2_ref[0, rows, :] = _norm_modulate(x1, npre_ref[...], mod_ref[0, 3:4, :],
                                            mod_ref[0, 4:5, :]).astype(BF16)


def _mix_call(uc, og, gates, x, wpw, wh, wo, npost, npre, mod3, tm=512):
    b, l, d = x.shape
    dc = uc.shape[2]
    gc = gh = gates
    row = lambda width, col=0: pl.BlockSpec((1, tm, width), lambda i, j: (i, j, col))
    const = lambda shape: pl.BlockSpec(shape, lambda i, j: (0,) * len(shape), pipeline_mode=pl.Buffered(1))
    return pl.pallas_call(
        _mix_kernel,
        grid=(b, l // tm),
        in_specs=[row(dc), row(dc), row(d, 0), row(d, 1), row(d),
                  const(wpw.shape), const(wh.shape), const(wo.shape),
                  const((1, d)), const((1, d)),
                  pl.BlockSpec((1, N_MOD, d), lambda i, j: (i, 0, 0))],
        out_specs=[row(d), row(d)],
        out_shape=[jax.ShapeDtypeStruct((b, l, d), F32), jax.ShapeDtypeStruct((b, l, d), BF16)],
        compiler_params=_params(("parallel", "parallel")),
        name="mix",
    )(uc, og, gc, gh, x, wpw, wh, wo, npost, npre, mod3)


def _mlp_kernel(h_ref, w1_ref, w2_ref, x1_ref, npost_ref, mod_ref, o_ref, *, n_ff):
    j = pl.program_id(2)

    @pl.when(j == 0)
    def _():
        o_ref[...] = jnp.zeros(o_ref.shape, F32)

    a = jnp.dot(h_ref[0], w1_ref[...], preferred_element_type=F32)
    a = jnp.square(jnp.maximum(a, 0.0)).astype(BF16)
    o_ref[0] += jnp.dot(a, w2_ref[...], preferred_element_type=F32)

    @pl.when(j == n_ff - 1)
    def _():
        rb = 32
        for r in range(o_ref.shape[1] // rb):
            rows = slice(r * rb, (r + 1) * rb)
            y = o_ref[0, rows, :]
            ms = jnp.mean(y * y, axis=-1, keepdims=True)
            yn = y * lax.rsqrt(ms + EPS) * npost_ref[...]
            o_ref[0, rows, :] = x1_ref[0, rows, :] + mod_ref[0, 5:6, :] * yn


def _mlp_call(h2, w1, w2, x1, npost, mod3, tm=1024, tf=512):
    b, l, d = x1.shape
    dff = w1.shape[1]
    n_ff = dff // tf
    return pl.pallas_call(
        functools.partial(_mlp_kernel, n_ff=n_ff),
        grid=(b, l // tm, n_ff),
        in_specs=[pl.BlockSpec((1, tm, d), lambda i, m, j: (i, m, 0)),
                  pl.BlockSpec((d, tf), lambda i, m, j: (0, j)),
                  pl.BlockSpec((tf, d), lambda i, m, j: (j, 0)),
                  pl.BlockSpec((1, tm, d), lambda i, m, j: (i, m, 0), pipeline_mode=pl.Buffered(1)),
                  pl.BlockSpec((1, d), lambda i, m, j: (0, 0)),
                  pl.BlockSpec((1, N_MOD, d), lambda i, m, j: (i, 0, 0))],
        out_specs=pl.BlockSpec((1, tm, d), lambda i, m, j: (i, m, 0)),
        out_shape=jax.ShapeDtypeStruct((b, l, d), F32),
        compiler_params=_params(("parallel", "parallel", "arbitrary")),
        name="mlp",
    )(h2, w1, w2, x1, npost, mod3)


def kernel(x, c, ctx, c_ctx, w_mod, b_mod, norm_pre_mix, norm_post_mix, norm_pre_mlp, norm_post_mlp, w_in,
           conv_dw_w, conv_dw_b, conv_ln_g, conv_ln_b, conv_pw_w, hgrn_lb_logits, hgrn_norm_g, hgrn_out_w,
           w_out, mlp_w1, mlp_w2):
    assert w_mod.shape[0] == 1, "single-layer block"
    b, l, d = x.shape
    lc = ctx.shape[1]
    dh = hgrn_norm_g.shape[1]
    dc = conv_dw_w.shape[2]
    assert l % (CONV_TILE_ROWS * GRID_W) == 0 and CONV_PAD <= CONV_TILE_ROWS and CONV_PAD <= _HPAD
    assert l % SCAN_BLOCK == 0 and lc % SCAN_CHUNK == 0

    mod_rows = 16
    cc = jnp.concatenate([c, c_ctx[None, :], jnp.zeros((mod_rows - b - 1, d), F32)], axis=0)
    mod3 = _mod_call(cc, w_mod[0], b_mod).reshape(mod_rows, N_MOD, d)

    w_in0 = w_in[0]
    lb_f, lb_b = hgrn_lb_logits[0], hgrn_lb_logits[1]
    off = lambda n: n * dh

    pc = _ctx_proj_call(ctx, norm_pre_mix, mod3, b, w_in0, 3 * dh)
    s_f, s_b = _ctx_state_call(pc, lb_f, lb_b, dh)

    h, v = _prenorm_v_call(x, norm_pre_mix, mod3, w_in0, dh)
    h = h.reshape(b * l, d)
    seq = lambda t: t.reshape(b, l, t.shape[-1])
    lf, kk = _proj_call(h, w_in0, [off(1)], 2 * dh, _ep_forget, [F32, BF16],
                        extra=(jnp.concatenate([lb_f, lb_b], axis=1),), name="proj_forget")
    qg_scale = jnp.concatenate([jnp.full((1, dh), HEAD_DIM ** -0.5, F32), jnp.ones((1, dh), F32)], axis=1)
    qg, wpw, wh, wo = _proj_call(h, w_in0, [off(3)], 2 * dh, _ep_scaled_silu, [BF16], extra=(qg_scale,),
                                 side=(conv_pw_w[0], hgrn_out_w[0], w_out[0]), name="proj_qg")
    (u,) = _proj_call(h, w_in0, [off(5), off(5) + dc], dc, _ep_glu, [F32], tn=512, name="proj_glu")
    gates, w1, w2 = _proj_call(h, w_in0, [off(5) + 2 * dc], 2 * d, _ep_sigmoid, [BF16],
                               side=(mlp_w1[0], mlp_w2[0]), name="proj_gates")

    lf, kk, qg = seq(lf), seq(kk), seq(qg)
    o_b = _scan_call((kk, 1), (v, 0), (lf, 1), s_b, (qg, 0), reverse=True)
    og = _scan_call((kk, 0), (v, 0), (lf, 0), s_f, (qg, 0), reverse=False, ob=(o_b, 0), g=(qg, 1),
                    ng=hgrn_norm_g)

    uc = _conv_call(seq(u), conv_dw_w[0], conv_dw_b, conv_ln_g, conv_ln_b)

    x1, h2 = _mix_call(uc, og, seq(gates), x, wpw, wh, wo, norm_post_mix, norm_pre_mlp, mod3)
    return _mlp_call(h2, w1, w2, x1, norm_post_mlp, mod3)
```

```python
import functools

import jax
import jax.numpy as jnp
from jax import lax
from jax.experimental import pallas as pl
from jax.experimental.pallas import tpu as pltpu

F32 = jnp.float32
BF16 = jnp.bfloat16

GRID_W = 64
CONV_WIDTH = 31
CONV_PAD = (CONV_WIDTH - 1) // 2
HEAD_DIM = 128
N_MOD = 6
EPS = 1e-6

SCAN_CHUNK = 64
SCAN_BLOCK = 512

VMEM_LIMIT = 56 * 1024 * 1024

_NT = (((1,), (1,)), ((), ()))
_TN = (((0,), (0,)), ((), ()))


def _params(sem):
    return pltpu.CompilerParams(dimension_semantics=sem, vmem_limit_bytes=VMEM_LIMIT)


def _sigmoid(x):
    return 0.5 * jnp.tanh(0.5 * x) + 0.5


def _silu(x):
    return x * _sigmoid(x)


def _norm_modulate(x, gain, shift, scale):
    ms = jnp.mean(x * x, axis=-1, keepdims=True)
    return (x * lax.rsqrt(ms + EPS) * gain) * (1.0 + scale) + shift


def _forget_gate(raw, logits):
    e = jnp.exp(logits - jnp.max(logits, axis=0, keepdims=True))
    lb = e[0:1] / jnp.sum(e, axis=0, keepdims=True)
    return lb + (1.0 - lb) * _sigmoid(raw)


def _mod_kernel(c_ref, w_ref, b_ref, o_ref):
    s = _silu(c_ref[...]).astype(BF16)
    o_ref[...] = jnp.dot(s, w_ref[...].astype(BF16), preferred_element_type=F32) + b_ref[...]


def _mod_call(cc, w_mod, b_mod, tn=1024):
    rows, d = cc.shape
    n = w_mod.shape[1]
    return pl.pallas_call(
        _mod_kernel,
        grid=(n // tn,),
        in_specs=[pl.BlockSpec((rows, d), lambda j: (0, 0)),
                  pl.BlockSpec((d, tn), lambda j: (0, j)),
                  pl.BlockSpec((1, tn), lambda j: (0, j))],
        out_specs=pl.BlockSpec((rows, tn), lambda j: (0, j)),
        out_shape=jax.ShapeDtypeStruct((rows, n), F32),
        compiler_params=_params(("parallel",)),
        name="mod",
    )(cc, w_mod, b_mod)


def _proj_kernel(*refs, n_w, n_extra, n_side, n_out, epilogue):
    h_ref = refs[0]
    pos = 1
    w_refs = refs[pos:pos + n_w]
    pos += n_w
    extra = refs[pos:pos + n_extra]
    pos += n_extra
    side_in = refs[pos:pos + n_side]
    pos += n_side
    out_refs = refs[pos:pos + n_out]
    pos += n_out
    side_out = refs[pos:pos + n_side]
    pos += n_side
    wbf_refs = refs[pos:]

    @pl.when(pl.program_id(1) == 0)
    def _():
        for w_ref, wbf_ref in zip(w_refs, wbf_refs):
            wbf_ref[...] = w_ref[...].astype(BF16)

    for si_ref, so_ref in zip(side_in, side_out):
        so_ref[...] = si_ref[...].astype(BF16)

    h = h_ref[...]
    accs = [jnp.dot(h, wbf_ref[...], preferred_element_type=F32) for wbf_ref in wbf_refs]
    outs = epilogue(accs, extra)
    for o_ref, o in zip(out_refs, outs):
        o_ref[...] = o.astype(o_ref.dtype)


def _ep_forget(accs, extra):
    f = _forget_gate(accs[0], extra[0][...])
    return jnp.log2(f), 1.0 - f


def _ep_scaled_silu(accs, extra):
    return (_silu(accs[0]) * extra[0][...],)


def _ep_glu(accs, extra):
    return (accs[0] * _sigmoid(accs[1]),)


def _ep_sigmoid(accs, extra):
    return (_sigmoid(accs[0]),)


def _proj_call(h, w, col_offsets, width, epilogue, out_dtypes, extra=(), side=(), tm=1024, tn=1024,
               name="proj"):
    m, k = h.shape
    tm = min(tm, m)
    n_w = len(col_offsets)
    n_j, n_i = width // tn, m // tm
    in_specs = [pl.BlockSpec((tm, k), lambda j, i: (i, 0))]
    for off in col_offsets:
        in_specs.append(pl.BlockSpec((k, tn), lambda j, i, off=off: (0, off // tn + j)))
    for e in extra:
        in_specs.append(pl.BlockSpec((e.shape[0], tn), lambda j, i: (0, j)))
    side_specs = [pl.BlockSpec((a.shape[0] // (n_j * n_i), a.shape[1]), lambda j, i: (j * n_i + i, 0))
                  for a in side]
    return pl.pallas_call(
        functools.partial(_proj_kernel, n_w=n_w, n_extra=len(extra), n_side=len(side),
                          n_out=len(out_dtypes), epilogue=epilogue),
        grid=(n_j, n_i),
        in_specs=in_specs + side_specs,
        out_specs=[pl.BlockSpec((tm, tn), lambda j, i: (i, j)) for _ in out_dtypes] + side_specs,
        out_shape=[jax.ShapeDtypeStruct((m, width), dt) for dt in out_dtypes]
                  + [jax.ShapeDtypeStruct(a.shape, BF16) for a in side],
        scratch_shapes=[pltpu.VMEM((k, tn), BF16) for _ in range(n_w)],
        compiler_params=_params(("parallel", "arbitrary")),
        name=name,
    )(h, *([w] * n_w), *extra, *side)


def _prenorm_v_kernel(x_ref, g_ref, mod_ref, w_ref, h_ref, v_ref, wbf_ref):
    @pl.when((pl.program_id(0) == 0) & (pl.program_id(1) == 0))
    def _():
        wbf_ref[...] = w_ref[...].astype(BF16)

    h = _norm_modulate(x_ref[0], g_ref[...], mod_ref[0, 0:1, :], mod_ref[0, 1:2, :]).astype(BF16)
    h_ref[0] = h
    v_ref[0] = jnp.dot(h, wbf_ref[...], preferred_element_type=F32).astype(BF16)


def _prenorm_v_call(x, g, mod3, w, width, tl=512):
    b, l, d = x.shape
    return pl.pallas_call(
        _prenorm_v_kernel,
        grid=(b, l // tl),
        in_specs=[pl.BlockSpec((1, tl, d), lambda i, j: (i, j, 0)),
                  pl.BlockSpec((1, d), lambda i, j: (0, 0)),
                  pl.BlockSpec((1, N_MOD, d), lambda i, j: (i, 0, 0)),
                  pl.BlockSpec((d, width), lambda i, j: (0, 0), pipeline_mode=pl.Buffered(1))],
        out_specs=[pl.BlockSpec((1, tl, d), lambda i, j: (i, j, 0)),
                   pl.BlockSpec((1, tl, width), lambda i, j: (i, j, 0))],
        out_shape=[jax.ShapeDtypeStruct((b, l, d), BF16), jax.ShapeDtypeStruct((b, l, width), BF16)],
        scratch_shapes=[pltpu.VMEM((d, width), BF16)],
        compiler_params=_params(("arbitrary", "arbitrary")),
        name="prenorm_v",
    )(x, g, mod3, w)


def _scan_masks(c, reverse):
    t = lax.broadcasted_iota(jnp.int32, (c, c), 0)
    s = lax.broadcasted_iota(jnp.int32, (c, c), 1)
    half = c // 2
    before = (s >= t) if reverse else (s <= t)
    mask1 = ((t < half) == (s < half)) & before
    tri = jnp.where(before, 1.0, 0.0).astype(BF16)
    return mask1, jnp.concatenate([tri, tri], axis=1)


def _scan_cumsum(lf, tri2):
    hi = lf.astype(BF16)
    lo = (lf - hi.astype(F32)).astype(BF16)
    return jnp.dot(tri2, jnp.concatenate([hi, lo], axis=0), preferred_element_type=F32)


def _scan_refs(a, reverse):
    c = a.shape[0]
    half, quarter = c // 2, c // 4
    if reverse:
        return a[quarter:quarter + 1], a[half + quarter:half + quarter + 1], a[half:half + 1], a[0:1]
    return a[quarter - 1:quarter], a[half + quarter - 1:half + quarter], a[half - 1:half], a[c - 1:c]


def _scan_products(a, q, k, v, st, st_kv, reverse):
    c = a.shape[0]
    half = c // 2
    m1_lo, m1_hi, m0, a_tot = _scan_refs(a, reverse)
    lo_rows, hi_rows = slice(0, half), slice(half, c)
    row = lax.broadcasted_iota(jnp.int32, a.shape, 0)
    e1 = a - jnp.where(row < half, m1_lo, m1_hi)
    pq = q.astype(F32) * jnp.exp2(e1)
    pk = k.astype(F32) * jnp.exp2(-e1)
    s1 = lax.dot_general(pq.astype(BF16), pk.astype(BF16), _NT, preferred_element_type=F32)
    if reverse:
        q_rows, m1_q, k_rows, m1_k = lo_rows, m1_lo, hi_rows, m1_hi
    else:
        q_rows, m1_q, k_rows, m1_k = hi_rows, m1_hi, lo_rows, m1_lo
    q0 = (pq[q_rows] * jnp.exp2(m1_q - m0)).astype(BF16)
    k0 = (pk[k_rows] * jnp.exp2(m0 - m1_k)).astype(BF16)
    s0 = lax.dot_general(q0, k0, _NT, preferred_element_type=F32)
    qi = jnp.concatenate([pq[lo_rows] * jnp.exp2(m1_lo), pq[hi_rows] * jnp.exp2(m1_hi)], axis=0)
    inter = jnp.dot(qi.astype(BF16), st_kv, preferred_element_type=F32)
    kh = jnp.concatenate([pk[lo_rows] * jnp.exp2(a_tot - m1_lo), pk[hi_rows] * jnp.exp2(a_tot - m1_hi)], axis=0)
    upd = lax.dot_general(v, kh.astype(BF16), _TN, preferred_element_type=F32)
    return s1, s0, inter, st * jnp.exp2(a_tot) + upd


def _scan_output(s1, s0, inter, v, mask1, reverse):
    c = s1.shape[0]
    half = c // 2
    lo_rows, hi_rows = slice(0, half), slice(half, c)
    q_rows, k_rows = (lo_rows, hi_rows) if reverse else (hi_rows, lo_rows)
    intra = jnp.dot(jnp.where(mask1, s1, 0.0).astype(BF16), v, preferred_element_type=F32)
    cross = jnp.dot(s0.astype(BF16), v[k_rows], preferred_element_type=F32)
    o = inter + intra
    o_q = o[q_rows] + cross
    return jnp.concatenate([o_q, o[hi_rows]] if reverse else [o[lo_rows], o_q], axis=0)


def _scan_kernel(*refs, reverse, readout):
    k_ref, v_ref, lf_ref, s0_ref, q_ref = refs[:5]
    if readout:
        (ob_ref, g_ref, ng_ref, o_ref, st_ref, stkv_ref, a_ref) = refs[5:]
    else:
        (o_ref, st_ref, stkv_ref, a_ref) = refs[5:]
    c = SCAN_CHUNK
    n_batch, rows = k_ref.shape[0], k_ref.shape[1]
    n_chunks = rows // c
    n_heads = k_ref.shape[2] // HEAD_DIM
    recurrences = [(h, bi) for h in range(n_heads) for bi in range(n_batch)]

    @pl.when(pl.program_id(0) == 0)
    def _():
        st_ref[...] = s0_ref[...]
        for h, bi in recurrences:
            stkv_ref[bi, h] = s0_ref[bi, h].T.astype(BF16)

    mask1, tri2 = _scan_masks(c, reverse)

    def cumsum_body(ci, carry):
        r0 = pl.multiple_of(ci * c, c)
        for bi in range(n_batch):
            a_ref[bi, pl.ds(r0, c), :] = _scan_cumsum(lf_ref[bi, pl.ds(r0, c), :], tri2)
        return carry

    lax.fori_loop(0, n_chunks, cumsum_body, 0, unroll=4)

    def chunk_body(ci, carry):
        cidx = (n_chunks - 1 - ci) if reverse else ci
        r0 = pl.multiple_of(cidx * c, c)
        tile = lambda ref, h, bi: ref[bi, pl.ds(r0, c), h * HEAD_DIM:(h + 1) * HEAD_DIM]
        wave = []
        for h, bi in recurrences:
            s1, s0, inter, st_new = _scan_products(
                tile(a_ref, h, bi), tile(q_ref, h, bi), tile(k_ref, h, bi), tile(v_ref, h, bi),
                st_ref[bi, h], stkv_ref[bi, h], reverse)
            st_ref[bi, h] = st_new
            stkv_ref[bi, h] = st_new.T.astype(BF16)
            wave.append((s1, s0, inter))
        for (h, bi), (s1, s0, inter) in zip(recurrences, wave):
            lanes = slice(h * HEAD_DIM, (h + 1) * HEAD_DIM)
            o = _scan_output(s1, s0, inter, tile(v_ref, h, bi), mask1, reverse)
            if readout:
                o = o + tile(ob_ref, h, bi)
                ms = jnp.mean(o * o, axis=-1, keepdims=True)
                y = o * lax.rsqrt(ms + EPS) * ng_ref[:, lanes]
                o = y * tile(g_ref, h, bi).astype(F32)
            o_ref[bi, pl.ds(r0, c), lanes] = o.astype(o_ref.dtype)
        return carry

    lax.fori_loop(0, n_chunks, chunk_body, 0, unroll=4)


def _scan_call(k, v, lf, s0, q, *, reverse, ob=None, g=None, ng=None):
    b, l, _ = v[0].shape
    dh = ng.shape[1] if ng is not None else s0.shape[1] * HEAD_DIM
    tb = min(SCAN_BLOCK, l)
    n_blocks = l // tb
    readout = ob is not None
    blk = (lambda j: n_blocks - 1 - j) if reverse else (lambda j: j)
    seq_spec = lambda group: pl.BlockSpec((b, tb, dh), lambda j: (0, blk(j), group))
    st_spec = pl.BlockSpec(s0.shape, lambda j: (0, 0, 0, 0))
    in_specs = [seq_spec(k[1]), seq_spec(v[1]), seq_spec(lf[1]), st_spec, seq_spec(q[1])]
    args = [k[0], v[0], lf[0], s0, q[0]]
    if readout:
        in_specs += [seq_spec(ob[1]), seq_spec(g[1]), pl.BlockSpec((1, dh), lambda j: (0, 0))]
        args += [ob[0], g[0], ng]
    return pl.pallas_call(
        functools.partial(_scan_kernel, reverse=reverse, readout=readout),
        grid=(n_blocks,),
        in_specs=in_specs,
        out_specs=seq_spec(0),
        out_shape=jax.ShapeDtypeStruct((b, l, dh), BF16 if readout else F32),
        scratch_shapes=[pltpu.VMEM(s0.shape, F32), pltpu.VMEM(s0.shape, BF16), pltpu.VMEM((b, tb, dh), F32)],
        compiler_params=_params(("arbitrary",)),
        name="scan_readout_fwd" if readout else "scan_out_bwd",
    )(*args)


def _ctx_proj_kernel(ctx_ref, g_ref, mod_ref, w_ref, o_ref, hc_ref):
    @pl.when(pl.program_id(0) == 0)
    def _():
        for bi in range(ctx_ref.shape[0]):
            hc_ref[bi] = _norm_modulate(ctx_ref[bi], g_ref[...], mod_ref[0, 0:1, :],
                                        mod_ref[0, 1:2, :]).astype(BF16)

    w = w_ref[...].astype(BF16)
    for bi in range(ctx_ref.shape[0]):
        o_ref[bi] = jnp.dot(hc_ref[bi], w, preferred_element_type=F32)


def _ctx_proj_call(ctx, g, mod3, mod_row, w, width, tn=1024):
    b, lc, d = ctx.shape
    return pl.pallas_call(
        _ctx_proj_kernel,
        grid=(width // tn,),
        in_specs=[pl.BlockSpec((b, lc, d), lambda j: (0, 0, 0)),
                  pl.BlockSpec((1, d), lambda j: (0, 0)),
                  pl.BlockSpec((1, N_MOD, d), lambda j: (mod_row, 0, 0)),
                  pl.BlockSpec((d, tn), lambda j: (0, j))],
        out_specs=pl.BlockSpec((b, lc, tn), lambda j: (0, 0, j)),
        out_shape=jax.ShapeDtypeStruct((b, lc, width), F32),
        scratch_shapes=[pltpu.VMEM((b, lc, d), BF16)],
        compiler_params=_params(("arbitrary",)),
        name="ctx_proj",
    )(ctx, g, mod3, w)


def _ctx_state_kernel(pv_ref, pf_ref, pb_ref, lgf_ref, lgb_ref, sf_ref, sb_ref):
    c = SCAN_CHUNK
    n_batch, rows = pv_ref.shape[0], pv_ref.shape[1]
    n_chunks = rows // c
    n_heads = pv_ref.shape[2] // HEAD_DIM
    for reverse, p_ref, lg_ref, s_ref in ((False, pf_ref, lgf_ref, sf_ref), (True, pb_ref, lgb_ref, sb_ref)):
        _, tri2 = _scan_masks(c, reverse)
        s_ref[...] = jnp.zeros(s_ref.shape, F32)

        def chunk_body(ci, carry, reverse=reverse, p_ref=p_ref, lg_ref=lg_ref, s_ref=s_ref, tri2=tri2):
            cidx = (n_chunks - 1 - ci) if reverse else ci
            r0 = pl.multiple_of(cidx * c, c)
            recurrences = [(h, bi) for h in range(n_heads) for bi in range(n_batch)]
            tile = lambda ref, h, bi: ref[bi, pl.ds(r0, c), h * HEAD_DIM:(h + 1) * HEAD_DIM]
            gates = [_forget_gate(tile(p_ref, h, bi), lg_ref[:, h * HEAD_DIM:(h + 1) * HEAD_DIM])
                     for h, bi in recurrences]
            sums = [_scan_cumsum(jnp.log2(f), tri2) for f in gates]
            for (h, bi), f, a in zip(recurrences, gates, sums):
                a_tot = _scan_refs(a, reverse)[3]
                kh = ((1.0 - f) * jnp.exp2(a_tot - a)).astype(BF16)
                upd = lax.dot_general(tile(pv_ref, h, bi).astype(BF16), kh, _TN, preferred_element_type=F32)
                s_ref[bi, h] = s_ref[bi, h] * jnp.exp2(a_tot) + upd
            return carry

        lax.fori_loop(0, n_chunks, chunk_body, 0)


def _ctx_state_call(pc, logits_f, logits_b, dh):
    b, lc, _ = pc.shape
    n_heads = dh // HEAD_DIM
    group = lambda n: pl.BlockSpec((b, lc, dh), lambda j: (0, 0, n))
    lg_spec = pl.BlockSpec(logits_f.shape, lambda j: (0, 0))
    st_spec = pl.BlockSpec((b, n_heads, HEAD_DIM, HEAD_DIM), lambda j: (0, 0, 0, 0))
    st_shape = jax.ShapeDtypeStruct((b, n_heads, HEAD_DIM, HEAD_DIM), F32)
    return pl.pallas_call(
        _ctx_state_kernel,
        grid=(1,),
        in_specs=[group(0), group(1), group(2), lg_spec, lg_spec],
        out_specs=[st_spec, st_spec],
        out_shape=[st_shape, st_shape],
        compiler_params=_params(("arbitrary",)),
        name="ctx_state",
    )(pc, pc, pc, logits_f, logits_b)


CONV_TILE_ROWS = 16
_HPAD = 16
_HROW = GRID_W + 2 * _HPAD


def _conv_kernel(cur_ref, prev_ref, next_ref, w_ref, b_ref, lg_ref, lb_ref, o_ref,
                 hpad_ref, vpad_ref, y_ref, *, n_tiles):
    tr = CONV_TILE_ROWS
    half = w_ref.shape[1] // 2
    n_lane_tiles = half // 128
    i = pl.program_id(1)

    hpad_ref[:, :, :_HPAD, :] = jnp.zeros((tr, n_lane_tiles, _HPAD, 128), F32)
    hpad_ref[:, :, _HPAD + GRID_W:, :] = jnp.zeros((tr, n_lane_tiles, _HPAD, 128), F32)
    for r in range(tr):
        for lt in range(n_lane_tiles):
            hpad_ref[r, lt, _HPAD:_HPAD + GRID_W, :] = cur_ref[0, r * GRID_W:(r + 1) * GRID_W,
                                                               lt * 128:(lt + 1) * 128]
    halo = CONV_PAD * GRID_W
    for lt in range(n_lane_tiles):
        lanes = slice(lt * 128, (lt + 1) * 128)
        top = prev_ref[0, tr * GRID_W - halo:, lanes]
        bot = next_ref[0, :halo, lanes]
        vpad_ref[lt, :halo, :] = jnp.where(i > 0, top, 0.0)
        vpad_ref[lt, halo:halo + tr * GRID_W, :] = cur_ref[0, :, half + lt * 128:half + (lt + 1) * 128]
        vpad_ref[lt, halo + tr * GRID_W:, :] = jnp.where(i < n_tiles - 1, bot, 0.0)

    def h_row(r, carry):
        o0 = pl.multiple_of(r * GRID_W, GRID_W)
        for lt in range(n_lane_tiles):
            lanes = slice(lt * 128, (lt + 1) * 128)
            acc = jnp.zeros((GRID_W, 128), F32)
            for jj in range(CONV_WIDTH):
                off = _HPAD - CONV_PAD + jj
                acc = acc + hpad_ref[r, lt, off:off + GRID_W, :] * w_ref[jj:jj + 1, lanes]
            y_ref[pl.ds(o0, GRID_W), lanes] = acc + b_ref[:, lanes]
        return carry

    lax.fori_loop(0, tr, h_row, 0)

    n_sub = GRID_W // 8
    for lt in range(n_lane_tiles):
        vl = slice(half + lt * 128, half + (lt + 1) * 128)
        wv = [jnp.broadcast_to(w_ref[jj:jj + 1, vl], (8, 128)) for jj in range(CONV_WIDTH)]
        bias = b_ref[:, vl]

        def v_row(r, carry, lt=lt, vl=vl, wv=wv, bias=bias):
            v0 = pl.multiple_of(r * GRID_W, GRID_W)
            acc = [jnp.zeros((8, 128), F32)] * n_sub
            for jj in range(CONV_WIDTH):
                src = pl.multiple_of(v0 + jj * GRID_W, GRID_W)
                tap = vpad_ref[lt, pl.ds(src, GRID_W), :]
                acc = [acc[s] + tap[8 * s:8 * s + 8, :] * wv[jj] for s in range(n_sub)]
            y_ref[pl.ds(v0, GRID_W), vl] = jnp.concatenate(acc, axis=0) + bias
            return carry

        lax.fori_loop(0, tr, v_row, 0, unroll=2)

    def ln_row(r, carry):
        o0 = pl.multiple_of(r * GRID_W, GRID_W)
        y = y_ref[pl.ds(o0, GRID_W), :]
        mu = jnp.mean(y, axis=-1, keepdims=True)
        yc = y - mu
        var = jnp.mean(yc * yc, axis=-1, keepdims=True)
        z = yc * lax.rsqrt(var + EPS) * lg_ref[...] + lb_ref[...]
        o_ref[0, pl.ds(o0, GRID_W), :] = _silu(z).astype(BF16)
        return carry

    lax.fori_loop(0, tr, ln_row, 0, unroll=2)


def _conv_call(u, w, bias, ln_g, ln_b):
    b, l, ch = u.shape
    half = ch // 2
    tile = CONV_TILE_ROWS * GRID_W
    n_tiles = l // tile
    return pl.pallas_call(
        functools.partial(_conv_kernel, n_tiles=n_tiles),
        grid=(b, n_tiles),
        in_specs=[pl.BlockSpec((1, tile, ch), lambda i, j: (i, j, 0)),
                  pl.BlockSpec((1, tile, half), lambda i, j: (i, jnp.maximum(j - 1, 0), 1)),
                  pl.BlockSpec((1, tile, half), lambda i, j: (i, jnp.minimum(j + 1, n_tiles - 1), 1)),
                  pl.BlockSpec((CONV_WIDTH, ch), lambda i, j: (0, 0)),
                  pl.BlockSpec((1, ch), lambda i, j: (0, 0)),
                  pl.BlockSpec((1, ch), lambda i, j: (0, 0)),
                  pl.BlockSpec((1, ch), lambda i, j: (0, 0))],
        out_specs=pl.BlockSpec((1, tile, ch), lambda i, j: (i, j, 0)),
        out_shape=jax.ShapeDtypeStruct((b, l, ch), BF16),
        scratch_shapes=[pltpu.VMEM((CONV_TILE_ROWS, half // 128, _HROW, 128), F32),
                        pltpu.VMEM((half // 128, tile + 2 * CONV_PAD * GRID_W, 128), F32),
                        pltpu.VMEM((tile, ch), F32)],
        compiler_params=_params(("parallel", "parallel")),
        name="conv",
    )(u, u, u, w, bias, ln_g, ln_b)


MIX_SLICE = 128

def _mix_kernel(uc_ref, og_ref, gc_ref, gh_ref, x_ref, wpw_ref, wh_ref, wo_ref,
                npost_ref, npre_ref, mod_ref, x1_ref, h2_ref):
    tm = x_ref.shape[1]
    n_split = tm // MIX_SLICE
    for r in range(n_split):
        rows = slice(r * MIX_SLICE, (r + 1) * MIX_SLICE)
        y_c = jnp.dot(uc_ref[0, rows, :], wpw_ref[...], preferred_element_type=F32)
        y_h = jnp.dot(og_ref[0, rows, :], wh_ref[...], preferred_element_type=F32)
        z = gc_ref[0, rows, :].astype(F32) * y_c + gh_ref[0, rows, :].astype(F32) * y_h
        y = jnp.dot(z.astype(BF16), wo_ref[...], preferred_element_type=F32)
        ms = jnp.mean(y * y, axis=-1, keepdims=True)
        yn = y * lax.rsqrt(ms + EPS) * npost_ref[...]
        x1 = x_ref[0, rows, :] + mod_ref[0, 2:3, :] * yn
        x1_ref[0, rows, :] = x1
        h2_ref[0, rows, :] = _norm_modulate(x1, npre_ref[...], mod_ref[0, 3:4, :],
                                            mod_ref[0, 4:5, :]).astype(BF16)


def _mix_call(uc, og, gates, x, wpw, wh, wo, npost, npre, mod3, tm=512):
    b, l, d = x.shape
    dc = uc.shape[2]
    gc = gh = gates
    row = lambda width, col=0: pl.BlockSpec((1, tm, width), lambda i, j: (i, j, col))
    const = lambda shape: pl.BlockSpec(shape, lambda i, j: (0,) * len(shape), pipeline_mode=pl.Buffered(1))
    return pl.pallas_call(
        _mix_kernel,
        grid=(b, l // tm),
        in_specs=[row(dc), row(dc), row(d, 0), row(d, 1), row(d),
                  const(wpw.shape), const(wh.shape), const(wo.shape),
                  const((1, d)), const((1, d)),
                  pl.BlockSpec((1, N_MOD, d), lambda i, j: (i, 0, 0))],
        out_specs=[row(d), row(d)],
        out_shape=[jax.ShapeDtypeStruct((b, l, d), F32), jax.ShapeDtypeStruct((b, l, d), BF16)],
        compiler_params=_params(("parallel", "parallel")),
        name="mix",
    )(uc, og, gc, gh, x, wpw, wh, wo, npost, npre, mod3)


def _mlp_kernel(h_ref, w1_ref, w2_ref, x1_ref, npost_ref, mod_ref, o_ref, *, n_ff):
    j = pl.program_id(2)

    @pl.when(j == 0)
    def _():
        o_ref[...] = jnp.zeros(o_ref.shape, F32)

    a = jnp.dot(h_ref[0], w1_ref[...], preferred_element_type=F32)
    a = jnp.square(jnp.maximum(a, 0.0)).astype(BF16)
    o_ref[0] += jnp.dot(a, w2_ref[...], preferred_element_type=F32)

    @pl.when(j == n_ff - 1)
    def _():
        rb = 32
        for r in range(o_ref.shape[1] // rb):
            rows = slice(r * rb, (r + 1) * rb)
            y = o_ref[0, rows, :]
            ms = jnp.mean(y * y, axis=-1, keepdims=True)
            yn = y * lax.rsqrt(ms + EPS) * npost_ref[...]
            o_ref[0, rows, :] = x1_ref[0, rows, :] + mod_ref[0, 5:6, :] * yn


def _mlp_call(h2, w1, w2, x1, npost, mod3, tm=1024, tf=1024):
    b, l, d = x1.shape
    dff = w1.shape[1]
    n_ff = dff // tf
    return pl.pallas_call(
        functools.partial(_mlp_kernel, n_ff=n_ff),
        grid=(b, l // tm, n_ff),
        in_specs=[pl.BlockSpec((1, tm, d), lambda i, m, j: (i, m, 0)),
                  pl.BlockSpec((d, tf), lambda i, m, j: (0, j)),
                  pl.BlockSpec((tf, d), lambda i, m, j: (j, 0)),
                  pl.BlockSpec((1, tm, d), lambda i, m, j: (i, m, 0), pipeline_mode=pl.Buffered(1)),
                  pl.BlockSpec((1, d), lambda i, m, j: (0, 0)),
                  pl.BlockSpec((1, N_MOD, d), lambda i, m, j: (i, 0, 0))],
        out_specs=pl.BlockSpec((1, tm, d), lambda i, m, j: (i, m, 0)),
        out_shape=jax.ShapeDtypeStruct((b, l, d), F32),
        compiler_params=_params(("parallel", "parallel", "arbitrary")),
        name="mlp",
    )(h2, w1, w2, x1, npost, mod3)


def kernel(x, c, ctx, c_ctx, w_mod, b_mod, norm_pre_mix, norm_post_mix, norm_pre_mlp, norm_post_mlp, w_in,
           conv_dw_w, conv_dw_b, conv_ln_g, conv_ln_b, conv_pw_w, hgrn_lb_logits, hgrn_norm_g, hgrn_out_w,
           w_out, mlp_w1, mlp_w2):
    assert w_mod.shape[0] == 1, "single-layer block"
    b, l, d = x.shape
    lc = ctx.shape[1]
    dh = hgrn_norm_g.shape[1]
    dc = conv_dw_w.shape[2]
    assert l % (CONV_TILE_ROWS * GRID_W) == 0 and CONV_PAD <= CONV_TILE_ROWS and CONV_PAD <= _HPAD
    assert l % SCAN_BLOCK == 0 and lc % SCAN_CHUNK == 0

    mod_rows = 16
    cc = jnp.concatenate([c, c_ctx[None, :], jnp.zeros((mod_rows - b - 1, d), F32)], axis=0)
    mod3 = _mod_call(cc, w_mod[0], b_mod).reshape(mod_rows, N_MOD, d)

    w_in0 = w_in[0]
    lb_f, lb_b = hgrn_lb_logits[0], hgrn_lb_logits[1]
    off = lambda n: n * dh

    pc = _ctx_proj_call(ctx, norm_pre_mix, mod3, b, w_in0, 3 * dh)
    s_f, s_b = _ctx_state_call(pc, lb_f, lb_b, dh)

    h, v = _prenorm_v_call(x, norm_pre_mix, mod3, w_in0, dh)
    h = h.reshape(b * l, d)
    seq = lambda t: t.reshape(b, l, t.shape[-1])
    lf, kk = _proj_call(h, w_in0, [off(1)], 2 * dh, _ep_forget, [F32, BF16],
                        extra=(jnp.concatenate([lb_f, lb_b], axis=1),), name="proj_forget")
    qg_scale = jnp.concatenate([jnp.full((1, dh), HEAD_DIM ** -0.5, F32), jnp.ones((1, dh), F32)], axis=1)
    qg, wpw, wh, wo = _proj_call(h, w_in0, [off(3)], 2 * dh, _ep_scaled_silu, [BF16], extra=(qg_scale,),
                                 side=(conv_pw_w[0], hgrn_out_w[0], w_out[0]), name="proj_qg")
    (u,) = _proj_call(h, w_in0, [off(5), off(5) + dc], dc, _ep_glu, [F32], tn=512, name="proj_glu")
    gates, w1, w2 = _proj_call(h, w_in0, [off(5) + 2 * dc], 2 * d, _ep_sigmoid, [BF16],
                               side=(mlp_w1[0], mlp_w2[0]), name="proj_gates")

    lf, kk, qg = seq(lf), seq(kk), seq(qg)
    o_b = _scan_call((kk, 1), (v, 0), (lf, 1), s_b, (qg, 0), reverse=True)
    og = _scan_call((kk, 0), (v, 0), (lf, 0), s_f, (qg, 0), reverse=False, ob=(o_b, 0), g=(qg, 1),
                    ng=hgrn_norm_g)

    uc = _conv_call(seq(u), conv_dw_w[0], conv_dw_b, conv_ln_g, conv_ln_b)

    x1, h2 = _mix_call(uc, og, seq(gates), x, wpw, wh, wo, norm_post_mix, norm_pre_mlp, mod3)
    return _mlp_call(h2, w1, w2, x1, norm_post_mlp, mod3)
```

```python
import functools

import jax
import jax.numpy as jnp
from jax import lax
from jax.experimental import pallas as pl
from jax.experimental.pallas import tpu as pltpu

F32 = jnp.float32
BF16 = jnp.bfloat16

GRID_W = 64
CONV_WIDTH = 31
CONV_PAD = (CONV_WIDTH - 1) // 2
HEAD_DIM = 128
N_MOD = 6
N_MOD_EARLY = 2
MOD_ROW_STRIDE = 16
EPS = 1e-6

SCAN_CHUNK = 64
SCAN_BLOCK = 512

VMEM_LIMIT = 56 * 1024 * 1024

_NT = (((1,), (1,)), ((), ()))
_TN = (((0,), (0,)), ((), ()))


def _params(sem):
    return pltpu.CompilerParams(dimension_semantics=sem, vmem_limit_bytes=VMEM_LIMIT)


def _sigmoid(x):
    return 0.5 * jnp.tanh(0.5 * x) + 0.5


def _silu(x):
    return x * _sigmoid(x)


def _norm_modulate(x, gain, shift, scale):
    ms = jnp.mean(x * x, axis=-1, keepdims=True)
    return (x * lax.rsqrt(ms + EPS) * gain) * (1.0 + scale) + shift


def _forget_gate(raw, logits):
    e = jnp.exp(logits - jnp.max(logits, axis=0, keepdims=True))
    lb = e[0:1] / jnp.sum(e, axis=0, keepdims=True)
    return lb + (1.0 - lb) * _sigmoid(raw)


def _mod_rows(c_ref, w_ref, b_ref):
    s = _silu(c_ref[...]).astype(BF16)
    return jnp.dot(s, w_ref[...].astype(BF16), preferred_element_type=F32) + b_ref[...]


def _mod_kernel(c_ref, w_ref, b_ref, o_ref):
    o_ref[0] = _mod_rows(c_ref, w_ref, b_ref)


def _mod_call(cc, w_mod, b_mod, n_seg, tn=1024):
    rows, d = cc.shape
    return pl.pallas_call(
        _mod_kernel,
        grid=(n_seg * d // tn,),
        in_specs=[pl.BlockSpec((rows, d), lambda j: (0, 0)),
                  pl.BlockSpec((d, tn), lambda j: (0, j)),
                  pl.BlockSpec((1, tn), lambda j: (0, j))],
        out_specs=pl.BlockSpec((1, rows, tn), lambda j: (j // (d // tn), 0, j % (d // tn))),
        out_shape=jax.ShapeDtypeStruct((n_seg, rows, d), F32),
        compiler_params=_params(("parallel",)),
        name="mod",
    )(cc, w_mod, b_mod)


def _mod_block(n_seg, d, row_of):
    return pl.BlockSpec((n_seg, 8, d), lambda *idx: (0, row_of(*idx) * (MOD_ROW_STRIDE // 8), 0))


def _proj_kernel(*refs, n_w, n_extra, n_side, n_mod, n_out, epilogue):
    h_ref = refs[0]
    pos = 1
    w_refs = refs[pos:pos + n_w]
    pos += n_w
    extra = refs[pos:pos + n_extra]
    pos += n_extra
    side_in = refs[pos:pos + n_side]
    pos += n_side
    mod_in = refs[pos:pos + 3 * n_mod]
    pos += 3 * n_mod
    out_refs = refs[pos:pos + n_out]
    pos += n_out
    side_out = refs[pos:pos + n_side]
    pos += n_side
    mod_out = refs[pos:pos + n_mod]
    pos += n_mod
    wbf_refs = refs[pos:]

    @pl.when(pl.program_id(1) == 0)
    def _():
        for w_ref, wbf_ref in zip(w_refs, wbf_refs):
            wbf_ref[...] = w_ref[...].astype(BF16)

    for si_ref, so_ref in zip(side_in, side_out):
        so_ref[...] = si_ref[...].astype(BF16)
    if n_mod:
        mod_out[0][0] = _mod_rows(*mod_in)

    h = h_ref[...]
    accs = [jnp.dot(h, wbf_ref[...], preferred_element_type=F32) for wbf_ref in wbf_refs]
    outs = epilogue(accs, [e_ref[...] for e_ref in extra])
    for o_ref, o in zip(out_refs, outs):
        o_ref[...] = o.astype(o_ref.dtype)


def _ep_forget(accs, extra):
    f = _forget_gate(accs[0], extra[0])
    return jnp.log2(f), 1.0 - f


def _ep_scaled_silu(accs, extra):
    return (_silu(accs[0]) * extra[0],)


def _ep_glu(accs, extra):
    return (accs[0] * _sigmoid(accs[1]),)


def _ep_sigmoid(accs, extra):
    return (_sigmoid(accs[0]),)


def _proj_call(h, w, col_offsets, width, epilogue, out_dtypes, extra=(), side=(), late_mod=None, tm=1024,
               tn=1024, name="proj"):
    m, k = h.shape
    tm = min(tm, m)
    n_w = len(col_offsets)
    n_j, n_i = width // tn, m // tm
    mod_specs, mod_out_specs, mod_out_shape, mod_args = [], [], [], []
    if late_mod is not None:
        cc, w_mod, b_mod = late_mod
        rows, d = cc.shape
        n_late = w_mod.shape[1] // d - N_MOD_EARLY
        tnm = n_late * d // (n_j * n_i)
        per = d // tnm
        step = lambda j, i: j * n_i + i
        mod_specs = [pl.BlockSpec((rows, d), lambda j, i: (0, 0)),
                     pl.BlockSpec((d, tnm), lambda j, i: (0, N_MOD_EARLY * per + step(j, i))),
                     pl.BlockSpec((1, tnm), lambda j, i: (0, N_MOD_EARLY * per + step(j, i)))]
        mod_out_specs = [pl.BlockSpec((1, rows, tnm), lambda j, i: (step(j, i) // per, 0, step(j, i) % per))]
        mod_out_shape = [jax.ShapeDtypeStruct((n_late, rows, d), F32)]
        mod_args = [cc, w_mod, b_mod]
    in_specs = [pl.BlockSpec((tm, k), lambda j, i: (i, 0))]
    for off in col_offsets:
        in_specs.append(pl.BlockSpec((k, tn), lambda j, i, off=off: (0, off // tn + j)))
    for e in extra:
        in_specs.append(pl.BlockSpec((e.shape[0], tn), lambda j, i: (0, j)))
    side_specs = [pl.BlockSpec((a.shape[0] // (n_j * n_i), a.shape[1]), lambda j, i: (j * n_i + i, 0))
                  for a in side]
    return pl.pallas_call(
        functools.partial(_proj_kernel, n_w=n_w, n_extra=len(extra), n_side=len(side), n_mod=len(mod_out_specs),
                          n_out=len(out_dtypes), epilogue=epilogue),
        grid=(n_j, n_i),
        in_specs=in_specs + side_specs + mod_specs,
        out_specs=[pl.BlockSpec((tm, tn), lambda j, i: (i, j)) for _ in out_dtypes] + side_specs + mod_out_specs,
        out_shape=[jax.ShapeDtypeStruct((m, width), dt) for dt in out_dtypes]
                  + [jax.ShapeDtypeStruct(a.shape, BF16) for a in side] + mod_out_shape,
        scratch_shapes=[pltpu.VMEM((k, tn), BF16) for _ in range(n_w)],
        compiler_params=_params(("parallel", "arbitrary")),
        name=name,
    )(h, *([w] * n_w), *extra, *side, *mod_args)


def _prenorm_v_kernel(x_ref, g_ref, mod_ref, w_ref, h_ref, v_ref, wbf_ref):
    @pl.when((pl.program_id(0) == 0) & (pl.program_id(1) == 0))
    def _():
        wbf_ref[...] = w_ref[...].astype(BF16)

    h = _norm_modulate(x_ref[0], g_ref[...], mod_ref[0, 0:1, :], mod_ref[1, 0:1, :]).astype(BF16)
    h_ref[0] = h
    v_ref[0] = jnp.dot(h, wbf_ref[...], preferred_element_type=F32).astype(BF16)


def _prenorm_v_call(x, g, mod_early, w, width, tl=512):
    b, l, d = x.shape
    return pl.pallas_call(
        _prenorm_v_kernel,
        grid=(b, l // tl),
        in_specs=[pl.BlockSpec((1, tl, d), lambda i, j: (i, j, 0)),
                  pl.BlockSpec((1, d), lambda i, j: (0, 0)),
                  _mod_block(N_MOD_EARLY, d, lambda i, j: i),
                  pl.BlockSpec((d, width), lambda i, j: (0, 0), pipeline_mode=pl.Buffered(1))],
        out_specs=[pl.BlockSpec((1, tl, d), lambda i, j: (i, j, 0)),
                   pl.BlockSpec((1, tl, width), lambda i, j: (i, j, 0))],
        out_shape=[jax.ShapeDtypeStruct((b, l, d), BF16), jax.ShapeDtypeStruct((b, l, width), BF16)],
        scratch_shapes=[pltpu.VMEM((d, width), BF16)],
        compiler_params=_params(("arbitrary", "arbitrary")),
        name="prenorm_v",
    )(x, g, mod_early, w)


def _scan_masks(c, reverse):
    t = lax.broadcasted_iota(jnp.int32, (c, c), 0)
    s = lax.broadcasted_iota(jnp.int32, (c, c), 1)
    half = c // 2
    before = (s >= t) if reverse else (s <= t)
    mask1 = ((t < half) == (s < half)) & before
    tri = jnp.where(before, 1.0, 0.0).astype(BF16)
    return mask1, jnp.concatenate([tri, tri], axis=1)


def _scan_cumsum(lf, tri2):
    hi = lf.astype(BF16)
    lo = (lf - hi.astype(F32)).astype(BF16)
    return jnp.dot(tri2, jnp.concatenate([hi, lo], axis=0), preferred_element_type=F32)


def _scan_refs(a, reverse):
    c = a.shape[0]
    half, quarter = c // 2, c // 4
    if reverse:
        return a[quarter:quarter + 1], a[half + quarter:half + quarter + 1], a[half:half + 1], a[0:1]
    return a[quarter - 1:quarter], a[half + quarter - 1:half + quarter], a[half - 1:half], a[c - 1:c]


def _scan_products(a, q, k, v, st, st_kv, reverse):
    c = a.shape[0]
    half = c // 2
    m1_lo, m1_hi, m0, a_tot = _scan_refs(a, reverse)
    lo_rows, hi_rows = slice(0, half), slice(half, c)
    row = lax.broadcasted_iota(jnp.int32, a.shape, 0)
    e1 = a - jnp.where(row < half, m1_lo, m1_hi)
    pq = q.astype(F32) * jnp.exp2(e1)
    pk = k.astype(F32) * jnp.exp2(-e1)
    s1 = lax.dot_general(pq.astype(BF16), pk.astype(BF16), _NT, preferred_element_type=F32)
    if reverse:
        q_rows, m1_q, k_rows, m1_k = lo_rows, m1_lo, hi_rows, m1_hi
    else:
        q_rows, m1_q, k_rows, m1_k = hi_rows, m1_hi, lo_rows, m1_lo
    q0 = (pq[q_rows] * jnp.exp2(m1_q - m0)).astype(BF16)
    k0 = (pk[k_rows] * jnp.exp2(m0 - m1_k)).astype(BF16)
    s0 = lax.dot_general(q0, k0, _NT, preferred_element_type=F32)
    qi = jnp.concatenate([pq[lo_rows] * jnp.exp2(m1_lo), pq[hi_rows] * jnp.exp2(m1_hi)], axis=0)
    inter = jnp.dot(qi.astype(BF16), st_kv, preferred_element_type=F32)
    kh = jnp.concatenate([pk[lo_rows] * jnp.exp2(a_tot - m1_lo), pk[hi_rows] * jnp.exp2(a_tot - m1_hi)], axis=0)
    upd = lax.dot_general(v, kh.astype(BF16), _TN, preferred_element_type=F32)
    return s1, s0, inter, st * jnp.exp2(a_tot) + upd


def _scan_output(s1, s0, inter, v, mask1, reverse):
    c = s1.shape[0]
    half = c // 2
    lo_rows, hi_rows = slice(0, half), slice(half, c)
    q_rows, k_rows = (lo_rows, hi_rows) if reverse else (hi_rows, lo_rows)
    intra = jnp.dot(jnp.where(mask1, s1, 0.0).astype(BF16), v, preferred_element_type=F32)
    cross = jnp.dot(s0.astype(BF16), v[k_rows], preferred_element_type=F32)
    o = inter + intra
    o_q = o[q_rows] + cross
    return jnp.concatenate([o_q, o[hi_rows]] if reverse else [o[lo_rows], o_q], axis=0)


def _scan_kernel(*refs, reverse, readout):
    k_ref, v_ref, lf_ref, s0_ref, q_ref = refs[:5]
    if readout:
        (ob_ref, g_ref, ng_ref, o_ref, st_ref, stkv_ref, a_ref) = refs[5:]
    else:
        (o_ref, st_ref, stkv_ref, a_ref) = refs[5:]
    c = SCAN_CHUNK
    n_batch, rows = k_ref.shape[0], k_ref.shape[1]
    n_chunks = rows // c
    n_heads = k_ref.shape[2] // HEAD_DIM
    recurrences = [(h, bi) for h in range(n_heads) for bi in range(n_batch)]

    @pl.when(pl.program_id(0) == 0)
    def _():
        st_ref[...] = s0_ref[...]
        for h, bi in recurrences:
            stkv_ref[bi, h] = s0_ref[bi, h].T.astype(BF16)

    mask1, tri2 = _scan_masks(c, reverse)

    def cumsum_body(ci, carry):
        r0 = pl.multiple_of(ci * c, c)
        for bi in range(n_batch):
            a_ref[bi, pl.ds(r0, c), :] = _scan_cumsum(lf_ref[bi, pl.ds(r0, c), :], tri2)
        return carry

    lax.fori_loop(0, n_chunks, cumsum_body, 0, unroll=4)

    def chunk_body(ci, carry):
        cidx = (n_chunks - 1 - ci) if reverse else ci
        r0 = pl.multiple_of(cidx * c, c)
        tile = lambda ref, h, bi: ref[bi, pl.ds(r0, c), h * HEAD_DIM:(h + 1) * HEAD_DIM]
        wave = []
        for h, bi in recurrences:
            s1, s0, inter, st_new = _scan_products(
                tile(a_ref, h, bi), tile(q_ref, h, bi), tile(k_ref, h, bi), tile(v_ref, h, bi),
                st_ref[bi, h], stkv_ref[bi, h], reverse)
            st_ref[bi, h] = st_new
            stkv_ref[bi, h] = st_new.T.astype(BF16)
            wave.append((s1, s0, inter))
        for (h, bi), (s1, s0, inter) in zip(recurrences, wave):
            lanes = slice(h * HEAD_DIM, (h + 1) * HEAD_DIM)
            o = _scan_output(s1, s0, inter, tile(v_ref, h, bi), mask1, reverse)
            if readout:
                o = o + tile(ob_ref, h, bi)
                ms = jnp.mean(o * o, axis=-1, keepdims=True)
                y = o * lax.rsqrt(ms + EPS) * ng_ref[:, lanes]
                o = y * tile(g_ref, h, bi).astype(F32)
            o_ref[bi, pl.ds(r0, c), lanes] = o.astype(o_ref.dtype)
        return carry

    lax.fori_loop(0, n_chunks, chunk_body, 0, unroll=4)


def _scan_call(k, v, lf, s0, q, *, reverse, ob=None, g=None, ng=None):
    b, l, _ = v[0].shape
    dh = ng.shape[1] if ng is not None else s0.shape[1] * HEAD_DIM
    tb = min(SCAN_BLOCK, l)
    n_blocks = l // tb
    readout = ob is not None
    blk = (lambda j: n_blocks - 1 - j) if reverse else (lambda j: j)
    seq_spec = lambda group: pl.BlockSpec((b, tb, dh), lambda j: (0, blk(j), group))
    st_spec = pl.BlockSpec(s0.shape, lambda j: (0, 0, 0, 0))
    in_specs = [seq_spec(k[1]), seq_spec(v[1]), seq_spec(lf[1]), st_spec, seq_spec(q[1])]
    args = [k[0], v[0], lf[0], s0, q[0]]
    if readout:
        in_specs += [seq_spec(ob[1]), seq_spec(g[1]), pl.BlockSpec((1, dh), lambda j: (0, 0))]
        args += [ob[0], g[0], ng]
    return pl.pallas_call(
        functools.partial(_scan_kernel, reverse=reverse, readout=readout),
        grid=(n_blocks,),
        in_specs=in_specs,
        out_specs=seq_spec(0),
        out_shape=jax.ShapeDtypeStruct((b, l, dh), BF16 if readout else F32),
        scratch_shapes=[pltpu.VMEM(s0.shape, F32), pltpu.VMEM(s0.shape, BF16), pltpu.VMEM((b, tb, dh), F32)],
        compiler_params=_params(("arbitrary",)),
        name="scan_readout_fwd" if readout else "scan_out_bwd",
    )(*args)


def _ctx_proj_kernel(ctx_ref, g_ref, mod_ref, w_ref, o_ref, hc_ref):
    @pl.when(pl.program_id(0) == 0)
    def _():
        for bi in range(ctx_ref.shape[0]):
            hc_ref[bi] = _norm_modulate(ctx_ref[bi], g_ref[...], mod_ref[0, 0:1, :],
                                        mod_ref[1, 0:1, :]).astype(BF16)

    w = w_ref[...].astype(BF16)
    for bi in range(ctx_ref.shape[0]):
        o_ref[bi] = jnp.dot(hc_ref[bi], w, preferred_element_type=F32)


def _ctx_proj_call(ctx, g, mod_early, mod_row, w, width, tn=1024):
    b, lc, d = ctx.shape
    return pl.pallas_call(
        _ctx_proj_kernel,
        grid=(width // tn,),
        in_specs=[pl.BlockSpec((b, lc, d), lambda j: (0, 0, 0)),
                  pl.BlockSpec((1, d), lambda j: (0, 0)),
                  _mod_block(N_MOD_EARLY, d, lambda j: mod_row),
                  pl.BlockSpec((d, tn), lambda j: (0, j))],
        out_specs=pl.BlockSpec((b, lc, tn), lambda j: (0, 0, j)),
        out_shape=jax.ShapeDtypeStruct((b, lc, width), F32),
        scratch_shapes=[pltpu.VMEM((b, lc, d), BF16)],
        compiler_params=_params(("arbitrary",)),
        name="ctx_proj",
    )(ctx, g, mod_early, w)


def _ctx_state_kernel(pv_ref, pf_ref, pb_ref, lgf_ref, lgb_ref, sf_ref, sb_ref):
    c = SCAN_CHUNK
    n_batch, rows = pv_ref.shape[0], pv_ref.shape[1]
    n_chunks = rows // c
    n_heads = pv_ref.shape[2] // HEAD_DIM
    for reverse, p_ref, lg_ref, s_ref in ((False, pf_ref, lgf_ref, sf_ref), (True, pb_ref, lgb_ref, sb_ref)):
        _, tri2 = _scan_masks(c, reverse)
        s_ref[...] = jnp.zeros(s_ref.shape, F32)

        def chunk_body(ci, carry, reverse=reverse, p_ref=p_ref, lg_ref=lg_ref, s_ref=s_ref, tri2=tri2):
            cidx = (n_chunks - 1 - ci) if reverse else ci
            r0 = pl.multiple_of(cidx * c, c)
            recurrences = [(h, bi) for h in range(n_heads) for bi in range(n_batch)]
            tile = lambda ref, h, bi: ref[bi, pl.ds(r0, c), h * HEAD_DIM:(h + 1) * HEAD_DIM]
            gates = [_forget_gate(tile(p_ref, h, bi), lg_ref[:, h * HEAD_DIM:(h + 1) * HEAD_DIM])
                     for h, bi in recurrences]
            sums = [_scan_cumsum(jnp.log2(f), tri2) for f in gates]
            for (h, bi), f, a in zip(recurrences, gates, sums):
                a_tot = _scan_refs(a, reverse)[3]
                kh = ((1.0 - f) * jnp.exp2(a_tot - a)).astype(BF16)
                upd = lax.dot_general(tile(pv_ref, h, bi).astype(BF16), kh, _TN, preferred_element_type=F32)
                s_ref[bi, h] = s_ref[bi, h] * jnp.exp2(a_tot) + upd
            return carry

        lax.fori_loop(0, n_chunks, chunk_body, 0)


def _ctx_state_call(pc, logits_f, logits_b, dh):
    b, lc, _ = pc.shape
    n_heads = dh // HEAD_DIM
    group = lambda n: pl.BlockSpec((b, lc, dh), lambda j: (0, 0, n))
    lg_spec = pl.BlockSpec(logits_f.shape, lambda j: (0, 0))
    st_spec = pl.BlockSpec((b, n_heads, HEAD_DIM, HEAD_DIM), lambda j: (0, 0, 0, 0))
    st_shape = jax.ShapeDtypeStruct((b, n_heads, HEAD_DIM, HEAD_DIM), F32)
    return pl.pallas_call(
        _ctx_state_kernel,
        grid=(1,),
        in_specs=[group(0), group(1), group(2), lg_spec, lg_spec],
        out_specs=[st_spec, st_spec],
        out_shape=[st_shape, st_shape],
        compiler_params=_params(("arbitrary",)),
        name="ctx_state",
    )(pc, pc, pc, logits_f, logits_b)


CONV_TILE_ROWS = 16
_HPAD = 16
_HROW = GRID_W + 2 * _HPAD


def _conv_kernel(cur_ref, prev_ref, next_ref, w_ref, b_ref, lg_ref, lb_ref, o_ref,
                 hpad_ref, vpad_ref, y_ref, *, n_tiles):
    tr = CONV_TILE_ROWS
    half = w_ref.shape[1] // 2
    n_lane_tiles = half // 128
    i = pl.program_id(1)

    hpad_ref[:, :, :_HPAD, :] = jnp.zeros((tr, n_lane_tiles, _HPAD, 128), F32)
    hpad_ref[:, :, _HPAD + GRID_W:, :] = jnp.zeros((tr, n_lane_tiles, _HPAD, 128), F32)
    for r in range(tr):
        for lt in range(n_lane_tiles):
            hpad_ref[r, lt, _HPAD:_HPAD + GRID_W, :] = cur_ref[0, r * GRID_W:(r + 1) * GRID_W,
                                                               lt * 128:(lt + 1) * 128]
    halo = CONV_PAD * GRID_W
    for lt in range(n_lane_tiles):
        lanes = slice(lt * 128, (lt + 1) * 128)
        top = prev_ref[0, tr * GRID_W - halo:, lanes]
        bot = next_ref[0, :halo, lanes]
        vpad_ref[lt, :halo, :] = jnp.where(i > 0, top, 0.0)
        vpad_ref[lt, halo:halo + tr * GRID_W, :] = cur_ref[0, :, half + lt * 128:half + (lt + 1) * 128]
        vpad_ref[lt, halo + tr * GRID_W:, :] = jnp.where(i < n_tiles - 1, bot, 0.0)

    def h_row(r, carry):
        o0 = pl.multiple_of(r * GRID_W, GRID_W)
        for lt in range(n_lane_tiles):
            lanes = slice(lt * 128, (lt + 1) * 128)
            acc = jnp.zeros((GRID_W, 128), F32)
            for jj in range(CONV_WIDTH):
                off = _HPAD - CONV_PAD + jj
                acc = acc + hpad_ref[r, lt, off:off + GRID_W, :] * w_ref[jj:jj + 1, lanes]
            y_ref[pl.ds(o0, GRID_W), lanes] = acc + b_ref[:, lanes]
        return carry

    lax.fori_loop(0, tr, h_row, 0)

    n_sub = GRID_W // 8
    for lt in range(n_lane_tiles):
        vl = slice(half + lt * 128, half + (lt + 1) * 128)
        wv = [jnp.broadcast_to(w_ref[jj:jj + 1, vl], (8, 128)) for jj in range(CONV_WIDTH)]
        bias = b_ref[:, vl]

        def v_row(r, carry, lt=lt, vl=vl, wv=wv, bias=bias):
            v0 = pl.multiple_of(r * GRID_W, GRID_W)
            acc = [jnp.zeros((8, 128), F32)] * n_sub
            for jj in range(CONV_WIDTH):
                src = pl.multiple_of(v0 + jj * GRID_W, GRID_W)
                tap = vpad_ref[lt, pl.ds(src, GRID_W), :]
                acc = [acc[s] + tap[8 * s:8 * s + 8, :] * wv[jj] for s in range(n_sub)]
            y_ref[pl.ds(v0, GRID_W), vl] = jnp.concatenate(acc, axis=0) + bias
            return carry

        lax.fori_loop(0, tr, v_row, 0, unroll=2)

    def ln_row(r, carry):
        o0 = pl.multiple_of(r * GRID_W, GRID_W)
        y = y_ref[pl.ds(o0, GRID_W), :]
        mu = jnp.mean(y, axis=-1, keepdims=True)
        yc = y - mu
        var = jnp.mean(yc * yc, axis=-1, keepdims=True)
        z = yc * lax.rsqrt(var + EPS) * lg_ref[...] + lb_ref[...]
        o_ref[0, pl.ds(o0, GRID_W), :] = _silu(z).astype(BF16)
        return carry

    lax.fori_loop(0, tr, ln_row, 0, unroll=2)


def _conv_call(u, w, bias, ln_g, ln_b):
    b, l, ch = u.shape
    half = ch // 2
    tile = CONV_TILE_ROWS * GRID_W
    n_tiles = l // tile
    return pl.pallas_call(
        functools.partial(_conv_kernel, n_tiles=n_tiles),
        grid=(b, n_tiles),
        in_specs=[pl.BlockSpec((1, tile, ch), lambda i, j: (i, j, 0)),
                  pl.BlockSpec((1, tile, half), lambda i, j: (i, jnp.maximum(j - 1, 0), 1)),
                  pl.BlockSpec((1, tile, half), lambda i, j: (i, jnp.minimum(j + 1, n_tiles - 1), 1)),
                  pl.BlockSpec((CONV_WIDTH, ch), lambda i, j: (0, 0)),
                  pl.BlockSpec((1, ch), lambda i, j: (0, 0)),
                  pl.BlockSpec((1, ch), lambda i, j: (0, 0)),
                  pl.BlockSpec((1, ch), lambda i, j: (0, 0))],
        out_specs=pl.BlockSpec((1, tile, ch), lambda i, j: (i, j, 0)),
        out_shape=jax.ShapeDtypeStruct((b, l, ch), BF16),
        scratch_shapes=[pltpu.VMEM((CONV_TILE_ROWS, half // 128, _HROW, 128), F32),
                        pltpu.VMEM((half // 128, tile + 2 * CONV_PAD * GRID_W, 128), F32),
                        pltpu.VMEM((tile, ch), F32)],
        compiler_params=_params(("parallel", "parallel")),
        name="conv",
    )(u, u, u, w, bias, ln_g, ln_b)


MIX_SLICE = 128

def _mix_kernel(uc_ref, og_ref, gc_ref, gh_ref, x_ref, wpw_ref, wh_ref, wo_ref,
                npost_ref, npre_ref, mod_ref, x1_ref, h2_ref):
    tm = x_ref.shape[1]
    n_split = tm // MIX_SLICE
    for r in range(n_split):
        rows = slice(r * MIX_SLICE, (r + 1) * MIX_SLICE)
        y_c = jnp.dot(uc_ref[0, rows, :], wpw_ref[...], preferred_element_type=F32)
        y_h = jnp.dot(og_ref[0, rows, :], wh_ref[...], preferred_element_type=F32)
        z = gc_ref[0, rows, :].astype(F32) * y_c + gh_ref[0, rows, :].astype(F32) * y_h
        y = jnp.dot(z.astype(BF16), wo_ref[...], preferred_element_type=F32)
        ms = jnp.mean(y * y, axis=-1, keepdims=True)
        yn = y * lax.rsqrt(ms + EPS) * npost_ref[...]
        x1 = x_ref[0, rows, :] + mod_ref[0, 0:1, :] * yn
        x1_ref[0, rows, :] = x1
        h2_ref[0, rows, :] = _norm_modulate(x1, npre_ref[...], mod_ref[1, 0:1, :],
                                            mod_ref[2, 0:1, :]).astype(BF16)


def _mix_call(uc, og, gates, x, wpw, wh, wo, npost, npre, mod_late, tm=512):
    b, l, d = x.shape
    dc = uc.shape[2]
    gc = gh = gates
    row = lambda width, col=0: pl.BlockSpec((1, tm, width), lambda i, j: (i, j, col))
    const = lambda shape: pl.BlockSpec(shape, lambda i, j: (0,) * len(shape), pipeline_mode=pl.Buffered(1))
    return pl.pallas_call(
        _mix_kernel,
        grid=(b, l // tm),
        in_specs=[row(dc), row(dc), row(d, 0), row(d, 1), row(d),
                  const(wpw.shape), const(wh.shape), const(wo.shape),
                  const((1, d)), const((1, d)),
                  _mod_block(N_MOD - N_MOD_EARLY, d, lambda i, j: i)],
        out_specs=[row(d), row(d)],
        out_shape=[jax.ShapeDtypeStruct((b, l, d), F32), jax.ShapeDtypeStruct((b, l, d), BF16)],
        compiler_params=_params(("parallel", "parallel")),
        name="mix",
    )(uc, og, gc, gh, x, wpw, wh, wo, npost, npre, mod_late)


def _mlp_kernel(h_ref, w1_ref, w2_ref, x1_ref, npost_ref, mod_ref, o_ref, *, n_ff):
    j = pl.program_id(2)

    @pl.when(j == 0)
    def _():
        o_ref[...] = jnp.zeros(o_ref.shape, F32)

    a = jnp.dot(h_ref[0], w1_ref[...], preferred_element_type=F32)
    a = jnp.square(jnp.maximum(a, 0.0)).astype(BF16)
    o_ref[0] += jnp.dot(a, w2_ref[...], preferred_element_type=F32)

    @pl.when(j == n_ff - 1)
    def _():
        rb = 32
        for r in range(o_ref.shape[1] // rb):
            rows = slice(r * rb, (r + 1) * rb)
            y = o_ref[0, rows, :]
            ms = jnp.mean(y * y, axis=-1, keepdims=True)
            yn = y * lax.rsqrt(ms + EPS) * npost_ref[...]
            o_ref[0, rows, :] = x1_ref[0, rows, :] + mod_ref[3, 0:1, :] * yn


def _mlp_call(h2, w1, w2, x1, npost, mod_late, tm=1024, tf=1024):
    b, l, d = x1.shape
    dff = w1.shape[1]
    n_ff = dff // tf
    return pl.pallas_call(
        functools.partial(_mlp_kernel, n_ff=n_ff),
        grid=(b, l // tm, n_ff),
        in_specs=[pl.BlockSpec((1, tm, d), lambda i, m, j: (i, m, 0)),
                  pl.BlockSpec((d, tf), lambda i, m, j: (0, j)),
                  pl.BlockSpec((tf, d), lambda i, m, j: (j, 0)),
                  pl.BlockSpec((1, tm, d), lambda i, m, j: (i, m, 0), pipeline_mode=pl.Buffered(1)),
                  pl.BlockSpec((1, d), lambda i, m, j: (0, 0)),
                  _mod_block(N_MOD - N_MOD_EARLY, d, lambda i, m, j: i)],
        out_specs=pl.BlockSpec((1, tm, d), lambda i, m, j: (i, m, 0)),
        out_shape=jax.ShapeDtypeStruct((b, l, d), F32),
        compiler_params=_params(("parallel", "parallel", "arbitrary")),
        name="mlp",
    )(h2, w1, w2, x1, npost, mod_late)


def kernel(x, c, ctx, c_ctx, w_mod, b_mod, norm_pre_mix, norm_post_mix, norm_pre_mlp, norm_post_mlp, w_in,
           conv_dw_w, conv_dw_b, conv_ln_g, conv_ln_b, conv_pw_w, hgrn_lb_logits, hgrn_norm_g, hgrn_out_w,
           w_out, mlp_w1, mlp_w2):
    assert w_mod.shape[0] == 1, "single-layer block"
    b, l, d = x.shape
    lc = ctx.shape[1]
    dh = hgrn_norm_g.shape[1]
    dc = conv_dw_w.shape[2]
    assert l % (CONV_TILE_ROWS * GRID_W) == 0 and CONV_PAD <= CONV_TILE_ROWS and CONV_PAD <= _HPAD
    assert l % SCAN_BLOCK == 0 and lc % SCAN_CHUNK == 0

    cond = jnp.concatenate([c, c_ctx[None, :]], axis=0)
    cc = jnp.pad(cond[:, None, :], ((0, 0), (0, MOD_ROW_STRIDE - 1), (0, 0))).reshape(-1, d)
    mod_early = _mod_call(cc, w_mod[0], b_mod, N_MOD_EARLY)

    w_in0 = w_in[0]
    lb_f, lb_b = hgrn_lb_logits[0], hgrn_lb_logits[1]
    off = lambda n: n * dh

    pc = _ctx_proj_call(ctx, norm_pre_mix, mod_early, b, w_in0, 3 * dh)
    s_f, s_b = _ctx_state_call(pc, lb_f, lb_b, dh)

    h, v = _prenorm_v_call(x, norm_pre_mix, mod_early, w_in0, dh)
    h = h.reshape(b * l, d)
    seq = lambda t: t.reshape(b, l, t.shape[-1])
    lf, kk = _proj_call(h, w_in0, [off(1)], 2 * dh, _ep_forget, [F32, BF16],
                        extra=(jnp.concatenate([lb_f, lb_b], axis=1),), name="proj_forget")
    qg_scale = jnp.concatenate([jnp.full((1, dh), HEAD_DIM ** -0.5, F32), jnp.ones((1, dh), F32)], axis=1)
    qg, wpw, wh, wo, mod_late = _proj_call(h, w_in0, [off(3)], 2 * dh, _ep_scaled_silu, [BF16], extra=(qg_scale,),
                                           side=(conv_pw_w[0], hgrn_out_w[0], w_out[0]),
                                           late_mod=(cc, w_mod[0], b_mod), name="proj_qg")
    (u,) = _proj_call(h, w_in0, [off(5), off(5) + dc], dc, _ep_glu, [F32], tn=512, name="proj_glu")
    gates, w1, w2 = _proj_call(h, w_in0, [off(5) + 2 * dc], 2 * d, _ep_sigmoid, [BF16],
                               side=(mlp_w1[0], mlp_w2[0]), name="proj_gates")

    lf, kk, qg = seq(lf), seq(kk), seq(qg)
    o_b = _scan_call((kk, 1), (v, 0), (lf, 1), s_b, (qg, 0), reverse=True)
    og = _scan_call((kk, 0), (v, 0), (lf, 0), s_f, (qg, 0), reverse=False, ob=(o_b, 0), g=(qg, 1),
                    ng=hgrn_norm_g)

    uc = _conv_call(seq(u), conv_dw_w[0], conv_dw_b, conv_ln_g, conv_ln_b)

    x1, h2 = _mix_call(uc, og, seq(gates), x, wpw, wh, wo, norm_post_mix, norm_pre_mlp, mod_late)
    return _mlp_call(h2, w1, w2, x1, norm_post_mlp, mod_late)
```

```python
import functools

import jax
import jax.numpy as jnp
from jax import lax
from jax.experimental import pallas as pl
from jax.experimental.pallas import tpu as pltpu

F32 = jnp.float32
BF16 = jnp.bfloat16

GRID_W = 64
CONV_WIDTH = 31
CONV_PAD = (CONV_WIDTH - 1) // 2
HEAD_DIM = 128
N_MOD = 6
N_MOD_EARLY = 2
MOD_ROW_STRIDE = 16
EPS = 1e-6

SCAN_CHUNK = 64
SCAN_BLOCK = 512

VMEM_LIMIT = 56 * 1024 * 1024

_NT = (((1,), (1,)), ((), ()))
_TN = (((0,), (0,)), ((), ()))


def _params(sem):
    return pltpu.CompilerParams(dimension_semantics=sem, vmem_limit_bytes=VMEM_LIMIT)


def _sigmoid(x):
    return 0.5 * jnp.tanh(0.5 * x) + 0.5


def _silu(x):
    return x * _sigmoid(x)


def _norm_modulate(x, gain, shift, scale):
    ms = jnp.mean(x * x, axis=-1, keepdims=True)
    return (x * lax.rsqrt(ms + EPS) * gain) * (1.0 + scale) + shift


def _forget_gate(raw, logits):
    e = jnp.exp(logits - jnp.max(logits, axis=0, keepdims=True))
    lb = e[0:1] / jnp.sum(e, axis=0, keepdims=True)
    return lb + (1.0 - lb) * _sigmoid(raw)


def _mod_rows(c_ref, w_ref, b_ref):
    s = _silu(c_ref[...]).astype(BF16)
    return jnp.dot(s, w_ref[...].astype(BF16), preferred_element_type=F32) + b_ref[...]


def _mod_kernel(c_ref, w_ref, b_ref, o_ref):
    o_ref[0] = _mod_rows(c_ref, w_ref, b_ref)


def _mod_call(cc, w_mod, b_mod, n_seg, tn=1024):
    rows, d = cc.shape
    return pl.pallas_call(
        _mod_kernel,
        grid=(n_seg * d // tn,),
        in_specs=[pl.BlockSpec((rows, d), lambda j: (0, 0)),
                  pl.BlockSpec((d, tn), lambda j: (0, j)),
                  pl.BlockSpec((1, tn), lambda j: (0, j))],
        out_specs=pl.BlockSpec((1, rows, tn), lambda j: (j // (d // tn), 0, j % (d // tn))),
        out_shape=jax.ShapeDtypeStruct((n_seg, rows, d), F32),
        compiler_params=_params(("parallel",)),
        name="mod",
    )(cc, w_mod, b_mod)


def _mod_block(n_seg, d, row_of):
    return pl.BlockSpec((n_seg, 8, d), lambda *idx: (0, row_of(*idx) * (MOD_ROW_STRIDE // 8), 0))


def _proj_kernel(*refs, n_w, n_extra, n_side, n_mod, n_out, epilogue):
    h_ref = refs[0]
    pos = 1
    w_refs = refs[pos:pos + n_w]
    pos += n_w
    extra = refs[pos:pos + n_extra]
    pos += n_extra
    side_in = refs[pos:pos + n_side]
    pos += n_side
    mod_in = refs[pos:pos + 3 * n_mod]
    pos += 3 * n_mod
    out_refs = refs[pos:pos + n_out]
    pos += n_out
    side_out = refs[pos:pos + n_side]
    pos += n_side
    mod_out = refs[pos:pos + n_mod]
    pos += n_mod
    wbf_refs = refs[pos:]

    @pl.when(pl.program_id(1) == 0)
    def _():
        for w_ref, wbf_ref in zip(w_refs, wbf_refs):
            wbf_ref[...] = w_ref[...].astype(BF16)

    for si_ref, so_ref in zip(side_in, side_out):
        so_ref[...] = si_ref[...].astype(BF16)
    if n_mod:
        mod_out[0][0] = _mod_rows(*mod_in)

    h = h_ref[...]
    accs = [jnp.dot(h, wbf_ref[...], preferred_element_type=F32) for wbf_ref in wbf_refs]
    outs = epilogue(accs, [e_ref[...] for e_ref in extra])
    for o_ref, o in zip(out_refs, outs):
        o_ref[...] = o.astype(o_ref.dtype)


def _ep_forget(accs, extra):
    f = _forget_gate(accs[0], extra[0])
    return jnp.log2(f), 1.0 - f


def _ep_scaled_silu(accs, extra):
    return (_silu(accs[0]) * extra[0],)


def _ep_glu(accs, extra):
    return (accs[0] * _sigmoid(accs[1]),)


def _ep_sigmoid(accs, extra):
    return (_sigmoid(accs[0]),)


def _proj_call(h, w, col_offsets, width, epilogue, out_dtypes, extra=(), side=(), late_mod=None, tm=1024,
               tn=1024, name="proj"):
    m, k = h.shape
    tm = min(tm, m)
    n_w = len(col_offsets)
    n_j, n_i = width // tn, m // tm
    mod_specs, mod_out_specs, mod_out_shape, mod_args = [], [], [], []
    if late_mod is not None:
        cc, w_mod, b_mod = late_mod
        rows, d = cc.shape
        n_late = w_mod.shape[1] // d - N_MOD_EARLY
        tnm = n_late * d // (n_j * n_i)
        per = d // tnm
        step = lambda j, i: j * n_i + i
        mod_specs = [pl.BlockSpec((rows, d), lambda j, i: (0, 0)),
                     pl.BlockSpec((d, tnm), lambda j, i: (0, N_MOD_EARLY * per + step(j, i))),
                     pl.BlockSpec((1, tnm), lambda j, i: (0, N_MOD_EARLY * per + step(j, i)))]
        mod_out_specs = [pl.BlockSpec((1, rows, tnm), lambda j, i: (step(j, i) // per, 0, step(j, i) % per))]
        mod_out_shape = [jax.ShapeDtypeStruct((n_late, rows, d), F32)]
        mod_args = [cc, w_mod, b_mod]
    in_specs = [pl.BlockSpec((tm, k), lambda j, i: (i, 0))]
    for off in col_offsets:
        in_specs.append(pl.BlockSpec((k, tn), lambda j, i, off=off: (0, off // tn + j)))
    for e in extra:
        in_specs.append(pl.BlockSpec((e.shape[0], tn), lambda j, i: (0, j)))
    side_specs = [pl.BlockSpec((a.shape[0] // (n_j * n_i), a.shape[1]), lambda j, i: (j * n_i + i, 0))
                  for a in side]
    return pl.pallas_call(
        functools.partial(_proj_kernel, n_w=n_w, n_extra=len(extra), n_side=len(side), n_mod=len(mod_out_specs),
                          n_out=len(out_dtypes), epilogue=epilogue),
        grid=(n_j, n_i),
        in_specs=in_specs + side_specs + mod_specs,
        out_specs=[pl.BlockSpec((tm, tn), lambda j, i: (i, j)) for _ in out_dtypes] + side_specs + mod_out_specs,
        out_shape=[jax.ShapeDtypeStruct((m, width), dt) for dt in out_dtypes]
                  + [jax.ShapeDtypeStruct(a.shape, BF16) for a in side] + mod_out_shape,
        scratch_shapes=[pltpu.VMEM((k, tn), BF16) for _ in range(n_w)],
        compiler_params=_params(("parallel", "arbitrary")),
        name=name,
    )(h, *([w] * n_w), *extra, *side, *mod_args)


def _prenorm_v_kernel(x_ref, g_ref, mod_ref, w_ref, h_ref, v_ref, wbf_ref):
    @pl.when((pl.program_id(0) == 0) & (pl.program_id(1) == 0))
    def _():
        wbf_ref[...] = w_ref[...].astype(BF16)

    h = _norm_modulate(x_ref[0], g_ref[...], mod_ref[0, 0:1, :], mod_ref[1, 0:1, :]).astype(BF16)
    h_ref[0] = h
    v_ref[0] = jnp.dot(h, wbf_ref[...], preferred_element_type=F32).astype(BF16)


def _prenorm_v_call(x, g, mod_early, w, width, tl=1024):
    b, l, d = x.shape
    return pl.pallas_call(
        _prenorm_v_kernel,
        grid=(b, l // tl),
        in_specs=[pl.BlockSpec((1, tl, d), lambda i, j: (i, j, 0)),
                  pl.BlockSpec((1, d), lambda i, j: (0, 0)),
                  _mod_block(N_MOD_EARLY, d, lambda i, j: i),
                  pl.BlockSpec((d, width), lambda i, j: (0, 0), pipeline_mode=pl.Buffered(1))],
        out_specs=[pl.BlockSpec((1, tl, d), lambda i, j: (i, j, 0)),
                   pl.BlockSpec((1, tl, width), lambda i, j: (i, j, 0))],
        out_shape=[jax.ShapeDtypeStruct((b, l, d), BF16), jax.ShapeDtypeStruct((b, l, width), BF16)],
        scratch_shapes=[pltpu.VMEM((d, width), BF16)],
        compiler_params=_params(("arbitrary", "arbitrary")),
        name="prenorm_v",
    )(x, g, mod_early, w)


def _scan_masks(c, reverse):
    t = lax.broadcasted_iota(jnp.int32, (c, c), 0)
    s = lax.broadcasted_iota(jnp.int32, (c, c), 1)
    half = c // 2
    before = (s >= t) if reverse else (s <= t)
    mask1 = ((t < half) == (s < half)) & before
    tri = jnp.where(before, 1.0, 0.0).astype(BF16)
    return mask1, jnp.concatenate([tri, tri], axis=1)


def _scan_cumsum(lf, tri2):
    hi = lf.astype(BF16)
    lo = (lf - hi.astype(F32)).astype(BF16)
    return jnp.dot(tri2, jnp.concatenate([hi, lo], axis=0), preferred_element_type=F32)


def _scan_refs(a, reverse):
    c = a.shape[0]
    half, quarter = c // 2, c // 4
    if reverse:
        return a[quarter:quarter + 1], a[half + quarter:half + quarter + 1], a[half:half + 1], a[0:1]
    return a[quarter - 1:quarter], a[half + quarter - 1:half + quarter], a[half - 1:half], a[c - 1:c]


def _scan_products(a, q, k, v, st, st_kv, reverse):
    c = a.shape[0]
    half = c // 2
    m1_lo, m1_hi, m0, a_tot = _scan_refs(a, reverse)
    lo_rows, hi_rows = slice(0, half), slice(half, c)
    row = lax.broadcasted_iota(jnp.int32, a.shape, 0)
    e1 = a - jnp.where(row < half, m1_lo, m1_hi)
    pq = q.astype(F32) * jnp.exp2(e1)
    pk = k.astype(F32) * jnp.exp2(-e1)
    s1 = lax.dot_general(pq.astype(BF16), pk.astype(BF16), _NT, preferred_element_type=F32)
    if reverse:
        q_rows, m1_q, k_rows, m1_k = lo_rows, m1_lo, hi_rows, m1_hi
    else:
        q_rows, m1_q, k_rows, m1_k = hi_rows, m1_hi, lo_rows, m1_lo
    q0 = (pq[q_rows] * jnp.exp2(m1_q - m0)).astype(BF16)
    k0 = (pk[k_rows] * jnp.exp2(m0 - m1_k)).astype(BF16)
    s0 = lax.dot_general(q0, k0, _NT, preferred_element_type=F32)
    qi = jnp.concatenate([pq[lo_rows] * jnp.exp2(m1_lo), pq[hi_rows] * jnp.exp2(m1_hi)], axis=0)
    inter = jnp.dot(qi.astype(BF16), st_kv, preferred_element_type=F32)
    kh = jnp.concatenate([pk[lo_rows] * jnp.exp2(a_tot - m1_lo), pk[hi_rows] * jnp.exp2(a_tot - m1_hi)], axis=0)
    upd = lax.dot_general(v, kh.astype(BF16), _TN, preferred_element_type=F32)
    return s1, s0, inter, st * jnp.exp2(a_tot) + upd


def _scan_output(s1, s0, inter, v, mask1, reverse):
    c = s1.shape[0]
    half = c // 2
    lo_rows, hi_rows = slice(0, half), slice(half, c)
    q_rows, k_rows = (lo_rows, hi_rows) if reverse else (hi_rows, lo_rows)
    intra = jnp.dot(jnp.where(mask1, s1, 0.0).astype(BF16), v, preferred_element_type=F32)
    cross = jnp.dot(s0.astype(BF16), v[k_rows], preferred_element_type=F32)
    o = inter + intra
    o_q = o[q_rows] + cross
    return jnp.concatenate([o_q, o[hi_rows]] if reverse else [o[lo_rows], o_q], axis=0)


def _scan_kernel(*refs, reverse, readout):
    k_ref, v_ref, lf_ref, s0_ref, q_ref = refs[:5]
    if readout:
        (ob_ref, g_ref, ng_ref, o_ref, st_ref, stkv_ref, a_ref) = refs[5:]
    else:
        (o_ref, st_ref, stkv_ref, a_ref) = refs[5:]
    c = SCAN_CHUNK
    n_batch, rows = k_ref.shape[0], k_ref.shape[1]
    n_chunks = rows // c
    n_heads = k_ref.shape[2] // HEAD_DIM
    recurrences = [(h, bi) for h in range(n_heads) for bi in range(n_batch)]

    @pl.when(pl.program_id(0) == 0)
    def _():
        st_ref[...] = s0_ref[...]
        for h, bi in recurrences:
            stkv_ref[bi, h] = s0_ref[bi, h].T.astype(BF16)

    mask1, tri2 = _scan_masks(c, reverse)

    def cumsum_body(ci, carry):
        r0 = pl.multiple_of(ci * c, c)
        for bi in range(n_batch):
            a_ref[bi, pl.ds(r0, c), :] = _scan_cumsum(lf_ref[bi, pl.ds(r0, c), :], tri2)
        return carry

    lax.fori_loop(0, n_chunks, cumsum_body, 0, unroll=4)

    def chunk_body(ci, carry):
        cidx = (n_chunks - 1 - ci) if reverse else ci
        r0 = pl.multiple_of(cidx * c, c)
        tile = lambda ref, h, bi: ref[bi, pl.ds(r0, c), h * HEAD_DIM:(h + 1) * HEAD_DIM]
        wave = []
        for h, bi in recurrences:
            s1, s0, inter, st_new = _scan_products(
                tile(a_ref, h, bi), tile(q_ref, h, bi), tile(k_ref, h, bi), tile(v_ref, h, bi),
                st_ref[bi, h], stkv_ref[bi, h], reverse)
            st_ref[bi, h] = st_new
            stkv_ref[bi, h] = st_new.T.astype(BF16)
            wave.append((s1, s0, inter))
        for (h, bi), (s1, s0, inter) in zip(recurrences, wave):
            lanes = slice(h * HEAD_DIM, (h + 1) * HEAD_DIM)
            o = _scan_output(s1, s0, inter, tile(v_ref, h, bi), mask1, reverse)
            if readout:
                o = o + tile(ob_ref, h, bi)
                ms = jnp.mean(o * o, axis=-1, keepdims=True)
                y = o * lax.rsqrt(ms + EPS) * ng_ref[:, lanes]
                o = y * tile(g_ref, h, bi).astype(F32)
            o_ref[bi, pl.ds(r0, c), lanes] = o.astype(o_ref.dtype)
        return carry

    lax.fori_loop(0, n_chunks, chunk_body, 0, unroll=4)


def _scan_call(k, v, lf, s0, q, *, reverse, ob=None, g=None, ng=None):
    b, l, _ = v[0].shape
    dh = ng.shape[1] if ng is not None else s0.shape[1] * HEAD_DIM
    tb = min(SCAN_BLOCK, l)
    n_blocks = l // tb
    readout = ob is not None
    blk = (lambda j: n_blocks - 1 - j) if reverse else (lambda j: j)
    seq_spec = lambda group: pl.BlockSpec((b, tb, dh), lambda j: (0, blk(j), group))
    st_spec = pl.BlockSpec(s0.shape, lambda j: (0, 0, 0, 0))
    in_specs = [seq_spec(k[1]), seq_spec(v[1]), seq_spec(lf[1]), st_spec, seq_spec(q[1])]
    args = [k[0], v[0], lf[0], s0, q[0]]
    if readout:
        in_specs += [seq_spec(ob[1]), seq_spec(g[1]), pl.BlockSpec((1, dh), lambda j: (0, 0))]
        args += [ob[0], g[0], ng]
    return pl.pallas_call(
        functools.partial(_scan_kernel, reverse=reverse, readout=readout),
        grid=(n_blocks,),
        in_specs=in_specs,
        out_specs=seq_spec(0),
        out_shape=jax.ShapeDtypeStruct((b, l, dh), BF16 if readout else F32),
        scratch_shapes=[pltpu.VMEM(s0.shape, F32), pltpu.VMEM(s0.shape, BF16), pltpu.VMEM((b, tb, dh), F32)],
        compiler_params=_params(("arbitrary",)),
        name="scan_readout_fwd" if readout else "scan_out_bwd",
    )(*args)


def _ctx_proj_kernel(ctx_ref, g_ref, mod_ref, w_ref, o_ref, hc_ref):
    @pl.when(pl.program_id(0) == 0)
    def _():
        for bi in range(ctx_ref.shape[0]):
            hc_ref[bi] = _norm_modulate(ctx_ref[bi], g_ref[...], mod_ref[0, 0:1, :],
                                        mod_ref[1, 0:1, :]).astype(BF16)

    w = w_ref[...].astype(BF16)
    for bi in range(ctx_ref.shape[0]):
        o_ref[bi] = jnp.dot(hc_ref[bi], w, preferred_element_type=F32)


def _ctx_proj_call(ctx, g, mod_early, mod_row, w, width, tn=1024):
    b, lc, d = ctx.shape
    return pl.pallas_call(
        _ctx_proj_kernel,
        grid=(width // tn,),
        in_specs=[pl.BlockSpec((b, lc, d), lambda j: (0, 0, 0)),
                  pl.BlockSpec((1, d), lambda j: (0, 0)),
                  _mod_block(N_MOD_EARLY, d, lambda j: mod_row),
                  pl.BlockSpec((d, tn), lambda j: (0, j))],
        out_specs=pl.BlockSpec((b, lc, tn), lambda j: (0, 0, j)),
        out_shape=jax.ShapeDtypeStruct((b, lc, width), F32),
        scratch_shapes=[pltpu.VMEM((b, lc, d), BF16)],
        compiler_params=_params(("arbitrary",)),
        name="ctx_proj",
    )(ctx, g, mod_early, w)


def _ctx_state_kernel(pv_ref, pf_ref, pb_ref, lgf_ref, lgb_ref, sf_ref, sb_ref):
    c = SCAN_CHUNK
    n_batch, rows = pv_ref.shape[0], pv_ref.shape[1]
    n_chunks = rows // c
    n_heads = pv_ref.shape[2] // HEAD_DIM
    for reverse, p_ref, lg_ref, s_ref in ((False, pf_ref, lgf_ref, sf_ref), (True, pb_ref, lgb_ref, sb_ref)):
        _, tri2 = _scan_masks(c, reverse)
        s_ref[...] = jnp.zeros(s_ref.shape, F32)

        def chunk_body(ci, carry, reverse=reverse, p_ref=p_ref, lg_ref=lg_ref, s_ref=s_ref, tri2=tri2):
            cidx = (n_chunks - 1 - ci) if reverse else ci
            r0 = pl.multiple_of(cidx * c, c)
            recurrences = [(h, bi) for h in range(n_heads) for bi in range(n_batch)]
            tile = lambda ref, h, bi: ref[bi, pl.ds(r0, c), h * HEAD_DIM:(h + 1) * HEAD_DIM]
            gates = [_forget_gate(tile(p_ref, h, bi), lg_ref[:, h * HEAD_DIM:(h + 1) * HEAD_DIM])
                     for h, bi in recurrences]
            sums = [_scan_cumsum(jnp.log2(f), tri2) for f in gates]
            for (h, bi), f, a in zip(recurrences, gates, sums):
                a_tot = _scan_refs(a, reverse)[3]
                kh = ((1.0 - f) * jnp.exp2(a_tot - a)).astype(BF16)
                upd = lax.dot_general(tile(pv_ref, h, bi).astype(BF16), kh, _TN, preferred_element_type=F32)
                s_ref[bi, h] = s_ref[bi, h] * jnp.exp2(a_tot) + upd
            return carry

        lax.fori_loop(0, n_chunks, chunk_body, 0)


def _ctx_state_call(pc, logits_f, logits_b, dh):
    b, lc, _ = pc.shape
    n_heads = dh // HEAD_DIM
    group = lambda n: pl.BlockSpec((b, lc, dh), lambda j: (0, 0, n))
    lg_spec = pl.BlockSpec(logits_f.shape, lambda j: (0, 0))
    st_spec = pl.BlockSpec((b, n_heads, HEAD_DIM, HEAD_DIM), lambda j: (0, 0, 0, 0))
    st_shape = jax.ShapeDtypeStruct((b, n_heads, HEAD_DIM, HEAD_DIM), F32)
    return pl.pallas_call(
        _ctx_state_kernel,
        grid=(1,),
        in_specs=[group(0), group(1), group(2), lg_spec, lg_spec],
        out_specs=[st_spec, st_spec],
        out_shape=[st_shape, st_shape],
        compiler_params=_params(("arbitrary",)),
        name="ctx_state",
    )(pc, pc, pc, logits_f, logits_b)


CONV_TILE_ROWS = 16
_HPAD = 16
_HROW = GRID_W + 2 * _HPAD


def _conv_kernel(cur_ref, prev_ref, next_ref, w_ref, b_ref, lg_ref, lb_ref, o_ref,
                 hpad_ref, vpad_ref, y_ref, *, n_tiles):
    tr = CONV_TILE_ROWS
    half = w_ref.shape[1] // 2
    n_lane_tiles = half // 128
    i = pl.program_id(1)

    hpad_ref[:, :, :_HPAD, :] = jnp.zeros((tr, n_lane_tiles, _HPAD, 128), F32)
    hpad_ref[:, :, _HPAD + GRID_W:, :] = jnp.zeros((tr, n_lane_tiles, _HPAD, 128), F32)
    for r in range(tr):
        for lt in range(n_lane_tiles):
            hpad_ref[r, lt, _HPAD:_HPAD + GRID_W, :] = cur_ref[0, r * GRID_W:(r + 1) * GRID_W,
                                                               lt * 128:(lt + 1) * 128]
    halo = CONV_PAD * GRID_W
    for lt in range(n_lane_tiles):
        lanes = slice(lt * 128, (lt + 1) * 128)
        top = prev_ref[0, tr * GRID_W - halo:, lanes]
        bot = next_ref[0, :halo, lanes]
        vpad_ref[lt, :halo, :] = jnp.where(i > 0, top, 0.0)
        vpad_ref[lt, halo:halo + tr * GRID_W, :] = cur_ref[0, :, half + lt * 128:half + (lt + 1) * 128]
        vpad_ref[lt, halo + tr * GRID_W:, :] = jnp.where(i < n_tiles - 1, bot, 0.0)

    def h_row(r, carry):
        o0 = pl.multiple_of(r * GRID_W, GRID_W)
        for lt in range(n_lane_tiles):
            lanes = slice(lt * 128, (lt + 1) * 128)
            acc = jnp.zeros((GRID_W, 128), F32)
            for jj in range(CONV_WIDTH):
                off = _HPAD - CONV_PAD + jj
                acc = acc + hpad_ref[r, lt, off:off + GRID_W, :] * w_ref[jj:jj + 1, lanes]
            y_ref[pl.ds(o0, GRID_W), lanes] = acc + b_ref[:, lanes]
        return carry

    lax.fori_loop(0, tr, h_row, 0)

    n_sub = GRID_W // 8
    for lt in range(n_lane_tiles):
        vl = slice(half + lt * 128, half + (lt + 1) * 128)
        wv = [jnp.broadcast_to(w_ref[jj:jj + 1, vl], (8, 128)) for jj in range(CONV_WIDTH)]
        bias = b_ref[:, vl]

        def v_row(r, carry, lt=lt, vl=vl, wv=wv, bias=bias):
            v0 = pl.multiple_of(r * GRID_W, GRID_W)
            acc = [jnp.zeros((8, 128), F32)] * n_sub
            for jj in range(CONV_WIDTH):
                src = pl.multiple_of(v0 + jj * GRID_W, GRID_W)
                tap = vpad_ref[lt, pl.ds(src, GRID_W), :]
                acc = [acc[s] + tap[8 * s:8 * s + 8, :] * wv[jj] for s in range(n_sub)]
            y_ref[pl.ds(v0, GRID_W), vl] = jnp.concatenate(acc, axis=0) + bias
            return carry

        lax.fori_loop(0, tr, v_row, 0, unroll=2)

    def ln_row(r, carry):
        o0 = pl.multiple_of(r * GRID_W, GRID_W)
        y = y_ref[pl.ds(o0, GRID_W), :]
        mu = jnp.mean(y, axis=-1, keepdims=True)
        yc = y - mu
        var = jnp.mean(yc * yc, axis=-1, keepdims=True)
        z = yc * lax.rsqrt(var + EPS) * lg_ref[...] + lb_ref[...]
        o_ref[0, pl.ds(o0, GRID_W), :] = _silu(z).astype(BF16)
        return carry

    lax.fori_loop(0, tr, ln_row, 0, unroll=4)


def _conv_call(u, w, bias, ln_g, ln_b):
    b, l, ch = u.shape
    half = ch // 2
    tile = CONV_TILE_ROWS * GRID_W
    n_tiles = l // tile
    return pl.pallas_call(
        functools.partial(_conv_kernel, n_tiles=n_tiles),
        grid=(b, n_tiles),
        in_specs=[pl.BlockSpec((1, tile, ch), lambda i, j: (i, j, 0)),
                  pl.BlockSpec((1, tile, half), lambda i, j: (i, jnp.maximum(j - 1, 0), 1)),
                  pl.BlockSpec((1, tile, half), lambda i, j: (i, jnp.minimum(j + 1, n_tiles - 1), 1)),
                  pl.BlockSpec((CONV_WIDTH, ch), lambda i, j: (0, 0)),
                  pl.BlockSpec((1, ch), lambda i, j: (0, 0)),
                  pl.BlockSpec((1, ch), lambda i, j: (0, 0)),
                  pl.BlockSpec((1, ch), lambda i, j: (0, 0))],
        out_specs=pl.BlockSpec((1, tile, ch), lambda i, j: (i, j, 0)),
        out_shape=jax.ShapeDtypeStruct((b, l, ch), BF16),
        scratch_shapes=[pltpu.VMEM((CONV_TILE_ROWS, half // 128, _HROW, 128), F32),
                        pltpu.VMEM((half // 128, tile + 2 * CONV_PAD * GRID_W, 128), F32),
                        pltpu.VMEM((tile, ch), F32)],
        compiler_params=_params(("parallel", "parallel")),
        name="conv",
    )(u, u, u, w, bias, ln_g, ln_b)


MIX_SLICE = 128

def _mix_kernel(uc_ref, og_ref, gc_ref, gh_ref, x_ref, wpw_ref, wh_ref, wo_ref,
                npost_ref, npre_ref, mod_ref, x1_ref, h2_ref):
    tm = x_ref.shape[1]
    n_split = tm // MIX_SLICE
    for r in range(n_split):
        rows = slice(r * MIX_SLICE, (r + 1) * MIX_SLICE)
        y_c = jnp.dot(uc_ref[0, rows, :], wpw_ref[...], preferred_element_type=F32)
        y_h = jnp.dot(og_ref[0, rows, :], wh_ref[...], preferred_element_type=F32)
        z = gc_ref[0, rows, :].astype(F32) * y_c + gh_ref[0, rows, :].astype(F32) * y_h
        y = jnp.dot(z.astype(BF16), wo_ref[...], preferred_element_type=F32)
        ms = jnp.mean(y * y, axis=-1, keepdims=True)
        yn = y * lax.rsqrt(ms + EPS) * npost_ref[...]
        x1 = x_ref[0, rows, :] + mod_ref[0, 0:1, :] * yn
        x1_ref[0, rows, :] = x1
        h2_ref[0, rows, :] = _norm_modulate(x1, npre_ref[...], mod_ref[1, 0:1, :],
                                            mod_ref[2, 0:1, :]).astype(BF16)


def _mix_call(uc, og, gates, x, wpw, wh, wo, npost, npre, mod_late, tm=512):
    b, l, d = x.shape
    dc = uc.shape[2]
    gc = gh = gates
    row = lambda width, col=0: pl.BlockSpec((1, tm, width), lambda i, j: (i, j, col))
    const = lambda shape: pl.BlockSpec(shape, lambda i, j: (0,) * len(shape), pipeline_mode=pl.Buffered(1))
    return pl.pallas_call(
        _mix_kernel,
        grid=(b, l // tm),
        in_specs=[row(dc), row(dc), row(d, 0), row(d, 1), row(d),
                  const(wpw.shape), const(wh.shape), const(wo.shape),
                  const((1, d)), const((1, d)),
                  _mod_block(N_MOD - N_MOD_EARLY, d, lambda i, j: i)],
        out_specs=[row(d), row(d)],
        out_shape=[jax.ShapeDtypeStruct((b, l, d), F32), jax.ShapeDtypeStruct((b, l, d), BF16)],
        compiler_params=_params(("parallel", "parallel")),
        name="mix",
    )(uc, og, gc, gh, x, wpw, wh, wo, npost, npre, mod_late)


def _mlp_kernel(h_ref, w1_ref, w2_ref, x1_ref, npost_ref, mod_ref, o_ref, *, n_ff):
    j = pl.program_id(2)

    @pl.when(j == 0)
    def _():
        o_ref[...] = jnp.zeros(o_ref.shape, F32)

    a = jnp.dot(h_ref[0], w1_ref[...], preferred_element_type=F32)
    a = jnp.square(jnp.maximum(a, 0.0)).astype(BF16)
    o_ref[0] += jnp.dot(a, w2_ref[...], preferred_element_type=F32)

    @pl.when(j == n_ff - 1)
    def _():
        rb = 32
        for r in range(o_ref.shape[1] // rb):
            rows = slice(r * rb, (r + 1) * rb)
            y = o_ref[0, rows, :]
            ms = jnp.mean(y * y, axis=-1, keepdims=True)
            yn = y * lax.rsqrt(ms + EPS) * npost_ref[...]
            o_ref[0, rows, :] = x1_ref[0, rows, :] + mod_ref[3, 0:1, :] * yn


def _mlp_call(h2, w1, w2, x1, npost, mod_late, tm=1024, tf=1024):
    b, l, d = x1.shape
    dff = w1.shape[1]
    n_ff = dff // tf
    return pl.pallas_call(
        functools.partial(_mlp_kernel, n_ff=n_ff),
        grid=(b, l // tm, n_ff),
        in_specs=[pl.BlockSpec((1, tm, d), lambda i, m, j: (i, m, 0)),
                  pl.BlockSpec((d, tf), lambda i, m, j: (0, j)),
                  pl.BlockSpec((tf, d), lambda i, m, j: (j, 0)),
                  pl.BlockSpec((1, tm, d), lambda i, m, j: (i, m, 0), pipeline_mode=pl.Buffered(1)),
                  pl.BlockSpec((1, d), lambda i, m, j: (0, 0)),
                  _mod_block(N_MOD - N_MOD_EARLY, d, lambda i, m, j: i)],
        out_specs=pl.BlockSpec((1, tm, d), lambda i, m, j: (i, m, 0)),
        out_shape=jax.ShapeDtypeStruct((b, l, d), F32),
        compiler_params=_params(("parallel", "parallel", "arbitrary")),
        name="mlp",
    )(h2, w1, w2, x1, npost, mod_late)


def kernel(x, c, ctx, c_ctx, w_mod, b_mod, norm_pre_mix, norm_post_mix, norm_pre_mlp, norm_post_mlp, w_in,
           conv_dw_w, conv_dw_b, conv_ln_g, conv_ln_b, conv_pw_w, hgrn_lb_logits, hgrn_norm_g, hgrn_out_w,
           w_out, mlp_w1, mlp_w2):
    assert w_mod.shape[0] == 1, "single-layer block"
    b, l, d = x.shape
    lc = ctx.shape[1]
    dh = hgrn_norm_g.shape[1]
    dc = conv_dw_w.shape[2]
    assert l % (CONV_TILE_ROWS * GRID_W) == 0 and CONV_PAD <= CONV_TILE_ROWS and CONV_PAD <= _HPAD
    assert l % SCAN_BLOCK == 0 and lc % SCAN_CHUNK == 0

    cond = jnp.concatenate([c, c_ctx[None, :]], axis=0)
    cc = jnp.pad(cond[:, None, :], ((0, 0), (0, MOD_ROW_STRIDE - 1), (0, 0))).reshape(-1, d)
    mod_early = _mod_call(cc, w_mod[0], b_mod, N_MOD_EARLY)

    w_in0 = w_in[0]
    lb_f, lb_b = hgrn_lb_logits[0], hgrn_lb_logits[1]
    off = lambda n: n * dh

    pc = _ctx_proj_call(ctx, norm_pre_mix, mod_early, b, w_in0, 3 * dh)
    s_f, s_b = _ctx_state_call(pc, lb_f, lb_b, dh)

    h, v = _prenorm_v_call(x, norm_pre_mix, mod_early, w_in0, dh)
    h = h.reshape(b * l, d)
    seq = lambda t: t.reshape(b, l, t.shape[-1])
    lf, kk = _proj_call(h, w_in0, [off(1)], 2 * dh, _ep_forget, [F32, BF16],
                        extra=(jnp.concatenate([lb_f, lb_b], axis=1),), name="proj_forget")
    qg_scale = jnp.concatenate([jnp.full((1, dh), HEAD_DIM ** -0.5, F32), jnp.ones((1, dh), F32)], axis=1)
    qg, wpw, wh, wo, mod_late = _proj_call(h, w_in0, [off(3)], 2 * dh, _ep_scaled_silu, [BF16], extra=(qg_scale,),
                                           side=(conv_pw_w[0], hgrn_out_w[0], w_out[0]),
                                           late_mod=(cc, w_mod[0], b_mod), name="proj_qg")
    (u,) = _proj_call(h, w_in0, [off(5), off(5) + dc], dc, _ep_glu, [F32], tn=512, name="proj_glu")
    gates, w1, w2 = _proj_call(h, w_in0, [off(5) + 2 * dc], 2 * d, _ep_sigmoid, [BF16],
                               side=(mlp_w1[0], mlp_w2[0]), name="proj_gates")

    lf, kk, qg = seq(lf), seq(kk), seq(qg)
    o_b = _scan_call((kk, 1), (v, 0), (lf, 1), s_b, (qg, 0), reverse=True)
    og = _scan_call((kk, 0), (v, 0), (lf, 0), s_f, (qg, 0), reverse=False, ob=(o_b, 0), g=(qg, 1),
                    ng=hgrn_norm_g)

    uc = _conv_call(seq(u), conv_dw_w[0], conv_dw_b, conv_ln_g, conv_ln_b)

    x1, h2 = _mix_call(uc, og, seq(gates), x, wpw, wh, wo, norm_post_mix, norm_pre_mlp, mod_late)
    return _mlp_call(h2, w1, w2, x1, norm_post_mlp, mod_late)
```

```python
import functools

import jax
import jax.numpy as jnp
from jax import lax
from jax.experimental import pallas as pl
from jax.experimental.pallas import tpu as pltpu

F32 = jnp.float32
BF16 = jnp.bfloat16

GRID_W = 64
CONV_WIDTH = 31
CONV_PAD = (CONV_WIDTH - 1) // 2
HEAD_DIM = 128
N_MOD = 6
N_MOD_EARLY = 2
MOD_ROW_STRIDE = 16
EPS = 1e-6

SCAN_CHUNK = 64
SCAN_BLOCK = 512

VMEM_LIMIT = 56 * 1024 * 1024

_NT = (((1,), (1,)), ((), ()))
_TN = (((0,), (0,)), ((), ()))


def _params(sem):
    return pltpu.CompilerParams(dimension_semantics=sem, vmem_limit_bytes=VMEM_LIMIT)


def _sigmoid(x):
    return 0.5 * jnp.tanh(0.5 * x) + 0.5


def _silu(x):
    return x * _sigmoid(x)


def _norm_modulate(x, gain, shift, scale):
    ms = jnp.mean(x * x, axis=-1, keepdims=True)
    return (x * lax.rsqrt(ms + EPS) * gain) * (1.0 + scale) + shift


def _forget_gate(raw, logits):
    e = jnp.exp(logits - jnp.max(logits, axis=0, keepdims=True))
    lb = e[0:1] / jnp.sum(e, axis=0, keepdims=True)
    return lb + (1.0 - lb) * jax.nn.sigmoid(raw)


def _mod_rows(c_ref, w_ref, b_ref):
    s = _silu(c_ref[...]).astype(BF16)
    return jnp.dot(s, w_ref[...].astype(BF16), preferred_element_type=F32) + b_ref[...]


def _mod_kernel(c_ref, w_ref, b_ref, o_ref):
    o_ref[0] = _mod_rows(c_ref, w_ref, b_ref)


def _mod_call(cc, w_mod, b_mod, n_seg, tn=1024):
    rows, d = cc.shape
    return pl.pallas_call(
        _mod_kernel,
        grid=(n_seg * d // tn,),
        in_specs=[pl.BlockSpec((rows, d), lambda j: (0, 0)),
                  pl.BlockSpec((d, tn), lambda j: (0, j)),
                  pl.BlockSpec((1, tn), lambda j: (0, j))],
        out_specs=pl.BlockSpec((1, rows, tn), lambda j: (j // (d // tn), 0, j % (d // tn))),
        out_shape=jax.ShapeDtypeStruct((n_seg, rows, d), F32),
        compiler_params=_params(("parallel",)),
        name="mod",
    )(cc, w_mod, b_mod)


def _mod_block(n_seg, d, row_of):
    return pl.BlockSpec((n_seg, 8, d), lambda *idx: (0, row_of(*idx) * (MOD_ROW_STRIDE // 8), 0))


def _proj_kernel(*refs, n_w, n_extra, n_side, n_mod, n_out, epilogue):
    h_ref = refs[0]
    pos = 1
    w_refs = refs[pos:pos + n_w]
    pos += n_w
    extra = refs[pos:pos + n_extra]
    pos += n_extra
    side_in = refs[pos:pos + n_side]
    pos += n_side
    mod_in = refs[pos:pos + 3 * n_mod]
    pos += 3 * n_mod
    out_refs = refs[pos:pos + n_out]
    pos += n_out
    side_out = refs[pos:pos + n_side]
    pos += n_side
    mod_out = refs[pos:pos + n_mod]
    pos += n_mod
    wbf_refs = refs[pos:]

    @pl.when(pl.program_id(1) == 0)
    def _():
        for w_ref, wbf_ref in zip(w_refs, wbf_refs):
            wbf_ref[...] = w_ref[...].astype(BF16)

    for si_ref, so_ref in zip(side_in, side_out):
        so_ref[...] = si_ref[...].astype(BF16)
    if n_mod:
        mod_out[0][0] = _mod_rows(*mod_in)

    h = h_ref[...]
    accs = [jnp.dot(h, wbf_ref[...], preferred_element_type=F32) for wbf_ref in wbf_refs]
    outs = epilogue(accs, [e_ref[...] for e_ref in extra])
    for o_ref, o in zip(out_refs, outs):
        o_ref[...] = o.astype(o_ref.dtype)


def _ep_forget(accs, extra):
    f = _forget_gate(accs[0], extra[0])
    return jnp.log2(f), 1.0 - f


def _ep_scaled_silu(accs, extra):
    return (_silu(accs[0]) * extra[0],)


def _ep_glu(accs, extra):
    return (accs[0] * _sigmoid(accs[1]),)


def _ep_sigmoid(accs, extra):
    return (_sigmoid(accs[0]),)


def _proj_call(h, w, col_offsets, width, epilogue, out_dtypes, extra=(), side=(), late_mod=None, tm=1024,
               tn=1024, name="proj"):
    m, k = h.shape
    tm = min(tm, m)
    n_w = len(col_offsets)
    n_j, n_i = width // tn, m // tm
    mod_specs, mod_out_specs, mod_out_shape, mod_args = [], [], [], []
    if late_mod is not None:
        cc, w_mod, b_mod = late_mod
        rows, d = cc.shape
        n_late = w_mod.shape[1] // d - N_MOD_EARLY
        tnm = n_late * d // (n_j * n_i)
        per = d // tnm
        step = lambda j, i: j * n_i + i
        mod_specs = [pl.BlockSpec((rows, d), lambda j, i: (0, 0)),
                     pl.BlockSpec((d, tnm), lambda j, i: (0, N_MOD_EARLY * per + step(j, i))),
                     pl.BlockSpec((1, tnm), lambda j, i: (0, N_MOD_EARLY * per + step(j, i)))]
        mod_out_specs = [pl.BlockSpec((1, rows, tnm), lambda j, i: (step(j, i) // per, 0, step(j, i) % per))]
        mod_out_shape = [jax.ShapeDtypeStruct((n_late, rows, d), F32)]
        mod_args = [cc, w_mod, b_mod]
    in_specs = [pl.BlockSpec((tm, k), lambda j, i: (i, 0))]
    for off in col_offsets:
        in_specs.append(pl.BlockSpec((k, tn), lambda j, i, off=off: (0, off // tn + j)))
    for e in extra:
        in_specs.append(pl.BlockSpec((e.shape[0], tn), lambda j, i: (0, j)))
    side_specs = [pl.BlockSpec((a.shape[0] // (n_j * n_i), a.shape[1]), lambda j, i: (j * n_i + i, 0))
                  for a in side]
    return pl.pallas_call(
        functools.partial(_proj_kernel, n_w=n_w, n_extra=len(extra), n_side=len(side), n_mod=len(mod_out_specs),
                          n_out=len(out_dtypes), epilogue=epilogue),
        grid=(n_j, n_i),
        in_specs=in_specs + side_specs + mod_specs,
        out_specs=[pl.BlockSpec((tm, tn), lambda j, i: (i, j)) for _ in out_dtypes] + side_specs + mod_out_specs,
        out_shape=[jax.ShapeDtypeStruct((m, width), dt) for dt in out_dtypes]
                  + [jax.ShapeDtypeStruct(a.shape, BF16) for a in side] + mod_out_shape,
        scratch_shapes=[pltpu.VMEM((k, tn), BF16) for _ in range(n_w)],
        compiler_params=_params(("parallel", "arbitrary")),
        name=name,
    )(h, *([w] * n_w), *extra, *side, *mod_args)


def _prenorm_v_kernel(x_ref, g_ref, mod_ref, w_ref, h_ref, v_ref, wbf_ref):
    @pl.when((pl.program_id(0) == 0) & (pl.program_id(1) == 0))
    def _():
        wbf_ref[...] = w_ref[...].astype(BF16)

    h = _norm_modulate(x_ref[0], g_ref[...], mod_ref[0, 0:1, :], mod_ref[1, 0:1, :]).astype(BF16)
    h_ref[0] = h
    v_ref[0] = jnp.dot(h, wbf_ref[...], preferred_element_type=F32).astype(BF16)


def _prenorm_v_call(x, g, mod_early, w, width, tl=1024):
    b, l, d = x.shape
    return pl.pallas_call(
        _prenorm_v_kernel,
        grid=(b, l // tl),
        in_specs=[pl.BlockSpec((1, tl, d), lambda i, j: (i, j, 0)),
                  pl.BlockSpec((1, d), lambda i, j: (0, 0)),
                  _mod_block(N_MOD_EARLY, d, lambda i, j: i),
                  pl.BlockSpec((d, width), lambda i, j: (0, 0), pipeline_mode=pl.Buffered(1))],
        out_specs=[pl.BlockSpec((1, tl, d), lambda i, j: (i, j, 0)),
                   pl.BlockSpec((1, tl, width), lambda i, j: (i, j, 0))],
        out_shape=[jax.ShapeDtypeStruct((b, l, d), BF16), jax.ShapeDtypeStruct((b, l, width), BF16)],
        scratch_shapes=[pltpu.VMEM((d, width), BF16)],
        compiler_params=_params(("arbitrary", "arbitrary")),
        name="prenorm_v",
    )(x, g, mod_early, w)


def _scan_masks(c, reverse):
    t = lax.broadcasted_iota(jnp.int32, (c, c), 0)
    s = lax.broadcasted_iota(jnp.int32, (c, c), 1)
    half = c // 2
    before = (s >= t) if reverse else (s <= t)
    mask1 = ((t < half) == (s < half)) & before
    tri = jnp.where(before, 1.0, 0.0).astype(BF16)
    return mask1, jnp.concatenate([tri, tri], axis=1)


def _scan_cumsum(lf, tri2):
    hi = lf.astype(BF16)
    lo = (lf - hi.astype(F32)).astype(BF16)
    return jnp.dot(tri2, jnp.concatenate([hi, lo], axis=0), preferred_element_type=F32)


def _scan_refs(a, reverse):
    c = a.shape[0]
    half, quarter = c // 2, c // 4
    if reverse:
        return a[quarter:quarter + 1], a[half + quarter:half + quarter + 1], a[half:half + 1], a[0:1]
    return a[quarter - 1:quarter], a[half + quarter - 1:half + quarter], a[half - 1:half], a[c - 1:c]


def _scan_products(a, q, k, v, st, st_kv, reverse):
    c = a.shape[0]
    half = c // 2
    m1_lo, m1_hi, m0, a_tot = _scan_refs(a, reverse)
    lo_rows, hi_rows = slice(0, half), slice(half, c)
    row = lax.broadcasted_iota(jnp.int32, a.shape, 0)
    e1 = a - jnp.where(row < half, m1_lo, m1_hi)
    pq = q.astype(F32) * jnp.exp2(e1)
    pk = k.astype(F32) * jnp.exp2(-e1)
    s1 = lax.dot_general(pq.astype(BF16), pk.astype(BF16), _NT, preferred_element_type=F32)
    if reverse:
        q_rows, m1_q, k_rows, m1_k = lo_rows, m1_lo, hi_rows, m1_hi
    else:
        q_rows, m1_q, k_rows, m1_k = hi_rows, m1_hi, lo_rows, m1_lo
    q0 = (pq[q_rows] * jnp.exp2(m1_q - m0)).astype(BF16)
    k0 = (pk[k_rows] * jnp.exp2(m0 - m1_k)).astype(BF16)
    s0 = lax.dot_general(q0, k0, _NT, preferred_element_type=F32)
    qi = jnp.concatenate([pq[lo_rows] * jnp.exp2(m1_lo), pq[hi_rows] * jnp.exp2(m1_hi)], axis=0)
    inter = jnp.dot(qi.astype(BF16), st_kv, preferred_element_type=F32)
    kh = jnp.concatenate([pk[lo_rows] * jnp.exp2(a_tot - m1_lo), pk[hi_rows] * jnp.exp2(a_tot - m1_hi)], axis=0)
    upd = lax.dot_general(v, kh.astype(BF16), _TN, preferred_element_type=F32)
    return s1, s0, inter, st * jnp.exp2(a_tot) + upd


def _scan_output(s1, s0, inter, v, mask1, reverse):
    c = s1.shape[0]
    half = c // 2
    lo_rows, hi_rows = slice(0, half), slice(half, c)
    q_rows, k_rows = (lo_rows, hi_rows) if reverse else (hi_rows, lo_rows)
    intra = jnp.dot(jnp.where(mask1, s1, 0.0).astype(BF16), v, preferred_element_type=F32)
    cross = jnp.dot(s0.astype(BF16), v[k_rows], preferred_element_type=F32)
    o = inter + intra
    o_q = o[q_rows] + cross
    return jnp.concatenate([o_q, o[hi_rows]] if reverse else [o[lo_rows], o_q], axis=0)


def _scan_kernel(*refs, reverse, readout):
    k_ref, v_ref, lf_ref, s0_ref, q_ref = refs[:5]
    if readout:
        (ob_ref, g_ref, ng_ref, o_ref, st_ref, stkv_ref, a_ref) = refs[5:]
    else:
        (o_ref, st_ref, stkv_ref, a_ref) = refs[5:]
    c = SCAN_CHUNK
    n_batch, rows = k_ref.shape[0], k_ref.shape[1]
    n_chunks = rows // c
    n_heads = k_ref.shape[2] // HEAD_DIM
    recurrences = [(h, bi) for h in range(n_heads) for bi in range(n_batch)]

    @pl.when(pl.program_id(0) == 0)
    def _():
        st_ref[...] = s0_ref[...]
        for h, bi in recurrences:
            stkv_ref[bi, h] = s0_ref[bi, h].T.astype(BF16)

    mask1, tri2 = _scan_masks(c, reverse)

    def cumsum_body(ci, carry):
        r0 = pl.multiple_of(ci * c, c)
        for bi in range(n_batch):
            a_ref[bi, pl.ds(r0, c), :] = _scan_cumsum(lf_ref[bi, pl.ds(r0, c), :], tri2)
        return carry

    lax.fori_loop(0, n_chunks, cumsum_body, 0, unroll=4)

    def chunk_body(ci, carry):
        cidx = (n_chunks - 1 - ci) if reverse else ci
        r0 = pl.multiple_of(cidx * c, c)
        tile = lambda ref, h, bi: ref[bi, pl.ds(r0, c), h * HEAD_DIM:(h + 1) * HEAD_DIM]
        wave = []
        for h, bi in recurrences:
            s1, s0, inter, st_new = _scan_products(
                tile(a_ref, h, bi), tile(q_ref, h, bi), tile(k_ref, h, bi), tile(v_ref, h, bi),
                st_ref[bi, h], stkv_ref[bi, h], reverse)
            st_ref[bi, h] = st_new
            stkv_ref[bi, h] = st_new.T.astype(BF16)
            wave.append((s1, s0, inter))
        for (h, bi), (s1, s0, inter) in zip(recurrences, wave):
            lanes = slice(h * HEAD_DIM, (h + 1) * HEAD_DIM)
            o = _scan_output(s1, s0, inter, tile(v_ref, h, bi), mask1, reverse)
            if readout:
                o = o + tile(ob_ref, h, bi)
                ms = jnp.mean(o * o, axis=-1, keepdims=True)
                y = o * lax.rsqrt(ms + EPS) * ng_ref[:, lanes]
                o = y * tile(g_ref, h, bi).astype(F32)
            o_ref[bi, pl.ds(r0, c), lanes] = o.astype(o_ref.dtype)
        return carry

    lax.fori_loop(0, n_chunks, chunk_body, 0, unroll=4)


def _scan_call(k, v, lf, s0, q, *, reverse, ob=None, g=None, ng=None):
    b, l, _ = v[0].shape
    dh = ng.shape[1] if ng is not None else s0.shape[1] * HEAD_DIM
    tb = min(SCAN_BLOCK, l)
    n_blocks = l // tb
    readout = ob is not None
    blk = (lambda j: n_blocks - 1 - j) if reverse else (lambda j: j)
    seq_spec = lambda group: pl.BlockSpec((b, tb, dh), lambda j: (0, blk(j), group))
    st_spec = pl.BlockSpec(s0.shape, lambda j: (0, 0, 0, 0))
    in_specs = [seq_spec(k[1]), seq_spec(v[1]), seq_spec(lf[1]), st_spec, seq_spec(q[1])]
    args = [k[0], v[0], lf[0], s0, q[0]]
    if readout:
        in_specs += [seq_spec(ob[1]), seq_spec(g[1]), pl.BlockSpec((1, dh), lambda j: (0, 0))]
        args += [ob[0], g[0], ng]
    return pl.pallas_call(
        functools.partial(_scan_kernel, reverse=reverse, readout=readout),
        grid=(n_blocks,),
        in_specs=in_specs,
        out_specs=seq_spec(0),
        out_shape=jax.ShapeDtypeStruct((b, l, dh), BF16 if readout else F32),
        scratch_shapes=[pltpu.VMEM(s0.shape, F32), pltpu.VMEM(s0.shape, BF16), pltpu.VMEM((b, tb, dh), F32)],
        compiler_params=_params(("arbitrary",)),
        name="scan_readout_fwd" if readout else "scan_out_bwd",
    )(*args)


def _ctx_proj_kernel(ctx_ref, g_ref, mod_ref, w_ref, o_ref, hc_ref):
    @pl.when(pl.program_id(0) == 0)
    def _():
        for bi in range(ctx_ref.shape[0]):
            hc_ref[bi] = _norm_modulate(ctx_ref[bi], g_ref[...], mod_ref[0, 0:1, :],
                                        mod_ref[1, 0:1, :]).astype(BF16)

    w = w_ref[...].astype(BF16)
    for bi in range(ctx_ref.shape[0]):
        o_ref[bi] = jnp.dot(hc_ref[bi], w, preferred_element_type=F32)


def _ctx_proj_call(ctx, g, mod_early, mod_row, w, width, tn=1024):
    b, lc, d = ctx.shape
    return pl.pallas_call(
        _ctx_proj_kernel,
        grid=(width // tn,),
        in_specs=[pl.BlockSpec((b, lc, d), lambda j: (0, 0, 0)),
                  pl.BlockSpec((1, d), lambda j: (0, 0)),
                  _mod_block(N_MOD_EARLY, d, lambda j: mod_row),
                  pl.BlockSpec((d, tn), lambda j: (0, j))],
        out_specs=pl.BlockSpec((b, lc, tn), lambda j: (0, 0, j)),
        out_shape=jax.ShapeDtypeStruct((b, lc, width), F32),
        scratch_shapes=[pltpu.VMEM((b, lc, d), BF16)],
        compiler_params=_params(("arbitrary",)),
        name="ctx_proj",
    )(ctx, g, mod_early, w)


def _ctx_state_kernel(pv_ref, pf_ref, pb_ref, lgf_ref, lgb_ref, sf_ref, sb_ref):
    c = SCAN_CHUNK
    n_batch, rows = pv_ref.shape[0], pv_ref.shape[1]
    n_chunks = rows // c
    n_heads = pv_ref.shape[2] // HEAD_DIM
    for reverse, p_ref, lg_ref, s_ref in ((False, pf_ref, lgf_ref, sf_ref), (True, pb_ref, lgb_ref, sb_ref)):
        _, tri2 = _scan_masks(c, reverse)
        s_ref[...] = jnp.zeros(s_ref.shape, F32)

        def chunk_body(ci, carry, reverse=reverse, p_ref=p_ref, lg_ref=lg_ref, s_ref=s_ref, tri2=tri2):
            cidx = (n_chunks - 1 - ci) if reverse else ci
            r0 = pl.multiple_of(cidx * c, c)
            recurrences = [(h, bi) for h in range(n_heads) for bi in range(n_batch)]
            tile = lambda ref, h, bi: ref[bi, pl.ds(r0, c), h * HEAD_DIM:(h + 1) * HEAD_DIM]
            gates = [_forget_gate(tile(p_ref, h, bi), lg_ref[:, h * HEAD_DIM:(h + 1) * HEAD_DIM])
                     for h, bi in recurrences]
            sums = [_scan_cumsum(jnp.log2(f), tri2) for f in gates]
            for (h, bi), f, a in zip(recurrences, gates, sums):
                a_tot = _scan_refs(a, reverse)[3]
                kh = ((1.0 - f) * jnp.exp2(a_tot - a)).astype(BF16)
                upd = lax.dot_general(tile(pv_ref, h, bi).astype(BF16), kh, _TN, preferred_element_type=F32)
                s_ref[bi, h] = s_ref[bi, h] * jnp.exp2(a_tot) + upd
            return carry

        lax.fori_loop(0, n_chunks, chunk_body, 0)


def _ctx_state_call(pc, logits_f, logits_b, dh):
    b, lc, _ = pc.shape
    n_heads = dh // HEAD_DIM
    group = lambda n: pl.BlockSpec((b, lc, dh), lambda j: (0, 0, n))
    lg_spec = pl.BlockSpec(logits_f.shape, lambda j: (0, 0))
    st_spec = pl.BlockSpec((b, n_heads, HEAD_DIM, HEAD_DIM), lambda j: (0, 0, 0, 0))
    st_shape = jax.ShapeDtypeStruct((b, n_heads, HEAD_DIM, HEAD_DIM), F32)
    return pl.pallas_call(
        _ctx_state_kernel,
        grid=(1,),
        in_specs=[group(0), group(1), group(2), lg_spec, lg_spec],
        out_specs=[st_spec, st_spec],
        out_shape=[st_shape, st_shape],
        compiler_params=_params(("arbitrary",)),
        name="ctx_state",
    )(pc, pc, pc, logits_f, logits_b)


CONV_TILE_ROWS = 16
_HPAD = 16
_HROW = GRID_W + 2 * _HPAD


def _conv_kernel(cur_ref, prev_ref, next_ref, w_ref, b_ref, lg_ref, lb_ref, o_ref,
                 hpad_ref, vpad_ref, y_ref, *, n_tiles):
    tr = CONV_TILE_ROWS
    half = w_ref.shape[1] // 2
    n_lane_tiles = half // 128
    i = pl.program_id(1)

    hpad_ref[:, :, :_HPAD, :] = jnp.zeros((tr, n_lane_tiles, _HPAD, 128), F32)
    hpad_ref[:, :, _HPAD + GRID_W:, :] = jnp.zeros((tr, n_lane_tiles, _HPAD, 128), F32)
    for r in range(tr):
        for lt in range(n_lane_tiles):
            hpad_ref[r, lt, _HPAD:_HPAD + GRID_W, :] = cur_ref[0, r * GRID_W:(r + 1) * GRID_W,
                                                               lt * 128:(lt + 1) * 128]
    halo = CONV_PAD * GRID_W
    for lt in range(n_lane_tiles):
        lanes = slice(lt * 128, (lt + 1) * 128)
        top = prev_ref[0, tr * GRID_W - halo:, lanes]
        bot = next_ref[0, :halo, lanes]
        vpad_ref[lt, :halo, :] = jnp.where(i > 0, top, 0.0)
        vpad_ref[lt, halo:halo + tr * GRID_W, :] = cur_ref[0, :, half + lt * 128:half + (lt + 1) * 128]
        vpad_ref[lt, halo + tr * GRID_W:, :] = jnp.where(i < n_tiles - 1, bot, 0.0)

    def h_row(r, carry):
        o0 = pl.multiple_of(r * GRID_W, GRID_W)
        for lt in range(n_lane_tiles):
            lanes = slice(lt * 128, (lt + 1) * 128)
            acc = jnp.zeros((GRID_W, 128), F32)
            for jj in range(CONV_WIDTH):
                off = _HPAD - CONV_PAD + jj
                acc = acc + hpad_ref[r, lt, off:off + GRID_W, :] * w_ref[jj:jj + 1, lanes]
            y_ref[pl.ds(o0, GRID_W), lanes] = acc + b_ref[:, lanes]
        return carry

    lax.fori_loop(0, tr, h_row, 0)

    n_sub = GRID_W // 8
    for lt in range(n_lane_tiles):
        vl = slice(half + lt * 128, half + (lt + 1) * 128)
        wv = [jnp.broadcast_to(w_ref[jj:jj + 1, vl], (8, 128)) for jj in range(CONV_WIDTH)]
        bias = b_ref[:, vl]

        def v_row(r, carry, lt=lt, vl=vl, wv=wv, bias=bias):
            v0 = pl.multiple_of(r * GRID_W, GRID_W)
            acc = [jnp.zeros((8, 128), F32)] * n_sub
            for jj in range(CONV_WIDTH):
                src = pl.multiple_of(v0 + jj * GRID_W, GRID_W)
                tap = vpad_ref[lt, pl.ds(src, GRID_W), :]
                acc = [acc[s] + tap[8 * s:8 * s + 8, :] * wv[jj] for s in range(n_sub)]
            y_ref[pl.ds(v0, GRID_W), vl] = jnp.concatenate(acc, axis=0) + bias
            return carry

        lax.fori_loop(0, tr, v_row, 0, unroll=2)

    def ln_row(r, carry):
        o0 = pl.multiple_of(r * GRID_W, GRID_W)
        y = y_ref[pl.ds(o0, GRID_W), :]
        mu = jnp.mean(y, axis=-1, keepdims=True)
        yc = y - mu
        var = jnp.mean(yc * yc, axis=-1, keepdims=True)
        z = yc * lax.rsqrt(var + EPS) * lg_ref[...] + lb_ref[...]
        o_ref[0, pl.ds(o0, GRID_W), :] = _silu(z).astype(BF16)
        return carry

    lax.fori_loop(0, tr, ln_row, 0, unroll=4)


def _conv_call(u, w, bias, ln_g, ln_b):
    b, l, ch = u.shape
    half = ch // 2
    tile = CONV_TILE_ROWS * GRID_W
    n_tiles = l // tile
    return pl.pallas_call(
        functools.partial(_conv_kernel, n_tiles=n_tiles),
        grid=(b, n_tiles),
        in_specs=[pl.BlockSpec((1, tile, ch), lambda i, j: (i, j, 0)),
                  pl.BlockSpec((1, tile, half), lambda i, j: (i, jnp.maximum(j - 1, 0), 1)),
                  pl.BlockSpec((1, tile, half), lambda i, j: (i, jnp.minimum(j + 1, n_tiles - 1), 1)),
                  pl.BlockSpec((CONV_WIDTH, ch), lambda i, j: (0, 0)),
                  pl.BlockSpec((1, ch), lambda i, j: (0, 0)),
                  pl.BlockSpec((1, ch), lambda i, j: (0, 0)),
                  pl.BlockSpec((1, ch), lambda i, j: (0, 0))],
        out_specs=pl.BlockSpec((1, tile, ch), lambda i, j: (i, j, 0)),
        out_shape=jax.ShapeDtypeStruct((b, l, ch), BF16),
        scratch_shapes=[pltpu.VMEM((CONV_TILE_ROWS, half // 128, _HROW, 128), F32),
                        pltpu.VMEM((half // 128, tile + 2 * CONV_PAD * GRID_W, 128), F32),
                        pltpu.VMEM((tile, ch), F32)],
        compiler_params=_params(("parallel", "parallel")),
        name="conv",
    )(u, u, u, w, bias, ln_g, ln_b)


MIX_SLICE = 128

def _mix_kernel(uc_ref, og_ref, gc_ref, gh_ref, x_ref, wpw_ref, wh_ref, wo_ref,
                npost_ref, npre_ref, mod_ref, x1_ref, h2_ref):
    tm = x_ref.shape[1]
    n_split = tm // MIX_SLICE
    for r in range(n_split):
        rows = slice(r * MIX_SLICE, (r + 1) * MIX_SLICE)
        y_c = jnp.dot(uc_ref[0, rows, :], wpw_ref[...], preferred_element_type=F32)
        y_h = jnp.dot(og_ref[0, rows, :], wh_ref[...], preferred_element_type=F32)
        z = gc_ref[0, rows, :].astype(F32) * y_c + gh_ref[0, rows, :].astype(F32) * y_h
        y = jnp.dot(z.astype(BF16), wo_ref[...], preferred_element_type=F32)
        ms = jnp.mean(y * y, axis=-1, keepdims=True)
        yn = y * lax.rsqrt(ms + EPS) * npost_ref[...]
        x1 = x_ref[0, rows, :] + mod_ref[0, 0:1, :] * yn
        x1_ref[0, rows, :] = x1
        h2_ref[0, rows, :] = _norm_modulate(x1, npre_ref[...], mod_ref[1, 0:1, :],
                                            mod_ref[2, 0:1, :]).astype(BF16)


def _mix_call(uc, og, gates, x, wpw, wh, wo, npost, npre, mod_late, tm=512):
    b, l, d = x.shape
    dc = uc.shape[2]
    gc = gh = gates
    row = lambda width, col=0: pl.BlockSpec((1, tm, width), lambda i, j: (i, j, col))
    const = lambda shape: pl.BlockSpec(shape, lambda i, j: (0,) * len(shape), pipeline_mode=pl.Buffered(1))
    return pl.pallas_call(
        _mix_kernel,
        grid=(b, l // tm),
        in_specs=[row(dc), row(dc), row(d, 0), row(d, 1), row(d),
                  const(wpw.shape), const(wh.shape), const(wo.shape),
                  const((1, d)), const((1, d)),
                  _mod_block(N_MOD - N_MOD_EARLY, d, lambda i, j: i)],
        out_specs=[row(d), row(d)],
        out_shape=[jax.ShapeDtypeStruct((b, l, d), F32), jax.ShapeDtypeStruct((b, l, d), BF16)],
        compiler_params=_params(("parallel", "parallel")),
        name="mix",
    )(uc, og, gc, gh, x, wpw, wh, wo, npost, npre, mod_late)


def _mlp_kernel(h_ref, w1_ref, w2_ref, x1_ref, npost_ref, mod_ref, o_ref, *, n_ff):
    j = pl.program_id(2)

    @pl.when(j == 0)
    def _():
        o_ref[...] = jnp.zeros(o_ref.shape, F32)

    a = jnp.dot(h_ref[0], w1_ref[...], preferred_element_type=F32)
    a = jnp.square(jnp.maximum(a, 0.0)).astype(BF16)
    o_ref[0] += jnp.dot(a, w2_ref[...], preferred_element_type=F32)

    @pl.when(j == n_ff - 1)
    def _():
        rb = 32
        for r in range(o_ref.shape[1] // rb):
            rows = slice(r * rb, (r + 1) * rb)
            y = o_ref[0, rows, :]
            ms = jnp.mean(y * y, axis=-1, keepdims=True)
            yn = y * lax.rsqrt(ms + EPS) * npost_ref[...]
            o_ref[0, rows, :] = x1_ref[0, rows, :] + mod_ref[3, 0:1, :] * yn


def _mlp_call(h2, w1, w2, x1, npost, mod_late, tm=1024, tf=1024):
    b, l, d = x1.shape
    dff = w1.shape[1]
    n_ff = dff // tf
    return pl.pallas_call(
        functools.partial(_mlp_kernel, n_ff=n_ff),
        grid=(b, l // tm, n_ff),
        in_specs=[pl.BlockSpec((1, tm, d), lambda i, m, j: (i, m, 0)),
                  pl.BlockSpec((d, tf), lambda i, m, j: (0, j)),
                  pl.BlockSpec((tf, d), lambda i, m, j: (j, 0)),
                  pl.BlockSpec((1, tm, d), lambda i, m, j: (i, m, 0), pipeline_mode=pl.Buffered(1)),
                  pl.BlockSpec((1, d), lambda i, m, j: (0, 0)),
                  _mod_block(N_MOD - N_MOD_EARLY, d, lambda i, m, j: i)],
        out_specs=pl.BlockSpec((1, tm, d), lambda i, m, j: (i, m, 0)),
        out_shape=jax.ShapeDtypeStruct((b, l, d), F32),
        compiler_params=_params(("parallel", "parallel", "arbitrary")),
        name="mlp",
    )(h2, w1, w2, x1, npost, mod_late)


def kernel(x, c, ctx, c_ctx, w_mod, b_mod, norm_pre_mix, norm_post_mix, norm_pre_mlp, norm_post_mlp, w_in,
           conv_dw_w, conv_dw_b, conv_ln_g, conv_ln_b, conv_pw_w, hgrn_lb_logits, hgrn_norm_g, hgrn_out_w,
           w_out, mlp_w1, mlp_w2):
    assert w_mod.shape[0] == 1, "single-layer block"
    b, l, d = x.shape
    lc = ctx.shape[1]
    dh = hgrn_norm_g.shape[1]
    dc = conv_dw_w.shape[2]
    assert l % (CONV_TILE_ROWS * GRID_W) == 0 and CONV_PAD <= CONV_TILE_ROWS and CONV_PAD <= _HPAD
    assert l % SCAN_BLOCK == 0 and lc % SCAN_CHUNK == 0

    cond = jnp.concatenate([c, c_ctx[None, :]], axis=0)
    cc = jnp.pad(cond[:, None, :], ((0, 0), (0, MOD_ROW_STRIDE - 1), (0, 0))).reshape(-1, d)
    mod_early = _mod_call(cc, w_mod[0], b_mod, N_MOD_EARLY)

    w_in0 = w_in[0]
    lb_f, lb_b = hgrn_lb_logits[0], hgrn_lb_logits[1]
    off = lambda n: n * dh

    pc = _ctx_proj_call(ctx, norm_pre_mix, mod_early, b, w_in0, 3 * dh)
    s_f, s_b = _ctx_state_call(pc, lb_f, lb_b, dh)

    h, v = _prenorm_v_call(x, norm_pre_mix, mod_early, w_in0, dh)
    h = h.reshape(b * l, d)
    seq = lambda t: t.reshape(b, l, t.shape[-1])
    lf, kk = _proj_call(h, w_in0, [off(1)], 2 * dh, _ep_forget, [F32, BF16],
                        extra=(jnp.concatenate([lb_f, lb_b], axis=1),), name="proj_forget")
    qg_scale = jnp.concatenate([jnp.full((1, dh), HEAD_DIM ** -0.5, F32), jnp.ones((1, dh), F32)], axis=1)
    qg, wpw, wh, wo, mod_late = _proj_call(h, w_in0, [off(3)], 2 * dh, _ep_scaled_silu, [BF16], extra=(qg_scale,),
                                           side=(conv_pw_w[0], hgrn_out_w[0], w_out[0]),
                                           late_mod=(cc, w_mod[0], b_mod), name="proj_qg")
    (u,) = _proj_call(h, w_in0, [off(5), off(5) + dc], dc, _ep_glu, [F32], tn=512, name="proj_glu")
    gates, w1, w2 = _proj_call(h, w_in0, [off(5) + 2 * dc], 2 * d, _ep_sigmoid, [BF16],
                               side=(mlp_w1[0], mlp_w2[0]), name="proj_gates")

    lf, kk, qg = seq(lf), seq(kk), seq(qg)
    o_b = _scan_call((kk, 1), (v, 0), (lf, 1), s_b, (qg, 0), reverse=True)
    og = _scan_call((kk, 0), (v, 0), (lf, 0), s_f, (qg, 0), reverse=False, ob=(o_b, 0), g=(qg, 1),
                    ng=hgrn_norm_g)

    uc = _conv_call(seq(u), conv_dw_w[0], conv_dw_b, conv_ln_g, conv_ln_b)

    x1, h2 = _mix_call(uc, og, seq(gates), x, wpw, wh, wo, norm_post_mix, norm_pre_mlp, mod_late)
    return _mlp_call(h2, w1, w2, x1, norm_post_mlp, mod_late)
```

```python
import functools

import jax
import jax.numpy as jnp
from jax import lax
from jax.experimental import pallas as pl
from jax.experimental.pallas import tpu as pltpu

F32 = jnp.float32
BF16 = jnp.bfloat16

GRID_W = 64
CONV_WIDTH = 31
CONV_PAD = (CONV_WIDTH - 1) // 2
HEAD_DIM = 128
N_MOD = 6
N_MOD_EARLY = 2
MOD_ROW_STRIDE = 16
EPS = 1e-6

SCAN_CHUNK = 64
SCAN_BLOCK = 512

VMEM_LIMIT = 56 * 1024 * 1024

_NT = (((1,), (1,)), ((), ()))
_TN = (((0,), (0,)), ((), ()))


def _params(sem):
    return pltpu.CompilerParams(dimension_semantics=sem, vmem_limit_bytes=VMEM_LIMIT)


def _sigmoid(x):
    return 0.5 * jnp.tanh(0.5 * x) + 0.5


def _silu(x):
    return x * _sigmoid(x)


def _norm_modulate(x, gain, shift, scale):
    ms = jnp.mean(x * x, axis=-1, keepdims=True)
    return (x * lax.rsqrt(ms + EPS) * gain) * (1.0 + scale) + shift


def _forget_gate(raw, logits):
    e = jnp.exp(logits - jnp.max(logits, axis=0, keepdims=True))
    lb = e[0:1] / jnp.sum(e, axis=0, keepdims=True)
    return lb + (1.0 - lb) * jax.nn.sigmoid(raw)


def _mod_rows(c_ref, w_ref, b_ref):
    s = _silu(c_ref[...]).astype(BF16)
    return jnp.dot(s, w_ref[...].astype(BF16), preferred_element_type=F32) + b_ref[...]


def _mod_kernel(c_ref, w_ref, b_ref, o_ref):
    o_ref[0] = _mod_rows(c_ref, w_ref, b_ref)


def _mod_call(cc, w_mod, b_mod, n_seg, tn=1024):
    rows, d = cc.shape
    return pl.pallas_call(
        _mod_kernel,
        grid=(n_seg * d // tn,),
        in_specs=[pl.BlockSpec((rows, d), lambda j: (0, 0)),
                  pl.BlockSpec((d, tn), lambda j: (0, j)),
                  pl.BlockSpec((1, tn), lambda j: (0, j))],
        out_specs=pl.BlockSpec((1, rows, tn), lambda j: (j // (d // tn), 0, j % (d // tn))),
        out_shape=jax.ShapeDtypeStruct((n_seg, rows, d), F32),
        compiler_params=_params(("parallel",)),
        name="mod",
    )(cc, w_mod, b_mod)


def _mod_block(n_seg, d, row_of):
    return pl.BlockSpec((n_seg, 8, d), lambda *idx: (0, row_of(*idx) * (MOD_ROW_STRIDE // 8), 0))


def _proj_kernel(*refs, n_w, n_extra, n_side, n_mod, n_out, epilogue):
    h_ref = refs[0]
    pos = 1
    w_refs = refs[pos:pos + n_w]
    pos += n_w
    extra = refs[pos:pos + n_extra]
    pos += n_extra
    side_in = refs[pos:pos + n_side]
    pos += n_side
    mod_in = refs[pos:pos + 3 * n_mod]
    pos += 3 * n_mod
    out_refs = refs[pos:pos + n_out]
    pos += n_out
    side_out = refs[pos:pos + n_side]
    pos += n_side
    mod_out = refs[pos:pos + n_mod]
    pos += n_mod
    wbf_refs = refs[pos:]

    @pl.when(pl.program_id(1) == 0)
    def _():
        for w_ref, wbf_ref in zip(w_refs, wbf_refs):
            wbf_ref[...] = w_ref[...].astype(BF16)

    for si_ref, so_ref in zip(side_in, side_out):
        so_ref[...] = si_ref[...].astype(BF16)
    if n_mod:
        mod_out[0][0] = _mod_rows(*mod_in)

    h = h_ref[...]
    accs = [jnp.dot(h, wbf_ref[...], preferred_element_type=F32) for wbf_ref in wbf_refs]
    outs = epilogue(accs, [e_ref[...] for e_ref in extra])
    for o_ref, o in zip(out_refs, outs):
        o_ref[...] = o.astype(o_ref.dtype)


def _ep_forget(accs, extra):
    f = _forget_gate(accs[0], extra[0])
    return jnp.log2(f), 1.0 - f


def _ep_scaled_silu(accs, extra):
    return (_silu(accs[0]) * extra[0],)


def _ep_glu(accs, extra):
    return (accs[0] * _sigmoid(accs[1]),)


def _ep_sigmoid(accs, extra):
    return (_sigmoid(accs[0]),)


def _proj_call(h, w, col_offsets, width, epilogue, out_dtypes, extra=(), side=(), late_mod=None, tm=1024,
               tn=1024, name="proj"):
    m, k = h.shape
    tm = min(tm, m)
    n_w = len(col_offsets)
    n_j, n_i = width // tn, m // tm
    mod_specs, mod_out_specs, mod_out_shape, mod_args = [], [], [], []
    if late_mod is not None:
        cc, w_mod, b_mod = late_mod
        rows, d = cc.shape
        n_late = w_mod.shape[1] // d - N_MOD_EARLY
        tnm = n_late * d // (n_j * n_i)
        per = d // tnm
        step = lambda j, i: j * n_i + i
        mod_specs = [pl.BlockSpec((rows, d), lambda j, i: (0, 0)),
                     pl.BlockSpec((d, tnm), lambda j, i: (0, N_MOD_EARLY * per + step(j, i))),
                     pl.BlockSpec((1, tnm), lambda j, i: (0, N_MOD_EARLY * per + step(j, i)))]
        mod_out_specs = [pl.BlockSpec((1, rows, tnm), lambda j, i: (step(j, i) // per, 0, step(j, i) % per))]
        mod_out_shape = [jax.ShapeDtypeStruct((n_late, rows, d), F32)]
        mod_args = [cc, w_mod, b_mod]
    in_specs = [pl.BlockSpec((tm, k), lambda j, i: (i, 0))]
    for off in col_offsets:
        in_specs.append(pl.BlockSpec((k, tn), lambda j, i, off=off: (0, off // tn + j)))
    for e in extra:
        in_specs.append(pl.BlockSpec((e.shape[0], tn), lambda j, i: (0, j)))
    side_specs = [pl.BlockSpec((a.shape[0] // (n_j * n_i), a.shape[1]), lambda j, i: (j * n_i + i, 0))
                  for a in side]
    return pl.pallas_call(
        functools.partial(_proj_kernel, n_w=n_w, n_extra=len(extra), n_side=len(side), n_mod=len(mod_out_specs),
                          n_out=len(out_dtypes), epilogue=epilogue),
        grid=(n_j, n_i),
        in_specs=in_specs + side_specs + mod_specs,
        out_specs=[pl.BlockSpec((tm, tn), lambda j, i: (i, j)) for _ in out_dtypes] + side_specs + mod_out_specs,
        out_shape=[jax.ShapeDtypeStruct((m, width), dt) for dt in out_dtypes]
                  + [jax.ShapeDtypeStruct(a.shape, BF16) for a in side] + mod_out_shape,
        scratch_shapes=[pltpu.VMEM((k, tn), BF16) for _ in range(n_w)],
        compiler_params=_params(("parallel", "arbitrary")),
        name=name,
    )(h, *([w] * n_w), *extra, *side, *mod_args)


def _prenorm_v_kernel(x_ref, g_ref, mod_ref, w_ref, h_ref, v_ref, wbf_ref):
    @pl.when((pl.program_id(0) == 0) & (pl.program_id(1) == 0))
    def _():
        wbf_ref[...] = w_ref[...].astype(BF16)

    h = _norm_modulate(x_ref[0], g_ref[...], mod_ref[0, 0:1, :], mod_ref[1, 0:1, :]).astype(BF16)
    h_ref[0] = h
    v_ref[0] = jnp.dot(h, wbf_ref[...], preferred_element_type=F32).astype(BF16)


def _prenorm_v_call(x, g, mod_early, w, width, tl=1024):
    b, l, d = x.shape
    return pl.pallas_call(
        _prenorm_v_kernel,
        grid=(b, l // tl),
        in_specs=[pl.BlockSpec((1, tl, d), lambda i, j: (i, j, 0)),
                  pl.BlockSpec((1, d), lambda i, j: (0, 0)),
                  _mod_block(N_MOD_EARLY, d, lambda i, j: i),
                  pl.BlockSpec((d, width), lambda i, j: (0, 0), pipeline_mode=pl.Buffered(1))],
        out_specs=[pl.BlockSpec((1, tl, d), lambda i, j: (i, j, 0)),
                   pl.BlockSpec((1, tl, width), lambda i, j: (i, j, 0))],
        out_shape=[jax.ShapeDtypeStruct((b, l, d), BF16), jax.ShapeDtypeStruct((b, l, width), BF16)],
        scratch_shapes=[pltpu.VMEM((d, width), BF16)],
        compiler_params=_params(("arbitrary", "arbitrary")),
        name="prenorm_v",
    )(x, g, mod_early, w)


def _scan_masks(c, reverse):
    t = lax.broadcasted_iota(jnp.int32, (c, c), 0)
    s = lax.broadcasted_iota(jnp.int32, (c, c), 1)
    half = c // 2
    before = (s >= t) if reverse else (s <= t)
    mask1 = ((t < half) == (s < half)) & before
    tri = jnp.where(before, 1.0, 0.0).astype(BF16)
    return mask1, jnp.concatenate([tri, tri], axis=1)


def _scan_cumsum(lf, tri2):
    hi = lf.astype(BF16)
    lo = (lf - hi.astype(F32)).astype(BF16)
    return jnp.dot(tri2, jnp.concatenate([hi, lo], axis=0), preferred_element_type=F32)


def _scan_refs(a, reverse):
    c = a.shape[0]
    half, quarter = c // 2, c // 4
    if reverse:
        return a[quarter:quarter + 1], a[half + quarter:half + quarter + 1], a[half:half + 1], a[0:1]
    return a[quarter - 1:quarter], a[half + quarter - 1:half + quarter], a[half - 1:half], a[c - 1:c]


def _scan_products(a, q, k, v, st, st_kv, reverse):
    c = a.shape[0]
    half = c // 2
    m1_lo, m1_hi, m0, a_tot = _scan_refs(a, reverse)
    lo_rows, hi_rows = slice(0, half), slice(half, c)
    row = lax.broadcasted_iota(jnp.int32, a.shape, 0)
    e1 = a - jnp.where(row < half, m1_lo, m1_hi)
    pq = q.astype(F32) * jnp.exp2(e1)
    pk = k.astype(F32) * jnp.exp2(-e1)
    s1 = lax.dot_general(pq.astype(BF16), pk.astype(BF16), _NT, preferred_element_type=F32)
    if reverse:
        q_rows, m1_q, k_rows, m1_k = lo_rows, m1_lo, hi_rows, m1_hi
    else:
        q_rows, m1_q, k_rows, m1_k = hi_rows, m1_hi, lo_rows, m1_lo
    q0 = (pq[q_rows] * jnp.exp2(m1_q - m0)).astype(BF16)
    k0 = (pk[k_rows] * jnp.exp2(m0 - m1_k)).astype(BF16)
    s0 = lax.dot_general(q0, k0, _NT, preferred_element_type=F32)
    qi = jnp.concatenate([pq[lo_rows] * jnp.exp2(m1_lo), pq[hi_rows] * jnp.exp2(m1_hi)], axis=0)
    inter = jnp.dot(qi.astype(BF16), st_kv, preferred_element_type=F32)
    kh = jnp.concatenate([pk[lo_rows] * jnp.exp2(a_tot - m1_lo), pk[hi_rows] * jnp.exp2(a_tot - m1_hi)], axis=0)
    upd = lax.dot_general(v, kh.astype(BF16), _TN, preferred_element_type=F32)
    return s1, s0, inter, st * jnp.exp2(a_tot) + upd


def _scan_output(s1, s0, inter, v, mask1, reverse):
    c = s1.shape[0]
    half = c // 2
    lo_rows, hi_rows = slice(0, half), slice(half, c)
    q_rows, k_rows = (lo_rows, hi_rows) if reverse else (hi_rows, lo_rows)
    intra = jnp.dot(jnp.where(mask1, s1, 0.0).astype(BF16), v, preferred_element_type=F32)
    cross = jnp.dot(s0.astype(BF16), v[k_rows], preferred_element_type=F32)
    o = inter + intra
    o_q = o[q_rows] + cross
    return jnp.concatenate([o_q, o[hi_rows]] if reverse else [o[lo_rows], o_q], axis=0)


def _scan_kernel(*refs, reverse, readout):
    k_ref, v_ref, lf_ref, s0_ref, q_ref = refs[:5]
    if readout:
        (ob_ref, g_ref, ng_ref, o_ref, st_ref, stkv_ref, a_ref) = refs[5:]
    else:
        (o_ref, st_ref, stkv_ref, a_ref) = refs[5:]
    c = SCAN_CHUNK
    n_batch, rows = k_ref.shape[0], k_ref.shape[1]
    n_chunks = rows // c
    n_heads = k_ref.shape[2] // HEAD_DIM
    recurrences = [(h, bi) for h in range(n_heads) for bi in range(n_batch)]

    @pl.when(pl.program_id(0) == 0)
    def _():
        st_ref[...] = s0_ref[...]
        for h, bi in recurrences:
            stkv_ref[bi, h] = s0_ref[bi, h].T.astype(BF16)

    mask1, tri2 = _scan_masks(c, reverse)

    def cumsum_body(ci, carry):
        r0 = pl.multiple_of(ci * c, c)
        for bi in range(n_batch):
            a_ref[bi, pl.ds(r0, c), :] = _scan_cumsum(lf_ref[bi, pl.ds(r0, c), :], tri2)
        return carry

    lax.fori_loop(0, n_chunks, cumsum_body, 0, unroll=4)

    def chunk_body(ci, carry):
        cidx = (n_chunks - 1 - ci) if reverse else ci
        r0 = pl.multiple_of(cidx * c, c)
        tile = lambda ref, h, bi: ref[bi, pl.ds(r0, c), h * HEAD_DIM:(h + 1) * HEAD_DIM]
        wave = []
        for h, bi in recurrences:
            s1, s0, inter, st_new = _scan_products(
                tile(a_ref, h, bi), tile(q_ref, h, bi), tile(k_ref, h, bi), tile(v_ref, h, bi),
                st_ref[bi, h], stkv_ref[bi, h], reverse)
            st_ref[bi, h] = st_new
            stkv_ref[bi, h] = st_new.T.astype(BF16)
            wave.append((s1, s0, inter))
        for (h, bi), (s1, s0, inter) in zip(recurrences, wave):
            lanes = slice(h * HEAD_DIM, (h + 1) * HEAD_DIM)
            o = _scan_output(s1, s0, inter, tile(v_ref, h, bi), mask1, reverse)
            if readout:
                o = o + tile(ob_ref, h, bi)
                ms = jnp.mean(o * o, axis=-1, keepdims=True)
                y = o * lax.rsqrt(ms + EPS) * ng_ref[:, lanes]
                o = y * tile(g_ref, h, bi).astype(F32)
            o_ref[bi, pl.ds(r0, c), lanes] = o.astype(o_ref.dtype)
        return carry

    lax.fori_loop(0, n_chunks, chunk_body, 0, unroll=4)


def _scan_call(k, v, lf, s0, q, *, reverse, ob=None, g=None, ng=None):
    b, l, _ = v[0].shape
    dh = ng.shape[1] if ng is not None else s0.shape[1] * HEAD_DIM
    tb = min(SCAN_BLOCK, l)
    n_blocks = l // tb
    readout = ob is not None
    blk = (lambda j: n_blocks - 1 - j) if reverse else (lambda j: j)
    seq_spec = lambda group: pl.BlockSpec((b, tb, dh), lambda j: (0, blk(j), group))
    st_spec = pl.BlockSpec(s0.shape, lambda j: (0, 0, 0, 0))
    in_specs = [seq_spec(k[1]), seq_spec(v[1]), seq_spec(lf[1]), st_spec, seq_spec(q[1])]
    args = [k[0], v[0], lf[0], s0, q[0]]
    if readout:
        in_specs += [seq_spec(ob[1]), seq_spec(g[1]), pl.BlockSpec((1, dh), lambda j: (0, 0))]
        args += [ob[0], g[0], ng]
    return pl.pallas_call(
        functools.partial(_scan_kernel, reverse=reverse, readout=readout),
        grid=(n_blocks,),
        in_specs=in_specs,
        out_specs=seq_spec(0),
        out_shape=jax.ShapeDtypeStruct((b, l, dh), BF16 if readout else F32),
        scratch_shapes=[pltpu.VMEM(s0.shape, F32), pltpu.VMEM(s0.shape, BF16), pltpu.VMEM((b, tb, dh), F32)],
        compiler_params=_params(("arbitrary",)),
        name="scan_readout_fwd" if readout else "scan_out_bwd",
    )(*args)


def _ctx_proj_kernel(ctx_ref, g_ref, mod_ref, w_ref, o_ref, hc_ref):
    @pl.when(pl.program_id(0) == 0)
    def _():
        for bi in range(ctx_ref.shape[0]):
            hc_ref[bi] = _norm_modulate(ctx_ref[bi], g_ref[...], mod_ref[0, 0:1, :],
                                        mod_ref[1, 0:1, :]).astype(BF16)

    w = w_ref[...].astype(BF16)
    for bi in range(ctx_ref.shape[0]):
        o_ref[bi] = jnp.dot(hc_ref[bi], w, preferred_element_type=F32)


def _ctx_proj_call(ctx, g, mod_early, mod_row, w, width, tn=1024):
    b, lc, d = ctx.shape
    return pl.pallas_call(
        _ctx_proj_kernel,
        grid=(width // tn,),
        in_specs=[pl.BlockSpec((b, lc, d), lambda j: (0, 0, 0)),
                  pl.BlockSpec((1, d), lambda j: (0, 0)),
                  _mod_block(N_MOD_EARLY, d, lambda j: mod_row),
                  pl.BlockSpec((d, tn), lambda j: (0, j))],
        out_specs=pl.BlockSpec((b, lc, tn), lambda j: (0, 0, j)),
        out_shape=jax.ShapeDtypeStruct((b, lc, width), F32),
        scratch_shapes=[pltpu.VMEM((b, lc, d), BF16)],
        compiler_params=_params(("arbitrary",)),
        name="ctx_proj",
    )(ctx, g, mod_early, w)


def _ctx_state_kernel(pv_ref, pf_ref, pb_ref, lgf_ref, lgb_ref, sf_ref, sb_ref):
    c = SCAN_CHUNK
    n_batch, rows = pv_ref.shape[0], pv_ref.shape[1]
    n_chunks = rows // c
    n_heads = pv_ref.shape[2] // HEAD_DIM
    for reverse, p_ref, lg_ref, s_ref in ((False, pf_ref, lgf_ref, sf_ref), (True, pb_ref, lgb_ref, sb_ref)):
        _, tri2 = _scan_masks(c, reverse)
        s_ref[...] = jnp.zeros(s_ref.shape, F32)

        def chunk_body(ci, carry, reverse=reverse, p_ref=p_ref, lg_ref=lg_ref, s_ref=s_ref, tri2=tri2):
            cidx = (n_chunks - 1 - ci) if reverse else ci
            r0 = pl.multiple_of(cidx * c, c)
            recurrences = [(h, bi) for h in range(n_heads) for bi in range(n_batch)]
            tile = lambda ref, h, bi: ref[bi, pl.ds(r0, c), h * HEAD_DIM:(h + 1) * HEAD_DIM]
            gates = [_forget_gate(tile(p_ref, h, bi), lg_ref[:, h * HEAD_DIM:(h + 1) * HEAD_DIM])
                     for h, bi in recurrences]
            sums = [_scan_cumsum(jnp.log2(f), tri2) for f in gates]
            for (h, bi), f, a in zip(recurrences, gates, sums):
                a_tot = _scan_refs(a, reverse)[3]
                kh = ((1.0 - f) * jnp.exp2(a_tot - a)).astype(BF16)
                upd = lax.dot_general(tile(pv_ref, h, bi).astype(BF16), kh, _TN, preferred_element_type=F32)
                s_ref[bi, h] = s_ref[bi, h] * jnp.exp2(a_tot) + upd
            return carry

        lax.fori_loop(0, n_chunks, chunk_body, 0)


def _ctx_state_call(pc, logits_f, logits_b, dh):
    b, lc, _ = pc.shape
    n_heads = dh // HEAD_DIM
    group = lambda n: pl.BlockSpec((b, lc, dh), lambda j: (0, 0, n))
    lg_spec = pl.BlockSpec(logits_f.shape, lambda j: (0, 0))
    st_spec = pl.BlockSpec((b, n_heads, HEAD_DIM, HEAD_DIM), lambda j: (0, 0, 0, 0))
    st_shape = jax.ShapeDtypeStruct((b, n_heads, HEAD_DIM, HEAD_DIM), F32)
    return pl.pallas_call(
        _ctx_state_kernel,
        grid=(1,),
        in_specs=[group(0), group(1), group(2), lg_spec, lg_spec],
        out_specs=[st_spec, st_spec],
        out_shape=[st_shape, st_shape],
        compiler_params=_params(("arbitrary",)),
        name="ctx_state",
    )(pc, pc, pc, logits_f, logits_b)


CONV_TILE_ROWS = 16
_HPAD = 16
_HROW = GRID_W + 2 * _HPAD


def _conv_kernel(cur_ref, prev_ref, next_ref, w_ref, b_ref, lg_ref, lb_ref, o_ref,
                 hpad_ref, vpad_ref, y_ref, *, n_tiles):
    tr = CONV_TILE_ROWS
    half = w_ref.shape[1] // 2
    n_lane_tiles = half // 128
    i = pl.program_id(1)

    hpad_ref[:, :, :_HPAD, :] = jnp.zeros((tr, n_lane_tiles, _HPAD, 128), F32)
    hpad_ref[:, :, _HPAD + GRID_W:, :] = jnp.zeros((tr, n_lane_tiles, _HPAD, 128), F32)
    for r in range(tr):
        for lt in range(n_lane_tiles):
            hpad_ref[r, lt, _HPAD:_HPAD + GRID_W, :] = cur_ref[0, r * GRID_W:(r + 1) * GRID_W,
                                                               lt * 128:(lt + 1) * 128]
    halo = CONV_PAD * GRID_W
    for lt in range(n_lane_tiles):
        lanes = slice(lt * 128, (lt + 1) * 128)
        top = prev_ref[0, tr * GRID_W - halo:, lanes]
        bot = next_ref[0, :halo, lanes]
        vpad_ref[lt, :halo, :] = jnp.where(i > 0, top, 0.0)
        vpad_ref[lt, halo:halo + tr * GRID_W, :] = cur_ref[0, :, half + lt * 128:half + (lt + 1) * 128]
        vpad_ref[lt, halo + tr * GRID_W:, :] = jnp.where(i < n_tiles - 1, bot, 0.0)

    def h_row(r, carry):
        o0 = pl.multiple_of(r * GRID_W, GRID_W)
        for lt in range(n_lane_tiles):
            lanes = slice(lt * 128, (lt + 1) * 128)
            acc = jnp.zeros((GRID_W, 128), F32)
            for jj in range(CONV_WIDTH):
                off = _HPAD - CONV_PAD + jj
                acc = acc + hpad_ref[r, lt, off:off + GRID_W, :] * w_ref[jj:jj + 1, lanes]
            y_ref[pl.ds(o0, GRID_W), lanes] = acc + b_ref[:, lanes]
        return carry

    lax.fori_loop(0, tr, h_row, 0)

    n_sub = GRID_W // 8
    for lt in range(n_lane_tiles):
        vl = slice(half + lt * 128, half + (lt + 1) * 128)
        wv = [jnp.broadcast_to(w_ref[jj:jj + 1, vl], (8, 128)) for jj in range(CONV_WIDTH)]
        bias = b_ref[:, vl]

        def v_row(r, carry, lt=lt, vl=vl, wv=wv, bias=bias):
            v0 = pl.multiple_of(r * GRID_W, GRID_W)
            acc = [jnp.zeros((8, 128), F32)] * n_sub
            for jj in range(CONV_WIDTH):
                src = pl.multiple_of(v0 + jj * GRID_W, GRID_W)
                tap = vpad_ref[lt, pl.ds(src, GRID_W), :]
                acc = [acc[s] + tap[8 * s:8 * s + 8, :] * wv[jj] for s in range(n_sub)]
            y_ref[pl.ds(v0, GRID_W), vl] = jnp.concatenate(acc, axis=0) + bias
            return carry

        lax.fori_loop(0, tr, v_row, 0, unroll=2)

    def ln_row(r, carry):
        o0 = pl.multiple_of(r * GRID_W, GRID_W)
        y = y_ref[pl.ds(o0, GRID_W), :]
        mu = jnp.mean(y, axis=-1, keepdims=True)
        yc = y - mu
        var = jnp.mean(yc * yc, axis=-1, keepdims=True)
        z = yc * lax.rsqrt(var + EPS) * lg_ref[...] + lb_ref[...]
        o_ref[0, pl.ds(o0, GRID_W), :] = _silu(z).astype(BF16)
        return carry

    lax.fori_loop(0, tr, ln_row, 0, unroll=4)


def _conv_call(u, w, bias, ln_g, ln_b):
    b, l, ch = u.shape
    half = ch // 2
    tile = CONV_TILE_ROWS * GRID_W
    n_tiles = l // tile
    return pl.pallas_call(
        functools.partial(_conv_kernel, n_tiles=n_tiles),
        grid=(b, n_tiles),
        in_specs=[pl.BlockSpec((1, tile, ch), lambda i, j: (i, j, 0)),
                  pl.BlockSpec((1, tile, half), lambda i, j: (i, jnp.maximum(j - 1, 0), 1)),
                  pl.BlockSpec((1, tile, half), lambda i, j: (i, jnp.minimum(j + 1, n_tiles - 1), 1)),
                  pl.BlockSpec((CONV_WIDTH, ch), lambda i, j: (0, 0)),
                  pl.BlockSpec((1, ch), lambda i, j: (0, 0)),
                  pl.BlockSpec((1, ch), lambda i, j: (0, 0)),
                  pl.BlockSpec((1, ch), lambda i, j: (0, 0))],
        out_specs=pl.BlockSpec((1, tile, ch), lambda i, j: (i, j, 0)),
        out_shape=jax.ShapeDtypeStruct((b, l, ch), BF16),
        scratch_shapes=[pltpu.VMEM((CONV_TILE_ROWS, half // 128, _HROW, 128), F32),
                        pltpu.VMEM((half // 128, tile + 2 * CONV_PAD * GRID_W, 128), F32),
                        pltpu.VMEM((tile, ch), F32)],
        compiler_params=_params(("parallel", "parallel")),
        name="conv",
    )(u, u, u, w, bias, ln_g, ln_b)


MIX_SLICE = 128

def _mix_kernel(uc_ref, og_ref, gc_ref, gh_ref, x_ref, wpw_ref, wh_ref, wo_ref,
                npost_ref, npre_ref, mod_ref, x1_ref, h2_ref):
    tm = x_ref.shape[1]
    n_split = tm // MIX_SLICE
    for r in range(n_split):
        rows = slice(r * MIX_SLICE, (r + 1) * MIX_SLICE)
        y_c = jnp.dot(uc_ref[0, rows, :], wpw_ref[...], preferred_element_type=F32)
        y_h = jnp.dot(og_ref[0, rows, :], wh_ref[...], preferred_element_type=F32)
        z = gc_ref[0, rows, :].astype(F32) * y_c + gh_ref[0, rows, :].astype(F32) * y_h
        y = jnp.dot(z.astype(BF16), wo_ref[...], preferred_element_type=F32)
        ms = jnp.mean(y * y, axis=-1, keepdims=True)
        yn = y * lax.rsqrt(ms + EPS) * npost_ref[...]
        x1 = x_ref[0, rows, :] + mod_ref[0, 0:1, :] * yn
        x1_ref[0, rows, :] = x1
        h2_ref[0, rows, :] = _norm_modulate(x1, npre_ref[...], mod_ref[1, 0:1, :],
                                            mod_ref[2, 0:1, :]).astype(BF16)


def _mix_call(uc, og, gates, x, wpw, wh, wo, npost, npre, mod_late, tm=512):
    b, l, d = x.shape
    dc = uc.shape[2]
    gc = gh = gates
    row = lambda width, col=0: pl.BlockSpec((1, tm, width), lambda i, j: (i, j, col))
    const = lambda shape: pl.BlockSpec(shape, lambda i, j: (0,) * len(shape), pipeline_mode=pl.Buffered(1))
    return pl.pallas_call(
        _mix_kernel,
        grid=(b, l // tm),
        in_specs=[row(dc), row(dc), row(d, 0), row(d, 1), row(d),
                  const(wpw.shape), const(wh.shape), const(wo.shape),
                  const((1, d)), const((1, d)),
                  _mod_block(N_MOD - N_MOD_EARLY, d, lambda i, j: i)],
        out_specs=[row(d), row(d)],
        out_shape=[jax.ShapeDtypeStruct((b, l, d), F32), jax.ShapeDtypeStruct((b, l, d), BF16)],
        compiler_params=_params(("parallel", "parallel")),
        name="mix",
    )(uc, og, gc, gh, x, wpw, wh, wo, npost, npre, mod_late)


MLP_SLICE = 256

def _mlp_kernel(h_ref, w1_ref, w2_ref, x1_ref, npost_ref, mod_ref, o_ref, *, n_ff):
    j = pl.program_id(2)

    def partial_sum(rows):
        a = jnp.dot(h_ref[0, rows, :], w1_ref[...], preferred_element_type=F32)
        a = jnp.square(jnp.maximum(a, 0.0)).astype(BF16)
        return jnp.dot(a, w2_ref[...], preferred_element_type=F32)

    @pl.when(j == 0)
    def _():
        o_ref[...] = jnp.zeros(o_ref.shape, F32)

    @pl.when(j < n_ff - 1)
    def _():
        o_ref[0] += partial_sum(slice(None))

    @pl.when(j == n_ff - 1)
    def _():
        for r in range(o_ref.shape[1] // MLP_SLICE):
            rows = slice(r * MLP_SLICE, (r + 1) * MLP_SLICE)
            y = o_ref[0, rows, :] + partial_sum(rows)
            ms = jnp.mean(y * y, axis=-1, keepdims=True)
            yn = y * lax.rsqrt(ms + EPS) * npost_ref[...]
            o_ref[0, rows, :] = x1_ref[0, rows, :] + mod_ref[3, 0:1, :] * yn


def _mlp_call(h2, w1, w2, x1, npost, mod_late, tm=1024, tf=1024):
    b, l, d = x1.shape
    dff = w1.shape[1]
    n_ff = dff // tf
    return pl.pallas_call(
        functools.partial(_mlp_kernel, n_ff=n_ff),
        grid=(b, l // tm, n_ff),
        in_specs=[pl.BlockSpec((1, tm, d), lambda i, m, j: (i, m, 0)),
                  pl.BlockSpec((d, tf), lambda i, m, j: (0, j)),
                  pl.BlockSpec((tf, d), lambda i, m, j: (j, 0)),
                  pl.BlockSpec((1, tm, d), lambda i, m, j: (i, m, 0), pipeline_mode=pl.Buffered(1)),
                  pl.BlockSpec((1, d), lambda i, m, j: (0, 0)),
                  _mod_block(N_MOD - N_MOD_EARLY, d, lambda i, m, j: i)],
        out_specs=pl.BlockSpec((1, tm, d), lambda i, m, j: (i, m, 0)),
        out_shape=jax.ShapeDtypeStruct((b, l, d), F32),
        compiler_params=_params(("parallel", "parallel", "arbitrary")),
        name="mlp",
    )(h2, w1, w2, x1, npost, mod_late)


def kernel(x, c, ctx, c_ctx, w_mod, b_mod, norm_pre_mix, norm_post_mix, norm_pre_mlp, norm_post_mlp, w_in,
           conv_dw_w, conv_dw_b, conv_ln_g, conv_ln_b, conv_pw_w, hgrn_lb_logits, hgrn_norm_g, hgrn_out_w,
           w_out, mlp_w1, mlp_w2):
    assert w_mod.shape[0] == 1, "single-layer block"
    b, l, d = x.shape
    lc = ctx.shape[1]
    dh = hgrn_norm_g.shape[1]
    dc = conv_dw_w.shape[2]
    assert l % (CONV_TILE_ROWS * GRID_W) == 0 and CONV_PAD <= CONV_TILE_ROWS and CONV_PAD <= _HPAD
    assert l % SCAN_BLOCK == 0 and lc % SCAN_CHUNK == 0

    cond = jnp.concatenate([c, c_ctx[None, :]], axis=0)
    cc = jnp.pad(cond[:, None, :], ((0, 0), (0, MOD_ROW_STRIDE - 1), (0, 0))).reshape(-1, d)
    mod_early = _mod_call(cc, w_mod[0], b_mod, N_MOD_EARLY)

    w_in0 = w_in[0]
    lb_f, lb_b = hgrn_lb_logits[0], hgrn_lb_logits[1]
    off = lambda n: n * dh

    pc = _ctx_proj_call(ctx, norm_pre_mix, mod_early, b, w_in0, 3 * dh)
    s_f, s_b = _ctx_state_call(pc, lb_f, lb_b, dh)

    h, v = _prenorm_v_call(x, norm_pre_mix, mod_early, w_in0, dh)
    h = h.reshape(b * l, d)
    seq = lambda t: t.reshape(b, l, t.shape[-1])
    lf, kk = _proj_call(h, w_in0, [off(1)], 2 * dh, _ep_forget, [F32, BF16],
                        extra=(jnp.concatenate([lb_f, lb_b], axis=1),), name="proj_forget")
    qg_scale = jnp.concatenate([jnp.full((1, dh), HEAD_DIM ** -0.5, F32), jnp.ones((1, dh), F32)], axis=1)
    qg, wpw, wh, wo, mod_late = _proj_call(h, w_in0, [off(3)], 2 * dh, _ep_scaled_silu, [BF16], extra=(qg_scale,),
                                           side=(conv_pw_w[0], hgrn_out_w[0], w_out[0]),
                                           late_mod=(cc, w_mod[0], b_mod), name="proj_qg")
    (u,) = _proj_call(h, w_in0, [off(5), off(5) + dc], dc, _ep_glu, [F32], tn=512, name="proj_glu")
    gates, w1, w2 = _proj_call(h, w_in0, [off(5) + 2 * dc], 2 * d, _ep_sigmoid, [BF16],
                               side=(mlp_w1[0], mlp_w2[0]), name="proj_gates")

    lf, kk, qg = seq(lf), seq(kk), seq(qg)
    o_b = _scan_call((kk, 1), (v, 0), (lf, 1), s_b, (qg, 0), reverse=True)
    og = _scan_call((kk, 0), (v, 0), (lf, 0), s_f, (qg, 0), reverse=False, ob=(o_b, 0), g=(qg, 1),
                    ng=hgrn_norm_g)

    uc = _conv_call(seq(u), conv_dw_w[0], conv_dw_b, conv_ln_g, conv_ln_b)

    x1, h2 = _mix_call(uc, og, seq(gates), x, wpw, wh, wo, norm_post_mix, norm_pre_mlp, mod_late)
    return _mlp_call(h2, w1, w2, x1, norm_post_mlp, mod_late)
```

```python
import functools

import jax
import jax.numpy as jnp
from jax import lax
from jax.experimental import pallas as pl
from jax.experimental.pallas import tpu as pltpu

F32 = jnp.float32
BF16 = jnp.bfloat16

GRID_W = 64
CONV_WIDTH = 31
CONV_PAD = (CONV_WIDTH - 1) // 2
HEAD_DIM = 128
N_MOD = 6
N_MOD_EARLY = 2
MOD_ROW_STRIDE = 16
EPS = 1e-6

SCAN_CHUNK = 64
SCAN_BLOCK = 512

VMEM_LIMIT = 56 * 1024 * 1024

_NT = (((1,), (1,)), ((), ()))
_TN = (((0,), (0,)), ((), ()))


def _params(sem):
    return pltpu.CompilerParams(dimension_semantics=sem, vmem_limit_bytes=VMEM_LIMIT)


def _sigmoid(x):
    return 0.5 * jnp.tanh(0.5 * x) + 0.5


def _silu(x):
    return x * _sigmoid(x)


def _norm_modulate(x, gain, shift, scale):
    ms = jnp.mean(x * x, axis=-1, keepdims=True)
    return (x * lax.rsqrt(ms + EPS) * gain) * (1.0 + scale) + shift


def _forget_gate(raw, logits):
    e = jnp.exp(logits - jnp.max(logits, axis=0, keepdims=True))
    lb = e[0:1] / jnp.sum(e, axis=0, keepdims=True)
    return lb + (1.0 - lb) * jax.nn.sigmoid(raw)


def _mod_rows(c_ref, w_ref, b_ref):
    s = _silu(c_ref[...]).astype(BF16)
    return jnp.dot(s, w_ref[...].astype(BF16), preferred_element_type=F32) + b_ref[...]


def _mod_kernel(c_ref, w_ref, b_ref, o_ref):
    o_ref[0] = _mod_rows(c_ref, w_ref, b_ref)


def _mod_call(cc, w_mod, b_mod, n_seg, tn=1024):
    rows, d = cc.shape
    return pl.pallas_call(
        _mod_kernel,
        grid=(n_seg * d // tn,),
        in_specs=[pl.BlockSpec((rows, d), lambda j: (0, 0)),
                  pl.BlockSpec((d, tn), lambda j: (0, j)),
                  pl.BlockSpec((1, tn), lambda j: (0, j))],
        out_specs=pl.BlockSpec((1, rows, tn), lambda j: (j // (d // tn), 0, j % (d // tn))),
        out_shape=jax.ShapeDtypeStruct((n_seg, rows, d), F32),
        compiler_params=_params(("parallel",)),
        name="mod",
    )(cc, w_mod, b_mod)


def _mod_block(n_seg, d, row_of):
    return pl.BlockSpec((n_seg, 8, d), lambda *idx: (0, row_of(*idx) * (MOD_ROW_STRIDE // 8), 0))


def _proj_kernel(*refs, n_w, n_extra, n_side, n_mod, n_out, epilogue):
    h_ref = refs[0]
    pos = 1
    w_refs = refs[pos:pos + n_w]
    pos += n_w
    extra = refs[pos:pos + n_extra]
    pos += n_extra
    side_in = refs[pos:pos + n_side]
    pos += n_side
    mod_in = refs[pos:pos + 3 * n_mod]
    pos += 3 * n_mod
    out_refs = refs[pos:pos + n_out]
    pos += n_out
    side_out = refs[pos:pos + n_side]
    pos += n_side
    mod_out = refs[pos:pos + n_mod]
    pos += n_mod
    wbf_refs = refs[pos:]

    @pl.when(pl.program_id(1) == 0)
    def _():
        for w_ref, wbf_ref in zip(w_refs, wbf_refs):
            wbf_ref[...] = w_ref[...].astype(BF16)

    for si_ref, so_ref in zip(side_in, side_out):
        so_ref[...] = si_ref[...].astype(BF16)
    if n_mod:
        mod_out[0][0] = _mod_rows(*mod_in)

    h = h_ref[...]
    accs = [jnp.dot(h, wbf_ref[...], preferred_element_type=F32) for wbf_ref in wbf_refs]
    outs = epilogue(accs, [e_ref[...] for e_ref in extra])
    for o_ref, o in zip(out_refs, outs):
        o_ref[...] = o.astype(o_ref.dtype)


def _ep_forget(accs, extra):
    f = _forget_gate(accs[0], extra[0])
    return jnp.log2(f), 1.0 - f


def _ep_scaled_silu(accs, extra):
    return (_silu(accs[0]) * extra[0],)


def _ep_glu(accs, extra):
    return (accs[0] * _sigmoid(accs[1]),)


def _ep_sigmoid(accs, extra):
    return (_sigmoid(accs[0]),)


def _proj_call(h, w, col_offsets, width, epilogue, out_dtypes, extra=(), side=(), late_mod=None, tm=1024,
               tn=1024, name="proj"):
    m, k = h.shape
    tm = min(tm, m)
    n_w = len(col_offsets)
    n_j, n_i = width // tn, m // tm
    mod_specs, mod_out_specs, mod_out_shape, mod_args = [], [], [], []
    if late_mod is not None:
        cc, w_mod, b_mod = late_mod
        rows, d = cc.shape
        n_late = w_mod.shape[1] // d - N_MOD_EARLY
        tnm = n_late * d // (n_j * n_i)
        per = d // tnm
        step = lambda j, i: j * n_i + i
        mod_specs = [pl.BlockSpec((rows, d), lambda j, i: (0, 0)),
                     pl.BlockSpec((d, tnm), lambda j, i: (0, N_MOD_EARLY * per + step(j, i))),
                     pl.BlockSpec((1, tnm), lambda j, i: (0, N_MOD_EARLY * per + step(j, i)))]
        mod_out_specs = [pl.BlockSpec((1, rows, tnm), lambda j, i: (step(j, i) // per, 0, step(j, i) % per))]
        mod_out_shape = [jax.ShapeDtypeStruct((n_late, rows, d), F32)]
        mod_args = [cc, w_mod, b_mod]
    in_specs = [pl.BlockSpec((tm, k), lambda j, i: (i, 0))]
    for off in col_offsets:
        in_specs.append(pl.BlockSpec((k, tn), lambda j, i, off=off: (0, off // tn + j)))
    for e in extra:
        in_specs.append(pl.BlockSpec((e.shape[0], tn), lambda j, i: (0, j)))
    side_specs = [pl.BlockSpec((a.shape[0] // (n_j * n_i), a.shape[1]), lambda j, i: (j * n_i + i, 0))
                  for a in side]
    return pl.pallas_call(
        functools.partial(_proj_kernel, n_w=n_w, n_extra=len(extra), n_side=len(side), n_mod=len(mod_out_specs),
                          n_out=len(out_dtypes), epilogue=epilogue),
        grid=(n_j, n_i),
        in_specs=in_specs + side_specs + mod_specs,
        out_specs=[pl.BlockSpec((tm, tn), lambda j, i: (i, j)) for _ in out_dtypes] + side_specs + mod_out_specs,
        out_shape=[jax.ShapeDtypeStruct((m, width), dt) for dt in out_dtypes]
                  + [jax.ShapeDtypeStruct(a.shape, BF16) for a in side] + mod_out_shape,
        scratch_shapes=[pltpu.VMEM((k, tn), BF16) for _ in range(n_w)],
        compiler_params=_params(("parallel", "arbitrary")),
        name=name,
    )(h, *([w] * n_w), *extra, *side, *mod_args)


def _prenorm_v_kernel(x_ref, g_ref, mod_ref, w_ref, h_ref, v_ref, wbf_ref):
    @pl.when((pl.program_id(0) == 0) & (pl.program_id(1) == 0))
    def _():
        wbf_ref[...] = w_ref[...].astype(BF16)

    for r in range(x_ref.shape[1] // PRENORM_SLICE):
        rows = slice(r * PRENORM_SLICE, (r + 1) * PRENORM_SLICE)
        h = _norm_modulate(x_ref[0, rows, :], g_ref[...], mod_ref[0, 0:1, :], mod_ref[1, 0:1, :]).astype(BF16)
        h_ref[0, rows, :] = h
        v_ref[0, rows, :] = jnp.dot(h, wbf_ref[...], preferred_element_type=F32).astype(BF16)


PRENORM_SLICE = 256


def _prenorm_v_call(x, g, mod_early, w, width, tl=1024):
    b, l, d = x.shape
    return pl.pallas_call(
        _prenorm_v_kernel,
        grid=(b, l // tl),
        in_specs=[pl.BlockSpec((1, tl, d), lambda i, j: (i, j, 0)),
                  pl.BlockSpec((1, d), lambda i, j: (0, 0)),
                  _mod_block(N_MOD_EARLY, d, lambda i, j: i),
                  pl.BlockSpec((d, width), lambda i, j: (0, 0), pipeline_mode=pl.Buffered(1))],
        out_specs=[pl.BlockSpec((1, tl, d), lambda i, j: (i, j, 0)),
                   pl.BlockSpec((1, tl, width), lambda i, j: (i, j, 0))],
        out_shape=[jax.ShapeDtypeStruct((b, l, d), BF16), jax.ShapeDtypeStruct((b, l, width), BF16)],
        scratch_shapes=[pltpu.VMEM((d, width), BF16)],
        compiler_params=_params(("arbitrary", "arbitrary")),
        name="prenorm_v",
    )(x, g, mod_early, w)


def _scan_masks(c, reverse):
    t = lax.broadcasted_iota(jnp.int32, (c, c), 0)
    s = lax.broadcasted_iota(jnp.int32, (c, c), 1)
    half = c // 2
    before = (s >= t) if reverse else (s <= t)
    mask1 = ((t < half) == (s < half)) & before
    tri = jnp.where(before, 1.0, 0.0).astype(BF16)
    return mask1, jnp.concatenate([tri, tri], axis=1)


def _scan_cumsum(lf, tri2):
    hi = lf.astype(BF16)
    lo = (lf - hi.astype(F32)).astype(BF16)
    return jnp.dot(tri2, jnp.concatenate([hi, lo], axis=0), preferred_element_type=F32)


def _scan_refs(a, reverse):
    c = a.shape[0]
    half, quarter = c // 2, c // 4
    if reverse:
        return a[quarter:quarter + 1], a[half + quarter:half + quarter + 1], a[half:half + 1], a[0:1]
    return a[quarter - 1:quarter], a[half + quarter - 1:half + quarter], a[half - 1:half], a[c - 1:c]


def _scan_products(a, q, k, v, st, st_kv, reverse):
    c = a.shape[0]
    half = c // 2
    m1_lo, m1_hi, m0, a_tot = _scan_refs(a, reverse)
    lo_rows, hi_rows = slice(0, half), slice(half, c)
    row = lax.broadcasted_iota(jnp.int32, a.shape, 0)
    e1 = a - jnp.where(row < half, m1_lo, m1_hi)
    pq = q.astype(F32) * jnp.exp2(e1)
    pk = k.astype(F32) * jnp.exp2(-e1)
    s1 = lax.dot_general(pq.astype(BF16), pk.astype(BF16), _NT, preferred_element_type=F32)
    if reverse:
        q_rows, m1_q, k_rows, m1_k = lo_rows, m1_lo, hi_rows, m1_hi
    else:
        q_rows, m1_q, k_rows, m1_k = hi_rows, m1_hi, lo_rows, m1_lo
    q0 = (pq[q_rows] * jnp.exp2(m1_q - m0)).astype(BF16)
    k0 = (pk[k_rows] * jnp.exp2(m0 - m1_k)).astype(BF16)
    s0 = lax.dot_general(q0, k0, _NT, preferred_element_type=F32)
    qi = jnp.concatenate([pq[lo_rows] * jnp.exp2(m1_lo), pq[hi_rows] * jnp.exp2(m1_hi)], axis=0)
    inter = jnp.dot(qi.astype(BF16), st_kv, preferred_element_type=F32)
    kh = jnp.concatenate([pk[lo_rows] * jnp.exp2(a_tot - m1_lo), pk[hi_rows] * jnp.exp2(a_tot - m1_hi)], axis=0)
    upd = lax.dot_general(v, kh.astype(BF16), _TN, preferred_element_type=F32)
    return s1, s0, inter, st * jnp.exp2(a_tot) + upd


def _scan_output(s1, s0, inter, v, mask1, reverse):
    c = s1.shape[0]
    half = c // 2
    lo_rows, hi_rows = slice(0, half), slice(half, c)
    q_rows, k_rows = (lo_rows, hi_rows) if reverse else (hi_rows, lo_rows)
    intra = jnp.dot(jnp.where(mask1, s1, 0.0).astype(BF16), v, preferred_element_type=F32)
    cross = jnp.dot(s0.astype(BF16), v[k_rows], preferred_element_type=F32)
    o = inter + intra
    o_q = o[q_rows] + cross
    return jnp.concatenate([o_q, o[hi_rows]] if reverse else [o[lo_rows], o_q], axis=0)


def _scan_kernel(*refs, reverse, readout):
    k_ref, v_ref, lf_ref, s0_ref, q_ref = refs[:5]
    if readout:
        (ob_ref, g_ref, ng_ref, o_ref, st_ref, stkv_ref, a_ref) = refs[5:]
    else:
        (o_ref, st_ref, stkv_ref, a_ref) = refs[5:]
    c = SCAN_CHUNK
    n_batch, rows = k_ref.shape[0], k_ref.shape[1]
    n_chunks = rows // c
    n_heads = k_ref.shape[2] // HEAD_DIM
    recurrences = [(h, bi) for h in range(n_heads) for bi in range(n_batch)]

    @pl.when(pl.program_id(0) == 0)
    def _():
        st_ref[...] = s0_ref[...]
        for h, bi in recurrences:
            stkv_ref[bi, h] = s0_ref[bi, h].T.astype(BF16)

    mask1, tri2 = _scan_masks(c, reverse)

    def cumsum_body(ci, carry):
        r0 = pl.multiple_of(ci * c, c)
        for bi in range(n_batch):
            a_ref[bi, pl.ds(r0, c), :] = _scan_cumsum(lf_ref[bi, pl.ds(r0, c), :], tri2)
        return carry

    lax.fori_loop(0, n_chunks, cumsum_body, 0, unroll=4)

    def chunk_body(ci, carry):
        cidx = (n_chunks - 1 - ci) if reverse else ci
        r0 = pl.multiple_of(cidx * c, c)
        tile = lambda ref, h, bi: ref[bi, pl.ds(r0, c), h * HEAD_DIM:(h + 1) * HEAD_DIM]
        wave = []
        for h, bi in recurrences:
            s1, s0, inter, st_new = _scan_products(
                tile(a_ref, h, bi), tile(q_ref, h, bi), tile(k_ref, h, bi), tile(v_ref, h, bi),
                st_ref[bi, h], stkv_ref[bi, h], reverse)
            st_ref[bi, h] = st_new
            stkv_ref[bi, h] = st_new.T.astype(BF16)
            wave.append((s1, s0, inter))
        for (h, bi), (s1, s0, inter) in zip(recurrences, wave):
            lanes = slice(h * HEAD_DIM, (h + 1) * HEAD_DIM)
            o = _scan_output(s1, s0, inter, tile(v_ref, h, bi), mask1, reverse)
            if readout:
                o = o + tile(ob_ref, h, bi)
                ms = jnp.mean(o * o, axis=-1, keepdims=True)
                y = o * lax.rsqrt(ms + EPS) * ng_ref[:, lanes]
                o = y * tile(g_ref, h, bi).astype(F32)
            o_ref[bi, pl.ds(r0, c), lanes] = o.astype(o_ref.dtype)
        return carry

    lax.fori_loop(0, n_chunks, chunk_body, 0, unroll=4)


def _scan_call(k, v, lf, s0, q, *, reverse, ob=None, g=None, ng=None):
    b, l, _ = v[0].shape
    dh = ng.shape[1] if ng is not None else s0.shape[1] * HEAD_DIM
    tb = min(SCAN_BLOCK, l)
    n_blocks = l // tb
    readout = ob is not None
    blk = (lambda j: n_blocks - 1 - j) if reverse else (lambda j: j)
    seq_spec = lambda group: pl.BlockSpec((b, tb, dh), lambda j: (0, blk(j), group))
    st_spec = pl.BlockSpec(s0.shape, lambda j: (0, 0, 0, 0))
    in_specs = [seq_spec(k[1]), seq_spec(v[1]), seq_spec(lf[1]), st_spec, seq_spec(q[1])]
    args = [k[0], v[0], lf[0], s0, q[0]]
    if readout:
        in_specs += [seq_spec(ob[1]), seq_spec(g[1]), pl.BlockSpec((1, dh), lambda j: (0, 0))]
        args += [ob[0], g[0], ng]
    return pl.pallas_call(
        functools.partial(_scan_kernel, reverse=reverse, readout=readout),
        grid=(n_blocks,),
        in_specs=in_specs,
        out_specs=seq_spec(0),
        out_shape=jax.ShapeDtypeStruct((b, l, dh), BF16 if readout else F32),
        scratch_shapes=[pltpu.VMEM(s0.shape, F32), pltpu.VMEM(s0.shape, BF16), pltpu.VMEM((b, tb, dh), F32)],
        compiler_params=_params(("arbitrary",)),
        name="scan_readout_fwd" if readout else "scan_out_bwd",
    )(*args)


def _ctx_proj_kernel(ctx_ref, g_ref, mod_ref, w_ref, o_ref, hc_ref):
    @pl.when(pl.program_id(0) == 0)
    def _():
        for bi in range(ctx_ref.shape[0]):
            hc_ref[bi] = _norm_modulate(ctx_ref[bi], g_ref[...], mod_ref[0, 0:1, :],
                                        mod_ref[1, 0:1, :]).astype(BF16)

    w = w_ref[...].astype(BF16)
    for bi in range(ctx_ref.shape[0]):
        o_ref[bi] = jnp.dot(hc_ref[bi], w, preferred_element_type=F32)


def _ctx_proj_call(ctx, g, mod_early, mod_row, w, width, tn=1024):
    b, lc, d = ctx.shape
    return pl.pallas_call(
        _ctx_proj_kernel,
        grid=(width // tn,),
        in_specs=[pl.BlockSpec((b, lc, d), lambda j: (0, 0, 0)),
                  pl.BlockSpec((1, d), lambda j: (0, 0)),
                  _mod_block(N_MOD_EARLY, d, lambda j: mod_row),
                  pl.BlockSpec((d, tn), lambda j: (0, j))],
        out_specs=pl.BlockSpec((b, lc, tn), lambda j: (0, 0, j)),
        out_shape=jax.ShapeDtypeStruct((b, lc, width), F32),
        scratch_shapes=[pltpu.VMEM((b, lc, d), BF16)],
        compiler_params=_params(("arbitrary",)),
        name="ctx_proj",
    )(ctx, g, mod_early, w)


def _ctx_state_kernel(pv_ref, pf_ref, pb_ref, lgf_ref, lgb_ref, sf_ref, sb_ref):
    c = SCAN_CHUNK
    n_batch, rows = pv_ref.shape[0], pv_ref.shape[1]
    n_chunks = rows // c
    n_heads = pv_ref.shape[2] // HEAD_DIM
    for reverse, p_ref, lg_ref, s_ref in ((False, pf_ref, lgf_ref, sf_ref), (True, pb_ref, lgb_ref, sb_ref)):
        _, tri2 = _scan_masks(c, reverse)
        s_ref[...] = jnp.zeros(s_ref.shape, F32)

        def chunk_body(ci, carry, reverse=reverse, p_ref=p_ref, lg_ref=lg_ref, s_ref=s_ref, tri2=tri2):
            cidx = (n_chunks - 1 - ci) if reverse else ci
            r0 = pl.multiple_of(cidx * c, c)
            recurrences = [(h, bi) for h in range(n_heads) for bi in range(n_batch)]
            tile = lambda ref, h, bi: ref[bi, pl.ds(r0, c), h * HEAD_DIM:(h + 1) * HEAD_DIM]
            gates = [_forget_gate(tile(p_ref, h, bi), lg_ref[:, h * HEAD_DIM:(h + 1) * HEAD_DIM])
                     for h, bi in recurrences]
            sums = [_scan_cumsum(jnp.log2(f), tri2) for f in gates]
            for (h, bi), f, a in zip(recurrences, gates, sums):
                a_tot = _scan_refs(a, reverse)[3]
                kh = ((1.0 - f) * jnp.exp2(a_tot - a)).astype(BF16)
                upd = lax.dot_general(tile(pv_ref, h, bi).astype(BF16), kh, _TN, preferred_element_type=F32)
                s_ref[bi, h] = s_ref[bi, h] * jnp.exp2(a_tot) + upd
            return carry

        lax.fori_loop(0, n_chunks, chunk_body, 0)


def _ctx_state_call(pc, logits_f, logits_b, dh):
    b, lc, _ = pc.shape
    n_heads = dh // HEAD_DIM
    group = lambda n: pl.BlockSpec((b, lc, dh), lambda j: (0, 0, n))
    lg_spec = pl.BlockSpec(logits_f.shape, lambda j: (0, 0))
    st_spec = pl.BlockSpec((b, n_heads, HEAD_DIM, HEAD_DIM), lambda j: (0, 0, 0, 0))
    st_shape = jax.ShapeDtypeStruct((b, n_heads, HEAD_DIM, HEAD_DIM), F32)
    return pl.pallas_call(
        _ctx_state_kernel,
        grid=(1,),
        in_specs=[group(0), group(1), group(2), lg_spec, lg_spec],
        out_specs=[st_spec, st_spec],
        out_shape=[st_shape, st_shape],
        compiler_params=_params(("arbitrary",)),
        name="ctx_state",
    )(pc, pc, pc, logits_f, logits_b)


CONV_TILE_ROWS = 16
_HPAD = 16
_HROW = GRID_W + 2 * _HPAD


def _conv_kernel(cur_ref, prev_ref, next_ref, w_ref, b_ref, lg_ref, lb_ref, o_ref,
                 hpad_ref, vpad_ref, y_ref, *, n_tiles):
    tr = CONV_TILE_ROWS
    half = w_ref.shape[1] // 2
    n_lane_tiles = half // 128
    i = pl.program_id(1)

    hpad_ref[:, :, :_HPAD, :] = jnp.zeros((tr, n_lane_tiles, _HPAD, 128), F32)
    hpad_ref[:, :, _HPAD + GRID_W:, :] = jnp.zeros((tr, n_lane_tiles, _HPAD, 128), F32)
    for r in range(tr):
        for lt in range(n_lane_tiles):
            hpad_ref[r, lt, _HPAD:_HPAD + GRID_W, :] = cur_ref[0, r * GRID_W:(r + 1) * GRID_W,
                                                               lt * 128:(lt + 1) * 128]
    halo = CONV_PAD * GRID_W
    for lt in range(n_lane_tiles):
        lanes = slice(lt * 128, (lt + 1) * 128)
        top = prev_ref[0, tr * GRID_W - halo:, lanes]
        bot = next_ref[0, :halo, lanes]
        vpad_ref[lt, :halo, :] = jnp.where(i > 0, top, 0.0)
        vpad_ref[lt, halo:halo + tr * GRID_W, :] = cur_ref[0, :, half + lt * 128:half + (lt + 1) * 128]
        vpad_ref[lt, halo + tr * GRID_W:, :] = jnp.where(i < n_tiles - 1, bot, 0.0)

    def h_row(r, carry):
        o0 = pl.multiple_of(r * GRID_W, GRID_W)
        for lt in range(n_lane_tiles):
            lanes = slice(lt * 128, (lt + 1) * 128)
            acc = jnp.zeros((GRID_W, 128), F32)
            for jj in range(CONV_WIDTH):
                off = _HPAD - CONV_PAD + jj
                acc = acc + hpad_ref[r, lt, off:off + GRID_W, :] * w_ref[jj:jj + 1, lanes]
            y_ref[pl.ds(o0, GRID_W), lanes] = acc + b_ref[:, lanes]
        return carry

    lax.fori_loop(0, tr, h_row, 0)

    n_sub = GRID_W // 8
    for lt in range(n_lane_tiles):
        vl = slice(half + lt * 128, half + (lt + 1) * 128)
        wv = [jnp.broadcast_to(w_ref[jj:jj + 1, vl], (8, 128)) for jj in range(CONV_WIDTH)]
        bias = b_ref[:, vl]

        def v_row(r, carry, lt=lt, vl=vl, wv=wv, bias=bias):
            v0 = pl.multiple_of(r * GRID_W, GRID_W)
            acc = [jnp.zeros((8, 128), F32)] * n_sub
            for jj in range(CONV_WIDTH):
                src = pl.multiple_of(v0 + jj * GRID_W, GRID_W)
                tap = vpad_ref[lt, pl.ds(src, GRID_W), :]
                acc = [acc[s] + tap[8 * s:8 * s + 8, :] * wv[jj] for s in range(n_sub)]
            y_ref[pl.ds(v0, GRID_W), vl] = jnp.concatenate(acc, axis=0) + bias
            return carry

        lax.fori_loop(0, tr, v_row, 0, unroll=2)

    def ln_row(r, carry):
        o0 = pl.multiple_of(r * GRID_W, GRID_W)
        y = y_ref[pl.ds(o0, GRID_W), :]
        mu = jnp.mean(y, axis=-1, keepdims=True)
        yc = y - mu
        var = jnp.mean(yc * yc, axis=-1, keepdims=True)
        z = yc * lax.rsqrt(var + EPS) * lg_ref[...] + lb_ref[...]
        o_ref[0, pl.ds(o0, GRID_W), :] = _silu(z).astype(BF16)
        return carry

    lax.fori_loop(0, tr, ln_row, 0, unroll=4)


def _conv_call(u, w, bias, ln_g, ln_b):
    b, l, ch = u.shape
    half = ch // 2
    tile = CONV_TILE_ROWS * GRID_W
    n_tiles = l // tile
    return pl.pallas_call(
        functools.partial(_conv_kernel, n_tiles=n_tiles),
        grid=(b, n_tiles),
        in_specs=[pl.BlockSpec((1, tile, ch), lambda i, j: (i, j, 0)),
                  pl.BlockSpec((1, tile, half), lambda i, j: (i, jnp.maximum(j - 1, 0), 1)),
                  pl.BlockSpec((1, tile, half), lambda i, j: (i, jnp.minimum(j + 1, n_tiles - 1), 1)),
                  pl.BlockSpec((CONV_WIDTH, ch), lambda i, j: (0, 0)),
                  pl.BlockSpec((1, ch), lambda i, j: (0, 0)),
                  pl.BlockSpec((1, ch), lambda i, j: (0, 0)),
                  pl.BlockSpec((1, ch), lambda i, j: (0, 0))],
        out_specs=pl.BlockSpec((1, tile, ch), lambda i, j: (i, j, 0)),
        out_shape=jax.ShapeDtypeStruct((b, l, ch), BF16),
        scratch_shapes=[pltpu.VMEM((CONV_TILE_ROWS, half // 128, _HROW, 128), F32),
                        pltpu.VMEM((half // 128, tile + 2 * CONV_PAD * GRID_W, 128), F32),
                        pltpu.VMEM((tile, ch), F32)],
        compiler_params=_params(("parallel", "parallel")),
        name="conv",
    )(u, u, u, w, bias, ln_g, ln_b)


MIX_SLICE = 256

def _mix_kernel(uc_ref, og_ref, gc_ref, gh_ref, x_ref, wpw_ref, wh_ref, wo_ref,
                npost_ref, npre_ref, mod_ref, x1_ref, h2_ref):
    tm = x_ref.shape[1]
    n_split = tm // MIX_SLICE
    for r in range(n_split):
        rows = slice(r * MIX_SLICE, (r + 1) * MIX_SLICE)
        y_c = jnp.dot(uc_ref[0, rows, :], wpw_ref[...], preferred_element_type=F32)
        y_h = jnp.dot(og_ref[0, rows, :], wh_ref[...], preferred_element_type=F32)
        z = gc_ref[0, rows, :].astype(F32) * y_c + gh_ref[0, rows, :].astype(F32) * y_h
        y = jnp.dot(z.astype(BF16), wo_ref[...], preferred_element_type=F32)
        ms = jnp.mean(y * y, axis=-1, keepdims=True)
        yn = y * lax.rsqrt(ms + EPS) * npost_ref[...]
        x1 = x_ref[0, rows, :] + mod_ref[0, 0:1, :] * yn
        x1_ref[0, rows, :] = x1
        h2_ref[0, rows, :] = _norm_modulate(x1, npre_ref[...], mod_ref[1, 0:1, :],
                                            mod_ref[2, 0:1, :]).astype(BF16)


def _mix_call(uc, og, gates, x, wpw, wh, wo, npost, npre, mod_late, tm=512):
    b, l, d = x.shape
    dc = uc.shape[2]
    gc = gh = gates
    row = lambda width, col=0: pl.BlockSpec((1, tm, width), lambda i, j: (i, j, col))
    const = lambda shape: pl.BlockSpec(shape, lambda i, j: (0,) * len(shape), pipeline_mode=pl.Buffered(1))
    return pl.pallas_call(
        _mix_kernel,
        grid=(b, l // tm),
        in_specs=[row(dc), row(dc), row(d, 0), row(d, 1), row(d),
                  const(wpw.shape), const(wh.shape), const(wo.shape),
                  const((1, d)), const((1, d)),
                  _mod_block(N_MOD - N_MOD_EARLY, d, lambda i, j: i)],
        out_specs=[row(d), row(d)],
        out_shape=[jax.ShapeDtypeStruct((b, l, d), F32), jax.ShapeDtypeStruct((b, l, d), BF16)],
        compiler_params=_params(("parallel", "parallel")),
        name="mix",
    )(uc, og, gc, gh, x, wpw, wh, wo, npost, npre, mod_late)


MLP_SLICE = 256

def _mlp_kernel(h_ref, w1_ref, w2_ref, x1_ref, npost_ref, mod_ref, o_ref, *, n_ff):
    j = pl.program_id(2)

    def partial_sum(rows):
        a = jnp.dot(h_ref[0, rows, :], w1_ref[...], preferred_element_type=F32)
        a = jnp.square(jnp.maximum(a, 0.0)).astype(BF16)
        return jnp.dot(a, w2_ref[...], preferred_element_type=F32)

    @pl.when(j == 0)
    def _():
        o_ref[...] = jnp.zeros(o_ref.shape, F32)

    @pl.when(j < n_ff - 1)
    def _():
        o_ref[0] += partial_sum(slice(None))

    @pl.when(j == n_ff - 1)
    def _():
        for r in range(o_ref.shape[1] // MLP_SLICE):
            rows = slice(r * MLP_SLICE, (r + 1) * MLP_SLICE)
            y = o_ref[0, rows, :] + partial_sum(rows)
            ms = jnp.mean(y * y, axis=-1, keepdims=True)
            yn = y * lax.rsqrt(ms + EPS) * npost_ref[...]
            o_ref[0, rows, :] = x1_ref[0, rows, :] + mod_ref[3, 0:1, :] * yn


def _mlp_call(h2, w1, w2, x1, npost, mod_late, tm=1024, tf=1024):
    b, l, d = x1.shape
    dff = w1.shape[1]
    n_ff = dff // tf
    return pl.pallas_call(
        functools.partial(_mlp_kernel, n_ff=n_ff),
        grid=(b, l // tm, n_ff),
        in_specs=[pl.BlockSpec((1, tm, d), lambda i, m, j: (i, m, 0)),
                  pl.BlockSpec((d, tf), lambda i, m, j: (0, j)),
                  pl.BlockSpec((tf, d), lambda i, m, j: (j, 0)),
                  pl.BlockSpec((1, tm, d), lambda i, m, j: (i, m, 0), pipeline_mode=pl.Buffered(1)),
                  pl.BlockSpec((1, d), lambda i, m, j: (0, 0)),
                  _mod_block(N_MOD - N_MOD_EARLY, d, lambda i, m, j: i)],
        out_specs=pl.BlockSpec((1, tm, d), lambda i, m, j: (i, m, 0)),
        out_shape=jax.ShapeDtypeStruct((b, l, d), F32),
        compiler_params=_params(("parallel", "parallel", "arbitrary")),
        name="mlp",
    )(h2, w1, w2, x1, npost, mod_late)


def kernel(x, c, ctx, c_ctx, w_mod, b_mod, norm_pre_mix, norm_post_mix, norm_pre_mlp, norm_post_mlp, w_in,
           conv_dw_w, conv_dw_b, conv_ln_g, conv_ln_b, conv_pw_w, hgrn_lb_logits, hgrn_norm_g, hgrn_out_w,
           w_out, mlp_w1, mlp_w2):
    assert w_mod.shape[0] == 1, "single-layer block"
    b, l, d = x.shape
    lc = ctx.shape[1]
    dh = hgrn_norm_g.shape[1]
    dc = conv_dw_w.shape[2]
    assert l % (CONV_TILE_ROWS * GRID_W) == 0 and CONV_PAD <= CONV_TILE_ROWS and CONV_PAD <= _HPAD
    assert l % SCAN_BLOCK == 0 and lc % SCAN_CHUNK == 0

    cond = jnp.concatenate([c, c_ctx[None, :]], axis=0)
    cc = jnp.pad(cond[:, None, :], ((0, 0), (0, MOD_ROW_STRIDE - 1), (0, 0))).reshape(-1, d)
    mod_early = _mod_call(cc, w_mod[0], b_mod, N_MOD_EARLY)

    w_in0 = w_in[0]
    lb_f, lb_b = hgrn_lb_logits[0], hgrn_lb_logits[1]
    off = lambda n: n * dh

    pc = _ctx_proj_call(ctx, norm_pre_mix, mod_early, b, w_in0, 3 * dh)
    s_f, s_b = _ctx_state_call(pc, lb_f, lb_b, dh)

    h, v = _prenorm_v_call(x, norm_pre_mix, mod_early, w_in0, dh)
    h = h.reshape(b * l, d)
    seq = lambda t: t.reshape(b, l, t.shape[-1])
    lf, kk = _proj_call(h, w_in0, [off(1)], 2 * dh, _ep_forget, [F32, BF16],
                        extra=(jnp.concatenate([lb_f, lb_b], axis=1),), name="proj_forget")
    qg_scale = jnp.concatenate([jnp.full((1, dh), HEAD_DIM ** -0.5, F32), jnp.ones((1, dh), F32)], axis=1)
    qg, wpw, wh, wo, mod_late = _proj_call(h, w_in0, [off(3)], 2 * dh, _ep_scaled_silu, [BF16], extra=(qg_scale,),
                                           side=(conv_pw_w[0], hgrn_out_w[0], w_out[0]),
                                           late_mod=(cc, w_mod[0], b_mod), name="proj_qg")
    (u,) = _proj_call(h, w_in0, [off(5), off(5) + dc], dc, _ep_glu, [F32], tn=512, name="proj_glu")
    gates, w1, w2 = _proj_call(h, w_in0, [off(5) + 2 * dc], 2 * d, _ep_sigmoid, [BF16],
                               side=(mlp_w1[0], mlp_w2[0]), name="proj_gates")

    lf, kk, qg = seq(lf), seq(kk), seq(qg)
    o_b = _scan_call((kk, 1), (v, 0), (lf, 1), s_b, (qg, 0), reverse=True)
    og = _scan_call((kk, 0), (v, 0), (lf, 0), s_f, (qg, 0), reverse=False, ob=(o_b, 0), g=(qg, 1),
                    ng=hgrn_norm_g)

    uc = _conv_call(seq(u), conv_dw_w[0], conv_dw_b, conv_ln_g, conv_ln_b)

    x1, h2 = _mix_call(uc, og, seq(gates), x, wpw, wh, wo, norm_post_mix, norm_pre_mlp, mod_late)
    return _mlp_call(h2, w1, w2, x1, norm_post_mlp, mod_late)
```

```python
import functools

import jax
import jax.numpy as jnp
from jax import lax
from jax.experimental import pallas as pl
from jax.experimental.pallas import tpu as pltpu

F32 = jnp.float32
BF16 = jnp.bfloat16

GRID_W = 64
CONV_WIDTH = 31
CONV_PAD = (CONV_WIDTH - 1) // 2
HEAD_DIM = 128
N_MOD = 6
N_MOD_EARLY = 2
MOD_ROW_STRIDE = 16
EPS = 1e-6

SCAN_CHUNK = 64
SCAN_BLOCK = 512

VMEM_LIMIT = 56 * 1024 * 1024

_NT = (((1,), (1,)), ((), ()))
_TN = (((0,), (0,)), ((), ()))


def _params(sem):
    return pltpu.CompilerParams(dimension_semantics=sem, vmem_limit_bytes=VMEM_LIMIT)


def _sigmoid(x):
    return 0.5 * jnp.tanh(0.5 * x) + 0.5


def _silu(x):
    return x * _sigmoid(x)


def _norm_modulate(x, gain, shift, scale):
    ms = jnp.mean(x * x, axis=-1, keepdims=True)
    return (x * lax.rsqrt(ms + EPS) * gain) * (1.0 + scale) + shift


def _forget_gate(raw, logits):
    e = jnp.exp(logits - jnp.max(logits, axis=0, keepdims=True))
    lb = e[0:1] / jnp.sum(e, axis=0, keepdims=True)
    return lb + (1.0 - lb) * jax.nn.sigmoid(raw)


def _mod_rows(c_ref, w_ref, b_ref):
    s = _silu(c_ref[...]).astype(BF16)
    return jnp.dot(s, w_ref[...].astype(BF16), preferred_element_type=F32) + b_ref[...]


def _mod_kernel(c_ref, w_ref, b_ref, o_ref):
    o_ref[0] = _mod_rows(c_ref, w_ref, b_ref)


def _mod_call(cc, w_mod, b_mod, n_seg, tn=1024):
    rows, d = cc.shape
    return pl.pallas_call(
        _mod_kernel,
        grid=(n_seg * d // tn,),
        in_specs=[pl.BlockSpec((rows, d), lambda j: (0, 0)),
                  pl.BlockSpec((d, tn), lambda j: (0, j)),
                  pl.BlockSpec((1, tn), lambda j: (0, j))],
        out_specs=pl.BlockSpec((1, rows, tn), lambda j: (j // (d // tn), 0, j % (d // tn))),
        out_shape=jax.ShapeDtypeStruct((n_seg, rows, d), F32),
        compiler_params=_params(("parallel",)),
        name="mod",
    )(cc, w_mod, b_mod)


def _mod_block(n_seg, d, row_of):
    return pl.BlockSpec((n_seg, 8, d), lambda *idx: (0, row_of(*idx) * (MOD_ROW_STRIDE // 8), 0))


def _proj_kernel(*refs, n_w, n_extra, n_side, n_out, epilogue):
    h_ref = refs[0]
    pos = 1
    w_refs = refs[pos:pos + n_w]
    pos += n_w
    extra = refs[pos:pos + n_extra]
    pos += n_extra
    side_in = refs[pos:pos + n_side]
    pos += n_side
    out_refs = refs[pos:pos + n_out]
    pos += n_out
    side_out = refs[pos:pos + n_side]
    pos += n_side
    wbf_refs = refs[pos:]

    @pl.when(pl.program_id(1) == 0)
    def _():
        for w_ref, wbf_ref in zip(w_refs, wbf_refs):
            wbf_ref[...] = w_ref[...].astype(BF16)

    for si_ref, so_ref in zip(side_in, side_out):
        so_ref[...] = si_ref[...].astype(BF16)

    h = h_ref[...]
    accs = [jnp.dot(h, wbf_ref[...], preferred_element_type=F32) for wbf_ref in wbf_refs]
    outs = epilogue(accs, [e_ref[...] for e_ref in extra])
    for o_ref, o in zip(out_refs, outs):
        o_ref[...] = o.astype(o_ref.dtype)


def _ep_forget(accs, extra):
    f = _forget_gate(accs[0], extra[0])
    return jnp.log2(f), 1.0 - f


def _ep_scaled_silu(accs, extra):
    return (_silu(accs[0]) * extra[0],)


def _ep_glu(accs, extra):
    return (accs[0] * _sigmoid(accs[1]),)


def _ep_sigmoid(accs, extra):
    return (_sigmoid(accs[0]),)


def _proj_call(h, w, col_offsets, width, epilogue, out_dtypes, extra=(), side=(), tm=1024, tn=1024,
               name="proj"):
    m, k = h.shape
    tm = min(tm, m)
    n_w = len(col_offsets)
    n_j, n_i = width // tn, m // tm
    in_specs = [pl.BlockSpec((tm, k), lambda j, i: (i, 0))]
    for off in col_offsets:
        in_specs.append(pl.BlockSpec((k, tn), lambda j, i, off=off: (0, off // tn + j)))
    for e in extra:
        in_specs.append(pl.BlockSpec((e.shape[0], tn), lambda j, i: (0, j)))
    side_specs = [pl.BlockSpec((a.shape[0] // (n_j * n_i), a.shape[1]), lambda j, i: (j * n_i + i, 0))
                  for a in side]
    return pl.pallas_call(
        functools.partial(_proj_kernel, n_w=n_w, n_extra=len(extra), n_side=len(side),
                          n_out=len(out_dtypes), epilogue=epilogue),
        grid=(n_j, n_i),
        in_specs=in_specs + side_specs,
        out_specs=[pl.BlockSpec((tm, tn), lambda j, i: (i, j)) for _ in out_dtypes] + side_specs,
        out_shape=[jax.ShapeDtypeStruct((m, width), dt) for dt in out_dtypes]
                  + [jax.ShapeDtypeStruct(a.shape, BF16) for a in side],
        scratch_shapes=[pltpu.VMEM((k, tn), BF16) for _ in range(n_w)],
        compiler_params=_params(("parallel", "arbitrary")),
        name=name,
    )(h, *([w] * n_w), *extra, *side)


def _prenorm_v_kernel(x_ref, g_ref, mod_ref, w_ref, h_ref, v_ref, wbf_ref):
    @pl.when((pl.program_id(0) == 0) & (pl.program_id(1) == 0))
    def _():
        wbf_ref[...] = w_ref[...].astype(BF16)

    for r in range(x_ref.shape[1] // PRENORM_SLICE):
        rows = slice(r * PRENORM_SLICE, (r + 1) * PRENORM_SLICE)
        h = _norm_modulate(x_ref[0, rows, :], g_ref[...], mod_ref[0, 0:1, :], mod_ref[1, 0:1, :]).astype(BF16)
        h_ref[0, rows, :] = h
        v_ref[0, rows, :] = jnp.dot(h, wbf_ref[...], preferred_element_type=F32).astype(BF16)


PRENORM_SLICE = 256


def _prenorm_v_call(x, g, mod_early, w, width, tl=1024):
    b, l, d = x.shape
    return pl.pallas_call(
        _prenorm_v_kernel,
        grid=(b, l // tl),
        in_specs=[pl.BlockSpec((1, tl, d), lambda i, j: (i, j, 0)),
                  pl.BlockSpec((1, d), lambda i, j: (0, 0)),
                  _mod_block(N_MOD_EARLY, d, lambda i, j: i),
                  pl.BlockSpec((d, width), lambda i, j: (0, 0), pipeline_mode=pl.Buffered(1))],
        out_specs=[pl.BlockSpec((1, tl, d), lambda i, j: (i, j, 0)),
                   pl.BlockSpec((1, tl, width), lambda i, j: (i, j, 0))],
        out_shape=[jax.ShapeDtypeStruct((b, l, d), BF16), jax.ShapeDtypeStruct((b, l, width), BF16)],
        scratch_shapes=[pltpu.VMEM((d, width), BF16)],
        compiler_params=_params(("arbitrary", "arbitrary")),
        name="prenorm_v",
    )(x, g, mod_early, w)


def _scan_masks(c, reverse):
    t = lax.broadcasted_iota(jnp.int32, (c, c), 0)
    s = lax.broadcasted_iota(jnp.int32, (c, c), 1)
    half = c // 2
    before = (s >= t) if reverse else (s <= t)
    mask1 = ((t < half) == (s < half)) & before
    tri = jnp.where(before, 1.0, 0.0).astype(BF16)
    return mask1, jnp.concatenate([tri, tri], axis=1)


def _scan_cumsum(lf, tri2):
    hi = lf.astype(BF16)
    lo = (lf - hi.astype(F32)).astype(BF16)
    return jnp.dot(tri2, jnp.concatenate([hi, lo], axis=0), preferred_element_type=F32)


def _scan_refs(a, reverse):
    c = a.shape[0]
    half, quarter = c // 2, c // 4
    if reverse:
        return a[quarter:quarter + 1], a[half + quarter:half + quarter + 1], a[half:half + 1], a[0:1]
    return a[quarter - 1:quarter], a[half + quarter - 1:half + quarter], a[half - 1:half], a[c - 1:c]


def _scan_products(a, q, k, v, st, st_kv, reverse):
    c = a.shape[0]
    half = c // 2
    m1_lo, m1_hi, m0, a_tot = _scan_refs(a, reverse)
    lo_rows, hi_rows = slice(0, half), slice(half, c)
    row = lax.broadcasted_iota(jnp.int32, a.shape, 0)
    e1 = a - jnp.where(row < half, m1_lo, m1_hi)
    pq = q.astype(F32) * jnp.exp2(e1)
    pk = k.astype(F32) * jnp.exp2(-e1)
    s1 = lax.dot_general(pq.astype(BF16), pk.astype(BF16), _NT, preferred_element_type=F32)
    if reverse:
        q_rows, m1_q, k_rows, m1_k = lo_rows, m1_lo, hi_rows, m1_hi
    else:
        q_rows, m1_q, k_rows, m1_k = hi_rows, m1_hi, lo_rows, m1_lo
    q0 = (pq[q_rows] * jnp.exp2(m1_q - m0)).astype(BF16)
    k0 = (pk[k_rows] * jnp.exp2(m0 - m1_k)).astype(BF16)
    s0 = lax.dot_general(q0, k0, _NT, preferred_element_type=F32)
    qi = jnp.concatenate([pq[lo_rows] * jnp.exp2(m1_lo), pq[hi_rows] * jnp.exp2(m1_hi)], axis=0)
    inter = jnp.dot(qi.astype(BF16), st_kv, preferred_element_type=F32)
    kh = jnp.concatenate([pk[lo_rows] * jnp.exp2(a_tot - m1_lo), pk[hi_rows] * jnp.exp2(a_tot - m1_hi)], axis=0)
    upd = lax.dot_general(v, kh.astype(BF16), _TN, preferred_element_type=F32)
    return s1, s0, inter, st * jnp.exp2(a_tot) + upd


def _scan_output(s1, s0, inter, v, mask1, reverse):
    c = s1.shape[0]
    half = c // 2
    lo_rows, hi_rows = slice(0, half), slice(half, c)
    q_rows, k_rows = (lo_rows, hi_rows) if reverse else (hi_rows, lo_rows)
    intra = jnp.dot(jnp.where(mask1, s1, 0.0).astype(BF16), v, preferred_element_type=F32)
    cross = jnp.dot(s0.astype(BF16), v[k_rows], preferred_element_type=F32)
    o = inter + intra
    o_q = o[q_rows] + cross
    return jnp.concatenate([o_q, o[hi_rows]] if reverse else [o[lo_rows], o_q], axis=0)


def _scan_kernel(*refs, reverse, readout):
    k_ref, v_ref, lf_ref, s0_ref, q_ref = refs[:5]
    if readout:
        (ob_ref, g_ref, ng_ref, o_ref, st_ref, stkv_ref, a_ref) = refs[5:]
    else:
        (o_ref, st_ref, stkv_ref, a_ref) = refs[5:]
    c = SCAN_CHUNK
    n_batch, rows = k_ref.shape[0], k_ref.shape[1]
    n_chunks = rows // c
    n_heads = k_ref.shape[2] // HEAD_DIM
    recurrences = [(h, bi) for h in range(n_heads) for bi in range(n_batch)]

    @pl.when(pl.program_id(0) == 0)
    def _():
        st_ref[...] = s0_ref[...]
        for h, bi in recurrences:
            stkv_ref[bi, h] = s0_ref[bi, h].T.astype(BF16)

    mask1, tri2 = _scan_masks(c, reverse)

    def cumsum_body(ci, carry):
        r0 = pl.multiple_of(ci * c, c)
        for bi in range(n_batch):
            a_ref[bi, pl.ds(r0, c), :] = _scan_cumsum(lf_ref[bi, pl.ds(r0, c), :], tri2)
        return carry

    lax.fori_loop(0, n_chunks, cumsum_body, 0, unroll=4)

    def chunk_body(ci, carry):
        cidx = (n_chunks - 1 - ci) if reverse else ci
        r0 = pl.multiple_of(cidx * c, c)
        tile = lambda ref, h, bi: ref[bi, pl.ds(r0, c), h * HEAD_DIM:(h + 1) * HEAD_DIM]
        wave = []
        for h, bi in recurrences:
            s1, s0, inter, st_new = _scan_products(
                tile(a_ref, h, bi), tile(q_ref, h, bi), tile(k_ref, h, bi), tile(v_ref, h, bi),
                st_ref[bi, h], stkv_ref[bi, h], reverse)
            st_ref[bi, h] = st_new
            stkv_ref[bi, h] = st_new.T.astype(BF16)
            wave.append((s1, s0, inter))
        for (h, bi), (s1, s0, inter) in zip(recurrences, wave):
            lanes = slice(h * HEAD_DIM, (h + 1) * HEAD_DIM)
            o = _scan_output(s1, s0, inter, tile(v_ref, h, bi), mask1, reverse)
            if readout:
                o = o + tile(ob_ref, h, bi)
                ms = jnp.mean(o * o, axis=-1, keepdims=True)
                y = o * lax.rsqrt(ms + EPS) * ng_ref[:, lanes]
                o = y * tile(g_ref, h, bi).astype(F32)
            o_ref[bi, pl.ds(r0, c), lanes] = o.astype(o_ref.dtype)
        return carry

    lax.fori_loop(0, n_chunks, chunk_body, 0, unroll=4)


def _scan_call(k, v, lf, s0, q, *, reverse, ob=None, g=None, ng=None):
    b, l, _ = v[0].shape
    dh = ng.shape[1] if ng is not None else s0.shape[1] * HEAD_DIM
    tb = min(SCAN_BLOCK, l)
    n_blocks = l // tb
    readout = ob is not None
    blk = (lambda j: n_blocks - 1 - j) if reverse else (lambda j: j)
    seq_spec = lambda group: pl.BlockSpec((b, tb, dh), lambda j: (0, blk(j), group))
    st_spec = pl.BlockSpec(s0.shape, lambda j: (0, 0, 0, 0))
    in_specs = [seq_spec(k[1]), seq_spec(v[1]), seq_spec(lf[1]), st_spec, seq_spec(q[1])]
    args = [k[0], v[0], lf[0], s0, q[0]]
    if readout:
        in_specs += [seq_spec(ob[1]), seq_spec(g[1]), pl.BlockSpec((1, dh), lambda j: (0, 0))]
        args += [ob[0], g[0], ng]
    return pl.pallas_call(
        functools.partial(_scan_kernel, reverse=reverse, readout=readout),
        grid=(n_blocks,),
        in_specs=in_specs,
        out_specs=seq_spec(0),
        out_shape=jax.ShapeDtypeStruct((b, l, dh), BF16 if readout else F32),
        scratch_shapes=[pltpu.VMEM(s0.shape, F32), pltpu.VMEM(s0.shape, BF16), pltpu.VMEM((b, tb, dh), F32)],
        compiler_params=_params(("arbitrary",)),
        name="scan_readout_fwd" if readout else "scan_out_bwd",
    )(*args)


def _ctx_proj_kernel(ctx_ref, g_ref, mod_ref, w_ref, o_ref, hc_ref):
    @pl.when(pl.program_id(0) == 0)
    def _():
        for bi in range(ctx_ref.shape[0]):
            hc_ref[bi] = _norm_modulate(ctx_ref[bi], g_ref[...], mod_ref[0, 0:1, :],
                                        mod_ref[1, 0:1, :]).astype(BF16)

    w = w_ref[...].astype(BF16)
    for bi in range(ctx_ref.shape[0]):
        o_ref[bi] = jnp.dot(hc_ref[bi], w, preferred_element_type=F32)


def _ctx_proj_call(ctx, g, mod_early, mod_row, w, width, tn=1024):
    b, lc, d = ctx.shape
    return pl.pallas_call(
        _ctx_proj_kernel,
        grid=(width // tn,),
        in_specs=[pl.BlockSpec((b, lc, d), lambda j: (0, 0, 0)),
                  pl.BlockSpec((1, d), lambda j: (0, 0)),
                  _mod_block(N_MOD_EARLY, d, lambda j: mod_row),
                  pl.BlockSpec((d, tn), lambda j: (0, j))],
        out_specs=pl.BlockSpec((b, lc, tn), lambda j: (0, 0, j)),
        out_shape=jax.ShapeDtypeStruct((b, lc, width), F32),
        scratch_shapes=[pltpu.VMEM((b, lc, d), BF16)],
        compiler_params=_params(("arbitrary",)),
        name="ctx_proj",
    )(ctx, g, mod_early, w)


def _ctx_state_kernel(pv_ref, pf_ref, pb_ref, lgf_ref, lgb_ref, sf_ref, sb_ref):
    c = SCAN_CHUNK
    n_batch, rows = pv_ref.shape[0], pv_ref.shape[1]
    n_chunks = rows // c
    n_heads = pv_ref.shape[2] // HEAD_DIM
    for reverse, p_ref, lg_ref, s_ref in ((False, pf_ref, lgf_ref, sf_ref), (True, pb_ref, lgb_ref, sb_ref)):
        _, tri2 = _scan_masks(c, reverse)
        s_ref[...] = jnp.zeros(s_ref.shape, F32)

        def chunk_body(ci, carry, reverse=reverse, p_ref=p_ref, lg_ref=lg_ref, s_ref=s_ref, tri2=tri2):
            cidx = (n_chunks - 1 - ci) if reverse else ci
            r0 = pl.multiple_of(cidx * c, c)
            recurrences = [(h, bi) for h in range(n_heads) for bi in range(n_batch)]
            tile = lambda ref, h, bi: ref[bi, pl.ds(r0, c), h * HEAD_DIM:(h + 1) * HEAD_DIM]
            gates = [_forget_gate(tile(p_ref, h, bi), lg_ref[:, h * HEAD_DIM:(h + 1) * HEAD_DIM])
                     for h, bi in recurrences]
            sums = [_scan_cumsum(jnp.log2(f), tri2) for f in gates]
            for (h, bi), f, a in zip(recurrences, gates, sums):
                a_tot = _scan_refs(a, reverse)[3]
                kh = ((1.0 - f) * jnp.exp2(a_tot - a)).astype(BF16)
                upd = lax.dot_general(tile(pv_ref, h, bi).astype(BF16), kh, _TN, preferred_element_type=F32)
                s_ref[bi, h] = s_ref[bi, h] * jnp.exp2(a_tot) + upd
            return carry

        lax.fori_loop(0, n_chunks, chunk_body, 0)


def _ctx_state_call(pc, logits_f, logits_b, dh):
    b, lc, _ = pc.shape
    n_heads = dh // HEAD_DIM
    group = lambda n: pl.BlockSpec((b, lc, dh), lambda j: (0, 0, n))
    lg_spec = pl.BlockSpec(logits_f.shape, lambda j: (0, 0))
    st_spec = pl.BlockSpec((b, n_heads, HEAD_DIM, HEAD_DIM), lambda j: (0, 0, 0, 0))
    st_shape = jax.ShapeDtypeStruct((b, n_heads, HEAD_DIM, HEAD_DIM), F32)
    return pl.pallas_call(
        _ctx_state_kernel,
        grid=(1,),
        in_specs=[group(0), group(1), group(2), lg_spec, lg_spec],
        out_specs=[st_spec, st_spec],
        out_shape=[st_shape, st_shape],
        compiler_params=_params(("arbitrary",)),
        name="ctx_state",
    )(pc, pc, pc, logits_f, logits_b)


CONV_TILE_ROWS = 16
_HPAD = 16
_HROW = GRID_W + 2 * _HPAD


def _conv_kernel(cur_ref, prev_ref, next_ref, w_ref, b_ref, lg_ref, lb_ref, cc_ref, wm_ref, bm_ref,
                 o_ref, modl_ref, hpad_ref, vpad_ref, y_ref, *, n_tiles):
    tr = CONV_TILE_ROWS
    half = w_ref.shape[1] // 2
    n_lane_tiles = half // 128
    i = pl.program_id(1)

    modl_ref[0] = _mod_rows(cc_ref, wm_ref, bm_ref)

    hpad_ref[:, :, :_HPAD, :] = jnp.zeros((tr, n_lane_tiles, _HPAD, 128), F32)
    hpad_ref[:, :, _HPAD + GRID_W:, :] = jnp.zeros((tr, n_lane_tiles, _HPAD, 128), F32)
    for r in range(tr):
        for lt in range(n_lane_tiles):
            hpad_ref[r, lt, _HPAD:_HPAD + GRID_W, :] = cur_ref[0, r * GRID_W:(r + 1) * GRID_W,
                                                               lt * 128:(lt + 1) * 128]
    halo = CONV_PAD * GRID_W
    for lt in range(n_lane_tiles):
        lanes = slice(lt * 128, (lt + 1) * 128)
        top = prev_ref[0, tr * GRID_W - halo:, lanes]
        bot = next_ref[0, :halo, lanes]
        vpad_ref[lt, :halo, :] = jnp.where(i > 0, top, 0.0)
        vpad_ref[lt, halo:halo + tr * GRID_W, :] = cur_ref[0, :, half + lt * 128:half + (lt + 1) * 128]
        vpad_ref[lt, halo + tr * GRID_W:, :] = jnp.where(i < n_tiles - 1, bot, 0.0)

    def h_row(r, carry):
        o0 = pl.multiple_of(r * GRID_W, GRID_W)
        for lt in range(n_lane_tiles):
            lanes = slice(lt * 128, (lt + 1) * 128)
            acc = jnp.zeros((GRID_W, 128), F32)
            for jj in range(CONV_WIDTH):
                off = _HPAD - CONV_PAD + jj
                acc = acc + hpad_ref[r, lt, off:off + GRID_W, :] * w_ref[jj:jj + 1, lanes]
            y_ref[pl.ds(o0, GRID_W), lanes] = acc + b_ref[:, lanes]
        return carry

    lax.fori_loop(0, tr, h_row, 0)

    n_sub = GRID_W // 8
    for lt in range(n_lane_tiles):
        vl = slice(half + lt * 128, half + (lt + 1) * 128)
        wv = [jnp.broadcast_to(w_ref[jj:jj + 1, vl], (8, 128)) for jj in range(CONV_WIDTH)]
        bias = b_ref[:, vl]

        def v_row(r, carry, lt=lt, vl=vl, wv=wv, bias=bias):
            v0 = pl.multiple_of(r * GRID_W, GRID_W)
            acc = [jnp.zeros((8, 128), F32)] * n_sub
            for jj in range(CONV_WIDTH):
                src = pl.multiple_of(v0 + jj * GRID_W, GRID_W)
                tap = vpad_ref[lt, pl.ds(src, GRID_W), :]
                acc = [acc[s] + tap[8 * s:8 * s + 8, :] * wv[jj] for s in range(n_sub)]
            y_ref[pl.ds(v0, GRID_W), vl] = jnp.concatenate(acc, axis=0) + bias
            return carry

        lax.fori_loop(0, tr, v_row, 0, unroll=2)

    def ln_row(r, carry):
        o0 = pl.multiple_of(r * GRID_W, GRID_W)
        y = y_ref[pl.ds(o0, GRID_W), :]
        mu = jnp.mean(y, axis=-1, keepdims=True)
        yc = y - mu
        var = jnp.mean(yc * yc, axis=-1, keepdims=True)
        z = yc * lax.rsqrt(var + EPS) * lg_ref[...] + lb_ref[...]
        o_ref[0, pl.ds(o0, GRID_W), :] = _silu(z).astype(BF16)
        return carry

    lax.fori_loop(0, tr, ln_row, 0, unroll=4)


def _conv_call(u, w, bias, ln_g, ln_b, cc, w_mod, b_mod):
    b, l, ch = u.shape
    half = ch // 2
    tile = CONV_TILE_ROWS * GRID_W
    n_tiles = l // tile
    rows, d = cc.shape
    n_late = w_mod.shape[1] // d - N_MOD_EARLY
    tnm = n_late * d // (b * n_tiles)
    per = d // tnm
    step = lambda i, j: i * n_tiles + j
    return pl.pallas_call(
        functools.partial(_conv_kernel, n_tiles=n_tiles),
        grid=(b, n_tiles),
        in_specs=[pl.BlockSpec((1, tile, ch), lambda i, j: (i, j, 0)),
                  pl.BlockSpec((1, tile, half), lambda i, j: (i, jnp.maximum(j - 1, 0), 1)),
                  pl.BlockSpec((1, tile, half), lambda i, j: (i, jnp.minimum(j + 1, n_tiles - 1), 1)),
                  pl.BlockSpec((CONV_WIDTH, ch), lambda i, j: (0, 0)),
                  pl.BlockSpec((1, ch), lambda i, j: (0, 0)),
                  pl.BlockSpec((1, ch), lambda i, j: (0, 0)),
                  pl.BlockSpec((1, ch), lambda i, j: (0, 0)),
                  pl.BlockSpec((rows, d), lambda i, j: (0, 0)),
                  pl.BlockSpec((d, tnm), lambda i, j: (0, N_MOD_EARLY * per + step(i, j))),
                  pl.BlockSpec((1, tnm), lambda i, j: (0, N_MOD_EARLY * per + step(i, j)))],
        out_specs=[pl.BlockSpec((1, tile, ch), lambda i, j: (i, j, 0)),
                   pl.BlockSpec((1, rows, tnm), lambda i, j: (step(i, j) // per, 0, step(i, j) % per))],
        out_shape=[jax.ShapeDtypeStruct((b, l, ch), BF16), jax.ShapeDtypeStruct((n_late, rows, d), F32)],
        scratch_shapes=[pltpu.VMEM((CONV_TILE_ROWS, half // 128, _HROW, 128), F32),
                        pltpu.VMEM((half // 128, tile + 2 * CONV_PAD * GRID_W, 128), F32),
                        pltpu.VMEM((tile, ch), F32)],
        compiler_params=_params(("arbitrary", "arbitrary")),
        name="conv",
    )(u, u, u, w, bias, ln_g, ln_b, cc, w_mod, b_mod)


MIX_SLICE = 256

def _mix_kernel(uc_ref, og_ref, gc_ref, gh_ref, x_ref, wpw_ref, wh_ref, wo_ref,
                npost_ref, npre_ref, mod_ref, x1_ref, h2_ref):
    tm = x_ref.shape[1]
    n_split = tm // MIX_SLICE
    for r in range(n_split):
        rows = slice(r * MIX_SLICE, (r + 1) * MIX_SLICE)
        y_c = jnp.dot(uc_ref[0, rows, :], wpw_ref[...], preferred_element_type=F32)
        y_h = jnp.dot(og_ref[0, rows, :], wh_ref[...], preferred_element_type=F32)
        z = gc_ref[0, rows, :].astype(F32) * y_c + gh_ref[0, rows, :].astype(F32) * y_h
        y = jnp.dot(z.astype(BF16), wo_ref[...], preferred_element_type=F32)
        ms = jnp.mean(y * y, axis=-1, keepdims=True)
        yn = y * lax.rsqrt(ms + EPS) * npost_ref[...]
        x1 = x_ref[0, rows, :] + mod_ref[0, 0:1, :] * yn
        x1_ref[0, rows, :] = x1
        h2_ref[0, rows, :] = _norm_modulate(x1, npre_ref[...], mod_ref[1, 0:1, :],
                                            mod_ref[2, 0:1, :]).astype(BF16)


def _mix_call(uc, og, gates, x, wpw, wh, wo, npost, npre, mod_late, tm=512):
    b, l, d = x.shape
    dc = uc.shape[2]
    gc = gh = gates
    row = lambda width, col=0: pl.BlockSpec((1, tm, width), lambda i, j: (i, j, col))
    const = lambda shape: pl.BlockSpec(shape, lambda i, j: (0,) * len(shape), pipeline_mode=pl.Buffered(1))
    return pl.pallas_call(
        _mix_kernel,
        grid=(b, l // tm),
        in_specs=[row(dc), row(dc), row(d, 0), row(d, 1), row(d),
                  const(wpw.shape), const(wh.shape), const(wo.shape),
                  const((1, d)), const((1, d)),
                  _mod_block(N_MOD - N_MOD_EARLY, d, lambda i, j: i)],
        out_specs=[row(d), row(d)],
        out_shape=[jax.ShapeDtypeStruct((b, l, d), F32), jax.ShapeDtypeStruct((b, l, d), BF16)],
        compiler_params=_params(("parallel", "parallel")),
        name="mix",
    )(uc, og, gc, gh, x, wpw, wh, wo, npost, npre, mod_late)


MLP_SLICE = 256

def _mlp_kernel(h_ref, w1_ref, w2_ref, x1_ref, npost_ref, mod_ref, o_ref, *, n_ff):
    j = pl.program_id(2)

    def partial_sum(rows):
        a = jnp.dot(h_ref[0, rows, :], w1_ref[...], preferred_element_type=F32)
        a = jnp.square(jnp.maximum(a, 0.0)).astype(BF16)
        return jnp.dot(a, w2_ref[...], preferred_element_type=F32)

    @pl.when(j == 0)
    def _():
        o_ref[0] = partial_sum(slice(None))

    @pl.when((j > 0) & (j < n_ff - 1))
    def _():
        o_ref[0] += partial_sum(slice(None))

    @pl.when(j == n_ff - 1)
    def _():
        for r in range(o_ref.shape[1] // MLP_SLICE):
            rows = slice(r * MLP_SLICE, (r + 1) * MLP_SLICE)
            y = o_ref[0, rows, :] + partial_sum(rows)
            ms = jnp.mean(y * y, axis=-1, keepdims=True)
            yn = y * lax.rsqrt(ms + EPS) * npost_ref[...]
            o_ref[0, rows, :] = x1_ref[0, rows, :] + mod_ref[3, 0:1, :] * yn


def _mlp_call(h2, w1, w2, x1, npost, mod_late, tm=1024, tf=1024):
    b, l, d = x1.shape
    dff = w1.shape[1]
    n_ff = dff // tf
    assert n_ff >= 2, "first and last d_ff steps are distinct code paths"
    return pl.pallas_call(
        functools.partial(_mlp_kernel, n_ff=n_ff),
        grid=(b, l // tm, n_ff),
        in_specs=[pl.BlockSpec((1, tm, d), lambda i, m, j: (i, m, 0)),
                  pl.BlockSpec((d, tf), lambda i, m, j: (0, j)),
                  pl.BlockSpec((tf, d), lambda i, m, j: (j, 0)),
                  pl.BlockSpec((1, tm, d), lambda i, m, j: (i, m, 0), pipeline_mode=pl.Buffered(1)),
                  pl.BlockSpec((1, d), lambda i, m, j: (0, 0)),
                  _mod_block(N_MOD - N_MOD_EARLY, d, lambda i, m, j: i)],
        out_specs=pl.BlockSpec((1, tm, d), lambda i, m, j: (i, m, 0)),
        out_shape=jax.ShapeDtypeStruct((b, l, d), F32),
        compiler_params=_params(("parallel", "parallel", "arbitrary")),
        name="mlp",
    )(h2, w1, w2, x1, npost, mod_late)


def kernel(x, c, ctx, c_ctx, w_mod, b_mod, norm_pre_mix, norm_post_mix, norm_pre_mlp, norm_post_mlp, w_in,
           conv_dw_w, conv_dw_b, conv_ln_g, conv_ln_b, conv_pw_w, hgrn_lb_logits, hgrn_norm_g, hgrn_out_w,
           w_out, mlp_w1, mlp_w2):
    assert w_mod.shape[0] == 1, "single-layer block"
    b, l, d = x.shape
    lc = ctx.shape[1]
    dh = hgrn_norm_g.shape[1]
    dc = conv_dw_w.shape[2]
    assert l % (CONV_TILE_ROWS * GRID_W) == 0 and CONV_PAD <= CONV_TILE_ROWS and CONV_PAD <= _HPAD
    assert l % SCAN_BLOCK == 0 and lc % SCAN_CHUNK == 0

    cond = jnp.concatenate([c, c_ctx[None, :]], axis=0)
    cc = jnp.pad(cond[:, None, :], ((0, 0), (0, MOD_ROW_STRIDE - 1), (0, 0))).reshape(-1, d)
    mod_early = _mod_call(cc, w_mod[0], b_mod, N_MOD_EARLY)

    w_in0 = w_in[0]
    lb_f, lb_b = hgrn_lb_logits[0], hgrn_lb_logits[1]
    off = lambda n: n * dh

    pc = _ctx_proj_call(ctx, norm_pre_mix, mod_early, b, w_in0, 3 * dh)
    s_f, s_b = _ctx_state_call(pc, lb_f, lb_b, dh)

    h, v = _prenorm_v_call(x, norm_pre_mix, mod_early, w_in0, dh)
    h = h.reshape(b * l, d)
    seq = lambda t: t.reshape(b, l, t.shape[-1])
    lf, kk = _proj_call(h, w_in0, [off(1)], 2 * dh, _ep_forget, [F32, BF16],
                        extra=(jnp.concatenate([lb_f, lb_b], axis=1),), name="proj_forget")
    qg_scale = jnp.concatenate([jnp.full((1, dh), HEAD_DIM ** -0.5, F32), jnp.ones((1, dh), F32)], axis=1)
    qg, wpw, wh, wo = _proj_call(h, w_in0, [off(3)], 2 * dh, _ep_scaled_silu, [BF16], extra=(qg_scale,),
                                 side=(conv_pw_w[0], hgrn_out_w[0], w_out[0]), name="proj_qg")
    (u,) = _proj_call(h, w_in0, [off(5), off(5) + dc], dc, _ep_glu, [F32], tn=512, name="proj_glu")
    gates, w1, w2 = _proj_call(h, w_in0, [off(5) + 2 * dc], 2 * d, _ep_sigmoid, [BF16],
                               side=(mlp_w1[0], mlp_w2[0]), name="proj_gates")

    lf, kk, qg = seq(lf), seq(kk), seq(qg)
    o_b = _scan_call((kk, 1), (v, 0), (lf, 1), s_b, (qg, 0), reverse=True)
    og = _scan_call((kk, 0), (v, 0), (lf, 0), s_f, (qg, 0), reverse=False, ob=(o_b, 0), g=(qg, 1),
                    ng=hgrn_norm_g)

    uc, mod_late = _conv_call(seq(u), conv_dw_w[0], conv_dw_b, conv_ln_g, conv_ln_b, cc, w_mod[0], b_mod)

    x1, h2 = _mix_call(uc, og, seq(gates), x, wpw, wh, wo, norm_post_mix, norm_pre_mlp, mod_late)
    return _mlp_call(h2, w1, w2, x1, norm_post_mlp, mod_late)
```

```python
import functools

import jax
import jax.numpy as jnp
from jax import lax
from jax.experimental import pallas as pl
from jax.experimental.pallas import tpu as pltpu

F32 = jnp.float32
BF16 = jnp.bfloat16

GRID_W = 64
CONV_WIDTH = 31
CONV_PAD = (CONV_WIDTH - 1) // 2
HEAD_DIM = 128
N_MOD = 6
N_MOD_EARLY = 2
MOD_ROW_STRIDE = 16
EPS = 1e-6

SCAN_CHUNK = 64
SCAN_BLOCK = 512

VMEM_LIMIT = 56 * 1024 * 1024

_NT = (((1,), (1,)), ((), ()))
_TN = (((0,), (0,)), ((), ()))


def _params(sem):
    return pltpu.CompilerParams(dimension_semantics=sem, vmem_limit_bytes=VMEM_LIMIT)


def _sigmoid(x):
    return 0.5 * jnp.tanh(0.5 * x) + 0.5


def _silu(x):
    return x * _sigmoid(x)


def _norm_modulate(x, gain, shift, scale):
    ms = jnp.mean(x * x, axis=-1, keepdims=True)
    return (x * lax.rsqrt(ms + EPS) * gain) * (1.0 + scale) + shift


def _forget_gate(raw, logits):
    e = jnp.exp(logits - jnp.max(logits, axis=0, keepdims=True))
    lb = e[0:1] / jnp.sum(e, axis=0, keepdims=True)
    return lb + (1.0 - lb) * jax.nn.sigmoid(raw)


def _mod_rows(c_ref, w_ref, b_ref):
    s = _silu(c_ref[...]).astype(BF16)
    return jnp.dot(s, w_ref[...].astype(BF16), preferred_element_type=F32) + b_ref[...]


def _mod_kernel(c_ref, w_ref, b_ref, o_ref):
    o_ref[0] = _mod_rows(c_ref, w_ref, b_ref)


def _mod_call(cc, w_mod, b_mod, n_seg, tn=1024):
    rows, d = cc.shape
    return pl.pallas_call(
        _mod_kernel,
        grid=(n_seg * d // tn,),
        in_specs=[pl.BlockSpec((rows, d), lambda j: (0, 0)),
                  pl.BlockSpec((d, tn), lambda j: (0, j)),
                  pl.BlockSpec((1, tn), lambda j: (0, j))],
        out_specs=pl.BlockSpec((1, rows, tn), lambda j: (j // (d // tn), 0, j % (d // tn))),
        out_shape=jax.ShapeDtypeStruct((n_seg, rows, d), F32),
        compiler_params=_params(("parallel",)),
        name="mod",
    )(cc, w_mod, b_mod)


def _mod_block(n_seg, d, row_of):
    return pl.BlockSpec((n_seg, 8, d), lambda *idx: (0, row_of(*idx) * (MOD_ROW_STRIDE // 8), 0))


def _proj_kernel(*refs, n_w, n_extra, n_side, n_out, epilogue):
    h_ref = refs[0]
    pos = 1
    w_refs = refs[pos:pos + n_w]
    pos += n_w
    extra = refs[pos:pos + n_extra]
    pos += n_extra
    side_in = refs[pos:pos + n_side]
    pos += n_side
    out_refs = refs[pos:pos + n_out]
    pos += n_out
    side_out = refs[pos:pos + n_side]
    pos += n_side
    wbf_refs = refs[pos:] or w_refs

    if wbf_refs is not w_refs:
        @pl.when(pl.program_id(1) == 0)
        def _():
            for w_ref, wbf_ref in zip(w_refs, wbf_refs):
                wbf_ref[...] = w_ref[...].astype(BF16)

    for si_ref, so_ref in zip(side_in, side_out):
        so_ref[...] = si_ref[...].astype(BF16)

    h = h_ref[...]
    accs = [jnp.dot(h, wbf_ref[...], preferred_element_type=F32) for wbf_ref in wbf_refs]
    outs = epilogue(accs, [e_ref[...].reshape(e_ref.shape[-2:]) for e_ref in extra])
    for o_ref, o in zip(out_refs, outs):
        o_ref[...] = o.astype(o_ref.dtype)


def _ep_forget(accs, extra):
    f = _forget_gate(accs[0], extra[0])
    return jnp.log2(f), 1.0 - f


def _ep_scaled_silu(accs, extra):
    return (_silu(accs[0]) * jnp.where(pl.program_id(0) == 0, HEAD_DIM ** -0.5, 1.0),)


def _ep_glu(accs, extra):
    return (accs[0] * _sigmoid(accs[1]),)


def _ep_sigmoid(accs, extra):
    return (_sigmoid(accs[0]),)


def _proj_call(h, w, col_offsets, width, epilogue, out_dtypes, extra=(), side=(), tm=1024, tn=1024,
               name="proj"):
    m, k = h.shape
    tm = min(tm, m)
    n_w = len(col_offsets)
    n_j, n_i = width // tn, m // tm
    in_specs = [pl.BlockSpec((tm, k), lambda j, i: (i, 0))]
    for off in col_offsets:
        in_specs.append(pl.BlockSpec((k, tn), lambda j, i, off=off: (0, off // tn + j)))
    for e in extra:
        in_specs.append(pl.BlockSpec((1, e.shape[1], tn), lambda j, i: (j, 0, 0)))
    side_specs = [pl.BlockSpec((a.shape[0] // (n_j * n_i), a.shape[1]), lambda j, i: (j * n_i + i, 0))
                  for a in side]
    return pl.pallas_call(
        functools.partial(_proj_kernel, n_w=n_w, n_extra=len(extra), n_side=len(side),
                          n_out=len(out_dtypes), epilogue=epilogue),
        grid=(n_j, n_i),
        in_specs=in_specs + side_specs,
        out_specs=[pl.BlockSpec((tm, tn), lambda j, i: (i, j)) for _ in out_dtypes] + side_specs,
        out_shape=[jax.ShapeDtypeStruct((m, width), dt) for dt in out_dtypes]
                  + [jax.ShapeDtypeStruct(a.shape, BF16) for a in side],
        scratch_shapes=[pltpu.VMEM((k, tn), BF16) for _ in range(n_w)] if w.dtype != BF16 else [],
        compiler_params=_params(("parallel", "arbitrary")),
        name=name,
    )(h, *([w] * n_w), *extra, *side)


def _prenorm_v_kernel(x_ref, g_ref, mod_ref, w_ref, h_ref, v_ref):
    for r in range(x_ref.shape[1] // PRENORM_SLICE):
        rows = slice(r * PRENORM_SLICE, (r + 1) * PRENORM_SLICE)
        h = _norm_modulate(x_ref[0, rows, :], g_ref[...], mod_ref[0, 0:1, :], mod_ref[1, 0:1, :]).astype(BF16)
        h_ref[0, rows, :] = h
        v_ref[0, rows, :] = jnp.dot(h, w_ref[...], preferred_element_type=F32).astype(BF16)


PRENORM_SLICE = 256


def _prenorm_v_call(x, g, mod_early, w, width, tl=1024):
    b, l, d = x.shape
    return pl.pallas_call(
        _prenorm_v_kernel,
        grid=(b, l // tl),
        in_specs=[pl.BlockSpec((1, tl, d), lambda i, j: (i, j, 0)),
                  pl.BlockSpec((1, d), lambda i, j: (0, 0)),
                  _mod_block(N_MOD_EARLY, d, lambda i, j: i),
                  pl.BlockSpec((d, width), lambda i, j: (0, 0), pipeline_mode=pl.Buffered(1))],
        out_specs=[pl.BlockSpec((1, tl, d), lambda i, j: (i, j, 0)),
                   pl.BlockSpec((1, tl, width), lambda i, j: (i, j, 0))],
        out_shape=[jax.ShapeDtypeStruct((b, l, d), BF16), jax.ShapeDtypeStruct((b, l, width), BF16)],
        compiler_params=_params(("parallel", "parallel")),
        name="prenorm_v",
    )(x, g, mod_early, w)


def _scan_masks(c, reverse):
    t = lax.broadcasted_iota(jnp.int32, (c, c), 0)
    s = lax.broadcasted_iota(jnp.int32, (c, c), 1)
    half = c // 2
    before = (s >= t) if reverse else (s <= t)
    mask1 = ((t < half) == (s < half)) & before
    tri = jnp.where(before, 1.0, 0.0).astype(BF16)
    return mask1, jnp.concatenate([tri, tri], axis=1)


def _scan_cumsum(lf, tri2):
    hi = lf.astype(BF16)
    lo = (lf - hi.astype(F32)).astype(BF16)
    return jnp.dot(tri2, jnp.concatenate([hi, lo], axis=0), preferred_element_type=F32)


def _scan_refs(a, reverse):
    c = a.shape[0]
    half, quarter = c // 2, c // 4
    if reverse:
        return a[quarter:quarter + 1], a[half + quarter:half + quarter + 1], a[half:half + 1], a[0:1]
    return a[quarter - 1:quarter], a[half + quarter - 1:half + quarter], a[half - 1:half], a[c - 1:c]


def _scan_products(a, q, k, v, st, st_kv, reverse):
    c = a.shape[0]
    half = c // 2
    m1_lo, m1_hi, m0, a_tot = _scan_refs(a, reverse)
    lo_rows, hi_rows = slice(0, half), slice(half, c)
    row = lax.broadcasted_iota(jnp.int32, a.shape, 0)
    e1 = a - jnp.where(row < half, m1_lo, m1_hi)
    pq = q.astype(F32) * jnp.exp2(e1)
    pk = k.astype(F32) * jnp.exp2(-e1)
    s1 = lax.dot_general(pq.astype(BF16), pk.astype(BF16), _NT, preferred_element_type=F32)
    if reverse:
        q_rows, m1_q, k_rows, m1_k = lo_rows, m1_lo, hi_rows, m1_hi
    else:
        q_rows, m1_q, k_rows, m1_k = hi_rows, m1_hi, lo_rows, m1_lo
    q0 = (pq[q_rows] * jnp.exp2(m1_q - m0)).astype(BF16)
    k0 = (pk[k_rows] * jnp.exp2(m0 - m1_k)).astype(BF16)
    s0 = lax.dot_general(q0, k0, _NT, preferred_element_type=F32)
    qi = jnp.concatenate([pq[lo_rows] * jnp.exp2(m1_lo), pq[hi_rows] * jnp.exp2(m1_hi)], axis=0)
    inter = jnp.dot(qi.astype(BF16), st_kv, preferred_element_type=F32)
    kh = jnp.concatenate([pk[lo_rows] * jnp.exp2(a_tot - m1_lo), pk[hi_rows] * jnp.exp2(a_tot - m1_hi)], axis=0)
    upd = lax.dot_general(v, kh.astype(BF16), _TN, preferred_element_type=F32)
    return s1, s0, inter, st * jnp.exp2(a_tot) + upd


def _scan_output(s1, s0, inter, v, mask1, reverse):
    c = s1.shape[0]
    half = c // 2
    lo_rows, hi_rows = slice(0, half), slice(half, c)
    q_rows, k_rows = (lo_rows, hi_rows) if reverse else (hi_rows, lo_rows)
    intra = jnp.dot(jnp.where(mask1, s1, 0.0).astype(BF16), v, preferred_element_type=F32)
    cross = jnp.dot(s0.astype(BF16), v[k_rows], preferred_element_type=F32)
    o = inter + intra
    o_q = o[q_rows] + cross
    return jnp.concatenate([o_q, o[hi_rows]] if reverse else [o[lo_rows], o_q], axis=0)


def _scan_kernel(*refs, reverse, readout):
    k_ref, v_ref, lf_ref, s0_ref, q_ref = refs[:5]
    if readout:
        (ob_ref, g_ref, ng_ref, o_ref, st_ref, stkv_ref, a_ref) = refs[5:]
    else:
        (o_ref, st_ref, stkv_ref, a_ref) = refs[5:]
    c = SCAN_CHUNK
    n_batch, rows = k_ref.shape[0], k_ref.shape[1]
    n_chunks = rows // c
    n_heads = k_ref.shape[2] // HEAD_DIM
    recurrences = [(h, bi) for h in range(n_heads) for bi in range(n_batch)]

    @pl.when(pl.program_id(0) == 0)
    def _():
        st_ref[...] = s0_ref[...]
        for h, bi in recurrences:
            stkv_ref[bi, h] = s0_ref[bi, h].T.astype(BF16)

    mask1, tri2 = _scan_masks(c, reverse)

    def cumsum_body(ci, carry):
        r0 = pl.multiple_of(ci * c, c)
        for bi in range(n_batch):
            a_ref[bi, pl.ds(r0, c), :] = _scan_cumsum(lf_ref[bi, pl.ds(r0, c), :], tri2)
        return carry

    lax.fori_loop(0, n_chunks, cumsum_body, 0, unroll=4)

    def chunk_body(ci, carry):
        cidx = (n_chunks - 1 - ci) if reverse else ci
        r0 = pl.multiple_of(cidx * c, c)
        tile = lambda ref, h, bi: ref[bi, pl.ds(r0, c), h * HEAD_DIM:(h + 1) * HEAD_DIM]
        wave = []
        for h, bi in recurrences:
            s1, s0, inter, st_new = _scan_products(
                tile(a_ref, h, bi), tile(q_ref, h, bi), tile(k_ref, h, bi), tile(v_ref, h, bi),
                st_ref[bi, h], stkv_ref[bi, h], reverse)
            st_ref[bi, h] = st_new
            stkv_ref[bi, h] = st_new.T.astype(BF16)
            wave.append((s1, s0, inter))
        for (h, bi), (s1, s0, inter) in zip(recurrences, wave):
            lanes = slice(h * HEAD_DIM, (h + 1) * HEAD_DIM)
            o = _scan_output(s1, s0, inter, tile(v_ref, h, bi), mask1, reverse)
            if readout:
                o = o + tile(ob_ref, h, bi)
                ms = jnp.mean(o * o, axis=-1, keepdims=True)
                y = o * lax.rsqrt(ms + EPS) * ng_ref[:, lanes]
                o = y * tile(g_ref, h, bi).astype(F32)
            o_ref[bi, pl.ds(r0, c), lanes] = o.astype(o_ref.dtype)
        return carry

    lax.fori_loop(0, n_chunks, chunk_body, 0, unroll=4)


def _scan_call(k, v, lf, s0, q, *, reverse, ob=None, g=None, ng=None):
    b, l, _ = v[0].shape
    dh = ng.shape[1] if ng is not None else s0.shape[1] * HEAD_DIM
    tb = min(SCAN_BLOCK, l)
    n_blocks = l // tb
    readout = ob is not None
    blk = (lambda j: n_blocks - 1 - j) if reverse else (lambda j: j)
    seq_spec = lambda group: pl.BlockSpec((b, tb, dh), lambda j: (0, blk(j), group))
    st_spec = pl.BlockSpec(s0.shape, lambda j: (0, 0, 0, 0))
    in_specs = [seq_spec(k[1]), seq_spec(v[1]), seq_spec(lf[1]), st_spec, seq_spec(q[1])]
    args = [k[0], v[0], lf[0], s0, q[0]]
    if readout:
        in_specs += [seq_spec(ob[1]), seq_spec(g[1]), pl.BlockSpec((1, dh), lambda j: (0, 0))]
        args += [ob[0], g[0], ng]
    return pl.pallas_call(
        functools.partial(_scan_kernel, reverse=reverse, readout=readout),
        grid=(n_blocks,),
        in_specs=in_specs,
        out_specs=seq_spec(0),
        out_shape=jax.ShapeDtypeStruct((b, l, dh), BF16 if readout else F32),
        scratch_shapes=[pltpu.VMEM(s0.shape, F32), pltpu.VMEM(s0.shape, BF16), pltpu.VMEM((b, tb, dh), F32)],
        compiler_params=_params(("arbitrary",)),
        name="scan_readout_fwd" if readout else "scan_out_bwd",
    )(*args)


def _ctx_proj_kernel(ctx_ref, g_ref, mod_ref, w_ref, o_ref, wbf_ref, hc_ref):
    @pl.when(pl.program_id(0) == 0)
    def _():
        for bi in range(ctx_ref.shape[0]):
            hc_ref[bi] = _norm_modulate(ctx_ref[bi], g_ref[...], mod_ref[0, 0:1, :],
                                        mod_ref[1, 0:1, :]).astype(BF16)

    w = w_ref[...].astype(BF16)
    wbf_ref[...] = w
    for bi in range(ctx_ref.shape[0]):
        o_ref[bi] = jnp.dot(hc_ref[bi], w, preferred_element_type=F32)


def _ctx_proj_call(ctx, g, mod_early, mod_row, w, width, tn=1024):
    b, lc, d = ctx.shape
    return pl.pallas_call(
        _ctx_proj_kernel,
        grid=(width // tn,),
        in_specs=[pl.BlockSpec((b, lc, d), lambda j: (0, 0, 0)),
                  pl.BlockSpec((1, d), lambda j: (0, 0)),
                  _mod_block(N_MOD_EARLY, d, lambda j: mod_row),
                  pl.BlockSpec((d, tn), lambda j: (0, j))],
        out_specs=[pl.BlockSpec((b, lc, tn), lambda j: (0, 0, j)), pl.BlockSpec((d, tn), lambda j: (0, j))],
        out_shape=[jax.ShapeDtypeStruct((b, lc, width), F32), jax.ShapeDtypeStruct((d, width), BF16)],
        scratch_shapes=[pltpu.VMEM((b, lc, d), BF16)],
        compiler_params=_params(("arbitrary",)),
        name="ctx_proj",
    )(ctx, g, mod_early, w)


def _ctx_state_kernel(pv_ref, pf_ref, pb_ref, lg_ref, sf_ref, sb_ref):
    c = SCAN_CHUNK
    n_batch, rows = pv_ref.shape[0], pv_ref.shape[1]
    n_chunks = rows // c
    n_heads = pv_ref.shape[2] // HEAD_DIM
    for reverse, p_ref, s_ref in ((False, pf_ref, sf_ref), (True, pb_ref, sb_ref)):
        _, tri2 = _scan_masks(c, reverse)
        s_ref[...] = jnp.zeros(s_ref.shape, F32)

        def chunk_body(ci, carry, reverse=reverse, p_ref=p_ref, s_ref=s_ref, tri2=tri2):
            cidx = (n_chunks - 1 - ci) if reverse else ci
            r0 = pl.multiple_of(cidx * c, c)
            recurrences = [(h, bi) for h in range(n_heads) for bi in range(n_batch)]
            tile = lambda ref, h, bi: ref[bi, pl.ds(r0, c), h * HEAD_DIM:(h + 1) * HEAD_DIM]
            gates = [_forget_gate(tile(p_ref, h, bi), lg_ref[int(reverse), :, h * HEAD_DIM:(h + 1) * HEAD_DIM])
                     for h, bi in recurrences]
            sums = [_scan_cumsum(jnp.log2(f), tri2) for f in gates]
            for (h, bi), f, a in zip(recurrences, gates, sums):
                a_tot = _scan_refs(a, reverse)[3]
                kh = ((1.0 - f) * jnp.exp2(a_tot - a)).astype(BF16)
                upd = lax.dot_general(tile(pv_ref, h, bi).astype(BF16), kh, _TN, preferred_element_type=F32)
                s_ref[bi, h] = s_ref[bi, h] * jnp.exp2(a_tot) + upd
            return carry

        lax.fori_loop(0, n_chunks, chunk_body, 0)


def _ctx_state_call(pc, lb_logits, dh):
    b, lc, _ = pc.shape
    n_heads = dh // HEAD_DIM
    group = lambda n: pl.BlockSpec((b, lc, dh), lambda j: (0, 0, n))
    st_spec = pl.BlockSpec((b, n_heads, HEAD_DIM, HEAD_DIM), lambda j: (0, 0, 0, 0))
    st_shape = jax.ShapeDtypeStruct((b, n_heads, HEAD_DIM, HEAD_DIM), F32)
    return pl.pallas_call(
        _ctx_state_kernel,
        grid=(1,),
        in_specs=[group(0), group(1), group(2), pl.BlockSpec(lb_logits.shape, lambda j: (0, 0, 0))],
        out_specs=[st_spec, st_spec],
        out_shape=[st_shape, st_shape],
        compiler_params=_params(("arbitrary",)),
        name="ctx_state",
    )(pc, pc, pc, lb_logits)


CONV_TILE_ROWS = 16
_HPAD = 16
_HROW = GRID_W + 2 * _HPAD


def _conv_kernel(cur_ref, prev_ref, next_ref, w_ref, b_ref, lg_ref, lb_ref, cc_ref, wm_ref, bm_ref,
                 o_ref, modl_ref, hpad_ref, vpad_ref, y_ref, *, n_tiles):
    tr = CONV_TILE_ROWS
    half = w_ref.shape[1] // 2
    n_lane_tiles = half // 128
    i = pl.program_id(1)

    modl_ref[0] = _mod_rows(cc_ref, wm_ref, bm_ref)

    hpad_ref[:, :, :_HPAD, :] = jnp.zeros((tr, n_lane_tiles, _HPAD, 128), F32)
    hpad_ref[:, :, _HPAD + GRID_W:, :] = jnp.zeros((tr, n_lane_tiles, _HPAD, 128), F32)
    for r in range(tr):
        for lt in range(n_lane_tiles):
            hpad_ref[r, lt, _HPAD:_HPAD + GRID_W, :] = cur_ref[0, r * GRID_W:(r + 1) * GRID_W,
                                                               lt * 128:(lt + 1) * 128]
    halo = CONV_PAD * GRID_W
    for lt in range(n_lane_tiles):
        lanes = slice(lt * 128, (lt + 1) * 128)
        top = prev_ref[0, tr * GRID_W - halo:, lanes]
        bot = next_ref[0, :halo, lanes]
        vpad_ref[lt, :halo, :] = jnp.where(i > 0, top, 0.0)
        vpad_ref[lt, halo:halo + tr * GRID_W, :] = cur_ref[0, :, half + lt * 128:half + (lt + 1) * 128]
        vpad_ref[lt, halo + tr * GRID_W:, :] = jnp.where(i < n_tiles - 1, bot, 0.0)

    def h_row(r, carry):
        o0 = pl.multiple_of(r * GRID_W, GRID_W)
        for lt in range(n_lane_tiles):
            lanes = slice(lt * 128, (lt + 1) * 128)
            acc = jnp.zeros((GRID_W, 128), F32)
            for jj in range(CONV_WIDTH):
                off = _HPAD - CONV_PAD + jj
                acc = acc + hpad_ref[r, lt, off:off + GRID_W, :] * w_ref[jj:jj + 1, lanes]
            y_ref[pl.ds(o0, GRID_W), lanes] = acc + b_ref[:, lanes]
        return carry

    lax.fori_loop(0, tr, h_row, 0)

    n_sub = GRID_W // 8
    for lt in range(n_lane_tiles):
        vl = slice(half + lt * 128, half + (lt + 1) * 128)
        wv = [jnp.broadcast_to(w_ref[jj:jj + 1, vl], (8, 128)) for jj in range(CONV_WIDTH)]
        bias = b_ref[:, vl]

        def v_row(r, carry, lt=lt, vl=vl, wv=wv, bias=bias):
            v0 = pl.multiple_of(r * GRID_W, GRID_W)
            acc = [jnp.zeros((8, 128), F32)] * n_sub
            for jj in range(CONV_WIDTH):
                src = pl.multiple_of(v0 + jj * GRID_W, GRID_W)
                tap = vpad_ref[lt, pl.ds(src, GRID_W), :]
                acc = [acc[s] + tap[8 * s:8 * s + 8, :] * wv[jj] for s in range(n_sub)]
            y_ref[pl.ds(v0, GRID_W), vl] = jnp.concatenate(acc, axis=0) + bias
            return carry

        lax.fori_loop(0, tr, v_row, 0, unroll=2)

    def ln_row(r, carry):
        o0 = pl.multiple_of(r * GRID_W, GRID_W)
        y = y_ref[pl.ds(o0, GRID_W), :]
        mu = jnp.mean(y, axis=-1, keepdims=True)
        yc = y - mu
        var = jnp.mean(yc * yc, axis=-1, keepdims=True)
        z = yc * lax.rsqrt(var + EPS) * lg_ref[...] + lb_ref[...]
        o_ref[0, pl.ds(o0, GRID_W), :] = _silu(z).astype(BF16)
        return carry

    lax.fori_loop(0, tr, ln_row, 0, unroll=4)


def _conv_call(u, w, bias, ln_g, ln_b, cc, w_mod, b_mod):
    b, l, ch = u.shape
    half = ch // 2
    tile = CONV_TILE_ROWS * GRID_W
    n_tiles = l // tile
    rows, d = cc.shape
    n_late = w_mod.shape[1] // d - N_MOD_EARLY
    tnm = n_late * d // (b * n_tiles)
    per = d // tnm
    step = lambda i, j: i * n_tiles + j
    return pl.pallas_call(
        functools.partial(_conv_kernel, n_tiles=n_tiles),
        grid=(b, n_tiles),
        in_specs=[pl.BlockSpec((1, tile, ch), lambda i, j: (i, j, 0)),
                  pl.BlockSpec((1, tile, half), lambda i, j: (i, jnp.maximum(j - 1, 0), 1)),
                  pl.BlockSpec((1, tile, half), lambda i, j: (i, jnp.minimum(j + 1, n_tiles - 1), 1)),
                  pl.BlockSpec((CONV_WIDTH, ch), lambda i, j: (0, 0)),
                  pl.BlockSpec((1, ch), lambda i, j: (0, 0)),
                  pl.BlockSpec((1, ch), lambda i, j: (0, 0)),
                  pl.BlockSpec((1, ch), lambda i, j: (0, 0)),
                  pl.BlockSpec((rows, d), lambda i, j: (0, 0)),
                  pl.BlockSpec((d, tnm), lambda i, j: (0, N_MOD_EARLY * per + step(i, j))),
                  pl.BlockSpec((1, tnm), lambda i, j: (0, N_MOD_EARLY * per + step(i, j)))],
        out_specs=[pl.BlockSpec((1, tile, ch), lambda i, j: (i, j, 0)),
                   pl.BlockSpec((1, rows, tnm), lambda i, j: (step(i, j) // per, 0, step(i, j) % per))],
        out_shape=[jax.ShapeDtypeStruct((b, l, ch), BF16), jax.ShapeDtypeStruct((n_late, rows, d), F32)],
        scratch_shapes=[pltpu.VMEM((CONV_TILE_ROWS, half // 128, _HROW, 128), F32),
                        pltpu.VMEM((half // 128, tile + 2 * CONV_PAD * GRID_W, 128), F32),
                        pltpu.VMEM((tile, ch), F32)],
        compiler_params=_params(("arbitrary", "arbitrary")),
        name="conv",
    )(u, u, u, w, bias, ln_g, ln_b, cc, w_mod, b_mod)


MIX_SLICE = 256

def _mix_kernel(uc_ref, og_ref, gc_ref, gh_ref, x_ref, wpw_ref, wh_ref, wo_ref,
                npost_ref, npre_ref, mod_ref, x1_ref, h2_ref):
    tm = x_ref.shape[1]
    n_split = tm // MIX_SLICE
    for r in range(n_split):
        rows = slice(r * MIX_SLICE, (r + 1) * MIX_SLICE)
        y_c = jnp.dot(uc_ref[0, rows, :], wpw_ref[...], preferred_element_type=F32)
        y_h = jnp.dot(og_ref[0, rows, :], wh_ref[...], preferred_element_type=F32)
        z = gc_ref[0, rows, :].astype(F32) * y_c + gh_ref[0, rows, :].astype(F32) * y_h
        y = jnp.dot(z.astype(BF16), wo_ref[...], preferred_element_type=F32)
        ms = jnp.mean(y * y, axis=-1, keepdims=True)
        yn = y * lax.rsqrt(ms + EPS) * npost_ref[...]
        x1 = x_ref[0, rows, :] + mod_ref[0, 0:1, :] * yn
        x1_ref[0, rows, :] = x1
        h2_ref[0, rows, :] = _norm_modulate(x1, npre_ref[...], mod_ref[1, 0:1, :],
                                            mod_ref[2, 0:1, :]).astype(BF16)


def _mix_call(uc, og, gates, x, wpw, wh, wo, npost, npre, mod_late, tm=512):
    b, l, d = x.shape
    dc = uc.shape[2]
    gc = gh = gates
    row = lambda width, col=0: pl.BlockSpec((1, tm, width), lambda i, j: (i, j, col))
    const = lambda shape: pl.BlockSpec(shape, lambda i, j: (0,) * len(shape), pipeline_mode=pl.Buffered(1))
    return pl.pallas_call(
        _mix_kernel,
        grid=(b, l // tm),
        in_specs=[row(dc), row(dc), row(d, 0), row(d, 1), row(d),
                  const(wpw.shape), const(wh.shape), const(wo.shape),
                  const((1, d)), const((1, d)),
                  _mod_block(N_MOD - N_MOD_EARLY, d, lambda i, j: i)],
        out_specs=[row(d), row(d)],
        out_shape=[jax.ShapeDtypeStruct((b, l, d), F32), jax.ShapeDtypeStruct((b, l, d), BF16)],
        compiler_params=_params(("parallel", "parallel")),
        name="mix",
    )(uc, og, gc, gh, x, wpw, wh, wo, npost, npre, mod_late)


MLP_SLICE = 256

def _mlp_kernel(h_ref, w1_ref, w2_ref, x1_ref, npost_ref, mod_ref, o_ref, *, n_ff):
    j = pl.program_id(2)

    def partial_sum(rows):
        a = jnp.dot(h_ref[0, rows, :], w1_ref[...], preferred_element_type=F32)
        a = jnp.square(jnp.maximum(a, 0.0)).astype(BF16)
        return jnp.dot(a, w2_ref[...], preferred_element_type=F32)

    @pl.when(j == 0)
    def _():
        o_ref[0] = partial_sum(slice(None))

    @pl.when((j > 0) & (j < n_ff - 1))
    def _():
        o_ref[0] += partial_sum(slice(None))

    @pl.when(j == n_ff - 1)
    def _():
        for r in range(o_ref.shape[1] // MLP_SLICE):
            rows = slice(r * MLP_SLICE, (r + 1) * MLP_SLICE)
            y = o_ref[0, rows, :] + partial_sum(rows)
            ms = jnp.mean(y * y, axis=-1, keepdims=True)
            yn = y * lax.rsqrt(ms + EPS) * npost_ref[...]
            o_ref[0, rows, :] = x1_ref[0, rows, :] + mod_ref[3, 0:1, :] * yn


def _mlp_call(h2, w1, w2, x1, npost, mod_late, tm=1024, tf=1024):
    b, l, d = x1.shape
    dff = w1.shape[1]
    n_ff = dff // tf
    assert n_ff >= 2, "first and last d_ff steps are distinct code paths"
    return pl.pallas_call(
        functools.partial(_mlp_kernel, n_ff=n_ff),
        grid=(b, l // tm, n_ff),
        in_specs=[pl.BlockSpec((1, tm, d), lambda i, m, j: (i, m, 0)),
                  pl.BlockSpec((d, tf), lambda i, m, j: (0, j)),
                  pl.BlockSpec((tf, d), lambda i, m, j: (j, 0)),
                  pl.BlockSpec((1, tm, d), lambda i, m, j: (i, m, 0), pipeline_mode=pl.Buffered(1)),
                  pl.BlockSpec((1, d), lambda i, m, j: (0, 0)),
                  _mod_block(N_MOD - N_MOD_EARLY, d, lambda i, m, j: i)],
        out_specs=pl.BlockSpec((1, tm, d), lambda i, m, j: (i, m, 0)),
        out_shape=jax.ShapeDtypeStruct((b, l, d), F32),
        compiler_params=_params(("parallel", "parallel", "arbitrary")),
        name="mlp",
    )(h2, w1, w2, x1, npost, mod_late)


def kernel(x, c, ctx, c_ctx, w_mod, b_mod, norm_pre_mix, norm_post_mix, norm_pre_mlp, norm_post_mlp, w_in,
           conv_dw_w, conv_dw_b, conv_ln_g, conv_ln_b, conv_pw_w, hgrn_lb_logits, hgrn_norm_g, hgrn_out_w,
           w_out, mlp_w1, mlp_w2):
    assert w_mod.shape[0] == 1, "single-layer block"
    b, l, d = x.shape
    lc = ctx.shape[1]
    dh = hgrn_norm_g.shape[1]
    dc = conv_dw_w.shape[2]
    assert l % (CONV_TILE_ROWS * GRID_W) == 0 and CONV_PAD <= CONV_TILE_ROWS and CONV_PAD <= _HPAD
    assert l % SCAN_BLOCK == 0 and lc % SCAN_CHUNK == 0

    cond = jnp.concatenate([c, c_ctx[None, :]], axis=0)
    cc = jnp.pad(cond[:, None, :], ((0, 0), (0, MOD_ROW_STRIDE - 1), (0, 0))).reshape(-1, d)
    mod_early = _mod_call(cc, w_mod[0], b_mod, N_MOD_EARLY)

    w_in0 = w_in[0]
    off = lambda n: n * dh

    pc, w_vf = _ctx_proj_call(ctx, norm_pre_mix, mod_early, b, w_in0, 3 * dh)
    s_f, s_b = _ctx_state_call(pc, hgrn_lb_logits, dh)

    h, v = _prenorm_v_call(x, norm_pre_mix, mod_early, w_vf, dh)
    h = h.reshape(b * l, d)
    seq = lambda t: t.reshape(b, l, t.shape[-1])
    lf, kk = _proj_call(h, w_vf, [off(1)], 2 * dh, _ep_forget, [F32, BF16],
                        extra=(hgrn_lb_logits,), name="proj_forget")
    qg, wpw, wh, wo = _proj_call(h, w_in0, [off(3)], 2 * dh, _ep_scaled_silu, [BF16],
                                 side=(conv_pw_w[0], hgrn_out_w[0], w_out[0]), name="proj_qg")
    (u,) = _proj_call(h, w_in0, [off(5), off(5) + dc], dc, _ep_glu, [F32], tn=512, name="proj_glu")
    gates, w1, w2 = _proj_call(h, w_in0, [off(5) + 2 * dc], 2 * d, _ep_sigmoid, [BF16],
                               side=(mlp_w1[0], mlp_w2[0]), name="proj_gates")

    lf, kk, qg = seq(lf), seq(kk), seq(qg)
    o_b = _scan_call((kk, 1), (v, 0), (lf, 1), s_b, (qg, 0), reverse=True)
    og = _scan_call((kk, 0), (v, 0), (lf, 0), s_f, (qg, 0), reverse=False, ob=(o_b, 0), g=(qg, 1),
                    ng=hgrn_norm_g)

    uc, mod_late = _conv_call(seq(u), conv_dw_w[0], conv_dw_b, conv_ln_g, conv_ln_b, cc, w_mod[0], b_mod)

    x1, h2 = _mix_call(uc, og, seq(gates), x, wpw, wh, wo, norm_post_mix, norm_pre_mlp, mod_late)
    return _mlp_call(h2, w1, w2, x1, norm_post_mlp, mod_late)
```

```python
import functools

import jax
import jax.numpy as jnp
from jax import lax
from jax.experimental import pallas as pl
from jax.experimental.pallas import tpu as pltpu

F32 = jnp.float32
BF16 = jnp.bfloat16

GRID_W = 64
CONV_WIDTH = 31
CONV_PAD = (CONV_WIDTH - 1) // 2
HEAD_DIM = 128
N_MOD = 6
N_MOD_EARLY = 2
MOD_ROW_STRIDE = 16
EPS = 1e-6

SCAN_CHUNK = 64
SCAN_BLOCK = 512

VMEM_LIMIT = 56 * 1024 * 1024

_NT = (((1,), (1,)), ((), ()))
_TN = (((0,), (0,)), ((), ()))


def _params(sem):
    return pltpu.CompilerParams(dimension_semantics=sem, vmem_limit_bytes=VMEM_LIMIT)


def _sigmoid(x):
    return 0.5 * jnp.tanh(0.5 * x) + 0.5


def _silu(x):
    return x * _sigmoid(x)


def _norm_modulate(x, gain, shift, scale):
    ms = jnp.mean(x * x, axis=-1, keepdims=True)
    return (x * lax.rsqrt(ms + EPS) * gain) * (1.0 + scale) + shift


def _forget_gate(raw, logits):
    e = jnp.exp(logits - jnp.max(logits, axis=0, keepdims=True))
    lb = e[0:1] / jnp.sum(e, axis=0, keepdims=True)
    return lb + (1.0 - lb) * jax.nn.sigmoid(raw)


def _mod_rows(c_ref, w_ref, b_ref):
    s = _silu(c_ref[...]).astype(BF16)
    return jnp.dot(s, w_ref[...].astype(BF16), preferred_element_type=F32) + b_ref[...]


def _mod_kernel(c_ref, w_ref, b_ref, o_ref):
    o_ref[0] = _mod_rows(c_ref, w_ref, b_ref)


def _mod_call(cc, w_mod, b_mod, n_seg, tn=1024):
    rows, d = cc.shape
    return pl.pallas_call(
        _mod_kernel,
        grid=(n_seg * d // tn,),
        in_specs=[pl.BlockSpec((rows, d), lambda j: (0, 0)),
                  pl.BlockSpec((d, tn), lambda j: (0, j)),
                  pl.BlockSpec((1, tn), lambda j: (0, j))],
        out_specs=pl.BlockSpec((1, rows, tn), lambda j: (j // (d // tn), 0, j % (d // tn))),
        out_shape=jax.ShapeDtypeStruct((n_seg, rows, d), F32),
        compiler_params=_params(("parallel",)),
        name="mod",
    )(cc, w_mod, b_mod)


def _mod_block(n_seg, d, row_of):
    return pl.BlockSpec((n_seg, 8, d), lambda *idx: (0, row_of(*idx) * (MOD_ROW_STRIDE // 8), 0))


def _proj_kernel(*refs, n_w, n_side, n_out, epilogue):
    h_ref = refs[0]
    pos = 1
    w_refs = refs[pos:pos + n_w]
    pos += n_w
    side_in = refs[pos:pos + n_side]
    pos += n_side
    out_refs = refs[pos:pos + n_out]
    pos += n_out
    side_out = refs[pos:pos + n_side]
    pos += n_side
    wbf_refs = refs[pos:] or w_refs

    if wbf_refs is not w_refs:
        @pl.when(pl.program_id(1) == 0)
        def _():
            for w_ref, wbf_ref in zip(w_refs, wbf_refs):
                wbf_ref[...] = w_ref[...].astype(BF16)

    for si_ref, so_ref in zip(side_in, side_out):
        so_ref[...] = si_ref[...].astype(BF16)

    h = h_ref[...]
    accs = [jnp.dot(h, wbf_ref[...], preferred_element_type=F32) for wbf_ref in wbf_refs]
    outs = epilogue(accs)
    for o_ref, o in zip(out_refs, outs):
        o_ref[...] = o.astype(o_ref.dtype)


def _ep_scaled_silu(accs):
    return (_silu(accs[0]) * jnp.where(pl.program_id(0) == 0, HEAD_DIM ** -0.5, 1.0),)


def _ep_glu(accs):
    return (accs[0] * _sigmoid(accs[1]),)


def _ep_sigmoid(accs):
    return (_sigmoid(accs[0]),)


SIDE_COL_TILE = 1024


def _proj_call(h, w, col_offsets, width, epilogue, out_dtypes, side=(), tm=1024, tn=1024, name="proj"):
    m, k = h.shape
    tm = min(tm, m)
    n_w = len(col_offsets)
    n_j, n_i = width // tn, m // tm
    in_specs = [pl.BlockSpec((tm, k), lambda j, i: (i, 0))]
    for off in col_offsets:
        in_specs.append(pl.BlockSpec((k, tn), lambda j, i, off=off: (0, off // tn + j)))
    side_specs = [pl.BlockSpec((a.shape[0] // (n_j * n_i), a.shape[1]), lambda j, i: (j * n_i + i, 0))
                  for a in side]
    return pl.pallas_call(
        functools.partial(_proj_kernel, n_w=n_w, n_side=len(side), n_out=len(out_dtypes), epilogue=epilogue),
        grid=(n_j, n_i),
        in_specs=in_specs + side_specs,
        out_specs=[pl.BlockSpec((tm, tn), lambda j, i: (i, j)) for _ in out_dtypes] + side_specs,
        out_shape=[jax.ShapeDtypeStruct((m, width), dt) for dt in out_dtypes]
                  + [jax.ShapeDtypeStruct(a.shape, BF16) for a in side],
        scratch_shapes=[pltpu.VMEM((k, tn), BF16) for _ in range(n_w)] if w.dtype != BF16 else [],
        compiler_params=_params(("parallel", "arbitrary")),
        name=name,
    )(h, *([w] * n_w), *side)


PRENORM_SLICE = 128


def _prenorm_vf_kernel(x_ref, g_ref, mod_ref, w_ref, lg_ref, wside_ref,
                       h_ref, v_ref, lf_ref, wside_out_ref):
    wside_out_ref[...] = wside_ref[...].astype(BF16)
    dh = v_ref.shape[2]
    for r in range(x_ref.shape[1] // PRENORM_SLICE):
        rows = slice(r * PRENORM_SLICE, (r + 1) * PRENORM_SLICE)
        h = _norm_modulate(x_ref[0, rows, :], g_ref[...], mod_ref[0, 0:1, :], mod_ref[1, 0:1, :]).astype(BF16)
        h_ref[0, rows, :] = h
        acc = jnp.dot(h, w_ref[...], preferred_element_type=F32)
        v_ref[0, rows, :] = acc[:, :dh].astype(BF16)
        for dirn in range(2):
            f = _forget_gate(acc[:, (1 + dirn) * dh:(2 + dirn) * dh], lg_ref[dirn])
            lf_ref[0, rows, dirn * dh:(dirn + 1) * dh] = jnp.log2(f)


def _prenorm_vf_call(x, g, mod_early, w_vf, lb_logits, w_in, col0, tl=512):
    b, l, d = x.shape
    dh = w_vf.shape[1] // 3
    n_steps = b * (l // tl)
    ct = SIDE_COL_TILE
    n_cb = (w_in.shape[1] - col0) // ct
    rs = n_steps // n_cb
    assert col0 % ct == 0 and n_cb * rs == n_steps and w_in.shape[0] % rs == 0
    rb = w_in.shape[0] // rs
    step = lambda i, j: i * (l // tl) + j
    row = lambda width: pl.BlockSpec((1, tl, width), lambda i, j: (i, j, 0))
    return pl.pallas_call(
        _prenorm_vf_kernel,
        grid=(b, l // tl),
        in_specs=[row(d),
                  pl.BlockSpec((1, d), lambda i, j: (0, 0)),
                  _mod_block(N_MOD_EARLY, d, lambda i, j: i),
                  pl.BlockSpec(w_vf.shape, lambda i, j: (0, 0), pipeline_mode=pl.Buffered(1)),
                  pl.BlockSpec(lb_logits.shape, lambda i, j: (0, 0, 0)),
                  pl.BlockSpec((rb, ct), lambda i, j: (step(i, j) % rs, col0 // ct + step(i, j) // rs))],
        out_specs=[row(d), row(dh), row(2 * dh),
                   pl.BlockSpec((rb, ct), lambda i, j: (step(i, j) % rs, step(i, j) // rs))],
        out_shape=[jax.ShapeDtypeStruct((b, l, d), BF16), jax.ShapeDtypeStruct((b, l, dh), BF16),
                   jax.ShapeDtypeStruct((b, l, 2 * dh), F32),
                   jax.ShapeDtypeStruct((w_in.shape[0], w_in.shape[1] - col0), BF16)],
        compiler_params=_params(("parallel", "parallel")),
        name="prenorm_vf",
    )(x, g, mod_early, w_vf, lb_logits, w_in)


def _scan_masks(c, reverse):
    t = lax.broadcasted_iota(jnp.int32, (c, c), 0)
    s = lax.broadcasted_iota(jnp.int32, (c, c), 1)
    half = c // 2
    before = (s >= t) if reverse else (s <= t)
    mask1 = ((t < half) == (s < half)) & before
    tri = jnp.where(before, 1.0, 0.0).astype(BF16)
    return mask1, jnp.concatenate([tri, tri], axis=1)


def _scan_cumsum(lf, tri2):
    hi = lf.astype(BF16)
    lo = (lf - hi.astype(F32)).astype(BF16)
    return jnp.dot(tri2, jnp.concatenate([hi, lo], axis=0), preferred_element_type=F32)


def _scan_refs(a, reverse):
    c = a.shape[0]
    half, quarter = c // 2, c // 4
    if reverse:
        return a[quarter:quarter + 1], a[half + quarter:half + quarter + 1], a[half:half + 1], a[0:1]
    return a[quarter - 1:quarter], a[half + quarter - 1:half + quarter], a[half - 1:half], a[c - 1:c]


def _scan_products(a, q, lf, v, st, st_kv, reverse):
    c = a.shape[0]
    half = c // 2
    m1_lo, m1_hi, m0, a_tot = _scan_refs(a, reverse)
    lo_rows, hi_rows = slice(0, half), slice(half, c)
    row = lax.broadcasted_iota(jnp.int32, a.shape, 0)
    e1 = a - jnp.where(row < half, m1_lo, m1_hi)
    pq = q.astype(F32) * jnp.exp2(e1)
    pk = (1.0 - jnp.exp2(lf)) * jnp.exp2(-e1)
    s1 = lax.dot_general(pq.astype(BF16), pk.astype(BF16), _NT, preferred_element_type=F32)
    if reverse:
        q_rows, m1_q, k_rows, m1_k = lo_rows, m1_lo, hi_rows, m1_hi
    else:
        q_rows, m1_q, k_rows, m1_k = hi_rows, m1_hi, lo_rows, m1_lo
    q0 = (pq[q_rows] * jnp.exp2(m1_q - m0)).astype(BF16)
    k0 = (pk[k_rows] * jnp.exp2(m0 - m1_k)).astype(BF16)
    s0 = lax.dot_general(q0, k0, _NT, preferred_element_type=F32)
    qi = jnp.concatenate([pq[lo_rows] * jnp.exp2(m1_lo), pq[hi_rows] * jnp.exp2(m1_hi)], axis=0)
    inter = jnp.dot(qi.astype(BF16), st_kv, preferred_element_type=F32)
    kh = jnp.concatenate([pk[lo_rows] * jnp.exp2(a_tot - m1_lo), pk[hi_rows] * jnp.exp2(a_tot - m1_hi)], axis=0)
    upd = lax.dot_general(v, kh.astype(BF16), _TN, preferred_element_type=F32)
    return s1, s0, inter, st * jnp.exp2(a_tot) + upd


def _scan_output(s1, s0, inter, v, mask1, reverse):
    c = s1.shape[0]
    half = c // 2
    lo_rows, hi_rows = slice(0, half), slice(half, c)
    q_rows, k_rows = (lo_rows, hi_rows) if reverse else (hi_rows, lo_rows)
    intra = jnp.dot(jnp.where(mask1, s1, 0.0).astype(BF16), v, preferred_element_type=F32)
    cross = jnp.dot(s0.astype(BF16), v[k_rows], preferred_element_type=F32)
    o = inter + intra
    o_q = o[q_rows] + cross
    return jnp.concatenate([o_q, o[hi_rows]] if reverse else [o[lo_rows], o_q], axis=0)


def _scan_kernel(*refs, reverse, readout):
    v_ref, lf_ref, s0_ref, q_ref = refs[:4]
    if readout:
        (ob_ref, g_ref, ng_ref, o_ref, st_ref, stkv_ref, a_ref) = refs[4:]
    else:
        (o_ref, st_ref, stkv_ref, a_ref) = refs[4:]
    c = SCAN_CHUNK
    n_batch, rows = v_ref.shape[0], v_ref.shape[1]
    n_chunks = rows // c
    n_heads = v_ref.shape[2] // HEAD_DIM
    recurrences = [(h, bi) for h in range(n_heads) for bi in range(n_batch)]

    @pl.when(pl.program_id(0) == 0)
    def _():
        st_ref[...] = s0_ref[...]
        for h, bi in recurrences:
            stkv_ref[bi, h] = s0_ref[bi, h].T.astype(BF16)

    mask1, tri2 = _scan_masks(c, reverse)

    def cumsum_body(ci, carry):
        r0 = pl.multiple_of(ci * c, c)
        for bi in range(n_batch):
            a_ref[bi, pl.ds(r0, c), :] = _scan_cumsum(lf_ref[bi, pl.ds(r0, c), :], tri2)
        return carry

    lax.fori_loop(0, n_chunks, cumsum_body, 0, unroll=4)

    def chunk_body(ci, carry):
        cidx = (n_chunks - 1 - ci) if reverse else ci
        r0 = pl.multiple_of(cidx * c, c)
        tile = lambda ref, h, bi: ref[bi, pl.ds(r0, c), h * HEAD_DIM:(h + 1) * HEAD_DIM]
        wave = []
        for h, bi in recurrences:
            s1, s0, inter, st_new = _scan_products(
                tile(a_ref, h, bi), tile(q_ref, h, bi), tile(lf_ref, h, bi), tile(v_ref, h, bi),
                st_ref[bi, h], stkv_ref[bi, h], reverse)
            st_ref[bi, h] = st_new
            stkv_ref[bi, h] = st_new.T.astype(BF16)
            wave.append((s1, s0, inter))
        for (h, bi), (s1, s0, inter) in zip(recurrences, wave):
            lanes = slice(h * HEAD_DIM, (h + 1) * HEAD_DIM)
            o = _scan_output(s1, s0, inter, tile(v_ref, h, bi), mask1, reverse)
            if readout:
                o = o + tile(ob_ref, h, bi)
                ms = jnp.mean(o * o, axis=-1, keepdims=True)
                y = o * lax.rsqrt(ms + EPS) * ng_ref[:, lanes]
                o = y * tile(g_ref, h, bi).astype(F32)
            o_ref[bi, pl.ds(r0, c), lanes] = o.astype(o_ref.dtype)
        return carry

    lax.fori_loop(0, n_chunks, chunk_body, 0, unroll=4)


def _scan_call(v, lf, s0, q, *, reverse, ob=None, g=None, ng=None):
    b, l, _ = v[0].shape
    dh = ng.shape[1] if ng is not None else s0.shape[1] * HEAD_DIM
    tb = min(SCAN_BLOCK, l)
    n_blocks = l // tb
    readout = ob is not None
    blk = (lambda j: n_blocks - 1 - j) if reverse else (lambda j: j)
    seq_spec = lambda group: pl.BlockSpec((b, tb, dh), lambda j: (0, blk(j), group))
    st_spec = pl.BlockSpec(s0.shape, lambda j: (0, 0, 0, 0))
    in_specs = [seq_spec(v[1]), seq_spec(lf[1]), st_spec, seq_spec(q[1])]
    args = [v[0], lf[0], s0, q[0]]
    if readout:
        in_specs += [seq_spec(ob[1]), seq_spec(g[1]), pl.BlockSpec((1, dh), lambda j: (0, 0))]
        args += [ob[0], g[0], ng]
    return pl.pallas_call(
        functools.partial(_scan_kernel, reverse=reverse, readout=readout),
        grid=(n_blocks,),
        in_specs=in_specs,
        out_specs=seq_spec(0),
        out_shape=jax.ShapeDtypeStruct((b, l, dh), BF16 if readout else F32),
        scratch_shapes=[pltpu.VMEM(s0.shape, F32), pltpu.VMEM(s0.shape, BF16), pltpu.VMEM((b, tb, dh), F32)],
        compiler_params=_params(("arbitrary",)),
        name="scan_readout_fwd" if readout else "scan_out_bwd",
    )(*args)


def _ctx_proj_kernel(ctx_ref, g_ref, mod_ref, w_ref, o_ref, wbf_ref, hc_ref):
    @pl.when(pl.program_id(0) == 0)
    def _():
        for bi in range(ctx_ref.shape[0]):
            hc_ref[bi] = _norm_modulate(ctx_ref[bi], g_ref[...], mod_ref[0, 0:1, :],
                                        mod_ref[1, 0:1, :]).astype(BF16)

    w = w_ref[...].astype(BF16)
    wbf_ref[...] = w
    for bi in range(ctx_ref.shape[0]):
        o_ref[bi] = jnp.dot(hc_ref[bi], w, preferred_element_type=F32)


def _ctx_proj_call(ctx, g, mod_early, mod_row, w, width, tn=1024):
    b, lc, d = ctx.shape
    return pl.pallas_call(
        _ctx_proj_kernel,
        grid=(width // tn,),
        in_specs=[pl.BlockSpec((b, lc, d), lambda j: (0, 0, 0)),
                  pl.BlockSpec((1, d), lambda j: (0, 0)),
                  _mod_block(N_MOD_EARLY, d, lambda j: mod_row),
                  pl.BlockSpec((d, tn), lambda j: (0, j))],
        out_specs=[pl.BlockSpec((b, lc, tn), lambda j: (0, 0, j)), pl.BlockSpec((d, tn), lambda j: (0, j))],
        out_shape=[jax.ShapeDtypeStruct((b, lc, width), F32), jax.ShapeDtypeStruct((d, width), BF16)],
        scratch_shapes=[pltpu.VMEM((b, lc, d), BF16)],
        compiler_params=_params(("arbitrary",)),
        name="ctx_proj",
    )(ctx, g, mod_early, w)


def _ctx_state_kernel(pv_ref, pf_ref, pb_ref, lg_ref, sf_ref, sb_ref):
    c = SCAN_CHUNK
    n_batch, rows = pv_ref.shape[0], pv_ref.shape[1]
    n_chunks = rows // c
    n_heads = pv_ref.shape[2] // HEAD_DIM
    for reverse, p_ref, s_ref in ((False, pf_ref, sf_ref), (True, pb_ref, sb_ref)):
        _, tri2 = _scan_masks(c, reverse)
        s_ref[...] = jnp.zeros(s_ref.shape, F32)

        def chunk_body(ci, carry, reverse=reverse, p_ref=p_ref, s_ref=s_ref, tri2=tri2):
            cidx = (n_chunks - 1 - ci) if reverse else ci
            r0 = pl.multiple_of(cidx * c, c)
            recurrences = [(h, bi) for h in range(n_heads) for bi in range(n_batch)]
            tile = lambda ref, h, bi: ref[bi, pl.ds(r0, c), h * HEAD_DIM:(h + 1) * HEAD_DIM]
            gates = [_forget_gate(tile(p_ref, h, bi), lg_ref[int(reverse), :, h * HEAD_DIM:(h + 1) * HEAD_DIM])
                     for h, bi in recurrences]
            sums = [_scan_cumsum(jnp.log2(f), tri2) for f in gates]
            for (h, bi), f, a in zip(recurrences, gates, sums):
                a_tot = _scan_refs(a, reverse)[3]
                kh = ((1.0 - f) * jnp.exp2(a_tot - a)).astype(BF16)
                upd = lax.dot_general(tile(pv_ref, h, bi).astype(BF16), kh, _TN, preferred_element_type=F32)
                s_ref[bi, h] = s_ref[bi, h] * jnp.exp2(a_tot) + upd
            return carry

        lax.fori_loop(0, n_chunks, chunk_body, 0)


def _ctx_state_call(pc, lb_logits, dh):
    b, lc, _ = pc.shape
    n_heads = dh // HEAD_DIM
    group = lambda n: pl.BlockSpec((b, lc, dh), lambda j: (0, 0, n))
    st_spec = pl.BlockSpec((b, n_heads, HEAD_DIM, HEAD_DIM), lambda j: (0, 0, 0, 0))
    st_shape = jax.ShapeDtypeStruct((b, n_heads, HEAD_DIM, HEAD_DIM), F32)
    return pl.pallas_call(
        _ctx_state_kernel,
        grid=(1,),
        in_specs=[group(0), group(1), group(2), pl.BlockSpec(lb_logits.shape, lambda j: (0, 0, 0))],
        out_specs=[st_spec, st_spec],
        out_shape=[st_shape, st_shape],
        compiler_params=_params(("arbitrary",)),
        name="ctx_state",
    )(pc, pc, pc, lb_logits)


CONV_TILE_ROWS = 16
_HPAD = 16
_HROW = GRID_W + 2 * _HPAD


def _conv_kernel(cur_ref, prev_ref, next_ref, w_ref, b_ref, lg_ref, lb_ref, cc_ref, wm_ref, bm_ref,
                 o_ref, modl_ref, hpad_ref, vpad_ref, y_ref, *, n_tiles):
    tr = CONV_TILE_ROWS
    half = w_ref.shape[1] // 2
    n_lane_tiles = half // 128
    i = pl.program_id(1)

    modl_ref[0] = _mod_rows(cc_ref, wm_ref, bm_ref)

    hpad_ref[:, :, :_HPAD, :] = jnp.zeros((tr, n_lane_tiles, _HPAD, 128), F32)
    hpad_ref[:, :, _HPAD + GRID_W:, :] = jnp.zeros((tr, n_lane_tiles, _HPAD, 128), F32)
    for r in range(tr):
        for lt in range(n_lane_tiles):
            hpad_ref[r, lt, _HPAD:_HPAD + GRID_W, :] = cur_ref[0, r * GRID_W:(r + 1) * GRID_W,
                                                               lt * 128:(lt + 1) * 128]
    halo = CONV_PAD * GRID_W
    for lt in range(n_lane_tiles):
        lanes = slice(lt * 128, (lt + 1) * 128)
        top = prev_ref[0, tr * GRID_W - halo:, lanes]
        bot = next_ref[0, :halo, lanes]
        vpad_ref[lt, :halo, :] = jnp.where(i > 0, top, 0.0)
        vpad_ref[lt, halo:halo + tr * GRID_W, :] = cur_ref[0, :, half + lt * 128:half + (lt + 1) * 128]
        vpad_ref[lt, halo + tr * GRID_W:, :] = jnp.where(i < n_tiles - 1, bot, 0.0)

    def h_row(r, carry):
        o0 = pl.multiple_of(r * GRID_W, GRID_W)
        for lt in range(n_lane_tiles):
            lanes = slice(lt * 128, (lt + 1) * 128)
            acc = jnp.zeros((GRID_W, 128), F32)
            for jj in range(CONV_WIDTH):
                off = _HPAD - CONV_PAD + jj
                acc = acc + hpad_ref[r, lt, off:off + GRID_W, :] * w_ref[jj:jj + 1, lanes]
            y_ref[pl.ds(o0, GRID_W), lanes] = acc + b_ref[:, lanes]
        return carry

    lax.fori_loop(0, tr, h_row, 0)

    n_sub = GRID_W // 8
    for lt in range(n_lane_tiles):
        vl = slice(half + lt * 128, half + (lt + 1) * 128)
        wv = [jnp.broadcast_to(w_ref[jj:jj + 1, vl], (8, 128)) for jj in range(CONV_WIDTH)]
        bias = b_ref[:, vl]

        def v_row(r, carry, lt=lt, vl=vl, wv=wv, bias=bias):
            v0 = pl.multiple_of(r * GRID_W, GRID_W)
            acc = [jnp.zeros((8, 128), F32)] * n_sub
            for jj in range(CONV_WIDTH):
                src = pl.multiple_of(v0 + jj * GRID_W, GRID_W)
                tap = vpad_ref[lt, pl.ds(src, GRID_W), :]
                acc = [acc[s] + tap[8 * s:8 * s + 8, :] * wv[jj] for s in range(n_sub)]
            y_ref[pl.ds(v0, GRID_W), vl] = jnp.concatenate(acc, axis=0) + bias
            return carry

        lax.fori_loop(0, tr, v_row, 0, unroll=2)

    def ln_row(r, carry):
        o0 = pl.multiple_of(r * GRID_W, GRID_W)
        y = y_ref[pl.ds(o0, GRID_W), :]
        mu = jnp.mean(y, axis=-1, keepdims=True)
        yc = y - mu
        var = jnp.mean(yc * yc, axis=-1, keepdims=True)
        z = yc * lax.rsqrt(var + EPS) * lg_ref[...] + lb_ref[...]
        o_ref[0, pl.ds(o0, GRID_W), :] = _silu(z).astype(BF16)
        return carry

    lax.fori_loop(0, tr, ln_row, 0, unroll=4)


def _conv_call(u, w, bias, ln_g, ln_b, cc, w_mod, b_mod):
    b, l, ch = u.shape
    half = ch // 2
    tile = CONV_TILE_ROWS * GRID_W
    n_tiles = l // tile
    rows, d = cc.shape
    n_late = w_mod.shape[1] // d - N_MOD_EARLY
    tnm = n_late * d // (b * n_tiles)
    per = d // tnm
    step = lambda i, j: i * n_tiles + j
    return pl.pallas_call(
        functools.partial(_conv_kernel, n_tiles=n_tiles),
        grid=(b, n_tiles),
        in_specs=[pl.BlockSpec((1, tile, ch), lambda i, j: (i, j, 0)),
                  pl.BlockSpec((1, tile, half), lambda i, j: (i, jnp.maximum(j - 1, 0), 1)),
                  pl.BlockSpec((1, tile, half), lambda i, j: (i, jnp.minimum(j + 1, n_tiles - 1), 1)),
                  pl.BlockSpec((CONV_WIDTH, ch), lambda i, j: (0, 0)),
                  pl.BlockSpec((1, ch), lambda i, j: (0, 0)),
                  pl.BlockSpec((1, ch), lambda i, j: (0, 0)),
                  pl.BlockSpec((1, ch), lambda i, j: (0, 0)),
                  pl.BlockSpec((rows, d), lambda i, j: (0, 0)),
                  pl.BlockSpec((d, tnm), lambda i, j: (0, N_MOD_EARLY * per + step(i, j))),
                  pl.BlockSpec((1, tnm), lambda i, j: (0, N_MOD_EARLY * per + step(i, j)))],
        out_specs=[pl.BlockSpec((1, tile, ch), lambda i, j: (i, j, 0)),
                   pl.BlockSpec((1, rows, tnm), lambda i, j: (step(i, j) // per, 0, step(i, j) % per))],
        out_shape=[jax.ShapeDtypeStruct((b, l, ch), BF16), jax.ShapeDtypeStruct((n_late, rows, d), F32)],
        scratch_shapes=[pltpu.VMEM((CONV_TILE_ROWS, half // 128, _HROW, 128), F32),
                        pltpu.VMEM((half // 128, tile + 2 * CONV_PAD * GRID_W, 128), F32),
                        pltpu.VMEM((tile, ch), F32)],
        compiler_params=_params(("arbitrary", "arbitrary")),
        name="conv",
    )(u, u, u, w, bias, ln_g, ln_b, cc, w_mod, b_mod)


MIX_SLICE = 256

def _mix_kernel(uc_ref, og_ref, gc_ref, gh_ref, x_ref, wpw_ref, wh_ref, wo_ref,
                npost_ref, npre_ref, mod_ref, x1_ref, h2_ref):
    tm = x_ref.shape[1]
    n_split = tm // MIX_SLICE
    for r in range(n_split):
        rows = slice(r * MIX_SLICE, (r + 1) * MIX_SLICE)
        y_c = jnp.dot(uc_ref[0, rows, :], wpw_ref[...], preferred_element_type=F32)
        y_h = jnp.dot(og_ref[0, rows, :], wh_ref[...], preferred_element_type=F32)
        z = gc_ref[0, rows, :].astype(F32) * y_c + gh_ref[0, rows, :].astype(F32) * y_h
        y = jnp.dot(z.astype(BF16), wo_ref[...], preferred_element_type=F32)
        ms = jnp.mean(y * y, axis=-1, keepdims=True)
        yn = y * lax.rsqrt(ms + EPS) * npost_ref[...]
        x1 = x_ref[0, rows, :] + mod_ref[0, 0:1, :] * yn
        x1_ref[0, rows, :] = x1
        h2_ref[0, rows, :] = _norm_modulate(x1, npre_ref[...], mod_ref[1, 0:1, :],
                                            mod_ref[2, 0:1, :]).astype(BF16)


def _mix_call(uc, og, gates, x, wpw, wh, wo, npost, npre, mod_late, tm=512):
    b, l, d = x.shape
    dc = uc.shape[2]
    gc = gh = gates
    row = lambda width, col=0: pl.BlockSpec((1, tm, width), lambda i, j: (i, j, col))
    const = lambda shape: pl.BlockSpec(shape, lambda i, j: (0,) * len(shape), pipeline_mode=pl.Buffered(1))
    return pl.pallas_call(
        _mix_kernel,
        grid=(b, l // tm),
        in_specs=[row(dc), row(dc), row(d, 0), row(d, 1), row(d),
                  const(wpw.shape), const(wh.shape), const(wo.shape),
                  const((1, d)), const((1, d)),
                  _mod_block(N_MOD - N_MOD_EARLY, d, lambda i, j: i)],
        out_specs=[row(d), row(d)],
        out_shape=[jax.ShapeDtypeStruct((b, l, d), F32), jax.ShapeDtypeStruct((b, l, d), BF16)],
        compiler_params=_params(("parallel", "parallel")),
        name="mix",
    )(uc, og, gc, gh, x, wpw, wh, wo, npost, npre, mod_late)


MLP_SLICE = 256

def _mlp_kernel(h_ref, w1_ref, w2_ref, x1_ref, npost_ref, mod_ref, o_ref, *, n_ff):
    j = pl.program_id(2)

    def partial_sum(rows):
        a = jnp.dot(h_ref[0, rows, :], w1_ref[...], preferred_element_type=F32)
        a = jnp.square(jnp.maximum(a, 0.0)).astype(BF16)
        return jnp.dot(a, w2_ref[...], preferred_element_type=F32)

    @pl.when(j == 0)
    def _():
        o_ref[0] = partial_sum(slice(None))

    @pl.when((j > 0) & (j < n_ff - 1))
    def _():
        o_ref[0] += partial_sum(slice(None))

    @pl.when(j == n_ff - 1)
    def _():
        for r in range(o_ref.shape[1] // MLP_SLICE):
            rows = slice(r * MLP_SLICE, (r + 1) * MLP_SLICE)
            y = o_ref[0, rows, :] + partial_sum(rows)
            ms = jnp.mean(y * y, axis=-1, keepdims=True)
            yn = y * lax.rsqrt(ms + EPS) * npost_ref[...]
            o_ref[0, rows, :] = x1_ref[0, rows, :] + mod_ref[3, 0:1, :] * yn


def _mlp_call(h2, w1, w2, x1, npost, mod_late, tm=1024, tf=1024):
    b, l, d = x1.shape
    dff = w1.shape[1]
    n_ff = dff // tf
    assert n_ff >= 2, "first and last d_ff steps are distinct code paths"
    return pl.pallas_call(
        functools.partial(_mlp_kernel, n_ff=n_ff),
        grid=(b, l // tm, n_ff),
        in_specs=[pl.BlockSpec((1, tm, d), lambda i, m, j: (i, m, 0)),
                  pl.BlockSpec((d, tf), lambda i, m, j: (0, j)),
                  pl.BlockSpec((tf, d), lambda i, m, j: (j, 0)),
                  pl.BlockSpec((1, tm, d), lambda i, m, j: (i, m, 0), pipeline_mode=pl.Buffered(1)),
                  pl.BlockSpec((1, d), lambda i, m, j: (0, 0)),
                  _mod_block(N_MOD - N_MOD_EARLY, d, lambda i, m, j: i)],
        out_specs=pl.BlockSpec((1, tm, d), lambda i, m, j: (i, m, 0)),
        out_shape=jax.ShapeDtypeStruct((b, l, d), F32),
        compiler_params=_params(("parallel", "parallel", "arbitrary")),
        name="mlp",
    )(h2, w1, w2, x1, npost, mod_late)


def kernel(x, c, ctx, c_ctx, w_mod, b_mod, norm_pre_mix, norm_post_mix, norm_pre_mlp, norm_post_mlp, w_in,
           conv_dw_w, conv_dw_b, conv_ln_g, conv_ln_b, conv_pw_w, hgrn_lb_logits, hgrn_norm_g, hgrn_out_w,
           w_out, mlp_w1, mlp_w2):
    assert w_mod.shape[0] == 1, "single-layer block"
    b, l, d = x.shape
    lc = ctx.shape[1]
    dh = hgrn_norm_g.shape[1]
    dc = conv_dw_w.shape[2]
    assert l % (CONV_TILE_ROWS * GRID_W) == 0 and CONV_PAD <= CONV_TILE_ROWS and CONV_PAD <= _HPAD
    assert l % SCAN_BLOCK == 0 and lc % SCAN_CHUNK == 0

    cond = jnp.concatenate([c, c_ctx[None, :]], axis=0)
    cc = jnp.pad(cond[:, None, :], ((0, 0), (0, MOD_ROW_STRIDE - 1), (0, 0))).reshape(-1, d)
    mod_early = _mod_call(cc, w_mod[0], b_mod, N_MOD_EARLY)

    w_in0 = w_in[0]
    off = lambda n: n * dh

    pc, w_vf = _ctx_proj_call(ctx, norm_pre_mix, mod_early, b, w_in0, 3 * dh)
    s_f, s_b = _ctx_state_call(pc, hgrn_lb_logits, dh)

    h, v, lf, w_rest = _prenorm_vf_call(x, norm_pre_mix, mod_early, w_vf, hgrn_lb_logits, w_in0, off(3))
    h = h.reshape(b * l, d)
    seq = lambda t: t.reshape(b, l, t.shape[-1])
    rest = lambda col: col - off(3)
    qg, wpw, wh, wo = _proj_call(h, w_rest, [rest(off(3))], 2 * dh, _ep_scaled_silu, [BF16],
                                 side=(conv_pw_w[0], hgrn_out_w[0], w_out[0]), tm=2048, name="proj_qg")
    (u,) = _proj_call(h, w_rest, [rest(off(5)), rest(off(5) + dc)], dc, _ep_glu, [F32], name="proj_glu")
    gates, w1, w2 = _proj_call(h, w_rest, [rest(off(5) + 2 * dc)], 2 * d, _ep_sigmoid, [BF16],
                               side=(mlp_w1[0], mlp_w2[0]), name="proj_gates")

    qg = seq(qg)
    o_b = _scan_call((v, 0), (lf, 1), s_b, (qg, 0), reverse=True)
    og = _scan_call((v, 0), (lf, 0), s_f, (qg, 0), reverse=False, ob=(o_b, 0), g=(qg, 1),
                    ng=hgrn_norm_g)

    uc, mod_late = _conv_call(seq(u), conv_dw_w[0], conv_dw_b, conv_ln_g, conv_ln_b, cc, w_mod[0], b_mod)

    x1, h2 = _mix_call(uc, og, seq(gates), x, wpw, wh, wo, norm_post_mix, norm_pre_mlp, mod_late)
    return _mlp_call(h2, w1, w2, x1, norm_post_mlp, mod_late)
```

```python
import functools

import jax
import jax.numpy as jnp
from jax import lax
from jax.experimental import pallas as pl
from jax.experimental.pallas import tpu as pltpu

F32 = jnp.float32
BF16 = jnp.bfloat16

GRID_W = 64
CONV_WIDTH = 31
CONV_PAD = (CONV_WIDTH - 1) // 2
HEAD_DIM = 128
N_MOD = 6
N_MOD_EARLY = 2
MOD_ROW_STRIDE = 16
EPS = 1e-6

SCAN_CHUNK = 64
SCAN_BLOCK = 512

VMEM_LIMIT = 56 * 1024 * 1024

_NT = (((1,), (1,)), ((), ()))
_TN = (((0,), (0,)), ((), ()))


def _params(sem):
    return pltpu.CompilerParams(dimension_semantics=sem, vmem_limit_bytes=VMEM_LIMIT)


def _sigmoid(x):
    return 0.5 * jnp.tanh(0.5 * x) + 0.5


def _silu(x):
    return x * _sigmoid(x)


def _norm_modulate(x, gain, shift, scale):
    ms = jnp.mean(x * x, axis=-1, keepdims=True)
    return (x * lax.rsqrt(ms + EPS) * gain) * (1.0 + scale) + shift


def _forget_gate(raw, logits):
    e = jnp.exp(logits - jnp.max(logits, axis=0, keepdims=True))
    lb = e[0:1] / jnp.sum(e, axis=0, keepdims=True)
    return lb + (1.0 - lb) * jax.nn.sigmoid(raw)


def _mod_rows(c_ref, w_ref, b_ref):
    s = _silu(c_ref[...]).astype(BF16)
    return jnp.dot(s, w_ref[...].astype(BF16), preferred_element_type=F32) + b_ref[...]


def _mod_kernel(c_ref, w_ref, b_ref, o_ref):
    o_ref[0] = _mod_rows(c_ref, w_ref, b_ref)


def _mod_call(cc, w_mod, b_mod, n_seg, tn=1024):
    rows, d = cc.shape
    return pl.pallas_call(
        _mod_kernel,
        grid=(n_seg * d // tn,),
        in_specs=[pl.BlockSpec((rows, d), lambda j: (0, 0)),
                  pl.BlockSpec((d, tn), lambda j: (0, j)),
                  pl.BlockSpec((1, tn), lambda j: (0, j))],
        out_specs=pl.BlockSpec((1, rows, tn), lambda j: (j // (d // tn), 0, j % (d // tn))),
        out_shape=jax.ShapeDtypeStruct((n_seg, rows, d), F32),
        compiler_params=_params(("parallel",)),
        name="mod",
    )(cc, w_mod, b_mod)


def _mod_block(n_seg, d, row_of):
    return pl.BlockSpec((n_seg, 8, d), lambda *idx: (0, row_of(*idx) * (MOD_ROW_STRIDE // 8), 0))


def _proj_kernel(*refs, n_w, n_side, n_out, epilogue):
    h_ref = refs[0]
    pos = 1
    w_refs = refs[pos:pos + n_w]
    pos += n_w
    side_in = refs[pos:pos + n_side]
    pos += n_side
    out_refs = refs[pos:pos + n_out]
    pos += n_out
    side_out = refs[pos:pos + n_side]
    pos += n_side
    wbf_refs = refs[pos:] or w_refs

    if wbf_refs is not w_refs:
        @pl.when(pl.program_id(1) == 0)
        def _():
            for w_ref, wbf_ref in zip(w_refs, wbf_refs):
                wbf_ref[...] = w_ref[...].astype(BF16)

    for si_ref, so_ref in zip(side_in, side_out):
        so_ref[...] = si_ref[...].astype(BF16)

    h = h_ref[...]
    accs = [jnp.dot(h, wbf_ref[...], preferred_element_type=F32) for wbf_ref in wbf_refs]
    outs = epilogue(accs)
    for o_ref, o in zip(out_refs, outs):
        o_ref[...] = o.astype(o_ref.dtype)


def _ep_scaled_silu(accs):
    return (_silu(accs[0]) * jnp.where(pl.program_id(0) == 0, HEAD_DIM ** -0.5, 1.0),)


def _ep_glu(accs):
    return (accs[0] * _sigmoid(accs[1]),)


def _ep_sigmoid(accs):
    return (_sigmoid(accs[0]),)


SIDE_COL_TILE = 1024


def _col_cast_specs(arr, col0, n_cols, n_steps, step):
    ct = SIDE_COL_TILE
    n_cb = n_cols // ct
    rs = n_steps // n_cb
    assert col0 % ct == 0 and n_cb * ct == n_cols and n_cb * rs == n_steps and arr.shape[0] % rs == 0
    rb = arr.shape[0] // rs
    return (pl.BlockSpec((rb, ct), lambda *idx: (step(*idx) % rs, col0 // ct + step(*idx) // rs)),
            pl.BlockSpec((rb, ct), lambda *idx: (step(*idx) % rs, step(*idx) // rs)),
            jax.ShapeDtypeStruct((arr.shape[0], n_cols), BF16))


def _proj_call(h, w, col_offsets, width, epilogue, out_dtypes, side=(), side_cols=None, tm=1024, tn=1024,
               name="proj"):
    m, k = h.shape
    tm = min(tm, m)
    n_w = len(col_offsets)
    n_j, n_i = width // tn, m // tm
    in_specs = [pl.BlockSpec((tm, k), lambda j, i: (i, 0))]
    for off in col_offsets:
        in_specs.append(pl.BlockSpec((k, tn), lambda j, i, off=off: (0, off // tn + j)))
    side_in = [pl.BlockSpec((a.shape[0] // (n_j * n_i), a.shape[1]), lambda j, i: (j * n_i + i, 0))
               for a in side]
    side_out = list(side_in)
    side_shapes = [jax.ShapeDtypeStruct(a.shape, BF16) for a in side]
    side_args = list(side)
    if side_cols is not None:
        spec_in, spec_out, shape = _col_cast_specs(*side_cols, n_j * n_i, lambda j, i: j * n_i + i)
        side_in.append(spec_in)
        side_out.append(spec_out)
        side_shapes.append(shape)
        side_args.append(side_cols[0])
    return pl.pallas_call(
        functools.partial(_proj_kernel, n_w=n_w, n_side=len(side_args), n_out=len(out_dtypes),
                          epilogue=epilogue),
        grid=(n_j, n_i),
        in_specs=in_specs + side_in,
        out_specs=[pl.BlockSpec((tm, tn), lambda j, i: (i, j)) for _ in out_dtypes] + side_out,
        out_shape=[jax.ShapeDtypeStruct((m, width), dt) for dt in out_dtypes] + side_shapes,
        scratch_shapes=[pltpu.VMEM((k, tn), BF16) for _ in range(n_w)] if w.dtype != BF16 else [],
        compiler_params=_params(("parallel", "arbitrary")),
        name=name,
    )(h, *([w] * n_w), *side_args)


PRENORM_SLICE = 128


def _prenorm_vf_kernel(x_ref, g_ref, mod_ref, w_ref, lg_ref, wside_ref,
                       h_ref, v_ref, lf_ref, wside_out_ref):
    wside_out_ref[...] = wside_ref[...].astype(BF16)
    dh = v_ref.shape[2]
    for r in range(x_ref.shape[1] // PRENORM_SLICE):
        rows = slice(r * PRENORM_SLICE, (r + 1) * PRENORM_SLICE)
        h = _norm_modulate(x_ref[0, rows, :], g_ref[...], mod_ref[0, 0:1, :], mod_ref[1, 0:1, :]).astype(BF16)
        h_ref[0, rows, :] = h
        acc = jnp.dot(h, w_ref[...], preferred_element_type=F32)
        v_ref[0, rows, :] = acc[:, :dh].astype(BF16)
        for dirn in range(2):
            f = _forget_gate(acc[:, (1 + dirn) * dh:(2 + dirn) * dh], lg_ref[dirn])
            lf_ref[0, rows, dirn * dh:(dirn + 1) * dh] = jnp.log2(f)


def _prenorm_vf_call(x, g, mod_early, w_vf, lb_logits, w_in, col0, n_cols, tl=512):
    b, l, d = x.shape
    dh = w_vf.shape[1] // 3
    side_in, side_out, side_shape = _col_cast_specs(w_in, col0, n_cols, b * (l // tl),
                                                    lambda i, j: i * (l // tl) + j)
    row = lambda width: pl.BlockSpec((1, tl, width), lambda i, j: (i, j, 0))
    return pl.pallas_call(
        _prenorm_vf_kernel,
        grid=(b, l // tl),
        in_specs=[row(d),
                  pl.BlockSpec((1, d), lambda i, j: (0, 0)),
                  _mod_block(N_MOD_EARLY, d, lambda i, j: i),
                  pl.BlockSpec(w_vf.shape, lambda i, j: (0, 0), pipeline_mode=pl.Buffered(1)),
                  pl.BlockSpec(lb_logits.shape, lambda i, j: (0, 0, 0)),
                  side_in],
        out_specs=[row(d), row(dh), row(2 * dh), side_out],
        out_shape=[jax.ShapeDtypeStruct((b, l, d), BF16), jax.ShapeDtypeStruct((b, l, dh), BF16),
                   jax.ShapeDtypeStruct((b, l, 2 * dh), F32), side_shape],
        compiler_params=_params(("parallel", "parallel")),
        name="prenorm_vf",
    )(x, g, mod_early, w_vf, lb_logits, w_in)


def _scan_masks(c, reverse):
    t = lax.broadcasted_iota(jnp.int32, (c, c), 0)
    s = lax.broadcasted_iota(jnp.int32, (c, c), 1)
    half = c // 2
    before = (s >= t) if reverse else (s <= t)
    mask1 = ((t < half) == (s < half)) & before
    tri = jnp.where(before, 1.0, 0.0).astype(BF16)
    return mask1, jnp.concatenate([tri, tri], axis=1)


def _scan_cumsum(lf, tri2):
    hi = lf.astype(BF16)
    lo = (lf - hi.astype(F32)).astype(BF16)
    return jnp.dot(tri2, jnp.concatenate([hi, lo], axis=0), preferred_element_type=F32)


def _scan_refs(a, reverse):
    c = a.shape[0]
    half, quarter = c // 2, c // 4
    if reverse:
        return a[quarter:quarter + 1], a[half + quarter:half + quarter + 1], a[half:half + 1], a[0:1]
    return a[quarter - 1:quarter], a[half + quarter - 1:half + quarter], a[half - 1:half], a[c - 1:c]


def _scan_products(a, q, lf, v, st, st_kv, reverse):
    c = a.shape[0]
    half = c // 2
    m1_lo, m1_hi, m0, a_tot = _scan_refs(a, reverse)
    lo_rows, hi_rows = slice(0, half), slice(half, c)
    row = lax.broadcasted_iota(jnp.int32, a.shape, 0)
    e1 = a - jnp.where(row < half, m1_lo, m1_hi)
    pq = q.astype(F32) * jnp.exp2(e1)
    pk = (1.0 - jnp.exp2(lf)) * jnp.exp2(-e1)
    s1 = lax.dot_general(pq.astype(BF16), pk.astype(BF16), _NT, preferred_element_type=F32)
    if reverse:
        q_rows, m1_q, k_rows, m1_k = lo_rows, m1_lo, hi_rows, m1_hi
    else:
        q_rows, m1_q, k_rows, m1_k = hi_rows, m1_hi, lo_rows, m1_lo
    q0 = (pq[q_rows] * jnp.exp2(m1_q - m0)).astype(BF16)
    k0 = (pk[k_rows] * jnp.exp2(m0 - m1_k)).astype(BF16)
    s0 = lax.dot_general(q0, k0, _NT, preferred_element_type=F32)
    qi = jnp.concatenate([pq[lo_rows] * jnp.exp2(m1_lo), pq[hi_rows] * jnp.exp2(m1_hi)], axis=0)
    inter = jnp.dot(qi.astype(BF16), st_kv, preferred_element_type=F32)
    kh = jnp.concatenate([pk[lo_rows] * jnp.exp2(a_tot - m1_lo), pk[hi_rows] * jnp.exp2(a_tot - m1_hi)], axis=0)
    upd = lax.dot_general(v, kh.astype(BF16), _TN, preferred_element_type=F32)
    return s1, s0, inter, st * jnp.exp2(a_tot) + upd


def _scan_output(s1, s0, inter, v, mask1, reverse):
    c = s1.shape[0]
    half = c // 2
    lo_rows, hi_rows = slice(0, half), slice(half, c)
    q_rows, k_rows = (lo_rows, hi_rows) if reverse else (hi_rows, lo_rows)
    intra = jnp.dot(jnp.where(mask1, s1, 0.0).astype(BF16), v, preferred_element_type=F32)
    cross = jnp.dot(s0.astype(BF16), v[k_rows], preferred_element_type=F32)
    o = inter + intra
    o_q = o[q_rows] + cross
    return jnp.concatenate([o_q, o[hi_rows]] if reverse else [o[lo_rows], o_q], axis=0)


def _scan_kernel(*refs, reverse, readout):
    v_ref, lf_ref, s0_ref, q_ref = refs[:4]
    if readout:
        (ob_ref, g_ref, ng_ref, o_ref, st_ref, stkv_ref, a_ref) = refs[4:]
    else:
        (o_ref, st_ref, stkv_ref, a_ref) = refs[4:]
    c = SCAN_CHUNK
    n_batch, rows = v_ref.shape[0], v_ref.shape[1]
    n_chunks = rows // c
    n_heads = v_ref.shape[2] // HEAD_DIM
    recurrences = [(h, bi) for h in range(n_heads) for bi in range(n_batch)]

    @pl.when(pl.program_id(0) == 0)
    def _():
        st_ref[...] = s0_ref[...]
        for h, bi in recurrences:
            stkv_ref[bi, h] = s0_ref[bi, h].T.astype(BF16)

    mask1, tri2 = _scan_masks(c, reverse)

    def cumsum_body(ci, carry):
        r0 = pl.multiple_of(ci * c, c)
        for bi in range(n_batch):
            a_ref[bi, pl.ds(r0, c), :] = _scan_cumsum(lf_ref[bi, pl.ds(r0, c), :], tri2)
        return carry

    lax.fori_loop(0, n_chunks, cumsum_body, 0, unroll=4)

    def chunk_body(ci, carry):
        cidx = (n_chunks - 1 - ci) if reverse else ci
        r0 = pl.multiple_of(cidx * c, c)
        tile = lambda ref, h, bi: ref[bi, pl.ds(r0, c), h * HEAD_DIM:(h + 1) * HEAD_DIM]
        wave = []
        for h, bi in recurrences:
            s1, s0, inter, st_new = _scan_products(
                tile(a_ref, h, bi), tile(q_ref, h, bi), tile(lf_ref, h, bi), tile(v_ref, h, bi),
                st_ref[bi, h], stkv_ref[bi, h], reverse)
            st_ref[bi, h] = st_new
            stkv_ref[bi, h] = st_new.T.astype(BF16)
            wave.append((s1, s0, inter))
        for (h, bi), (s1, s0, inter) in zip(recurrences, wave):
            lanes = slice(h * HEAD_DIM, (h + 1) * HEAD_DIM)
            o = _scan_output(s1, s0, inter, tile(v_ref, h, bi), mask1, reverse)
            if readout:
                o = o + tile(ob_ref, h, bi)
                ms = jnp.mean(o * o, axis=-1, keepdims=True)
                y = o * lax.rsqrt(ms + EPS) * ng_ref[:, lanes]
                o = y * tile(g_ref, h, bi).astype(F32)
            o_ref[bi, pl.ds(r0, c), lanes] = o.astype(o_ref.dtype)
        return carry

    lax.fori_loop(0, n_chunks, chunk_body, 0, unroll=4)


def _scan_call(v, lf, s0, q, *, reverse, ob=None, g=None, ng=None):
    b, l, _ = v[0].shape
    dh = ng.shape[1] if ng is not None else s0.shape[1] * HEAD_DIM
    tb = min(SCAN_BLOCK, l)
    n_blocks = l // tb
    readout = ob is not None
    blk = (lambda j: n_blocks - 1 - j) if reverse else (lambda j: j)
    seq_spec = lambda group: pl.BlockSpec((b, tb, dh), lambda j: (0, blk(j), group))
    st_spec = pl.BlockSpec(s0.shape, lambda j: (0, 0, 0, 0))
    in_specs = [seq_spec(v[1]), seq_spec(lf[1]), st_spec, seq_spec(q[1])]
    args = [v[0], lf[0], s0, q[0]]
    if readout:
        in_specs += [seq_spec(ob[1]), seq_spec(g[1]), pl.BlockSpec((1, dh), lambda j: (0, 0))]
        args += [ob[0], g[0], ng]
    return pl.pallas_call(
        functools.partial(_scan_kernel, reverse=reverse, readout=readout),
        grid=(n_blocks,),
        in_specs=in_specs,
        out_specs=seq_spec(0),
        out_shape=jax.ShapeDtypeStruct((b, l, dh), BF16 if readout else F32),
        scratch_shapes=[pltpu.VMEM(s0.shape, F32), pltpu.VMEM(s0.shape, BF16), pltpu.VMEM((b, tb, dh), F32)],
        compiler_params=_params(("arbitrary",)),
        name="scan_readout_fwd" if readout else "scan_out_bwd",
    )(*args)


def _ctx_proj_kernel(ctx_ref, g_ref, mod_ref, w_ref, o_ref, wbf_ref, hc_ref):
    @pl.when(pl.program_id(0) == 0)
    def _():
        for bi in range(ctx_ref.shape[0]):
            hc_ref[bi] = _norm_modulate(ctx_ref[bi], g_ref[...], mod_ref[0, 0:1, :],
                                        mod_ref[1, 0:1, :]).astype(BF16)

    w = w_ref[...].astype(BF16)
    wbf_ref[...] = w
    for bi in range(ctx_ref.shape[0]):
        o_ref[bi] = jnp.dot(hc_ref[bi], w, preferred_element_type=F32)


def _ctx_proj_call(ctx, g, mod_early, mod_row, w, width, tn=1024):
    b, lc, d = ctx.shape
    return pl.pallas_call(
        _ctx_proj_kernel,
        grid=(width // tn,),
        in_specs=[pl.BlockSpec((b, lc, d), lambda j: (0, 0, 0)),
                  pl.BlockSpec((1, d), lambda j: (0, 0)),
                  _mod_block(N_MOD_EARLY, d, lambda j: mod_row),
                  pl.BlockSpec((d, tn), lambda j: (0, j))],
        out_specs=[pl.BlockSpec((b, lc, tn), lambda j: (0, 0, j)), pl.BlockSpec((d, tn), lambda j: (0, j))],
        out_shape=[jax.ShapeDtypeStruct((b, lc, width), F32), jax.ShapeDtypeStruct((d, width), BF16)],
        scratch_shapes=[pltpu.VMEM((b, lc, d), BF16)],
        compiler_params=_params(("arbitrary",)),
        name="ctx_proj",
    )(ctx, g, mod_early, w)


def _ctx_state_kernel(pv_ref, pf_ref, pb_ref, lg_ref, sf_ref, sb_ref):
    c = SCAN_CHUNK
    n_batch, rows = pv_ref.shape[0], pv_ref.shape[1]
    n_chunks = rows // c
    n_heads = pv_ref.shape[2] // HEAD_DIM
    for reverse, p_ref, s_ref in ((False, pf_ref, sf_ref), (True, pb_ref, sb_ref)):
        _, tri2 = _scan_masks(c, reverse)
        s_ref[...] = jnp.zeros(s_ref.shape, F32)

        def chunk_body(ci, carry, reverse=reverse, p_ref=p_ref, s_ref=s_ref, tri2=tri2):
            cidx = (n_chunks - 1 - ci) if reverse else ci
            r0 = pl.multiple_of(cidx * c, c)
            recurrences = [(h, bi) for h in range(n_heads) for bi in range(n_batch)]
            tile = lambda ref, h, bi: ref[bi, pl.ds(r0, c), h * HEAD_DIM:(h + 1) * HEAD_DIM]
            gates = [_forget_gate(tile(p_ref, h, bi), lg_ref[int(reverse), :, h * HEAD_DIM:(h + 1) * HEAD_DIM])
                     for h, bi in recurrences]
            sums = [_scan_cumsum(jnp.log2(f), tri2) for f in gates]
            for (h, bi), f, a in zip(recurrences, gates, sums):
                a_tot = _scan_refs(a, reverse)[3]
                kh = ((1.0 - f) * jnp.exp2(a_tot - a)).astype(BF16)
                upd = lax.dot_general(tile(pv_ref, h, bi).astype(BF16), kh, _TN, preferred_element_type=F32)
                s_ref[bi, h] = s_ref[bi, h] * jnp.exp2(a_tot) + upd
            return carry

        lax.fori_loop(0, n_chunks, chunk_body, 0)


def _ctx_state_call(pc, lb_logits, dh):
    b, lc, _ = pc.shape
    n_heads = dh // HEAD_DIM
    group = lambda n: pl.BlockSpec((b, lc, dh), lambda j: (0, 0, n))
    st_spec = pl.BlockSpec((b, n_heads, HEAD_DIM, HEAD_DIM), lambda j: (0, 0, 0, 0))
    st_shape = jax.ShapeDtypeStruct((b, n_heads, HEAD_DIM, HEAD_DIM), F32)
    return pl.pallas_call(
        _ctx_state_kernel,
        grid=(1,),
        in_specs=[group(0), group(1), group(2), pl.BlockSpec(lb_logits.shape, lambda j: (0, 0, 0))],
        out_specs=[st_spec, st_spec],
        out_shape=[st_shape, st_shape],
        compiler_params=_params(("arbitrary",)),
        name="ctx_state",
    )(pc, pc, pc, lb_logits)


CONV_TILE_ROWS = 16
_HPAD = 16
_HROW = GRID_W + 2 * _HPAD


def _conv_kernel(cur_ref, prev_ref, next_ref, w_ref, b_ref, lg_ref, lb_ref, cc_ref, wm_ref, bm_ref,
                 o_ref, modl_ref, hpad_ref, vpad_ref, y_ref, *, n_tiles):
    tr = CONV_TILE_ROWS
    half = w_ref.shape[1] // 2
    n_lane_tiles = half // 128
    i = pl.program_id(1)

    modl_ref[0] = _mod_rows(cc_ref, wm_ref, bm_ref)

    hpad_ref[:, :, :_HPAD, :] = jnp.zeros((tr, n_lane_tiles, _HPAD, 128), F32)
    hpad_ref[:, :, _HPAD + GRID_W:, :] = jnp.zeros((tr, n_lane_tiles, _HPAD, 128), F32)
    for r in range(tr):
        for lt in range(n_lane_tiles):
            hpad_ref[r, lt, _HPAD:_HPAD + GRID_W, :] = cur_ref[0, r * GRID_W:(r + 1) * GRID_W,
                                                               lt * 128:(lt + 1) * 128]
    halo = CONV_PAD * GRID_W
    for lt in range(n_lane_tiles):
        lanes = slice(lt * 128, (lt + 1) * 128)
        top = prev_ref[0, tr * GRID_W - halo:, lanes]
        bot = next_ref[0, :halo, lanes]
        vpad_ref[lt, :halo, :] = jnp.where(i > 0, top, 0.0)
        vpad_ref[lt, halo:halo + tr * GRID_W, :] = cur_ref[0, :, half + lt * 128:half + (lt + 1) * 128]
        vpad_ref[lt, halo + tr * GRID_W:, :] = jnp.where(i < n_tiles - 1, bot, 0.0)

    def h_row(r, carry):
        o0 = pl.multiple_of(r * GRID_W, GRID_W)
        for lt in range(n_lane_tiles):
            lanes = slice(lt * 128, (lt + 1) * 128)
            acc = jnp.zeros((GRID_W, 128), F32)
            for jj in range(CONV_WIDTH):
                off = _HPAD - CONV_PAD + jj
                acc = acc + hpad_ref[r, lt, off:off + GRID_W, :] * w_ref[jj:jj + 1, lanes]
            y_ref[pl.ds(o0, GRID_W), lanes] = acc + b_ref[:, lanes]
        return carry

    lax.fori_loop(0, tr, h_row, 0)

    n_sub = GRID_W // 8
    for lt in range(n_lane_tiles):
        vl = slice(half + lt * 128, half + (lt + 1) * 128)
        wv = [jnp.broadcast_to(w_ref[jj:jj + 1, vl], (8, 128)) for jj in range(CONV_WIDTH)]
        bias = b_ref[:, vl]

        def v_row(r, carry, lt=lt, vl=vl, wv=wv, bias=bias):
            v0 = pl.multiple_of(r * GRID_W, GRID_W)
            acc = [jnp.zeros((8, 128), F32)] * n_sub
            for jj in range(CONV_WIDTH):
                src = pl.multiple_of(v0 + jj * GRID_W, GRID_W)
                tap = vpad_ref[lt, pl.ds(src, GRID_W), :]
                acc = [acc[s] + tap[8 * s:8 * s + 8, :] * wv[jj] for s in range(n_sub)]
            y_ref[pl.ds(v0, GRID_W), vl] = jnp.concatenate(acc, axis=0) + bias
            return carry

        lax.fori_loop(0, tr, v_row, 0, unroll=2)

    def ln_row(r, carry):
        o0 = pl.multiple_of(r * GRID_W, GRID_W)
        y = y_ref[pl.ds(o0, GRID_W), :]
        mu = jnp.mean(y, axis=-1, keepdims=True)
        yc = y - mu
        var = jnp.mean(yc * yc, axis=-1, keepdims=True)
        z = yc * lax.rsqrt(var + EPS) * lg_ref[...] + lb_ref[...]
        o_ref[0, pl.ds(o0, GRID_W), :] = _silu(z).astype(BF16)
        return carry

    lax.fori_loop(0, tr, ln_row, 0, unroll=4)


def _conv_call(u, w, bias, ln_g, ln_b, cc, w_mod, b_mod):
    b, l, ch = u.shape
    half = ch // 2
    tile = CONV_TILE_ROWS * GRID_W
    n_tiles = l // tile
    rows, d = cc.shape
    n_late = w_mod.shape[1] // d - N_MOD_EARLY
    tnm = n_late * d // (b * n_tiles)
    per = d // tnm
    step = lambda i, j: i * n_tiles + j
    return pl.pallas_call(
        functools.partial(_conv_kernel, n_tiles=n_tiles),
        grid=(b, n_tiles),
        in_specs=[pl.BlockSpec((1, tile, ch), lambda i, j: (i, j, 0)),
                  pl.BlockSpec((1, tile, half), lambda i, j: (i, jnp.maximum(j - 1, 0), 1)),
                  pl.BlockSpec((1, tile, half), lambda i, j: (i, jnp.minimum(j + 1, n_tiles - 1), 1)),
                  pl.BlockSpec((CONV_WIDTH, ch), lambda i, j: (0, 0)),
                  pl.BlockSpec((1, ch), lambda i, j: (0, 0)),
                  pl.BlockSpec((1, ch), lambda i, j: (0, 0)),
                  pl.BlockSpec((1, ch), lambda i, j: (0, 0)),
                  pl.BlockSpec((rows, d), lambda i, j: (0, 0)),
                  pl.BlockSpec((d, tnm), lambda i, j: (0, N_MOD_EARLY * per + step(i, j))),
                  pl.BlockSpec((1, tnm), lambda i, j: (0, N_MOD_EARLY * per + step(i, j)))],
        out_specs=[pl.BlockSpec((1, tile, ch), lambda i, j: (i, j, 0)),
                   pl.BlockSpec((1, rows, tnm), lambda i, j: (step(i, j) // per, 0, step(i, j) % per))],
        out_shape=[jax.ShapeDtypeStruct((b, l, ch), BF16), jax.ShapeDtypeStruct((n_late, rows, d), F32)],
        scratch_shapes=[pltpu.VMEM((CONV_TILE_ROWS, half // 128, _HROW, 128), F32),
                        pltpu.VMEM((half // 128, tile + 2 * CONV_PAD * GRID_W, 128), F32),
                        pltpu.VMEM((tile, ch), F32)],
        compiler_params=_params(("arbitrary", "arbitrary")),
        name="conv",
    )(u, u, u, w, bias, ln_g, ln_b, cc, w_mod, b_mod)


MIX_SLICE = 256

def _mix_kernel(uc_ref, og_ref, gc_ref, gh_ref, x_ref, wpw_ref, wh_ref, wo_ref,
                npost_ref, npre_ref, mod_ref, x1_ref, h2_ref):
    tm = x_ref.shape[1]
    n_split = tm // MIX_SLICE
    for r in range(n_split):
        rows = slice(r * MIX_SLICE, (r + 1) * MIX_SLICE)
        y_c = jnp.dot(uc_ref[0, rows, :], wpw_ref[...], preferred_element_type=F32)
        y_h = jnp.dot(og_ref[0, rows, :], wh_ref[...], preferred_element_type=F32)
        z = gc_ref[0, rows, :].astype(F32) * y_c + gh_ref[0, rows, :].astype(F32) * y_h
        y = jnp.dot(z.astype(BF16), wo_ref[...], preferred_element_type=F32)
        ms = jnp.mean(y * y, axis=-1, keepdims=True)
        yn = y * lax.rsqrt(ms + EPS) * npost_ref[...]
        x1 = x_ref[0, rows, :] + mod_ref[0, 0:1, :] * yn
        x1_ref[0, rows, :] = x1
        h2_ref[0, rows, :] = _norm_modulate(x1, npre_ref[...], mod_ref[1, 0:1, :],
                                            mod_ref[2, 0:1, :]).astype(BF16)


def _mix_call(uc, og, gates, x, wpw, wh, wo, npost, npre, mod_late, tm=512):
    b, l, d = x.shape
    dc = uc.shape[2]
    gc = gh = gates
    row = lambda width, col=0: pl.BlockSpec((1, tm, width), lambda i, j: (i, j, col))
    const = lambda shape: pl.BlockSpec(shape, lambda i, j: (0,) * len(shape), pipeline_mode=pl.Buffered(1))
    return pl.pallas_call(
        _mix_kernel,
        grid=(b, l // tm),
        in_specs=[row(dc), row(dc), row(d, 0), row(d, 1), row(d),
                  const(wpw.shape), const(wh.shape), const(wo.shape),
                  const((1, d)), const((1, d)),
                  _mod_block(N_MOD - N_MOD_EARLY, d, lambda i, j: i)],
        out_specs=[row(d), row(d)],
        out_shape=[jax.ShapeDtypeStruct((b, l, d), F32), jax.ShapeDtypeStruct((b, l, d), BF16)],
        compiler_params=_params(("parallel", "parallel")),
        name="mix",
    )(uc, og, gc, gh, x, wpw, wh, wo, npost, npre, mod_late)


MLP_SLICE = 256

def _mlp_kernel(h_ref, w1_ref, w2_ref, x1_ref, npost_ref, mod_ref, o_ref, *, n_ff):
    j = pl.program_id(2)

    def partial_sum(rows):
        a = jnp.dot(h_ref[0, rows, :], w1_ref[...], preferred_element_type=F32)
        a = jnp.square(jnp.maximum(a, 0.0)).astype(BF16)
        return jnp.dot(a, w2_ref[...], preferred_element_type=F32)

    @pl.when(j == 0)
    def _():
        o_ref[0] = partial_sum(slice(None))

    @pl.when((j > 0) & (j < n_ff - 1))
    def _():
        o_ref[0] += partial_sum(slice(None))

    @pl.when(j == n_ff - 1)
    def _():
        for r in range(o_ref.shape[1] // MLP_SLICE):
            rows = slice(r * MLP_SLICE, (r + 1) * MLP_SLICE)
            y = o_ref[0, rows, :] + partial_sum(rows)
            ms = jnp.mean(y * y, axis=-1, keepdims=True)
            yn = y * lax.rsqrt(ms + EPS) * npost_ref[...]
            o_ref[0, rows, :] = x1_ref[0, rows, :] + mod_ref[3, 0:1, :] * yn


def _mlp_call(h2, w1, w2, x1, npost, mod_late, tm=1024, tf=1024):
    b, l, d = x1.shape
    dff = w1.shape[1]
    n_ff = dff // tf
    assert n_ff >= 2, "first and last d_ff steps are distinct code paths"
    return pl.pallas_call(
        functools.partial(_mlp_kernel, n_ff=n_ff),
        grid=(b, l // tm, n_ff),
        in_specs=[pl.BlockSpec((1, tm, d), lambda i, m, j: (i, m, 0)),
                  pl.BlockSpec((d, tf), lambda i, m, j: (0, j)),
                  pl.BlockSpec((tf, d), lambda i, m, j: (j, 0)),
                  pl.BlockSpec((1, tm, d), lambda i, m, j: (i, m, 0), pipeline_mode=pl.Buffered(1)),
                  pl.BlockSpec((1, d), lambda i, m, j: (0, 0)),
                  _mod_block(N_MOD - N_MOD_EARLY, d, lambda i, m, j: i)],
        out_specs=pl.BlockSpec((1, tm, d), lambda i, m, j: (i, m, 0)),
        out_shape=jax.ShapeDtypeStruct((b, l, d), F32),
        compiler_params=_params(("parallel", "parallel", "arbitrary")),
        name="mlp",
    )(h2, w1, w2, x1, npost, mod_late)


def kernel(x, c, ctx, c_ctx, w_mod, b_mod, norm_pre_mix, norm_post_mix, norm_pre_mlp, norm_post_mlp, w_in,
           conv_dw_w, conv_dw_b, conv_ln_g, conv_ln_b, conv_pw_w, hgrn_lb_logits, hgrn_norm_g, hgrn_out_w,
           w_out, mlp_w1, mlp_w2):
    assert w_mod.shape[0] == 1, "single-layer block"
    b, l, d = x.shape
    lc = ctx.shape[1]
    dh = hgrn_norm_g.shape[1]
    dc = conv_dw_w.shape[2]
    assert l % (CONV_TILE_ROWS * GRID_W) == 0 and CONV_PAD <= CONV_TILE_ROWS and CONV_PAD <= _HPAD
    assert l % SCAN_BLOCK == 0 and lc % SCAN_CHUNK == 0

    cond = jnp.concatenate([c, c_ctx[None, :]], axis=0)
    cc = jnp.pad(cond[:, None, :], ((0, 0), (0, MOD_ROW_STRIDE - 1), (0, 0))).reshape(-1, d)
    mod_early = _mod_call(cc, w_mod[0], b_mod, N_MOD_EARLY)

    w_in0 = w_in[0]
    off = lambda n: n * dh

    pc, w_vf = _ctx_proj_call(ctx, norm_pre_mix, mod_early, b, w_in0, 3 * dh)
    s_f, s_b = _ctx_state_call(pc, hgrn_lb_logits, dh)

    gate_col = off(5) + 2 * dc
    h, v, lf, w_a = _prenorm_vf_call(x, norm_pre_mix, mod_early, w_vf, hgrn_lb_logits, w_in0, off(3),
                                     gate_col - off(3))
    h = h.reshape(b * l, d)
    seq = lambda t: t.reshape(b, l, t.shape[-1])
    rest = lambda col: col - off(3)
    qg, wpw, wh, wo, w_b = _proj_call(h, w_a, [rest(off(3))], 2 * dh, _ep_scaled_silu, [BF16],
                                      side=(conv_pw_w[0], hgrn_out_w[0], w_out[0]),
                                      side_cols=(w_in0, gate_col, 2 * d), name="proj_qg")
    (u,) = _proj_call(h, w_a, [rest(off(5)), rest(off(5) + dc)], dc, _ep_glu, [F32], name="proj_glu")
    gates, w1, w2 = _proj_call(h, w_b, [0], 2 * d, _ep_sigmoid, [BF16],
                               side=(mlp_w1[0], mlp_w2[0]), name="proj_gates")

    qg = seq(qg)
    o_b = _scan_call((v, 0), (lf, 1), s_b, (qg, 0), reverse=True)
    og = _scan_call((v, 0), (lf, 0), s_f, (qg, 0), reverse=False, ob=(o_b, 0), g=(qg, 1),
                    ng=hgrn_norm_g)

    uc, mod_late = _conv_call(seq(u), conv_dw_w[0], conv_dw_b, conv_ln_g, conv_ln_b, cc, w_mod[0], b_mod)

    x1, h2 = _mix_call(uc, og, seq(gates), x, wpw, wh, wo, norm_post_mix, norm_pre_mlp, mod_late)
    return _mlp_call(h2, w1, w2, x1, norm_post_mlp, mod_late)
```

```python
import functools

import jax
import jax.numpy as jnp
from jax import lax
from jax.experimental import pallas as pl
from jax.experimental.pallas import tpu as pltpu

F32 = jnp.float32
BF16 = jnp.bfloat16

GRID_W = 64
CONV_WIDTH = 31
CONV_PAD = (CONV_WIDTH - 1) // 2
HEAD_DIM = 128
N_MOD = 6
N_MOD_EARLY = 2
MOD_ROW_STRIDE = 16
EPS = 1e-6

SCAN_CHUNK = 64
SCAN_BLOCK = 512

VMEM_LIMIT = 56 * 1024 * 1024

_NT = (((1,), (1,)), ((), ()))
_TN = (((0,), (0,)), ((), ()))


def _params(sem):
    return pltpu.CompilerParams(dimension_semantics=sem, vmem_limit_bytes=VMEM_LIMIT)


def _sigmoid(x):
    return 0.5 * jnp.tanh(0.5 * x) + 0.5


def _silu(x):
    return x * _sigmoid(x)


def _norm_modulate(x, gain, shift, scale):
    ms = jnp.mean(x * x, axis=-1, keepdims=True)
    return (x * lax.rsqrt(ms + EPS) * gain) * (1.0 + scale) + shift


def _forget_gate(raw, logits):
    e = jnp.exp(logits - jnp.max(logits, axis=0, keepdims=True))
    lb = e[0:1] / jnp.sum(e, axis=0, keepdims=True)
    return lb + (1.0 - lb) * jax.nn.sigmoid(raw)


def _mod_rows(c_ref, w_ref, b_ref):
    s = _silu(c_ref[...]).astype(BF16)
    return jnp.dot(s, w_ref[...].astype(BF16), preferred_element_type=F32) + b_ref[...]


def _mod_kernel(c_ref, w_ref, b_ref, o_ref):
    o_ref[0] = _mod_rows(c_ref, w_ref, b_ref)


def _mod_call(cc, w_mod, b_mod, n_seg, tn=1024):
    rows, d = cc.shape
    return pl.pallas_call(
        _mod_kernel,
        grid=(n_seg * d // tn,),
        in_specs=[pl.BlockSpec((rows, d), lambda j: (0, 0)),
                  pl.BlockSpec((d, tn), lambda j: (0, j)),
                  pl.BlockSpec((1, tn), lambda j: (0, j))],
        out_specs=pl.BlockSpec((1, rows, tn), lambda j: (j // (d // tn), 0, j % (d // tn))),
        out_shape=jax.ShapeDtypeStruct((n_seg, rows, d), F32),
        compiler_params=_params(("parallel",)),
        name="mod",
    )(cc, w_mod, b_mod)


def _mod_block(n_seg, d, row_of):
    return pl.BlockSpec((n_seg, 8, d), lambda *idx: (0, row_of(*idx) * (MOD_ROW_STRIDE // 8), 0))


def _proj_kernel(*refs, n_w, n_side, n_out, epilogue):
    h_ref = refs[0]
    pos = 1
    w_refs = refs[pos:pos + n_w]
    pos += n_w
    side_in = refs[pos:pos + n_side]
    pos += n_side
    out_refs = refs[pos:pos + n_out]
    pos += n_out
    side_out = refs[pos:pos + n_side]
    pos += n_side
    wbf_refs = refs[pos:] or w_refs

    if wbf_refs is not w_refs:
        @pl.when(pl.program_id(1) == 0)
        def _():
            for w_ref, wbf_ref in zip(w_refs, wbf_refs):
                wbf_ref[...] = w_ref[...].astype(BF16)

    for si_ref, so_ref in zip(side_in, side_out):
        so_ref[...] = si_ref[...].astype(BF16)

    h = h_ref[...]
    accs = [jnp.dot(h, wbf_ref[...], preferred_element_type=F32) for wbf_ref in wbf_refs]
    outs = epilogue(accs)
    for o_ref, o in zip(out_refs, outs):
        o_ref[...] = o.astype(o_ref.dtype)


def _ep_scaled_silu(accs):
    return (_silu(accs[0]) * jnp.where(pl.program_id(0) == 0, HEAD_DIM ** -0.5, 1.0),)


def _ep_glu(accs):
    return (accs[0] * _sigmoid(accs[1]),)


def _ep_sigmoid(accs):
    return (_sigmoid(accs[0]),)


SIDE_COL_TILE = 1024


def _col_cast_specs(arr, col0, n_cols, n_steps, step):
    ct = SIDE_COL_TILE
    n_cb = n_cols // ct
    rs = n_steps // n_cb
    assert col0 % ct == 0 and n_cb * ct == n_cols and n_cb * rs == n_steps and arr.shape[0] % rs == 0
    rb = arr.shape[0] // rs
    return (pl.BlockSpec((rb, ct), lambda *idx: (step(*idx) % rs, col0 // ct + step(*idx) // rs)),
            pl.BlockSpec((rb, ct), lambda *idx: (step(*idx) % rs, step(*idx) // rs)),
            jax.ShapeDtypeStruct((arr.shape[0], n_cols), BF16))


def _proj_call(h, w, col_offsets, width, epilogue, out_dtypes, side=(), side_cols=None, tm=1024, tn=1024,
               name="proj"):
    m, k = h.shape
    tm = min(tm, m)
    n_w = len(col_offsets)
    n_j, n_i = width // tn, m // tm
    in_specs = [pl.BlockSpec((tm, k), lambda j, i: (i, 0))]
    for off in col_offsets:
        in_specs.append(pl.BlockSpec((k, tn), lambda j, i, off=off: (0, off // tn + j)))
    side_in = [pl.BlockSpec((a.shape[0] // (n_j * n_i), a.shape[1]), lambda j, i: (j * n_i + i, 0))
               for a in side]
    side_out = list(side_in)
    side_shapes = [jax.ShapeDtypeStruct(a.shape, BF16) for a in side]
    side_args = list(side)
    if side_cols is not None:
        spec_in, spec_out, shape = _col_cast_specs(*side_cols, n_j * n_i, lambda j, i: j * n_i + i)
        side_in.append(spec_in)
        side_out.append(spec_out)
        side_shapes.append(shape)
        side_args.append(side_cols[0])
    return pl.pallas_call(
        functools.partial(_proj_kernel, n_w=n_w, n_side=len(side_args), n_out=len(out_dtypes),
                          epilogue=epilogue),
        grid=(n_j, n_i),
        in_specs=in_specs + side_in,
        out_specs=[pl.BlockSpec((tm, tn), lambda j, i: (i, j)) for _ in out_dtypes] + side_out,
        out_shape=[jax.ShapeDtypeStruct((m, width), dt) for dt in out_dtypes] + side_shapes,
        scratch_shapes=[pltpu.VMEM((k, tn), BF16) for _ in range(n_w)] if w.dtype != BF16 else [],
        compiler_params=_params(("parallel", "arbitrary")),
        name=name,
    )(h, *([w] * n_w), *side_args)


PRENORM_SLICE = 256


def _prenorm_vf_kernel(x_ref, g_ref, mod_ref, w_ref, lg_ref, wside_ref,
                       h_ref, v_ref, lf_ref, wside_out_ref):
    wside_out_ref[...] = wside_ref[...].astype(BF16)
    dh = v_ref.shape[2]
    for r in range(x_ref.shape[1] // PRENORM_SLICE):
        rows = slice(r * PRENORM_SLICE, (r + 1) * PRENORM_SLICE)
        h = _norm_modulate(x_ref[0, rows, :], g_ref[...], mod_ref[0, 0:1, :], mod_ref[1, 0:1, :]).astype(BF16)
        h_ref[0, rows, :] = h
        acc = jnp.dot(h, w_ref[...], preferred_element_type=F32)
        v_ref[0, rows, :] = acc[:, :dh].astype(BF16)
        for dirn in range(2):
            f = _forget_gate(acc[:, (1 + dirn) * dh:(2 + dirn) * dh], lg_ref[dirn])
            lf_ref[0, rows, dirn * dh:(dirn + 1) * dh] = jnp.log2(f)


def _prenorm_vf_call(x, g, mod_early, w_vf, lb_logits, w_in, col0, n_cols, tl=512):
    b, l, d = x.shape
    dh = w_vf.shape[1] // 3
    side_in, side_out, side_shape = _col_cast_specs(w_in, col0, n_cols, b * (l // tl),
                                                    lambda i, j: i * (l // tl) + j)
    row = lambda width: pl.BlockSpec((1, tl, width), lambda i, j: (i, j, 0))
    return pl.pallas_call(
        _prenorm_vf_kernel,
        grid=(b, l // tl),
        in_specs=[row(d),
                  pl.BlockSpec((1, d), lambda i, j: (0, 0)),
                  _mod_block(N_MOD_EARLY, d, lambda i, j: i),
                  pl.BlockSpec(w_vf.shape, lambda i, j: (0, 0), pipeline_mode=pl.Buffered(1)),
                  pl.BlockSpec(lb_logits.shape, lambda i, j: (0, 0, 0)),
                  side_in],
        out_specs=[row(d), row(dh), row(2 * dh), side_out],
        out_shape=[jax.ShapeDtypeStruct((b, l, d), BF16), jax.ShapeDtypeStruct((b, l, dh), BF16),
                   jax.ShapeDtypeStruct((b, l, 2 * dh), F32), side_shape],
        compiler_params=_params(("parallel", "parallel")),
        name="prenorm_vf",
    )(x, g, mod_early, w_vf, lb_logits, w_in)


def _scan_masks(c, reverse):
    t = lax.broadcasted_iota(jnp.int32, (c, c), 0)
    s = lax.broadcasted_iota(jnp.int32, (c, c), 1)
    half = c // 2
    before = (s >= t) if reverse else (s <= t)
    mask1 = ((t < half) == (s < half)) & before
    tri = jnp.where(before, 1.0, 0.0).astype(BF16)
    return mask1, jnp.concatenate([tri, tri], axis=1)


def _scan_cumsum(lf, tri2):
    hi = lf.astype(BF16)
    lo = (lf - hi.astype(F32)).astype(BF16)
    return jnp.dot(tri2, jnp.concatenate([hi, lo], axis=0), preferred_element_type=F32)


def _scan_refs(a, reverse):
    c = a.shape[0]
    half, quarter = c // 2, c // 4
    if reverse:
        return a[quarter:quarter + 1], a[half + quarter:half + quarter + 1], a[half:half + 1], a[0:1]
    return a[quarter - 1:quarter], a[half + quarter - 1:half + quarter], a[half - 1:half], a[c - 1:c]


def _scan_products(a, q, lf, v, st, st_kv, reverse):
    c = a.shape[0]
    half = c // 2
    m1_lo, m1_hi, m0, a_tot = _scan_refs(a, reverse)
    lo_rows, hi_rows = slice(0, half), slice(half, c)
    row = lax.broadcasted_iota(jnp.int32, a.shape, 0)
    e1 = a - jnp.where(row < half, m1_lo, m1_hi)
    pq = q.astype(F32) * jnp.exp2(e1)
    pk = (1.0 - jnp.exp2(lf)) * jnp.exp2(-e1)
    s1 = lax.dot_general(pq.astype(BF16), pk.astype(BF16), _NT, preferred_element_type=F32)
    if reverse:
        q_rows, m1_q, k_rows, m1_k = lo_rows, m1_lo, hi_rows, m1_hi
    else:
        q_rows, m1_q, k_rows, m1_k = hi_rows, m1_hi, lo_rows, m1_lo
    q0 = (pq[q_rows] * jnp.exp2(m1_q - m0)).astype(BF16)
    k0 = (pk[k_rows] * jnp.exp2(m0 - m1_k)).astype(BF16)
    s0 = lax.dot_general(q0, k0, _NT, preferred_element_type=F32)
    qi = jnp.concatenate([pq[lo_rows] * jnp.exp2(m1_lo), pq[hi_rows] * jnp.exp2(m1_hi)], axis=0)
    inter = jnp.dot(qi.astype(BF16), st_kv, preferred_element_type=F32)
    kh = jnp.concatenate([pk[lo_rows] * jnp.exp2(a_tot - m1_lo), pk[hi_rows] * jnp.exp2(a_tot - m1_hi)], axis=0)
    upd = lax.dot_general(v, kh.astype(BF16), _TN, preferred_element_type=F32)
    return s1, s0, inter, st * jnp.exp2(a_tot) + upd


def _scan_output(s1, s0, inter, v, mask1, reverse):
    c = s1.shape[0]
    half = c // 2
    lo_rows, hi_rows = slice(0, half), slice(half, c)
    q_rows, k_rows = (lo_rows, hi_rows) if reverse else (hi_rows, lo_rows)
    intra = jnp.dot(jnp.where(mask1, s1, 0.0).astype(BF16), v, preferred_element_type=F32)
    cross = jnp.dot(s0.astype(BF16), v[k_rows], preferred_element_type=F32)
    o = inter + intra
    o_q = o[q_rows] + cross
    return jnp.concatenate([o_q, o[hi_rows]] if reverse else [o[lo_rows], o_q], axis=0)


def _scan_kernel(*refs, reverse, readout):
    v_ref, lf_ref, s0_ref, q_ref = refs[:4]
    if readout:
        (ob_ref, g_ref, ng_ref, o_ref, st_ref, stkv_ref, a_ref) = refs[4:]
    else:
        (o_ref, st_ref, stkv_ref, a_ref) = refs[4:]
    c = SCAN_CHUNK
    n_batch, rows = v_ref.shape[0], v_ref.shape[1]
    n_chunks = rows // c
    n_heads = v_ref.shape[2] // HEAD_DIM
    recurrences = [(h, bi) for h in range(n_heads) for bi in range(n_batch)]

    @pl.when(pl.program_id(0) == 0)
    def _():
        st_ref[...] = s0_ref[...]
        for h, bi in recurrences:
            stkv_ref[bi, h] = s0_ref[bi, h].T.astype(BF16)

    mask1, tri2 = _scan_masks(c, reverse)

    def cumsum_body(ci, carry):
        r0 = pl.multiple_of(ci * c, c)
        for bi in range(n_batch):
            a_ref[bi, pl.ds(r0, c), :] = _scan_cumsum(lf_ref[bi, pl.ds(r0, c), :], tri2)
        return carry

    lax.fori_loop(0, n_chunks, cumsum_body, 0, unroll=4)

    def chunk_body(ci, carry):
        cidx = (n_chunks - 1 - ci) if reverse else ci
        r0 = pl.multiple_of(cidx * c, c)
        tile = lambda ref, h, bi: ref[bi, pl.ds(r0, c), h * HEAD_DIM:(h + 1) * HEAD_DIM]
        wave = []
        for h, bi in recurrences:
            s1, s0, inter, st_new = _scan_products(
                tile(a_ref, h, bi), tile(q_ref, h, bi), tile(lf_ref, h, bi), tile(v_ref, h, bi),
                st_ref[bi, h], stkv_ref[bi, h], reverse)
            st_ref[bi, h] = st_new
            stkv_ref[bi, h] = st_new.T.astype(BF16)
            wave.append((s1, s0, inter))
        for (h, bi), (s1, s0, inter) in zip(recurrences, wave):
            lanes = slice(h * HEAD_DIM, (h + 1) * HEAD_DIM)
            o = _scan_output(s1, s0, inter, tile(v_ref, h, bi), mask1, reverse)
            if readout:
                o = o + tile(ob_ref, h, bi)
                ms = jnp.mean(o * o, axis=-1, keepdims=True)
                y = o * lax.rsqrt(ms + EPS) * ng_ref[:, lanes]
                o = y * tile(g_ref, h, bi).astype(F32)
            o_ref[bi, pl.ds(r0, c), lanes] = o.astype(o_ref.dtype)
        return carry

    lax.fori_loop(0, n_chunks, chunk_body, 0, unroll=4)


def _scan_call(v, lf, s0, q, *, reverse, ob=None, g=None, ng=None):
    b, l, _ = v[0].shape
    dh = ng.shape[1] if ng is not None else s0.shape[1] * HEAD_DIM
    tb = min(SCAN_BLOCK, l)
    n_blocks = l // tb
    readout = ob is not None
    blk = (lambda j: n_blocks - 1 - j) if reverse else (lambda j: j)
    seq_spec = lambda group: pl.BlockSpec((b, tb, dh), lambda j: (0, blk(j), group))
    st_spec = pl.BlockSpec(s0.shape, lambda j: (0, 0, 0, 0))
    in_specs = [seq_spec(v[1]), seq_spec(lf[1]), st_spec, seq_spec(q[1])]
    args = [v[0], lf[0], s0, q[0]]
    if readout:
        in_specs += [seq_spec(ob[1]), seq_spec(g[1]), pl.BlockSpec((1, dh), lambda j: (0, 0))]
        args += [ob[0], g[0], ng]
    return pl.pallas_call(
        functools.partial(_scan_kernel, reverse=reverse, readout=readout),
        grid=(n_blocks,),
        in_specs=in_specs,
        out_specs=seq_spec(0),
        out_shape=jax.ShapeDtypeStruct((b, l, dh), BF16 if readout else F32),
        scratch_shapes=[pltpu.VMEM(s0.shape, F32), pltpu.VMEM(s0.shape, BF16), pltpu.VMEM((b, tb, dh), F32)],
        compiler_params=_params(("arbitrary",)),
        name="scan_readout_fwd" if readout else "scan_out_bwd",
    )(*args)


def _ctx_proj_kernel(ctx_ref, g_ref, mod_ref, w_ref, o_ref, wbf_ref, hc_ref):
    @pl.when(pl.program_id(0) == 0)
    def _():
        for bi in range(ctx_ref.shape[0]):
            hc_ref[bi] = _norm_modulate(ctx_ref[bi], g_ref[...], mod_ref[0, 0:1, :],
                                        mod_ref[1, 0:1, :]).astype(BF16)

    w = w_ref[...].astype(BF16)
    wbf_ref[...] = w
    for bi in range(ctx_ref.shape[0]):
        o_ref[bi] = jnp.dot(hc_ref[bi], w, preferred_element_type=F32)


def _ctx_proj_call(ctx, g, mod_early, mod_row, w, width, tn=1024):
    b, lc, d = ctx.shape
    return pl.pallas_call(
        _ctx_proj_kernel,
        grid=(width // tn,),
        in_specs=[pl.BlockSpec((b, lc, d), lambda j: (0, 0, 0)),
                  pl.BlockSpec((1, d), lambda j: (0, 0)),
                  _mod_block(N_MOD_EARLY, d, lambda j: mod_row),
                  pl.BlockSpec((d, tn), lambda j: (0, j))],
        out_specs=[pl.BlockSpec((b, lc, tn), lambda j: (0, 0, j)), pl.BlockSpec((d, tn), lambda j: (0, j))],
        out_shape=[jax.ShapeDtypeStruct((b, lc, width), F32), jax.ShapeDtypeStruct((d, width), BF16)],
        scratch_shapes=[pltpu.VMEM((b, lc, d), BF16)],
        compiler_params=_params(("arbitrary",)),
        name="ctx_proj",
    )(ctx, g, mod_early, w)


def _ctx_state_kernel(pv_ref, pf_ref, pb_ref, lg_ref, sf_ref, sb_ref):
    c = SCAN_CHUNK
    n_batch, rows = pv_ref.shape[0], pv_ref.shape[1]
    n_chunks = rows // c
    n_heads = pv_ref.shape[2] // HEAD_DIM
    for reverse, p_ref, s_ref in ((False, pf_ref, sf_ref), (True, pb_ref, sb_ref)):
        _, tri2 = _scan_masks(c, reverse)
        s_ref[...] = jnp.zeros(s_ref.shape, F32)

        def chunk_body(ci, carry, reverse=reverse, p_ref=p_ref, s_ref=s_ref, tri2=tri2):
            cidx = (n_chunks - 1 - ci) if reverse else ci
            r0 = pl.multiple_of(cidx * c, c)
            recurrences = [(h, bi) for h in range(n_heads) for bi in range(n_batch)]
            tile = lambda ref, h, bi: ref[bi, pl.ds(r0, c), h * HEAD_DIM:(h + 1) * HEAD_DIM]
            gates = [_forget_gate(tile(p_ref, h, bi), lg_ref[int(reverse), :, h * HEAD_DIM:(h + 1) * HEAD_DIM])
                     for h, bi in recurrences]
            sums = [_scan_cumsum(jnp.log2(f), tri2) for f in gates]
            for (h, bi), f, a in zip(recurrences, gates, sums):
                a_tot = _scan_refs(a, reverse)[3]
                kh = ((1.0 - f) * jnp.exp2(a_tot - a)).astype(BF16)
                upd = lax.dot_general(tile(pv_ref, h, bi).astype(BF16), kh, _TN, preferred_element_type=F32)
                s_ref[bi, h] = s_ref[bi, h] * jnp.exp2(a_tot) + upd
            return carry

        lax.fori_loop(0, n_chunks, chunk_body, 0)


def _ctx_state_call(pc, lb_logits, dh):
    b, lc, _ = pc.shape
    n_heads = dh // HEAD_DIM
    group = lambda n: pl.BlockSpec((b, lc, dh), lambda j: (0, 0, n))
    st_spec = pl.BlockSpec((b, n_heads, HEAD_DIM, HEAD_DIM), lambda j: (0, 0, 0, 0))
    st_shape = jax.ShapeDtypeStruct((b, n_heads, HEAD_DIM, HEAD_DIM), F32)
    return pl.pallas_call(
        _ctx_state_kernel,
        grid=(1,),
        in_specs=[group(0), group(1), group(2), pl.BlockSpec(lb_logits.shape, lambda j: (0, 0, 0))],
        out_specs=[st_spec, st_spec],
        out_shape=[st_shape, st_shape],
        compiler_params=_params(("arbitrary",)),
        name="ctx_state",
    )(pc, pc, pc, lb_logits)


CONV_TILE_ROWS = 16
_HPAD = 16
_HROW = GRID_W + 2 * _HPAD


def _conv_kernel(cur_ref, prev_ref, next_ref, w_ref, b_ref, lg_ref, lb_ref, cc_ref, wm_ref, bm_ref,
                 o_ref, modl_ref, hpad_ref, vpad_ref, y_ref, *, n_tiles):
    tr = CONV_TILE_ROWS
    half = w_ref.shape[1] // 2
    n_lane_tiles = half // 128
    i = pl.program_id(1)

    modl_ref[0] = _mod_rows(cc_ref, wm_ref, bm_ref)

    hpad_ref[:, :, :_HPAD, :] = jnp.zeros((tr, n_lane_tiles, _HPAD, 128), F32)
    hpad_ref[:, :, _HPAD + GRID_W:, :] = jnp.zeros((tr, n_lane_tiles, _HPAD, 128), F32)
    for r in range(tr):
        for lt in range(n_lane_tiles):
            hpad_ref[r, lt, _HPAD:_HPAD + GRID_W, :] = cur_ref[0, r * GRID_W:(r + 1) * GRID_W,
                                                               lt * 128:(lt + 1) * 128]
    halo = CONV_PAD * GRID_W
    for lt in range(n_lane_tiles):
        lanes = slice(lt * 128, (lt + 1) * 128)
        top = prev_ref[0, tr * GRID_W - halo:, lanes]
        bot = next_ref[0, :halo, lanes]
        vpad_ref[lt, :halo, :] = jnp.where(i > 0, top, 0.0)
        vpad_ref[lt, halo:halo + tr * GRID_W, :] = cur_ref[0, :, half + lt * 128:half + (lt + 1) * 128]
        vpad_ref[lt, halo + tr * GRID_W:, :] = jnp.where(i < n_tiles - 1, bot, 0.0)

    def h_row(r, carry):
        o0 = pl.multiple_of(r * GRID_W, GRID_W)
        for lt in range(n_lane_tiles):
            lanes = slice(lt * 128, (lt + 1) * 128)
            acc = jnp.zeros((GRID_W, 128), F32)
            for jj in range(CONV_WIDTH):
                off = _HPAD - CONV_PAD + jj
                acc = acc + hpad_ref[r, lt, off:off + GRID_W, :] * w_ref[jj:jj + 1, lanes]
            y_ref[pl.ds(o0, GRID_W), lanes] = acc + b_ref[:, lanes]
        return carry

    lax.fori_loop(0, tr, h_row, 0)

    n_sub = GRID_W // 8
    for lt in range(n_lane_tiles):
        vl = slice(half + lt * 128, half + (lt + 1) * 128)
        wv = [jnp.broadcast_to(w_ref[jj:jj + 1, vl], (8, 128)) for jj in range(CONV_WIDTH)]
        bias = b_ref[:, vl]

        def v_row(r, carry, lt=lt, vl=vl, wv=wv, bias=bias):
            v0 = pl.multiple_of(r * GRID_W, GRID_W)
            acc = [jnp.zeros((8, 128), F32)] * n_sub
            for jj in range(CONV_WIDTH):
                src = pl.multiple_of(v0 + jj * GRID_W, GRID_W)
                tap = vpad_ref[lt, pl.ds(src, GRID_W), :]
                acc = [acc[s] + tap[8 * s:8 * s + 8, :] * wv[jj] for s in range(n_sub)]
            y_ref[pl.ds(v0, GRID_W), vl] = jnp.concatenate(acc, axis=0) + bias
            return carry

        lax.fori_loop(0, tr, v_row, 0, unroll=2)

    def ln_row(r, carry):
        o0 = pl.multiple_of(r * GRID_W, GRID_W)
        y = y_ref[pl.ds(o0, GRID_W), :]
        mu = jnp.mean(y, axis=-1, keepdims=True)
        yc = y - mu
        var = jnp.mean(yc * yc, axis=-1, keepdims=True)
        z = yc * lax.rsqrt(var + EPS) * lg_ref[...] + lb_ref[...]
        o_ref[0, pl.ds(o0, GRID_W), :] = _silu(z).astype(BF16)
        return carry

    lax.fori_loop(0, tr, ln_row, 0, unroll=4)


def _conv_call(u, w, bias, ln_g, ln_b, cc, w_mod, b_mod):
    b, l, ch = u.shape
    half = ch // 2
    tile = CONV_TILE_ROWS * GRID_W
    n_tiles = l // tile
    rows, d = cc.shape
    n_late = w_mod.shape[1] // d - N_MOD_EARLY
    tnm = n_late * d // (b * n_tiles)
    per = d // tnm
    step = lambda i, j: i * n_tiles + j
    return pl.pallas_call(
        functools.partial(_conv_kernel, n_tiles=n_tiles),
        grid=(b, n_tiles),
        in_specs=[pl.BlockSpec((1, tile, ch), lambda i, j: (i, j, 0)),
                  pl.BlockSpec((1, tile, half), lambda i, j: (i, jnp.maximum(j - 1, 0), 1)),
                  pl.BlockSpec((1, tile, half), lambda i, j: (i, jnp.minimum(j + 1, n_tiles - 1), 1)),
                  pl.BlockSpec((CONV_WIDTH, ch), lambda i, j: (0, 0)),
                  pl.BlockSpec((1, ch), lambda i, j: (0, 0)),
                  pl.BlockSpec((1, ch), lambda i, j: (0, 0)),
                  pl.BlockSpec((1, ch), lambda i, j: (0, 0)),
                  pl.BlockSpec((rows, d), lambda i, j: (0, 0)),
                  pl.BlockSpec((d, tnm), lambda i, j: (0, N_MOD_EARLY * per + step(i, j))),
                  pl.BlockSpec((1, tnm), lambda i, j: (0, N_MOD_EARLY * per + step(i, j)))],
        out_specs=[pl.BlockSpec((1, tile, ch), lambda i, j: (i, j, 0)),
                   pl.BlockSpec((1, rows, tnm), lambda i, j: (step(i, j) // per, 0, step(i, j) % per))],
        out_shape=[jax.ShapeDtypeStruct((b, l, ch), BF16), jax.ShapeDtypeStruct((n_late, rows, d), F32)],
        scratch_shapes=[pltpu.VMEM((CONV_TILE_ROWS, half // 128, _HROW, 128), F32),
                        pltpu.VMEM((half // 128, tile + 2 * CONV_PAD * GRID_W, 128), F32),
                        pltpu.VMEM((tile, ch), F32)],
        compiler_params=_params(("arbitrary", "arbitrary")),
        name="conv",
    )(u, u, u, w, bias, ln_g, ln_b, cc, w_mod, b_mod)


MIX_SLICE = 256

def _mix_kernel(uc_ref, og_ref, gc_ref, gh_ref, x_ref, wpw_ref, wh_ref, wo_ref,
                npost_ref, npre_ref, mod_ref, x1_ref, h2_ref):
    tm = x_ref.shape[1]
    n_split = tm // MIX_SLICE
    for r in range(n_split):
        rows = slice(r * MIX_SLICE, (r + 1) * MIX_SLICE)
        y_c = jnp.dot(uc_ref[0, rows, :], wpw_ref[...], preferred_element_type=F32)
        y_h = jnp.dot(og_ref[0, rows, :], wh_ref[...], preferred_element_type=F32)
        z = gc_ref[0, rows, :].astype(F32) * y_c + gh_ref[0, rows, :].astype(F32) * y_h
        y = jnp.dot(z.astype(BF16), wo_ref[...], preferred_element_type=F32)
        ms = jnp.mean(y * y, axis=-1, keepdims=True)
        yn = y * lax.rsqrt(ms + EPS) * npost_ref[...]
        x1 = x_ref[0, rows, :] + mod_ref[0, 0:1, :] * yn
        x1_ref[0, rows, :] = x1
        h2_ref[0, rows, :] = _norm_modulate(x1, npre_ref[...], mod_ref[1, 0:1, :],
                                            mod_ref[2, 0:1, :]).astype(BF16)


def _mix_call(uc, og, gates, x, wpw, wh, wo, npost, npre, mod_late, tm=512):
    b, l, d = x.shape
    dc = uc.shape[2]
    gc = gh = gates
    row = lambda width, col=0: pl.BlockSpec((1, tm, width), lambda i, j: (i, j, col))
    const = lambda shape: pl.BlockSpec(shape, lambda i, j: (0,) * len(shape), pipeline_mode=pl.Buffered(1))
    return pl.pallas_call(
        _mix_kernel,
        grid=(b, l // tm),
        in_specs=[row(dc), row(dc), row(d, 0), row(d, 1), row(d),
                  const(wpw.shape), const(wh.shape), const(wo.shape),
                  const((1, d)), const((1, d)),
                  _mod_block(N_MOD - N_MOD_EARLY, d, lambda i, j: i)],
        out_specs=[row(d), row(d)],
        out_shape=[jax.ShapeDtypeStruct((b, l, d), F32), jax.ShapeDtypeStruct((b, l, d), BF16)],
        compiler_params=_params(("parallel", "parallel")),
        name="mix",
    )(uc, og, gc, gh, x, wpw, wh, wo, npost, npre, mod_late)


MLP_SLICE = 256

def _mlp_kernel(h_ref, w1_ref, w2_ref, x1_ref, npost_ref, mod_ref, o_ref, *, n_ff):
    j = pl.program_id(2)

    def partial_sum(rows):
        a = jnp.dot(h_ref[0, rows, :], w1_ref[...], preferred_element_type=F32)
        a = jnp.square(jnp.maximum(a, 0.0)).astype(BF16)
        return jnp.dot(a, w2_ref[...], preferred_element_type=F32)

    @pl.when(j == 0)
    def _():
        o_ref[0] = partial_sum(slice(None))

    @pl.when((j > 0) & (j < n_ff - 1))
    def _():
        o_ref[0] += partial_sum(slice(None))

    @pl.when(j == n_ff - 1)
    def _():
        for r in range(o_ref.shape[1] // MLP_SLICE):
            rows = slice(r * MLP_SLICE, (r + 1) * MLP_SLICE)
            y = o_ref[0, rows, :] + partial_sum(rows)
            ms = jnp.mean(y * y, axis=-1, keepdims=True)
            yn = y * lax.rsqrt(ms + EPS) * npost_ref[...]
            o_ref[0, rows, :] = x1_ref[0, rows, :] + mod_ref[3, 0:1, :] * yn


def _mlp_call(h2, w1, w2, x1, npost, mod_late, tm=1024, tf=1024):
    b, l, d = x1.shape
    dff = w1.shape[1]
    n_ff = dff // tf
    assert n_ff >= 2, "first and last d_ff steps are distinct code paths"
    return pl.pallas_call(
        functools.partial(_mlp_kernel, n_ff=n_ff),
        grid=(b, l // tm, n_ff),
        in_specs=[pl.BlockSpec((1, tm, d), lambda i, m, j: (i, m, 0)),
                  pl.BlockSpec((d, tf), lambda i, m, j: (0, j)),
                  pl.BlockSpec((tf, d), lambda i, m, j: (j, 0)),
                  pl.BlockSpec((1, tm, d), lambda i, m, j: (i, m, 0), pipeline_mode=pl.Buffered(1)),
                  pl.BlockSpec((1, d), lambda i, m, j: (0, 0)),
                  _mod_block(N_MOD - N_MOD_EARLY, d, lambda i, m, j: i)],
        out_specs=pl.BlockSpec((1, tm, d), lambda i, m, j: (i, m, 0)),
        out_shape=jax.ShapeDtypeStruct((b, l, d), F32),
        compiler_params=_params(("parallel", "parallel", "arbitrary")),
        name="mlp",
    )(h2, w1, w2, x1, npost, mod_late)


def kernel(x, c, ctx, c_ctx, w_mod, b_mod, norm_pre_mix, norm_post_mix, norm_pre_mlp, norm_post_mlp, w_in,
           conv_dw_w, conv_dw_b, conv_ln_g, conv_ln_b, conv_pw_w, hgrn_lb_logits, hgrn_norm_g, hgrn_out_w,
           w_out, mlp_w1, mlp_w2):
    assert w_mod.shape[0] == 1, "single-layer block"
    b, l, d = x.shape
    lc = ctx.shape[1]
    dh = hgrn_norm_g.shape[1]
    dc = conv_dw_w.shape[2]
    assert l % (CONV_TILE_ROWS * GRID_W) == 0 and CONV_PAD <= CONV_TILE_ROWS and CONV_PAD <= _HPAD
    assert l % SCAN_BLOCK == 0 and lc % SCAN_CHUNK == 0

    cond = jnp.concatenate([c, c_ctx[None, :]], axis=0)
    cc = jnp.pad(cond[:, None, :], ((0, 0), (0, MOD_ROW_STRIDE - 1), (0, 0))).reshape(-1, d)
    mod_early = _mod_call(cc, w_mod[0], b_mod, N_MOD_EARLY)

    w_in0 = w_in[0]
    off = lambda n: n * dh

    pc, w_vf = _ctx_proj_call(ctx, norm_pre_mix, mod_early, b, w_in0, 3 * dh)
    s_f, s_b = _ctx_state_call(pc, hgrn_lb_logits, dh)

    gate_col = off(5) + 2 * dc
    h, v, lf, w_a = _prenorm_vf_call(x, norm_pre_mix, mod_early, w_vf, hgrn_lb_logits, w_in0, off(3),
                                     gate_col - off(3))
    h = h.reshape(b * l, d)
    seq = lambda t: t.reshape(b, l, t.shape[-1])
    rest = lambda col: col - off(3)
    qg, wpw, wh, wo, w_b = _proj_call(h, w_a, [rest(off(3))], 2 * dh, _ep_scaled_silu, [BF16],
                                      side=(conv_pw_w[0], hgrn_out_w[0], w_out[0]),
                                      side_cols=(w_in0, gate_col, 2 * d), name="proj_qg")
    (u,) = _proj_call(h, w_a, [rest(off(5)), rest(off(5) + dc)], dc, _ep_glu, [F32], name="proj_glu")
    gates, w1, w2 = _proj_call(h, w_b, [0], 2 * d, _ep_sigmoid, [BF16],
                               side=(mlp_w1[0], mlp_w2[0]), name="proj_gates")

    qg = seq(qg)
    o_b = _scan_call((v, 0), (lf, 1), s_b, (qg, 0), reverse=True)
    og = _scan_call((v, 0), (lf, 0), s_f, (qg, 0), reverse=False, ob=(o_b, 0), g=(qg, 1),
                    ng=hgrn_norm_g)

    uc, mod_late = _conv_call(seq(u), conv_dw_w[0], conv_dw_b, conv_ln_g, conv_ln_b, cc, w_mod[0], b_mod)

    x1, h2 = _mix_call(uc, og, seq(gates), x, wpw, wh, wo, norm_post_mix, norm_pre_mlp, mod_late)
    return _mlp_call(h2, w1, w2, x1, norm_post_mlp, mod_late)
```

```python
import functools

import jax
import jax.numpy as jnp
from jax import lax
from jax.experimental import pallas as pl
from jax.experimental.pallas import tpu as pltpu

F32 = jnp.float32
BF16 = jnp.bfloat16

GRID_W = 64
CONV_WIDTH = 31
CONV_PAD = (CONV_WIDTH - 1) // 2
HEAD_DIM = 128
N_MOD = 6
N_MOD_EARLY = 2
MOD_ROW_STRIDE = 16
EPS = 1e-6

SCAN_CHUNK = 64
SCAN_BLOCK = 512

VMEM_LIMIT = 56 * 1024 * 1024

_NT = (((1,), (1,)), ((), ()))
_TN = (((0,), (0,)), ((), ()))


def _params(sem):
    return pltpu.CompilerParams(dimension_semantics=sem, vmem_limit_bytes=VMEM_LIMIT)


def _sigmoid(x):
    return 0.5 * jnp.tanh(0.5 * x) + 0.5


def _silu(x):
    return x * _sigmoid(x)


def _norm_modulate(x, gain, shift, scale):
    ms = jnp.mean(x * x, axis=-1, keepdims=True)
    return (x * lax.rsqrt(ms + EPS) * gain) * (1.0 + scale) + shift


def _forget_gate(raw, logits):
    e = jnp.exp(logits - jnp.max(logits, axis=0, keepdims=True))
    lb = e[0:1] / jnp.sum(e, axis=0, keepdims=True)
    return lb + (1.0 - lb) * jax.nn.sigmoid(raw)


def _mod_rows(c_ref, w_ref, b_ref):
    s = _silu(c_ref[...]).astype(BF16)
    return jnp.dot(s, w_ref[...].astype(BF16), preferred_element_type=F32) + b_ref[...]


def _mod_kernel(c_ref, w_ref, b_ref, o_ref):
    o_ref[0] = _mod_rows(c_ref, w_ref, b_ref)


def _mod_call(cc, w_mod, b_mod, n_seg, tn=1024):
    rows, d = cc.shape
    return pl.pallas_call(
        _mod_kernel,
        grid=(n_seg * d // tn,),
        in_specs=[pl.BlockSpec((rows, d), lambda j: (0, 0)),
                  pl.BlockSpec((d, tn), lambda j: (0, j)),
                  pl.BlockSpec((1, tn), lambda j: (0, j))],
        out_specs=pl.BlockSpec((1, rows, tn), lambda j: (j // (d // tn), 0, j % (d // tn))),
        out_shape=jax.ShapeDtypeStruct((n_seg, rows, d), F32),
        compiler_params=_params(("parallel",)),
        name="mod",
    )(cc, w_mod, b_mod)


def _mod_block(n_seg, d, row_of):
    return pl.BlockSpec((n_seg, 8, d), lambda *idx: (0, row_of(*idx) * (MOD_ROW_STRIDE // 8), 0))


def _proj_kernel(*refs, n_w, n_side, n_out, epilogue):
    h_ref = refs[0]
    pos = 1
    w_refs = refs[pos:pos + n_w]
    pos += n_w
    side_in = refs[pos:pos + n_side]
    pos += n_side
    out_refs = refs[pos:pos + n_out]
    pos += n_out
    side_out = refs[pos:pos + n_side]
    pos += n_side
    wbf_refs = refs[pos:] or w_refs

    if wbf_refs is not w_refs:
        @pl.when(pl.program_id(1) == 0)
        def _():
            for w_ref, wbf_ref in zip(w_refs, wbf_refs):
                wbf_ref[...] = w_ref[...].astype(BF16)

    for si_ref, so_ref in zip(side_in, side_out):
        so_ref[...] = si_ref[...].astype(BF16)

    h = h_ref[...]
    accs = [jnp.dot(h, wbf_ref[...], preferred_element_type=F32) for wbf_ref in wbf_refs]
    outs = epilogue(accs)
    for o_ref, o in zip(out_refs, outs):
        o_ref[...] = o.astype(o_ref.dtype)


def _ep_scaled_silu(accs):
    return (_silu(accs[0]) * jnp.where(pl.program_id(0) == 0, HEAD_DIM ** -0.5, 1.0),)


def _ep_glu(accs):
    return (accs[0] * _sigmoid(accs[1]),)


def _ep_sigmoid(accs):
    return (_sigmoid(accs[0]),)


SIDE_COL_TILE = 1024


def _col_cast_specs(arr, col0, n_cols, n_steps, step):
    ct = SIDE_COL_TILE
    n_cb = n_cols // ct
    rs = n_steps // n_cb
    assert col0 % ct == 0 and n_cb * ct == n_cols and n_cb * rs == n_steps and arr.shape[0] % rs == 0
    rb = arr.shape[0] // rs
    return (pl.BlockSpec((rb, ct), lambda *idx: (step(*idx) % rs, col0 // ct + step(*idx) // rs)),
            pl.BlockSpec((rb, ct), lambda *idx: (step(*idx) % rs, step(*idx) // rs)),
            jax.ShapeDtypeStruct((arr.shape[0], n_cols), BF16))


def _proj_call(h, w, col_offsets, width, epilogue, out_dtypes, side=(), side_cols=None, tm=1024, tn=1024,
               name="proj"):
    m, k = h.shape
    tm = min(tm, m)
    n_w = len(col_offsets)
    n_j, n_i = width // tn, m // tm
    in_specs = [pl.BlockSpec((tm, k), lambda j, i: (i, 0))]
    for off in col_offsets:
        in_specs.append(pl.BlockSpec((k, tn), lambda j, i, off=off: (0, off // tn + j)))
    side_in = [pl.BlockSpec((a.shape[0] // (n_j * n_i), a.shape[1]), lambda j, i: (j * n_i + i, 0))
               for a in side]
    side_out = list(side_in)
    side_shapes = [jax.ShapeDtypeStruct(a.shape, BF16) for a in side]
    side_args = list(side)
    if side_cols is not None:
        spec_in, spec_out, shape = _col_cast_specs(*side_cols, n_j * n_i, lambda j, i: j * n_i + i)
        side_in.append(spec_in)
        side_out.append(spec_out)
        side_shapes.append(shape)
        side_args.append(side_cols[0])
    return pl.pallas_call(
        functools.partial(_proj_kernel, n_w=n_w, n_side=len(side_args), n_out=len(out_dtypes),
                          epilogue=epilogue),
        grid=(n_j, n_i),
        in_specs=in_specs + side_in,
        out_specs=[pl.BlockSpec((tm, tn), lambda j, i: (i, j)) for _ in out_dtypes] + side_out,
        out_shape=[jax.ShapeDtypeStruct((m, width), dt) for dt in out_dtypes] + side_shapes,
        scratch_shapes=[pltpu.VMEM((k, tn), BF16) for _ in range(n_w)] if w.dtype != BF16 else [],
        compiler_params=_params(("parallel", "arbitrary")),
        name=name,
    )(h, *([w] * n_w), *side_args)


PRENORM_SLICE = 256


def _prenorm_vf_kernel(x_ref, g_ref, mod_ref, w_ref, lg_ref, wside_ref,
                       h_ref, v_ref, lf_ref, wside_out_ref):
    wside_out_ref[...] = wside_ref[...].astype(BF16)
    dh = v_ref.shape[2]
    for r in range(x_ref.shape[1] // PRENORM_SLICE):
        rows = slice(r * PRENORM_SLICE, (r + 1) * PRENORM_SLICE)
        h = _norm_modulate(x_ref[0, rows, :], g_ref[...], mod_ref[0, 0:1, :], mod_ref[1, 0:1, :]).astype(BF16)
        h_ref[0, rows, :] = h
        acc = jnp.dot(h, w_ref[...], preferred_element_type=F32)
        v_ref[0, rows, :] = acc[:, :dh].astype(BF16)
        for dirn in range(2):
            f = _forget_gate(acc[:, (1 + dirn) * dh:(2 + dirn) * dh], lg_ref[dirn])
            lf_ref[0, rows, dirn * dh:(dirn + 1) * dh] = jnp.log2(f)


def _prenorm_vf_call(x, g, mod_early, w_vf, lb_logits, w_in, col0, n_cols, tl=512):
    b, l, d = x.shape
    dh = w_vf.shape[1] // 3
    side_in, side_out, side_shape = _col_cast_specs(w_in, col0, n_cols, b * (l // tl),
                                                    lambda i, j: i * (l // tl) + j)
    row = lambda width: pl.BlockSpec((1, tl, width), lambda i, j: (i, j, 0))
    return pl.pallas_call(
        _prenorm_vf_kernel,
        grid=(b, l // tl),
        in_specs=[row(d),
                  pl.BlockSpec((1, d), lambda i, j: (0, 0)),
                  _mod_block(N_MOD_EARLY, d, lambda i, j: i),
                  pl.BlockSpec(w_vf.shape, lambda i, j: (0, 0), pipeline_mode=pl.Buffered(1)),
                  pl.BlockSpec(lb_logits.shape, lambda i, j: (0, 0, 0)),
                  side_in],
        out_specs=[row(d), row(dh), row(2 * dh), side_out],
        out_shape=[jax.ShapeDtypeStruct((b, l, d), BF16), jax.ShapeDtypeStruct((b, l, dh), BF16),
                   jax.ShapeDtypeStruct((b, l, 2 * dh), F32), side_shape],
        compiler_params=_params(("parallel", "parallel")),
        name="prenorm_vf",
    )(x, g, mod_early, w_vf, lb_logits, w_in)


def _scan_masks(c, reverse):
    t = lax.broadcasted_iota(jnp.int32, (c, c), 0)
    s = lax.broadcasted_iota(jnp.int32, (c, c), 1)
    half = c // 2
    before = (s >= t) if reverse else (s <= t)
    mask1 = ((t < half) == (s < half)) & before
    tri = jnp.where(before, 1.0, 0.0).astype(BF16)
    return mask1, jnp.concatenate([tri, tri], axis=1)


def _scan_cumsum(lf, tri2):
    hi = lf.astype(BF16)
    lo = (lf - hi.astype(F32)).astype(BF16)
    return jnp.dot(tri2, jnp.concatenate([hi, lo], axis=0), preferred_element_type=F32)


def _scan_refs(a, reverse):
    c = a.shape[0]
    half, quarter = c // 2, c // 4
    if reverse:
        return a[quarter:quarter + 1], a[half + quarter:half + quarter + 1], a[half:half + 1], a[0:1]
    return a[quarter - 1:quarter], a[half + quarter - 1:half + quarter], a[half - 1:half], a[c - 1:c]


def _scan_products(a, q, lf, v, st, st_kv, reverse):
    c = a.shape[0]
    half = c // 2
    m1_lo, m1_hi, m0, a_tot = _scan_refs(a, reverse)
    lo_rows, hi_rows = slice(0, half), slice(half, c)
    row = lax.broadcasted_iota(jnp.int32, a.shape, 0)
    e1 = a - jnp.where(row < half, m1_lo, m1_hi)
    pq = q.astype(F32) * jnp.exp2(e1)
    pk = (1.0 - jnp.exp2(lf)) * jnp.exp2(-e1)
    s1 = lax.dot_general(pq.astype(BF16), pk.astype(BF16), _NT, preferred_element_type=F32)
    if reverse:
        q_rows, m1_q, k_rows, m1_k = lo_rows, m1_lo, hi_rows, m1_hi
    else:
        q_rows, m1_q, k_rows, m1_k = hi_rows, m1_hi, lo_rows, m1_lo
    q0 = (pq[q_rows] * jnp.exp2(m1_q - m0)).astype(BF16)
    k0 = (pk[k_rows] * jnp.exp2(m0 - m1_k)).astype(BF16)
    s0 = lax.dot_general(q0, k0, _NT, preferred_element_type=F32)
    qi = jnp.concatenate([pq[lo_rows] * jnp.exp2(m1_lo), pq[hi_rows] * jnp.exp2(m1_hi)], axis=0)
    inter = jnp.dot(qi.astype(BF16), st_kv, preferred_element_type=F32)
    kh = jnp.concatenate([pk[lo_rows] * jnp.exp2(a_tot - m1_lo), pk[hi_rows] * jnp.exp2(a_tot - m1_hi)], axis=0)
    upd = lax.dot_general(v, kh.astype(BF16), _TN, preferred_element_type=F32)
    return s1, s0, inter, st * jnp.exp2(a_tot) + upd


def _scan_output(s1, s0, inter, v, mask1, reverse):
    c = s1.shape[0]
    half = c // 2
    lo_rows, hi_rows = slice(0, half), slice(half, c)
    q_rows, k_rows = (lo_rows, hi_rows) if reverse else (hi_rows, lo_rows)
    intra = jnp.dot(jnp.where(mask1, s1, 0.0).astype(BF16), v, preferred_element_type=F32)
    cross = jnp.dot(s0.astype(BF16), v[k_rows], preferred_element_type=F32)
    o = inter + intra
    o_q = o[q_rows] + cross
    return jnp.concatenate([o_q, o[hi_rows]] if reverse else [o[lo_rows], o_q], axis=0)


def _scan_kernel(*refs, reverse, readout):
    v_ref, lf_ref, s0_ref, q_ref = refs[:4]
    if readout:
        (ob_ref, g_ref, ng_ref, o_ref, st_ref, stkv_ref, a_ref) = refs[4:]
    else:
        (o_ref, st_ref, stkv_ref, a_ref) = refs[4:]
    c = SCAN_CHUNK
    n_batch, rows = v_ref.shape[0], v_ref.shape[1]
    n_chunks = rows // c
    n_heads = v_ref.shape[2] // HEAD_DIM
    recurrences = [(h, bi) for h in range(n_heads) for bi in range(n_batch)]

    @pl.when(pl.program_id(0) == 0)
    def _():
        st_ref[...] = s0_ref[...]
        for h, bi in recurrences:
            stkv_ref[bi, h] = s0_ref[bi, h].T.astype(BF16)

    mask1, tri2 = _scan_masks(c, reverse)

    def cumsum_body(ci, carry):
        r0 = pl.multiple_of(ci * c, c)
        for bi in range(n_batch):
            a_ref[bi, pl.ds(r0, c), :] = _scan_cumsum(lf_ref[bi, pl.ds(r0, c), :], tri2)
        return carry

    lax.fori_loop(0, n_chunks, cumsum_body, 0, unroll=4)

    def chunk_body(ci, carry):
        cidx = (n_chunks - 1 - ci) if reverse else ci
        r0 = pl.multiple_of(cidx * c, c)
        tile = lambda ref, h, bi: ref[bi, pl.ds(r0, c), h * HEAD_DIM:(h + 1) * HEAD_DIM]
        wave = []
        for h, bi in recurrences:
            s1, s0, inter, st_new = _scan_products(
                tile(a_ref, h, bi), tile(q_ref, h, bi), tile(lf_ref, h, bi), tile(v_ref, h, bi),
                st_ref[bi, h], stkv_ref[bi, h], reverse)
            st_ref[bi, h] = st_new
            stkv_ref[bi, h] = st_new.T.astype(BF16)
            wave.append((s1, s0, inter))
        for (h, bi), (s1, s0, inter) in zip(recurrences, wave):
            lanes = slice(h * HEAD_DIM, (h + 1) * HEAD_DIM)
            o = _scan_output(s1, s0, inter, tile(v_ref, h, bi), mask1, reverse)
            if readout:
                o = o + tile(ob_ref, h, bi)
                ms = jnp.mean(o * o, axis=-1, keepdims=True)
                y = o * lax.rsqrt(ms + EPS) * ng_ref[:, lanes]
                o = y * tile(g_ref, h, bi).astype(F32)
            o_ref[bi, pl.ds(r0, c), lanes] = o.astype(o_ref.dtype)
        return carry

    lax.fori_loop(0, n_chunks, chunk_body, 0, unroll=4)


def _scan_call(v, lf, s0, q, *, reverse, ob=None, g=None, ng=None):
    b, l, _ = v[0].shape
    dh = ng.shape[1] if ng is not None else s0.shape[1] * HEAD_DIM
    tb = min(SCAN_BLOCK, l)
    n_blocks = l // tb
    readout = ob is not None
    blk = (lambda j: n_blocks - 1 - j) if reverse else (lambda j: j)
    seq_spec = lambda group: pl.BlockSpec((b, tb, dh), lambda j: (0, blk(j), group))
    st_spec = pl.BlockSpec(s0.shape, lambda j: (0, 0, 0, 0))
    in_specs = [seq_spec(v[1]), seq_spec(lf[1]), st_spec, seq_spec(q[1])]
    args = [v[0], lf[0], s0, q[0]]
    if readout:
        in_specs += [seq_spec(ob[1]), seq_spec(g[1]), pl.BlockSpec((1, dh), lambda j: (0, 0))]
        args += [ob[0], g[0], ng]
    return pl.pallas_call(
        functools.partial(_scan_kernel, reverse=reverse, readout=readout),
        grid=(n_blocks,),
        in_specs=in_specs,
        out_specs=seq_spec(0),
        out_shape=jax.ShapeDtypeStruct((b, l, dh), BF16 if readout else F32),
        scratch_shapes=[pltpu.VMEM(s0.shape, F32), pltpu.VMEM(s0.shape, BF16), pltpu.VMEM((b, tb, dh), F32)],
        compiler_params=_params(("arbitrary",)),
        name="scan_readout_fwd" if readout else "scan_out_bwd",
    )(*args)


def _ctx_proj_kernel(ctx_ref, g_ref, mod_ref, w_ref, o_ref, wbf_ref, hc_ref):
    @pl.when(pl.program_id(0) == 0)
    def _():
        for bi in range(ctx_ref.shape[0]):
            hc_ref[bi] = _norm_modulate(ctx_ref[bi], g_ref[...], mod_ref[0, 0:1, :],
                                        mod_ref[1, 0:1, :]).astype(BF16)

    w = w_ref[...].astype(BF16)
    wbf_ref[...] = w
    for bi in range(ctx_ref.shape[0]):
        o_ref[bi] = jnp.dot(hc_ref[bi], w, preferred_element_type=F32)


def _ctx_proj_call(ctx, g, mod_early, mod_row, w, width, tn=1024):
    b, lc, d = ctx.shape
    return pl.pallas_call(
        _ctx_proj_kernel,
        grid=(width // tn,),
        in_specs=[pl.BlockSpec((b, lc, d), lambda j: (0, 0, 0)),
                  pl.BlockSpec((1, d), lambda j: (0, 0)),
                  _mod_block(N_MOD_EARLY, d, lambda j: mod_row),
                  pl.BlockSpec((d, tn), lambda j: (0, j))],
        out_specs=[pl.BlockSpec((b, lc, tn), lambda j: (0, 0, j)), pl.BlockSpec((d, tn), lambda j: (0, j))],
        out_shape=[jax.ShapeDtypeStruct((b, lc, width), F32), jax.ShapeDtypeStruct((d, width), BF16)],
        scratch_shapes=[pltpu.VMEM((b, lc, d), BF16)],
        compiler_params=_params(("arbitrary",)),
        name="ctx_proj",
    )(ctx, g, mod_early, w)


def _ctx_state_kernel(pv_ref, pf_ref, pb_ref, lg_ref, sf_ref, sb_ref):
    c = SCAN_CHUNK
    n_batch, rows = pv_ref.shape[0], pv_ref.shape[1]
    n_chunks = rows // c
    n_heads = pv_ref.shape[2] // HEAD_DIM
    for reverse, p_ref, s_ref in ((False, pf_ref, sf_ref), (True, pb_ref, sb_ref)):
        _, tri2 = _scan_masks(c, reverse)
        s_ref[...] = jnp.zeros(s_ref.shape, F32)

        def chunk_body(ci, carry, reverse=reverse, p_ref=p_ref, s_ref=s_ref, tri2=tri2):
            cidx = (n_chunks - 1 - ci) if reverse else ci
            r0 = pl.multiple_of(cidx * c, c)
            recurrences = [(h, bi) for h in range(n_heads) for bi in range(n_batch)]
            tile = lambda ref, h, bi: ref[bi, pl.ds(r0, c), h * HEAD_DIM:(h + 1) * HEAD_DIM]
            gates = [_forget_gate(tile(p_ref, h, bi), lg_ref[int(reverse), :, h * HEAD_DIM:(h + 1) * HEAD_DIM])
                     for h, bi in recurrences]
            sums = [_scan_cumsum(jnp.log2(f), tri2) for f in gates]
            for (h, bi), f, a in zip(recurrences, gates, sums):
                a_tot = _scan_refs(a, reverse)[3]
                kh = ((1.0 - f) * jnp.exp2(a_tot - a)).astype(BF16)
                upd = lax.dot_general(tile(pv_ref, h, bi).astype(BF16), kh, _TN, preferred_element_type=F32)
                s_ref[bi, h] = s_ref[bi, h] * jnp.exp2(a_tot) + upd
            return carry

        lax.fori_loop(0, n_chunks, chunk_body, 0)


def _ctx_state_call(pc, lb_logits, dh):
    b, lc, _ = pc.shape
    n_heads = dh // HEAD_DIM
    group = lambda n: pl.BlockSpec((b, lc, dh), lambda j: (0, 0, n))
    st_spec = pl.BlockSpec((b, n_heads, HEAD_DIM, HEAD_DIM), lambda j: (0, 0, 0, 0))
    st_shape = jax.ShapeDtypeStruct((b, n_heads, HEAD_DIM, HEAD_DIM), F32)
    return pl.pallas_call(
        _ctx_state_kernel,
        grid=(1,),
        in_specs=[group(0), group(1), group(2), pl.BlockSpec(lb_logits.shape, lambda j: (0, 0, 0))],
        out_specs=[st_spec, st_spec],
        out_shape=[st_shape, st_shape],
        compiler_params=_params(("arbitrary",)),
        name="ctx_state",
    )(pc, pc, pc, lb_logits)


CONV_TILE_ROWS = 16
_HPAD = 16
_HROW = GRID_W + 2 * _HPAD


def _conv_kernel(cur_ref, prev_ref, next_ref, w_ref, b_ref, lg_ref, lb_ref, cc_ref, wm_ref, bm_ref,
                 o_ref, modl_ref, hpad_ref, vpad_ref, y_ref, *, n_tiles):
    tr = CONV_TILE_ROWS
    half = w_ref.shape[1] // 2
    n_lane_tiles = half // 128
    i = pl.program_id(1)

    modl_ref[0] = _mod_rows(cc_ref, wm_ref, bm_ref)

    hpad_ref[:, :, :_HPAD, :] = jnp.zeros((tr, n_lane_tiles, _HPAD, 128), F32)
    hpad_ref[:, :, _HPAD + GRID_W:, :] = jnp.zeros((tr, n_lane_tiles, _HPAD, 128), F32)
    for r in range(tr):
        for lt in range(n_lane_tiles):
            hpad_ref[r, lt, _HPAD:_HPAD + GRID_W, :] = cur_ref[0, r * GRID_W:(r + 1) * GRID_W,
                                                               lt * 128:(lt + 1) * 128]
    halo = CONV_PAD * GRID_W
    for lt in range(n_lane_tiles):
        lanes = slice(lt * 128, (lt + 1) * 128)
        top = prev_ref[0, tr * GRID_W - halo:, lanes]
        bot = next_ref[0, :halo, lanes]
        vpad_ref[lt, :halo, :] = jnp.where(i > 0, top, 0.0)
        vpad_ref[lt, halo:halo + tr * GRID_W, :] = cur_ref[0, :, half + lt * 128:half + (lt + 1) * 128]
        vpad_ref[lt, halo + tr * GRID_W:, :] = jnp.where(i < n_tiles - 1, bot, 0.0)

    def h_row(r, carry):
        o0 = pl.multiple_of(r * GRID_W, GRID_W)
        for lt in range(n_lane_tiles):
            lanes = slice(lt * 128, (lt + 1) * 128)
            acc = jnp.zeros((GRID_W, 128), F32)
            for jj in range(CONV_WIDTH):
                off = _HPAD - CONV_PAD + jj
                acc = acc + hpad_ref[r, lt, off:off + GRID_W, :] * w_ref[jj:jj + 1, lanes]
            y_ref[pl.ds(o0, GRID_W), lanes] = acc + b_ref[:, lanes]
        return carry

    lax.fori_loop(0, tr, h_row, 0)

    n_sub = GRID_W // 8
    for lt in range(n_lane_tiles):
        vl = slice(half + lt * 128, half + (lt + 1) * 128)
        wv = [jnp.broadcast_to(w_ref[jj:jj + 1, vl], (8, 128)) for jj in range(CONV_WIDTH)]
        bias = b_ref[:, vl]

        def v_row(r, carry, lt=lt, vl=vl, wv=wv, bias=bias):
            v0 = pl.multiple_of(r * GRID_W, GRID_W)
            acc = [jnp.zeros((8, 128), F32)] * n_sub
            for jj in range(CONV_WIDTH):
                src = pl.multiple_of(v0 + jj * GRID_W, GRID_W)
                tap = vpad_ref[lt, pl.ds(src, GRID_W), :]
                acc = [acc[s] + tap[8 * s:8 * s + 8, :] * wv[jj] for s in range(n_sub)]
            y_ref[pl.ds(v0, GRID_W), vl] = jnp.concatenate(acc, axis=0) + bias
            return carry

        lax.fori_loop(0, tr, v_row, 0, unroll=4)

    def ln_row(r, carry):
        o0 = pl.multiple_of(r * GRID_W, GRID_W)
        y = y_ref[pl.ds(o0, GRID_W), :]
        mu = jnp.mean(y, axis=-1, keepdims=True)
        yc = y - mu
        var = jnp.mean(yc * yc, axis=-1, keepdims=True)
        z = yc * lax.rsqrt(var + EPS) * lg_ref[...] + lb_ref[...]
        o_ref[0, pl.ds(o0, GRID_W), :] = _silu(z).astype(BF16)
        return carry

    lax.fori_loop(0, tr, ln_row, 0, unroll=4)


def _conv_call(u, w, bias, ln_g, ln_b, cc, w_mod, b_mod):
    b, l, ch = u.shape
    half = ch // 2
    tile = CONV_TILE_ROWS * GRID_W
    n_tiles = l // tile
    rows, d = cc.shape
    n_late = w_mod.shape[1] // d - N_MOD_EARLY
    tnm = n_late * d // (b * n_tiles)
    per = d // tnm
    step = lambda i, j: i * n_tiles + j
    return pl.pallas_call(
        functools.partial(_conv_kernel, n_tiles=n_tiles),
        grid=(b, n_tiles),
        in_specs=[pl.BlockSpec((1, tile, ch), lambda i, j: (i, j, 0)),
                  pl.BlockSpec((1, tile, half), lambda i, j: (i, jnp.maximum(j - 1, 0), 1)),
                  pl.BlockSpec((1, tile, half), lambda i, j: (i, jnp.minimum(j + 1, n_tiles - 1), 1)),
                  pl.BlockSpec((CONV_WIDTH, ch), lambda i, j: (0, 0)),
                  pl.BlockSpec((1, ch), lambda i, j: (0, 0)),
                  pl.BlockSpec((1, ch), lambda i, j: (0, 0)),
                  pl.BlockSpec((1, ch), lambda i, j: (0, 0)),
                  pl.BlockSpec((rows, d), lambda i, j: (0, 0)),
                  pl.BlockSpec((d, tnm), lambda i, j: (0, N_MOD_EARLY * per + step(i, j))),
                  pl.BlockSpec((1, tnm), lambda i, j: (0, N_MOD_EARLY * per + step(i, j)))],
        out_specs=[pl.BlockSpec((1, tile, ch), lambda i, j: (i, j, 0)),
                   pl.BlockSpec((1, rows, tnm), lambda i, j: (step(i, j) // per, 0, step(i, j) % per))],
        out_shape=[jax.ShapeDtypeStruct((b, l, ch), BF16), jax.ShapeDtypeStruct((n_late, rows, d), F32)],
        scratch_shapes=[pltpu.VMEM((CONV_TILE_ROWS, half // 128, _HROW, 128), F32),
                        pltpu.VMEM((half // 128, tile + 2 * CONV_PAD * GRID_W, 128), F32),
                        pltpu.VMEM((tile, ch), F32)],
        compiler_params=_params(("arbitrary", "arbitrary")),
        name="conv",
    )(u, u, u, w, bias, ln_g, ln_b, cc, w_mod, b_mod)


MIX_SLICE = 256

def _mix_kernel(uc_ref, og_ref, gc_ref, gh_ref, x_ref, wpw_ref, wh_ref, wo_ref,
                npost_ref, npre_ref, mod_ref, x1_ref, h2_ref):
    tm = x_ref.shape[1]
    n_split = tm // MIX_SLICE
    for r in range(n_split):
        rows = slice(r * MIX_SLICE, (r + 1) * MIX_SLICE)
        y_c = jnp.dot(uc_ref[0, rows, :], wpw_ref[...], preferred_element_type=F32)
        y_h = jnp.dot(og_ref[0, rows, :], wh_ref[...], preferred_element_type=F32)
        z = gc_ref[0, rows, :].astype(F32) * y_c + gh_ref[0, rows, :].astype(F32) * y_h
        y = jnp.dot(z.astype(BF16), wo_ref[...], preferred_element_type=F32)
        ms = jnp.mean(y * y, axis=-1, keepdims=True)
        yn = y * lax.rsqrt(ms + EPS) * npost_ref[...]
        x1 = x_ref[0, rows, :] + mod_ref[0, 0:1, :] * yn
        x1_ref[0, rows, :] = x1
        h2_ref[0, rows, :] = _norm_modulate(x1, npre_ref[...], mod_ref[1, 0:1, :],
                                            mod_ref[2, 0:1, :]).astype(BF16)


def _mix_call(uc, og, gates, x, wpw, wh, wo, npost, npre, mod_late, tm=512):
    b, l, d = x.shape
    dc = uc.shape[2]
    gc = gh = gates
    row = lambda width, col=0: pl.BlockSpec((1, tm, width), lambda i, j: (i, j, col))
    const = lambda shape: pl.BlockSpec(shape, lambda i, j: (0,) * len(shape), pipeline_mode=pl.Buffered(1))
    return pl.pallas_call(
        _mix_kernel,
        grid=(b, l // tm),
        in_specs=[row(dc), row(dc), row(d, 0), row(d, 1), row(d),
                  const(wpw.shape), const(wh.shape), const(wo.shape),
                  const((1, d)), const((1, d)),
                  _mod_block(N_MOD - N_MOD_EARLY, d, lambda i, j: i)],
        out_specs=[row(d), row(d)],
        out_shape=[jax.ShapeDtypeStruct((b, l, d), F32), jax.ShapeDtypeStruct((b, l, d), BF16)],
        compiler_params=_params(("parallel", "parallel")),
        name="mix",
    )(uc, og, gc, gh, x, wpw, wh, wo, npost, npre, mod_late)


MLP_SLICE = 256

def _mlp_kernel(h_ref, w1_ref, w2_ref, x1_ref, npost_ref, mod_ref, o_ref, *, n_ff):
    j = pl.program_id(2)

    def partial_sum(rows):
        a = jnp.dot(h_ref[0, rows, :], w1_ref[...], preferred_element_type=F32)
        a = jnp.square(jnp.maximum(a, 0.0)).astype(BF16)
        return jnp.dot(a, w2_ref[...], preferred_element_type=F32)

    @pl.when(j == 0)
    def _():
        o_ref[0] = partial_sum(slice(None))

    @pl.when((j > 0) & (j < n_ff - 1))
    def _():
        o_ref[0] += partial_sum(slice(None))

    @pl.when(j == n_ff - 1)
    def _():
        for r in range(o_ref.shape[1] // MLP_SLICE):
            rows = slice(r * MLP_SLICE, (r + 1) * MLP_SLICE)
            y = o_ref[0, rows, :] + partial_sum(rows)
            ms = jnp.mean(y * y, axis=-1, keepdims=True)
            yn = y * lax.rsqrt(ms + EPS) * npost_ref[...]
            o_ref[0, rows, :] = x1_ref[0, rows, :] + mod_ref[3, 0:1, :] * yn


def _mlp_call(h2, w1, w2, x1, npost, mod_late, tm=1024, tf=1024):
    b, l, d = x1.shape
    dff = w1.shape[1]
    n_ff = dff // tf
    assert n_ff >= 2, "first and last d_ff steps are distinct code paths"
    return pl.pallas_call(
        functools.partial(_mlp_kernel, n_ff=n_ff),
        grid=(b, l // tm, n_ff),
        in_specs=[pl.BlockSpec((1, tm, d), lambda i, m, j: (i, m, 0)),
                  pl.BlockSpec((d, tf), lambda i, m, j: (0, j)),
                  pl.BlockSpec((tf, d), lambda i, m, j: (j, 0)),
                  pl.BlockSpec((1, tm, d), lambda i, m, j: (i, m, 0), pipeline_mode=pl.Buffered(1)),
                  pl.BlockSpec((1, d), lambda i, m, j: (0, 0)),
                  _mod_block(N_MOD - N_MOD_EARLY, d, lambda i, m, j: i)],
        out_specs=pl.BlockSpec((1, tm, d), lambda i, m, j: (i, m, 0)),
        out_shape=jax.ShapeDtypeStruct((b, l, d), F32),
        compiler_params=_params(("parallel", "parallel", "arbitrary")),
        name="mlp",
    )(h2, w1, w2, x1, npost, mod_late)


def kernel(x, c, ctx, c_ctx, w_mod, b_mod, norm_pre_mix, norm_post_mix, norm_pre_mlp, norm_post_mlp, w_in,
           conv_dw_w, conv_dw_b, conv_ln_g, conv_ln_b, conv_pw_w, hgrn_lb_logits, hgrn_norm_g, hgrn_out_w,
           w_out, mlp_w1, mlp_w2):
    assert w_mod.shape[0] == 1, "single-layer block"
    b, l, d = x.shape
    lc = ctx.shape[1]
    dh = hgrn_norm_g.shape[1]
    dc = conv_dw_w.shape[2]
    assert l % (CONV_TILE_ROWS * GRID_W) == 0 and CONV_PAD <= CONV_TILE_ROWS and CONV_PAD <= _HPAD
    assert l % SCAN_BLOCK == 0 and lc % SCAN_CHUNK == 0

    cond = jnp.concatenate([c, c_ctx[None, :]], axis=0)
    cc = jnp.pad(cond[:, None, :], ((0, 0), (0, MOD_ROW_STRIDE - 1), (0, 0))).reshape(-1, d)
    mod_early = _mod_call(cc, w_mod[0], b_mod, N_MOD_EARLY)

    w_in0 = w_in[0]
    off = lambda n: n * dh

    pc, w_vf = _ctx_proj_call(ctx, norm_pre_mix, mod_early, b, w_in0, 3 * dh)
    s_f, s_b = _ctx_state_call(pc, hgrn_lb_logits, dh)

    gate_col = off(5) + 2 * dc
    h, v, lf, w_a = _prenorm_vf_call(x, norm_pre_mix, mod_early, w_vf, hgrn_lb_logits, w_in0, off(3),
                                     gate_col - off(3))
    h = h.reshape(b * l, d)
    seq = lambda t: t.reshape(b, l, t.shape[-1])
    rest = lambda col: col - off(3)
    qg, wpw, wh, wo, w_b = _proj_call(h, w_a, [rest(off(3))], 2 * dh, _ep_scaled_silu, [BF16],
                                      side=(conv_pw_w[0], hgrn_out_w[0], w_out[0]),
                                      side_cols=(w_in0, gate_col, 2 * d), name="proj_qg")
    (u,) = _proj_call(h, w_a, [rest(off(5)), rest(off(5) + dc)], dc, _ep_glu, [F32], name="proj_glu")
    gates, w1, w2 = _proj_call(h, w_b, [0], 2 * d, _ep_sigmoid, [BF16],
                               side=(mlp_w1[0], mlp_w2[0]), name="proj_gates")

    qg = seq(qg)
    o_b = _scan_call((v, 0), (lf, 1), s_b, (qg, 0), reverse=True)
    og = _scan_call((v, 0), (lf, 0), s_f, (qg, 0), reverse=False, ob=(o_b, 0), g=(qg, 1),
                    ng=hgrn_norm_g)

    uc, mod_late = _conv_call(seq(u), conv_dw_w[0], conv_dw_b, conv_ln_g, conv_ln_b, cc, w_mod[0], b_mod)

    x1, h2 = _mix_call(uc, og, seq(gates), x, wpw, wh, wo, norm_post_mix, norm_pre_mlp, mod_late)
    return _mlp_call(h2, w1, w2, x1, norm_post_mlp, mod_late)
```
